```python
import math
import jax
import jax.numpy as jnp
from jax import lax
import numpy as np

D_MODEL = 2048
BATCH = 2
SEQ = 4096
DEPTH = 4
DEC_BATCH = 128
DEC_SEQ = 4
PAST_LEN = 8192
PAGE_SIZE = 128

N_MIXERS = 4
N_GM_LAYERS = len(range(0, DEPTH, N_MIXERS))
N_SWA_LAYERS = len(range(1, DEPTH, N_MIXERS))
N_SC_LAYERS = len(range(2, DEPTH, N_MIXERS))
N_MLA_LAYERS = len(range(3, DEPTH, N_MIXERS))

RMS_EPS = 1e-6
NEG_INF = -1e30

D_FF = 5632
FFN_RES_W = 0.5

GM_CHUNK = 128
GM_WIDTH = 2 * D_MODEL
GM_GROUPS = 8
GM_GROUP_W = GM_WIDTH // GM_GROUPS

SWA_WINDOW = 128
SWA_HEAD_DIM = 64
SWA_HEADS = D_MODEL // SWA_HEAD_DIM
SWA_KV_HEADS = 8
SWA_GROUP = SWA_HEADS // SWA_KV_HEADS
SWA_SCALE = SWA_HEAD_DIM ** -0.5
N_BUCKETS = 32
BUCKET_MAX_DIST = 128

CONV_WIDTH = 3
SC_WIDTH = D_MODEL

MLA_HEADS = 16
MLA_Q_LORA = 512
MLA_KV_LORA = 512
MLA_NOPE = 128
MLA_ROPE = 64
MLA_V = 128
MLA_SCALE = (MLA_NOPE + MLA_ROPE) ** -0.5
ROPE_THETA = 10000.0
Q_BLOCK = 128

kernel_name = 'hybrid_gmlp_swa_conv_mla_step'


def rmsnorm(x, g):
    xf = x.astype(jnp.float32)
    y = xf * lax.rsqrt(jnp.mean(xf * xf, axis=-1, keepdims=True) + RMS_EPS)
    return (y * g.astype(jnp.float32)).astype(x.dtype)


def layernorm(x, g, b):
    xf = x.astype(jnp.float32)
    xc = xf - jnp.mean(xf, axis=-1, keepdims=True)
    y = xc * lax.rsqrt(jnp.mean(xc * xc, axis=-1, keepdims=True) + RMS_EPS)
    return (y * g.astype(jnp.float32) + b.astype(jnp.float32)).astype(x.dtype)


def ada_modulation(c, w, b):
    m = jax.nn.silu(c) @ w + b
    return m.reshape(c.shape[0], 3, 3, D_MODEL)


def ada_in(x, mod, g_pre):
    return rmsnorm(x, g_pre) * (1 + mod[:, None, 1]) + mod[:, None, 0]


def ada_out(x, out, mod, g_post, res_w):
    return x + res_w * mod[:, None, 2] * rmsnorm(out, g_post)


def half_ffn(x, mod, g_pre, g_post, w_in, w_out):
    h = ada_in(x, mod, g_pre)
    gate, up = jnp.split(h @ w_in, 2, axis=-1)
    return ada_out(x, (jax.nn.silu(gate) * up) @ w_out, mod, g_post, FFN_RES_W)


def rope(x, pos):
    half = x.shape[-1] // 2
    inv = ROPE_THETA ** (-jnp.arange(half, dtype=jnp.float32) / half)
    ang = pos.astype(jnp.float32)[:, None] * inv[None, :]
    shape = (ang.shape[0],) + (1,) * (x.ndim - 3) + (half,)
    cos, sin = jnp.cos(ang).reshape(shape), jnp.sin(ang).reshape(shape)
    x1, x2 = x[..., :half].astype(jnp.float32), x[..., half:].astype(jnp.float32)
    return jnp.concatenate([x1 * cos - x2 * sin, x2 * cos + x1 * sin], axis=-1).astype(x.dtype)


def t5_bucket(delta):
    n = jnp.maximum(delta, 0)
    max_exact = N_BUCKETS // 2
    log_ratio = jnp.log(jnp.maximum(n, 1).astype(jnp.float32) / max_exact) / math.log(BUCKET_MAX_DIST / max_exact)
    large = jnp.minimum(max_exact + (log_ratio * (N_BUCKETS - max_exact)).astype(jnp.int32), N_BUCKETS - 1)
    return jnp.where(n < max_exact, n, large)


def chunk_mlp(h, w_in, ln_g, ln_b, w_s, b_s, w_out):
    B, L, _ = h.shape
    u, v = jnp.split(jax.nn.gelu(h @ w_in), 2, axis=-1)
    v = layernorm(v, ln_g, ln_b)
    lc = min(L, GM_CHUNK)
    nc = L // lc
    vb = v.reshape(B, nc, lc, GM_GROUPS, GM_GROUP_W)
    ws = w_s[:, :lc, :lc] * jnp.tril(jnp.ones((lc, lc), w_s.dtype))
    mixed = jnp.einsum('gts,bnsgc->bntgc', ws, vb) + b_s[:, :lc].T[None, None, :, :, None]
    out = u * mixed.reshape(B, L, GM_WIDTH)
    return out @ w_out, v[:, -lc:]


def swa_qkv(h, w_qkv):
    B, L, _ = h.shape
    qkv = h @ w_qkv
    nq, nk = SWA_HEADS * SWA_HEAD_DIM, SWA_KV_HEADS * SWA_HEAD_DIM
    q = qkv[..., :nq].reshape(B, L, SWA_KV_HEADS, SWA_GROUP, SWA_HEAD_DIM)
    k = qkv[..., nq:nq + nk].reshape(B, L, SWA_KV_HEADS, SWA_HEAD_DIM)
    v = qkv[..., nq + nk:].reshape(B, L, SWA_KV_HEADS, SWA_HEAD_DIM)
    return q, k, v


def window_attend(q, k, v, delta, valid, sinks, rel_bias):
    B, N, Lq = q.shape[:3]
    Lk = k.shape[2]
    logits = jnp.einsum('bnqkgd,bnskd->bnkgqs', q, k).astype(jnp.float32) * SWA_SCALE
    bias = rel_bias[t5_bucket(delta)].astype(jnp.float32)
    bias = bias.transpose(2, 0, 1).reshape(SWA_KV_HEADS, SWA_GROUP, Lq, Lk)
    logits = jnp.where(valid[None, :, None, None], logits + bias, NEG_INF)
    sink = jnp.broadcast_to(sinks.astype(jnp.float32).reshape(SWA_KV_HEADS, SWA_GROUP, 1, 1), logits.shape[:-1] + (1,))
    p = jax.nn.softmax(jnp.concatenate([logits, sink], axis=-1), axis=-1)[..., :-1]
    o = jnp.einsum('bnkgqs,bnskd->bnqkgd', p.astype(v.dtype), v)
    return o.reshape(B, N * Lq, SWA_HEADS * SWA_HEAD_DIM)


def swa_prompt(h, w_qkv, w_o, sinks, rel_bias):
    q, k, v = swa_qkv(h, w_qkv)
    B, L, _ = h.shape
    nb = L // SWA_WINDOW
    qb = q.reshape(B, nb, SWA_WINDOW, SWA_KV_HEADS, SWA_GROUP, SWA_HEAD_DIM)

    def band(t):
        tb = t.reshape(B, nb, SWA_WINDOW, SWA_KV_HEADS, SWA_HEAD_DIM)
        prev = jnp.concatenate([jnp.zeros_like(tb[:, :1]), tb[:, :-1]], axis=1)
        return jnp.concatenate([prev, tb], axis=2)

    i_q = jnp.arange(SWA_WINDOW)
    i_k = jnp.arange(2 * SWA_WINDOW)
    delta = SWA_WINDOW + i_q[:, None] - i_k[None, :]
    k_pos = (jnp.arange(nb)[:, None] - 1) * SWA_WINDOW + i_k[None, :]
    valid = ((delta >= 0) & (delta < SWA_WINDOW))[None] & (k_pos >= 0)[:, None, :]
    o = window_attend(qb, band(k), band(v), delta, valid, sinks, rel_bias)
    keep = min(SWA_WINDOW, L)
    return o @ w_o, k[:, -keep:], v[:, -keep:]


def swa_sample(h, buf_k, buf_v, past_len, w_qkv, w_o, sinks, rel_bias):
    q, k, v = swa_qkv(h, w_qkv)
    L = h.shape[1]
    lb = buf_k.shape[1]
    kk = jnp.concatenate([buf_k, k], axis=1)
    vv = jnp.concatenate([buf_v, v], axis=1)
    q_pos = past_len + jnp.arange(L)
    k_pos = past_len - lb + jnp.arange(lb + L)
    delta = q_pos[:, None] - k_pos[None, :]
    valid = ((delta >= 0) & (delta < SWA_WINDOW))[None]
    o = window_attend(q[:, None], kk[:, None], vv[:, None], delta, valid, sinks, rel_bias)
    return o @ w_o, kk[:, -lb:], vv[:, -lb:]


def short_conv(h, prev, w_in, conv_w, w_out):
    L = h.shape[1]
    gb, gc, z = jnp.split(h @ w_in, 3, axis=-1)
    zp = jnp.concatenate([prev, gc * z], axis=1)
    y = conv_w[CONV_WIDTH - 1] * zp[:, CONV_WIDTH - 1:CONV_WIDTH - 1 + L]
    for t in range(CONV_WIDTH - 1):
        y = y + conv_w[t] * zp[:, t:t + L]
    return (gb * y) @ w_out, zp[:, -(CONV_WIDTH - 1):]


def mla_project(h, pos, w_qa, qa_norm, w_qb, w_kva, kva_norm, w_kvb):
    B, L, _ = h.shape
    q = (rmsnorm(h @ w_qa, qa_norm) @ w_qb).reshape(B, L, MLA_HEADS, MLA_NOPE + MLA_ROPE)
    q_nope, q_rope = q[..., :MLA_NOPE], rope(q[..., MLA_NOPE:], pos)
    kv = h @ w_kva
    c_kv = rmsnorm(kv[..., :MLA_KV_LORA], kva_norm)
    k_rope = rope(kv[..., MLA_KV_LORA:], pos)
    w_uk = w_kvb.reshape(MLA_KV_LORA, MLA_HEADS, MLA_NOPE + MLA_V)[..., :MLA_NOPE]
    q_lat = jnp.einsum('bqhn,rhn->bqhr', q_nope, w_uk)
    return q_lat, q_rope, c_kv, k_rope


def mla_logits(q_lat, q_rope, c_kv, k_rope):
    s = jnp.einsum('bqhr,bsr->bhqs', q_lat, c_kv) + jnp.einsum('bqhe,bse->bhqs', q_rope, k_rope)
    return s.astype(jnp.float32) * MLA_SCALE


def mla_out(o_lat, w_kvb, w_o):
    B, L = o_lat.shape[:2]
    w_uv = w_kvb.reshape(MLA_KV_LORA, MLA_HEADS, MLA_NOPE + MLA_V)[..., MLA_NOPE:]
    o = jnp.einsum('bqhr,rhv->bqhv', o_lat, w_uv)
    return o.reshape(B, L, MLA_HEADS * MLA_V) @ w_o


def mla_prompt(h, w_qa, qa_norm, w_qb, w_kva, kva_norm, w_kvb, w_o):
    B, L, _ = h.shape
    pos = jnp.arange(L)
    q_lat, q_rope, c_kv, k_rope = mla_project(h, pos, w_qa, qa_norm, w_qb, w_kva, kva_norm, w_kvb)
    nq = L // Q_BLOCK
    ql = q_lat.reshape(B, nq, Q_BLOCK, MLA_HEADS, MLA_KV_LORA).swapaxes(0, 1)
    qr = q_rope.reshape(B, nq, Q_BLOCK, MLA_HEADS, MLA_ROPE).swapaxes(0, 1)

    def one_block(args):
        ql_b, qr_b, blk = args
        q_pos = blk * Q_BLOCK + jnp.arange(Q_BLOCK)
        logits = jnp.where(pos[None, :] <= q_pos[:, None], mla_logits(ql_b, qr_b, c_kv, k_rope), NEG_INF)
        p = jax.nn.softmax(logits, axis=-1).astype(c_kv.dtype)
        return jnp.einsum('bhqs,bsr->bqhr', p, c_kv)

    o_lat = lax.map(one_block, (ql, qr, jnp.arange(nq)))
    o_lat = o_lat.swapaxes(0, 1).reshape(B, L, MLA_HEADS, MLA_KV_LORA)
    return mla_out(o_lat, w_kvb, w_o), c_kv, k_rope


def mla_sample(h, cache_kv, cache_kr, page_table, j, w_qa, qa_norm, w_qb, w_kva, kva_norm, w_kvb, w_o):
    DB, L, _ = h.shape
    past_len = page_table.shape[1] * PAGE_SIZE
    pos = past_len + jnp.arange(L)
    q_lat, q_rope, c_kv, k_rope = mla_project(h, pos, w_qa, qa_norm, w_qb, w_kva, kva_norm, w_kvb)
    past_kv = cache_kv[j, page_table].reshape(DB, past_len, MLA_KV_LORA)
    past_kr = cache_kr[j, page_table].reshape(DB, past_len, MLA_ROPE)
    lp = mla_logits(q_lat, q_rope, past_kv, past_kr)
    ln = jnp.where(jnp.tril(jnp.ones((L, L), bool)), mla_logits(q_lat, q_rope, c_kv, k_rope), NEG_INF)
    p = jax.nn.softmax(jnp.concatenate([lp, ln], axis=-1), axis=-1).astype(c_kv.dtype)
    o_lat = jnp.einsum('bhqs,bsr->bqhr', p[..., :past_len], past_kv) + jnp.einsum('bhqs,bsr->bqhr', p[..., past_len:], c_kv)
    return mla_out(o_lat, w_kvb, w_o), c_kv, k_rope


def setup_inputs(seed: int = 0) -> dict:
    key = jax.random.key(seed)
    ks = iter(jax.random.split(key, 48))

    def nrm(shape, s):
        return jax.random.normal(next(ks), shape, jnp.float32) * s

    n_pages = PAST_LEN // PAGE_SIZE
    n_used = DEC_BATCH * n_pages
    n_pool = n_used + (n_used + 3) // 4
    win_buf = min(SWA_WINDOW, PAST_LEN)
    page_table = jax.random.permutation(next(ks), n_pool)[:n_used].reshape(DEC_BATCH, n_pages).astype(jnp.int32)
    return {
        'x_prompt': nrm((BATCH, SEQ, D_MODEL), 1.0),
        'x_sample': nrm((DEC_BATCH, DEC_SEQ, D_MODEL), 1.0),
        'state_swa_k': nrm((N_SWA_LAYERS, DEC_BATCH, win_buf, SWA_KV_HEADS, SWA_HEAD_DIM), 1.0),
        'state_swa_v': nrm((N_SWA_LAYERS, DEC_BATCH, win_buf, SWA_KV_HEADS, SWA_HEAD_DIM), 1.0),
        'state_conv': nrm((N_SC_LAYERS, DEC_BATCH, CONV_WIDTH - 1, SC_WIDTH), 1.0),
        'cache_mla_kv': nrm((N_MLA_LAYERS, n_pool, PAGE_SIZE, MLA_KV_LORA), 1.0),
        'cache_mla_kr': nrm((N_MLA_LAYERS, n_pool, PAGE_SIZE, MLA_ROPE), 1.0),
        'page_table': page_table,
        'c_prompt': nrm((BATCH, D_MODEL), 1.0),
        'c_sample': nrm((DEC_BATCH, D_MODEL), 1.0),
        'ada_w': nrm((DEPTH, D_MODEL, 9 * D_MODEL), 0.5 * D_MODEL ** -0.5),
        'ada_b': nrm((DEPTH, 9 * D_MODEL), 0.01),
        'norm_pre': 1.0 + nrm((DEPTH, 3, D_MODEL), 0.05),
        'norm_post': 1.0 + nrm((DEPTH, 3, D_MODEL), 0.05),
        'ffn_w_in': nrm((DEPTH, 2, D_MODEL, 2 * D_FF), D_MODEL ** -0.5),
        'ffn_w_out': nrm((DEPTH, 2, D_FF, D_MODEL), D_FF ** -0.5),
        'gm_w_in': nrm((N_GM_LAYERS, D_MODEL, 2 * GM_WIDTH), D_MODEL ** -0.5),
        'gm_ln_g': 1.0 + nrm((N_GM_LAYERS, GM_WIDTH), 0.05),
        'gm_ln_b': nrm((N_GM_LAYERS, GM_WIDTH), 0.01),
        'gm_w_s': nrm((N_GM_LAYERS, GM_GROUPS, GM_CHUNK, GM_CHUNK), 0.5 * GM_CHUNK ** -0.5),
        'gm_b_s': 1.0 + nrm((N_GM_LAYERS, GM_GROUPS, GM_CHUNK), 0.1),
        'gm_w_out': nrm((N_GM_LAYERS, GM_WIDTH, D_MODEL), GM_WIDTH ** -0.5),
        'swa_w_qkv': nrm((N_SWA_LAYERS, D_MODEL, (SWA_HEADS + 2 * SWA_KV_HEADS) * SWA_HEAD_DIM), D_MODEL ** -0.5),
        'swa_w_o': nrm((N_SWA_LAYERS, SWA_HEADS * SWA_HEAD_DIM, D_MODEL), (SWA_HEADS * SWA_HEAD_DIM) ** -0.5),
        'swa_sinks': nrm((N_SWA_LAYERS, SWA_HEADS), 0.5),
        'rel_bias': nrm((N_BUCKETS, SWA_HEADS), 0.5),
        'sc_w_in': nrm((N_SC_LAYERS, D_MODEL, 3 * SC_WIDTH), D_MODEL ** -0.5),
        'sc_conv': nrm((N_SC_LAYERS, CONV_WIDTH, SC_WIDTH), CONV_WIDTH ** -0.5),
        'sc_w_out': nrm((N_SC_LAYERS, SC_WIDTH, D_MODEL), SC_WIDTH ** -0.5),
        'mla_w_qa': nrm((N_MLA_LAYERS, D_MODEL, MLA_Q_LORA), D_MODEL ** -0.5),
        'mla_qa_norm': 1.0 + nrm((N_MLA_LAYERS, MLA_Q_LORA), 0.05),
        'mla_w_qb': nrm((N_MLA_LAYERS, MLA_Q_LORA, MLA_HEADS * (MLA_NOPE + MLA_ROPE)), MLA_Q_LORA ** -0.5),
        'mla_w_kva': nrm((N_MLA_LAYERS, D_MODEL, MLA_KV_LORA + MLA_ROPE), D_MODEL ** -0.5),
        'mla_kva_norm': 1.0 + nrm((N_MLA_LAYERS, MLA_KV_LORA), 0.05),
        'mla_w_kvb': nrm((N_MLA_LAYERS, MLA_KV_LORA, MLA_HEADS * (MLA_NOPE + MLA_V)), MLA_KV_LORA ** -0.5),
        'mla_w_o': nrm((N_MLA_LAYERS, MLA_HEADS * MLA_V, D_MODEL), (MLA_HEADS * MLA_V) ** -0.5),
    }


def reference(x_prompt, x_sample, state_swa_k, state_swa_v, state_conv, cache_mla_kv, cache_mla_kr, page_table,
              c_prompt, c_sample, ada_w, ada_b, norm_pre, norm_post, ffn_w_in, ffn_w_out,
              gm_w_in, gm_ln_g, gm_ln_b, gm_w_s, gm_b_s, gm_w_out,
              swa_w_qkv, swa_w_o, swa_sinks, rel_bias,
              sc_w_in, sc_conv, sc_w_out,
              mla_w_qa, mla_qa_norm, mla_w_qb, mla_w_kva, mla_kva_norm, mla_w_kvb, mla_w_o):
    past_len = page_table.shape[1] * PAGE_SIZE
    xp, xs = x_prompt, x_sample
    gm_vp, gm_vs, swa_kp, swa_vp, swa_ks, swa_vs = [], [], [], [], [], []
    conv_p, conv_s, mla_kvp, mla_krp, mla_kvs, mla_krs = [], [], [], [], [], []
    for i in range(DEPTH):
        kind, j = i % N_MIXERS, i // N_MIXERS
        mp = ada_modulation(c_prompt, ada_w[i], ada_b[i])
        ms = ada_modulation(c_sample, ada_w[i], ada_b[i])
        xp = half_ffn(xp, mp[:, 0], norm_pre[i, 0], norm_post[i, 0], ffn_w_in[i, 0], ffn_w_out[i, 0])
        xs = half_ffn(xs, ms[:, 0], norm_pre[i, 0], norm_post[i, 0], ffn_w_in[i, 0], ffn_w_out[i, 0])
        hp = ada_in(xp, mp[:, 1], norm_pre[i, 1])
        hs = ada_in(xs, ms[:, 1], norm_pre[i, 1])
        if kind == 0:
            op, st_p = chunk_mlp(hp, gm_w_in[j], gm_ln_g[j], gm_ln_b[j], gm_w_s[j], gm_b_s[j], gm_w_out[j])
            os_, st_s = chunk_mlp(hs, gm_w_in[j], gm_ln_g[j], gm_ln_b[j], gm_w_s[j], gm_b_s[j], gm_w_out[j])
            gm_vp.append(st_p)
            gm_vs.append(st_s)
        elif kind == 1:
            op, kp, vp = swa_prompt(hp, swa_w_qkv[j], swa_w_o[j], swa_sinks[j], rel_bias)
            os_, ks_, vs_ = swa_sample(hs, state_swa_k[j], state_swa_v[j], past_len, swa_w_qkv[j], swa_w_o[j], swa_sinks[j], rel_bias)
            swa_kp.append(kp)
            swa_vp.append(vp)
            swa_ks.append(ks_)
            swa_vs.append(vs_)
        elif kind == 2:
            zeros = jnp.zeros((hp.shape[0], CONV_WIDTH - 1, SC_WIDTH), hp.dtype)
            op, st_p = short_conv(hp, zeros, sc_w_in[j], sc_conv[j], sc_w_out[j])
            os_, st_s = short_conv(hs, state_conv[j], sc_w_in[j], sc_conv[j], sc_w_out[j])
            conv_p.append(st_p)
            conv_s.append(st_s)
        else:
            op, kvp, krp = mla_prompt(hp, mla_w_qa[j], mla_qa_norm[j], mla_w_qb[j], mla_w_kva[j], mla_kva_norm[j], mla_w_kvb[j], mla_w_o[j])
            os_, kvs, krs = mla_sample(hs, cache_mla_kv, cache_mla_kr, page_table, j, mla_w_qa[j], mla_qa_norm[j], mla_w_qb[j], mla_w_kva[j], mla_kva_norm[j], mla_w_kvb[j], mla_w_o[j])
            mla_kvp.append(kvp)
            mla_krp.append(krp)
            mla_kvs.append(kvs)
            mla_krs.append(krs)
        xp = ada_out(xp, op, mp[:, 1], norm_post[i, 1], 1.0)
        xs = ada_out(xs, os_, ms[:, 1], norm_post[i, 1], 1.0)
        xp = half_ffn(xp, mp[:, 2], norm_pre[i, 2], norm_post[i, 2], ffn_w_in[i, 1], ffn_w_out[i, 1])
        xs = half_ffn(xs, ms[:, 2], norm_pre[i, 2], norm_post[i, 2], ffn_w_in[i, 1], ffn_w_out[i, 1])
    return (xp, xs,
            jnp.stack(gm_vp), jnp.stack(gm_vs),
            jnp.stack(swa_kp), jnp.stack(swa_vp), jnp.stack(swa_ks), jnp.stack(swa_vs),
            jnp.stack(conv_p), jnp.stack(conv_s),
            jnp.stack(mla_kvp), jnp.stack(mla_krp), jnp.stack(mla_kvs), jnp.stack(mla_krs))
```

```python
import functools
import math

import numpy as np
import jax
import jax.numpy as jnp
from jax import lax
from jax.experimental import pallas as pl
from jax.experimental.pallas import tpu as pltpu

F32 = jnp.float32
BF16 = jnp.bfloat16

VMEM_LIMIT_BYTES = 56 * 1024 * 1024
LANES = 128

RMS_EPS = 1e-6
NEG_INF = -1e30
FFN_RES_W = 0.5

D_MODEL = 2048
GM_GROUPS = 8
GM_CHUNK = 128
SWA_WINDOW = 128
SWA_HEAD_DIM = 64
SWA_HEADS = 32
SWA_KV_HEADS = 8
SWA_GROUP = 4
SWA_SCALE = SWA_HEAD_DIM ** -0.5
N_BUCKETS = 32
BUCKET_MAX_DIST = 128
CONV_WIDTH = 3
MLA_HEADS = 16
MLA_Q_LORA = 512
MLA_KV_LORA = 512
MLA_NOPE = 128
MLA_ROPE = 64
MLA_V = 128
MLA_SCALE = (MLA_NOPE + MLA_ROPE) ** -0.5
MLA_QK = MLA_KV_LORA + LANES
ROPE_THETA = 10000.0
PAGE_SIZE = 128
PAGES_PER_STEP = 16


def _params(n_axes):
    return pltpu.CompilerParams(dimension_semantics=("arbitrary",) * n_axes,
                                vmem_limit_bytes=VMEM_LIMIT_BYTES)


def _rms(x, g):
    return x * lax.rsqrt(jnp.mean(x * x, axis=-1, keepdims=True) + RMS_EPS) * g


def _lane_tile(x, n):
    return x if n == 1 else jnp.concatenate([x] * n, axis=1)


class _Group:
    def __init__(self, mod, tm, tiles_per_mod_row):
        self.mod = mod
        self.tm = tm
        self.tiles_per_mod_row = tiles_per_mod_row

    def mod_spec(self, layer, col):
        if self.tiles_per_mod_row is None:
            return pl.BlockSpec((None, self.tm, D_MODEL), lambda i, *_: (layer, i, col))
        return pl.BlockSpec((None, 8, D_MODEL), lambda i, *_: (layer, 0, col))

    def row(self):
        if self.tiles_per_mod_row is None:
            return None
        return pl.program_id(0) // self.tiles_per_mod_row

    @staticmethod
    def read(ref, row):
        return ref[...] if row is None else ref[pl.ds(row, 1), :]


def _shr(x, divisor):
    shift = divisor.bit_length() - 1
    assert 1 << shift == divisor
    return lax.shift_right_logical(x, shift)


def _ada_kernel(c_ref, w_ref, b_ref, o_ref, cs_sc):
    @pl.when((pl.program_id(0) == 0) & (pl.program_id(1) == 0))
    def _():
        c = c_ref[...]
        cs_sc[...] = (c * jax.nn.sigmoid(c)).astype(BF16)

    o_ref[...] = jnp.dot(cs_sc[...], w_ref[...].astype(BF16),
                         preferred_element_type=F32) + b_ref[...]


def _ada(c_all, ada_w, ada_b, tn=1024):
    n_layers, d, n = ada_w.shape
    rows = c_all.shape[0]
    return pl.pallas_call(
        _ada_kernel,
        grid=(n_layers, n // tn),
        in_specs=[pl.BlockSpec((rows, d), lambda l, j: (0, 0)),
                  pl.BlockSpec((None, d, tn), lambda l, j: (l, 0, j)),
                  pl.BlockSpec((None, 1, tn), lambda l, j: (l, 0, j))],
        out_specs=pl.BlockSpec((None, rows, tn), lambda l, j: (l, 0, j)),
        out_shape=jax.ShapeDtypeStruct((n_layers, rows, n), F32),
        scratch_shapes=[pltpu.VMEM((rows, d), BF16)],
        compiler_params=_params(2),
        name="ada_modulation",
    )(c_all, ada_w, ada_b.reshape(n_layers, 1, n))


def _ffn_kernel(x_ref, sh_ref, sc_ref, gt_ref, gpre_ref, gpost_ref, wg_ref, wu_ref, wo_ref,
                o_ref, h_sc, acc_sc, *, grp, n_f):
    j = pl.program_id(1)
    row = grp.row()

    @pl.when(j == 0)
    def _():
        h = _rms(x_ref[...], gpre_ref[...]) * (1.0 + grp.read(sc_ref, row)) + grp.read(sh_ref, row)
        h_sc[...] = h.astype(BF16)
        acc_sc[...] = jnp.zeros_like(acc_sc)

    h = h_sc[...]
    g = jnp.dot(h, wg_ref[...], preferred_element_type=F32)
    u = jnp.dot(h, wu_ref[...], preferred_element_type=F32)
    a = (g * jax.nn.sigmoid(g) * u).astype(BF16)
    acc_sc[...] += jnp.dot(a, wo_ref[...], preferred_element_type=F32)

    @pl.when(j == n_f - 1)
    def _():
        o_ref[...] = x_ref[...] + FFN_RES_W * grp.read(gt_ref, row) * _rms(acc_sc[...], gpost_ref[...])


def _half_ffn(x, grp, layer, sub, g_pre, g_post, w_in, w_out, tf=512):
    t, d = x.shape
    f = w_out.shape[0]
    tm, n_f = grp.tm, f // tf
    kern = functools.partial(_ffn_kernel, grp=grp, n_f=n_f)
    return pl.pallas_call(
        kern,
        grid=(t // tm, n_f),
        in_specs=[pl.BlockSpec((tm, d), lambda i, j: (i, 0)),
                  grp.mod_spec(layer, sub * 3 + 0),
                  grp.mod_spec(layer, sub * 3 + 1),
                  grp.mod_spec(layer, sub * 3 + 2),
                  pl.BlockSpec((1, d), lambda i, j: (0, 0)),
                  pl.BlockSpec((1, d), lambda i, j: (0, 0)),
                  pl.BlockSpec((d, tf), lambda i, j: (0, j)),
                  pl.BlockSpec((d, tf), lambda i, j: (0, n_f + j)),
                  pl.BlockSpec((tf, d), lambda i, j: (j, 0))],
        out_specs=pl.BlockSpec((tm, d), lambda i, j: (i, 0)),
        out_shape=jax.ShapeDtypeStruct((t, d), F32),
        scratch_shapes=[pltpu.VMEM((tm, d), BF16), pltpu.VMEM((tm, d), F32)],
        compiler_params=_params(2),
        name="half_ffn",
    )(x, grp.mod, grp.mod, grp.mod, g_pre.reshape(1, d), g_post.reshape(1, d), w_in, w_in, w_out)


def _nmm_kernel(x_ref, sh_ref, sc_ref, gpre_ref, w_ref, o_ref, h_sc, *, grp, act):
    row = grp.row()

    @pl.when(pl.program_id(1) == 0)
    def _():
        h = _rms(x_ref[...], gpre_ref[...]) * (1.0 + grp.read(sc_ref, row)) + grp.read(sh_ref, row)
        h_sc[...] = h.astype(BF16)

    y = jnp.dot(h_sc[...], w_ref[...], preferred_element_type=F32)
    if act == "gelu":
        y = jax.nn.gelu(y, approximate=True)
    o_ref[...] = y.astype(o_ref.dtype)


def _norm_mod_matmul(x, grp, layer, g_pre, w, act=None, tn=1024):
    t, d = x.shape
    n = w.shape[1]
    tn = min(tn, n)
    tm = grp.tm
    kern = functools.partial(_nmm_kernel, grp=grp, act=act)
    return pl.pallas_call(
        kern,
        grid=(t // tm, n // tn),
        in_specs=[pl.BlockSpec((tm, d), lambda i, j: (i, 0)),
                  grp.mod_spec(layer, 3 + 0),
                  grp.mod_spec(layer, 3 + 1),
                  pl.BlockSpec((1, d), lambda i, j: (0, 0)),
                  pl.BlockSpec((d, tn), lambda i, j: (0, j))],
        out_specs=pl.BlockSpec((tm, tn), lambda i, j: (i, j)),
        out_shape=jax.ShapeDtypeStruct((t, n), F32),
        scratch_shapes=[pltpu.VMEM((tm, d), BF16)],
        compiler_params=_params(2),
        name="norm_mod_matmul",
    )(x, grp.mod, grp.mod, g_pre.reshape(1, d), w)


def _mres_kernel(a_ref, w_ref, x_ref, gt_ref, gpost_ref, o_ref, acc_sc, *, grp, n_k):
    k = pl.program_id(1)
    row = grp.row()

    @pl.when(k == 0)
    def _():
        acc_sc[...] = jnp.zeros_like(acc_sc)

    acc_sc[...] += jnp.dot(a_ref[...].astype(BF16), w_ref[...], preferred_element_type=F32)

    @pl.when(k == n_k - 1)
    def _():
        o_ref[...] = x_ref[...] + grp.read(gt_ref, row) * _rms(acc_sc[...], gpost_ref[...])


def _matmul_residual(a, w, x, grp, layer, g_post, tk=1024):
    t, kdim = a.shape
    d = w.shape[1]
    tm, n_k = grp.tm, kdim // tk
    kern = functools.partial(_mres_kernel, grp=grp, n_k=n_k)
    return pl.pallas_call(
        kern,
        grid=(t // tm, n_k),
        in_specs=[pl.BlockSpec((tm, tk), lambda i, k: (i, k)),
                  pl.BlockSpec((tk, d), lambda i, k: (k, 0)),
                  pl.BlockSpec((tm, d), lambda i, k: (i, 0)),
                  grp.mod_spec(layer, 3 + 2),
                  pl.BlockSpec((1, d), lambda i, k: (0, 0))],
        out_specs=pl.BlockSpec((tm, d), lambda i, k: (i, 0)),
        out_shape=jax.ShapeDtypeStruct((t, d), F32),
        scratch_shapes=[pltpu.VMEM((tm, d), F32)],
        compiler_params=_params(2),
        name="matmul_residual",
    )(a, w, x, grp.mod, g_post.reshape(1, d))


def _gm_kernel(u_ref, v_ref, lng_ref, lnb_ref, m_ref, b_ref, o_ref, st_ref, *, seq_len):
    v = v_ref[...]
    vc = v - jnp.mean(v, axis=-1, keepdims=True)
    vn = vc * lax.rsqrt(jnp.mean(vc * vc, axis=-1, keepdims=True) + RMS_EPS) * lng_ref[...] + lnb_ref[...]
    st_ref[...] = vn
    vnb = vn.astype(BF16)
    c = v.shape[0]
    gw = v.shape[1] // GM_GROUPS
    row = lax.broadcasted_iota(jnp.int32, (c, c), 0)
    col = lax.broadcasted_iota(jnp.int32, (c, c), 1)
    keep = (col <= row) & (_shr(row, seq_len) == _shr(col, seq_len))
    for g in range(GM_GROUPS):
        mg = jnp.where(keep, m_ref[g], 0.0).astype(BF16)
        mixed = jnp.dot(mg, vnb[:, g * gw:(g + 1) * gw], preferred_element_type=F32)
        mixed = mixed + _lane_tile(b_ref[g], gw // LANES)
        o_ref[:, g * gw:(g + 1) * gw] = (u_ref[:, g * gw:(g + 1) * gw] * mixed).astype(o_ref.dtype)


def _gm_core(uv, ln_g, ln_b, m, bias, chunk, seq_len, chunks_per_state):
    t = uv.shape[0]
    w = uv.shape[1] // 2
    n_chunks = t // chunk
    n_states = n_chunks // chunks_per_state
    kern = functools.partial(_gm_kernel, seq_len=seq_len)
    return pl.pallas_call(
        kern,
        grid=(n_chunks,),
        in_specs=[pl.BlockSpec((chunk, w), lambda c: (c, 0)),
                  pl.BlockSpec((chunk, w), lambda c: (c, 1)),
                  pl.BlockSpec((1, w), lambda c: (0, 0)),
                  pl.BlockSpec((1, w), lambda c: (0, 0)),
                  pl.BlockSpec((GM_GROUPS, chunk, chunk), lambda c: (0, 0, 0)),
                  pl.BlockSpec((GM_GROUPS, chunk, LANES), lambda c: (0, 0, 0))],
        out_specs=[pl.BlockSpec((chunk, w), lambda c: (c, 0)),
                   pl.BlockSpec((None, chunk, w), lambda c: (c // chunks_per_state, 0, 0))],
        out_shape=[jax.ShapeDtypeStruct((t, w), BF16),
                   jax.ShapeDtypeStruct((n_states, chunk, w), F32)],
        compiler_params=_params(1),
        name="gm_core",
    )(uv, uv, ln_g.reshape(1, w), ln_b.reshape(1, w), m, bias)


def _softmax_sink(s, sink):
    mx = jnp.maximum(jnp.max(s, axis=-1, keepdims=True), sink)
    p = jnp.exp(s - mx)
    denom = jnp.sum(p, axis=-1, keepdims=True) + jnp.exp(sink - mx)
    return p / denom


def _half_lane_mask(rows, parity):
    lane = lax.broadcasted_iota(jnp.int32, (rows, LANES), 1)
    return (lane >= SWA_HEAD_DIM) if parity else (lane < SWA_HEAD_DIM)


def _swa_prompt_kernel(q_ref, kp_ref, ko_ref, vp_ref, vo_ref, bias_ref, sink_ref, o_ref, *, blocks_per_seq):
    w = SWA_WINDOW
    first = (pl.program_id(0) % blocks_per_seq) == 0
    q = (q_ref[...] * SWA_SCALE).astype(BF16)
    k = jnp.concatenate([kp_ref[...], ko_ref[...]], axis=0)
    v = jnp.concatenate([vp_ref[...], vo_ref[...]], axis=0)
    key_col = lax.broadcasted_iota(jnp.int32, (SWA_GROUP * w, 2 * w), 1)
    no_prev = key_col < jnp.where(first, w, 0)
    for m in range(SWA_KV_HEADS // 2):
        qs = jnp.concatenate([q[:, (4 * m + g) * LANES:(4 * m + g + 1) * LANES] for g in range(SWA_GROUP)],
                             axis=0)
        k2 = k[:, m * LANES:(m + 1) * LANES]
        v2 = v[:, m * LANES:(m + 1) * LANES]
        o_m = None
        for parity in range(2):
            keep = _half_lane_mask(2 * w, parity)
            kx = jnp.where(keep, k2, 0.0).astype(BF16)
            vx = jnp.where(keep, v2, 0.0).astype(BF16)
            s = lax.dot_general(qs, kx, (((1,), (1,)), ((), ())), preferred_element_type=F32)
            s = s + bias_ref[2 * m + parity]
            s = jnp.where(no_prev, NEG_INF, s)
            p = _softmax_sink(s, sink_ref[2 * m + parity])
            o_p = jnp.dot(p.astype(BF16), vx, preferred_element_type=F32)
            o_m = o_p if o_m is None else o_m + o_p
        for g in range(SWA_GROUP):
            o_ref[:, (4 * m + g) * LANES:(4 * m + g + 1) * LANES] = o_m[g * w:(g + 1) * w].astype(o_ref.dtype)


def _swa_prompt_core(qkv, bias, sink, seq_len):
    t = qkv.shape[0]
    w = SWA_WINDOW
    nq = SWA_HEADS * SWA_HEAD_DIM
    nkv = SWA_KV_HEADS * SWA_HEAD_DIM
    bps = seq_len // w
    kcol, vcol = nq // nkv, nq // nkv + 1

    def prev(i):
        return jnp.maximum(i - 1, 0)

    kern = functools.partial(_swa_prompt_kernel, blocks_per_seq=bps)
    return pl.pallas_call(
        kern,
        grid=(t // w,),
        in_specs=[pl.BlockSpec((w, nq), lambda i: (i, 0)),
                  pl.BlockSpec((w, nkv), lambda i: (prev(i), kcol)),
                  pl.BlockSpec((w, nkv), lambda i: (i, kcol)),
                  pl.BlockSpec((w, nkv), lambda i: (prev(i), vcol)),
                  pl.BlockSpec((w, nkv), lambda i: (i, vcol)),
                  pl.BlockSpec(bias.shape, lambda i: (0, 0, 0)),
                  pl.BlockSpec(sink.shape, lambda i: (0, 0, 0))],
        out_specs=pl.BlockSpec((w, nq), lambda i: (i, 0)),
        out_shape=jax.ShapeDtypeStruct((t, nq), BF16),
        compiler_params=_params(1),
        name="swa_prompt_core",
    )(qkv, qkv, qkv, qkv, qkv, bias, sink)


def _swa_sample_kernel(q_ref, k_ref, v_ref, bias_ref, sink_ref, o_ref, *, bb):
    n_keys = k_ref.shape[1]

    def body(b, carry):
        k = k_ref[b]
        v = v_ref[b]
        for m in range(SWA_KV_HEADS // 2):
            qs = (q_ref[b, m] * SWA_SCALE).astype(BF16)
            k2 = k[:, m * LANES:(m + 1) * LANES]
            v2 = v[:, m * LANES:(m + 1) * LANES]
            o_m = None
            for parity in range(2):
                keep = _half_lane_mask(n_keys, parity)
                kx = jnp.where(keep, k2, 0.0).astype(BF16)
                vx = jnp.where(keep, v2, 0.0).astype(BF16)
                s = lax.dot_general(qs, kx, (((1,), (1,)), ((), ())), preferred_element_type=F32)
                p = _softmax_sink(s + bias_ref[2 * m + parity], sink_ref[2 * m + parity])
                o_p = jnp.dot(p.astype(BF16), vx, preferred_element_type=F32)
                o_m = o_p if o_m is None else o_m + o_p
            o_ref[b, m] = o_m.astype(o_ref.dtype)
        return carry

    lax.fori_loop(0, bb, body, 0)


def _swa_sample_core(q, kk, vv, bias, sink, bb=8):
    b, n_pair, rows, _ = q.shape
    n_keys = kk.shape[1]
    nkv = kk.shape[2]
    kern = functools.partial(_swa_sample_kernel, bb=bb)
    return pl.pallas_call(
        kern,
        grid=(b // bb,),
        in_specs=[pl.BlockSpec((bb, n_pair, rows, LANES), lambda i: (i, 0, 0, 0)),
                  pl.BlockSpec((bb, n_keys, nkv), lambda i: (i, 0, 0)),
                  pl.BlockSpec((bb, n_keys, nkv), lambda i: (i, 0, 0)),
                  pl.BlockSpec(bias.shape, lambda i: (0, 0, 0)),
                  pl.BlockSpec(sink.shape, lambda i: (0, 0, 0))],
        out_specs=pl.BlockSpec((bb, n_pair, rows, LANES), lambda i: (i, 0, 0, 0)),
        out_shape=jax.ShapeDtypeStruct(q.shape, BF16),
        compiler_params=_params(1),
        name="swa_sample_core",
    )(q, kk, vv, bias, sink)


def _conv_prompt_kernel(gb_ref, gc_ref, z_ref, hc_ref, hz_ref, cw_ref, o_ref, st_ref, zp_sc, *, tiles_per_seq):
    tm = gb_ref.shape[0]
    first = (pl.program_id(0) % tiles_per_seq) == 0
    zz = gc_ref[...] * z_ref[...]
    halo = jnp.where(first, 0.0, hc_ref[...] * hz_ref[...])
    zp_sc[0:8, :] = halo
    zp_sc[8:8 + tm, :] = zz
    y = cw_ref[2:3, :] * zz + cw_ref[1:2, :] * zp_sc[7:7 + tm, :] + cw_ref[0:1, :] * zp_sc[6:6 + tm, :]
    o_ref[...] = (gb_ref[...] * y).astype(o_ref.dtype)
    st_ref[...] = zz[tm - 8:tm]


def _conv_prompt_core(g3, conv_w, seq_len, tm=256):
    t = g3.shape[0]
    c = g3.shape[1] // 3
    tps = seq_len // tm
    n_seq = t // seq_len

    def halo(col):
        return pl.BlockSpec((8, c), lambda i: (jnp.maximum(i * (tm // 8) - 1, 0), col))

    kern = functools.partial(_conv_prompt_kernel, tiles_per_seq=tps)
    return pl.pallas_call(
        kern,
        grid=(t // tm,),
        in_specs=[pl.BlockSpec((tm, c), lambda i: (i, 0)),
                  pl.BlockSpec((tm, c), lambda i: (i, 1)),
                  pl.BlockSpec((tm, c), lambda i: (i, 2)),
                  halo(1), halo(2),
                  pl.BlockSpec((CONV_WIDTH, c), lambda i: (0, 0))],
        out_specs=[pl.BlockSpec((tm, c), lambda i: (i, 0)),
                   pl.BlockSpec((None, 8, c), lambda i: (i // tps, 0, 0))],
        out_shape=[jax.ShapeDtypeStruct((t, c), BF16),
                   jax.ShapeDtypeStruct((n_seq, 8, c), F32)],
        scratch_shapes=[pltpu.VMEM((tm + 8, c), F32)],
        compiler_params=_params(1),
        name="conv_prompt_core",
    )(g3, g3, g3, g3, g3, conv_w)


def _conv_sample_kernel(g_ref, prev_ref, cw_ref, o_ref, st_ref, *, seq_len, c):
    zz = [prev_ref[:, 0:c], prev_ref[:, c:2 * c]]
    for t in range(seq_len):
        base = t * 3 * c
        zz.append(g_ref[:, base + c:base + 2 * c] * g_ref[:, base + 2 * c:base + 3 * c])
    for t in range(seq_len):
        y = cw_ref[2:3, :] * zz[t + 2] + cw_ref[1:2, :] * zz[t + 1] + cw_ref[0:1, :] * zz[t]
        o_ref[:, t * c:(t + 1) * c] = (g_ref[:, t * 3 * c:t * 3 * c + c] * y).astype(o_ref.dtype)
    st_ref[:, 0:c] = zz[seq_len]
    st_ref[:, c:2 * c] = zz[seq_len + 1]


def _conv_sample_core(g3, prev, conv_w, seq_len):
    b = g3.shape[0]
    c = g3.shape[1] // (3 * seq_len)
    kern = functools.partial(_conv_sample_kernel, seq_len=seq_len, c=c)
    return pl.pallas_call(
        kern,
        grid=(1,),
        in_specs=[pl.BlockSpec(g3.shape, lambda i: (0, 0)),
                  pl.BlockSpec(prev.shape, lambda i: (0, 0)),
                  pl.BlockSpec((CONV_WIDTH, c), lambda i: (0, 0))],
        out_specs=[pl.BlockSpec((b, seq_len * c), lambda i: (0, 0)),
                   pl.BlockSpec((b, 2 * c), lambda i: (0, 0))],
        out_shape=[jax.ShapeDtypeStruct((b, seq_len * c), BF16),
                   jax.ShapeDtypeStruct((b, 2 * c), F32)],
        compiler_params=_params(1),
        name="conv_sample_core",
    )(g3, prev, conv_w)


def _mla_proj_kernel(p_ref, cos_ref, sin_ref, qan_ref, kvn_ref, wn_ref, wr_ref, wrs_ref, wuk_ref,
                     q_ref, ckv_ref, kr_ref, kcat_ref):
    r = MLA_KV_LORA
    cos = cos_ref[...]
    sin = sin_ref[...]
    qa = _rms(p_ref[:, 0:MLA_Q_LORA], qan_ref[...]).astype(BF16)
    qn = jnp.dot(qa, wn_ref[...], preferred_element_type=F32)
    qr = jnp.dot(qa, wr_ref[...], preferred_element_type=F32)
    qrs = jnp.dot(qa, wrs_ref[...], preferred_element_type=F32)
    for h in range(MLA_HEADS):
        sl = slice(h * LANES, (h + 1) * LANES)
        q_lat = jnp.dot(qn[:, sl].astype(BF16), wuk_ref[h], preferred_element_type=F32)
        q_rope = qr[:, sl] * cos + qrs[:, sl] * sin
        q_ref[:, h * MLA_QK:h * MLA_QK + r] = (q_lat * MLA_SCALE).astype(q_ref.dtype)
        q_ref[:, h * MLA_QK + r:(h + 1) * MLA_QK] = (q_rope * MLA_SCALE).astype(q_ref.dtype)
    off = MLA_Q_LORA
    ckv = _rms(p_ref[:, off:off + r], kvn_ref[...])
    kr = p_ref[:, off + r:off + r + LANES] * cos + p_ref[:, off + r + LANES:off + r + 2 * LANES] * sin
    ckv_ref[...] = ckv
    kr_ref[...] = kr
    kcat_ref[:, 0:r] = ckv.astype(kcat_ref.dtype)
    kcat_ref[:, r:r + LANES] = kr.astype(kcat_ref.dtype)


def _mla_proj(p, cos, sin, pos_blocks, qa_norm, kva_norm, w_nope, w_rope, w_rope_sw, w_uk_t, tm=256):
    t = p.shape[0]
    hq = MLA_HEADS * MLA_QK
    const2 = lambda i: (0, 0)
    return pl.pallas_call(
        _mla_proj_kernel,
        grid=(t // tm,),
        in_specs=[pl.BlockSpec((tm, p.shape[1]), lambda i: (i, 0)),
                  pl.BlockSpec((tm, LANES), lambda i: (i % pos_blocks, 0)),
                  pl.BlockSpec((tm, LANES), lambda i: (i % pos_blocks, 0)),
                  pl.BlockSpec((1, MLA_Q_LORA), const2),
                  pl.BlockSpec((1, MLA_KV_LORA), const2),
                  pl.BlockSpec(w_nope.shape, const2),
                  pl.BlockSpec(w_rope.shape, const2),
                  pl.BlockSpec(w_rope_sw.shape, const2),
                  pl.BlockSpec(w_uk_t.shape, lambda i: (0, 0, 0))],
        out_specs=[pl.BlockSpec((tm, hq), lambda i: (i, 0)),
                   pl.BlockSpec((tm, MLA_KV_LORA), lambda i: (i, 0)),
                   pl.BlockSpec((tm, LANES), lambda i: (i, 0)),
                   pl.BlockSpec((tm, MLA_QK), lambda i: (i, 0))],
        out_shape=[jax.ShapeDtypeStruct((t, hq), BF16),
                   jax.ShapeDtypeStruct((t, MLA_KV_LORA), F32),
                   jax.ShapeDtypeStruct((t, LANES), F32),
                   jax.ShapeDtypeStruct((t, MLA_QK), BF16)],
        compiler_params=_params(1),
        name="mla_proj",
    )(p, cos, sin, qa_norm.reshape(1, -1), kva_norm.reshape(1, -1), w_nope, w_rope, w_rope_sw, w_uk_t)


def _flash_update(s, v, m_sc, l_sc, acc_sc):
    m_prev = m_sc[...]
    m_new = jnp.maximum(m_prev, jnp.max(s, axis=1, keepdims=True))
    alpha = jnp.exp(m_prev - m_new)
    p = jnp.exp(s - _lane_tile(m_new, s.shape[1] // LANES))
    l_sc[...] = alpha * l_sc[...] + jnp.sum(p, axis=1, keepdims=True)
    acc_sc[...] = acc_sc[...] * _lane_tile(alpha, acc_sc.shape[1] // LANES) + jnp.dot(
        p.astype(BF16), v, preferred_element_type=F32)
    m_sc[...] = m_new


def _flash_init(m_sc, l_sc, acc_sc):
    m_sc[...] = jnp.full_like(m_sc, NEG_INF)
    l_sc[...] = jnp.zeros_like(l_sc)
    acc_sc[...] = jnp.zeros_like(acc_sc)


def _flash_finish(o_ref, l_sc, acc_sc):
    o_ref[...] = (acc_sc[...] / _lane_tile(l_sc[...], acc_sc.shape[1] // LANES)).astype(o_ref.dtype)


def _mla_prompt_kernel(b_tab, qi_tab, ki_tab, q_ref, k_ref, o_ref, m_sc, l_sc, acc_sc, *, tq, tk):
    step = pl.program_id(0)
    qi = qi_tab[step]
    ki = ki_tab[step]
    last = ki == (qi * tq) // tk

    @pl.when(ki == 0)
    def _():
        _flash_init(m_sc, l_sc, acc_sc)

    k = k_ref[...]
    s = lax.dot_general(q_ref[...], k, (((1,), (1,)), ((), ())), preferred_element_type=F32)
    rows = s.shape[0]

    @pl.when(last)
    def _():
        q_pos = qi * tq + _shr(lax.broadcasted_iota(jnp.int32, (rows, tk), 0), MLA_HEADS)
        k_pos = ki * tk + lax.broadcasted_iota(jnp.int32, (rows, tk), 1)
        _flash_update(jnp.where(k_pos <= q_pos, s, NEG_INF), k[:, 0:MLA_KV_LORA], m_sc, l_sc, acc_sc)
        _flash_finish(o_ref, l_sc, acc_sc)

    @pl.when(jnp.logical_not(last))
    def _():
        _flash_update(s, k[:, 0:MLA_KV_LORA], m_sc, l_sc, acc_sc)


def _mla_prompt_attn(q2d, kcat, n_seq, seq_len, tq=128, tk=512):
    rows = tq * MLA_HEADS
    nq, nk = seq_len // tq, seq_len // tk
    steps = [(b, qi, ki) for b in range(n_seq) for qi in range(nq) for ki in range((qi * tq) // tk + 1)]
    b_tab, qi_tab, ki_tab = (jnp.asarray(np.array(col, np.int32)) for col in zip(*steps))
    kern = functools.partial(_mla_prompt_kernel, tq=tq, tk=tk)
    grid_spec = pltpu.PrefetchScalarGridSpec(
        num_scalar_prefetch=3,
        grid=(len(steps),),
        in_specs=[pl.BlockSpec((rows, MLA_QK), lambda s, bt, qt, kt: (bt[s] * nq + qt[s], 0)),
                  pl.BlockSpec((tk, MLA_QK), lambda s, bt, qt, kt: (bt[s] * nk + kt[s], 0))],
        out_specs=pl.BlockSpec((rows, MLA_KV_LORA), lambda s, bt, qt, kt: (bt[s] * nq + qt[s], 0)),
        scratch_shapes=[pltpu.VMEM((rows, LANES), F32), pltpu.VMEM((rows, LANES), F32),
                        pltpu.VMEM((rows, MLA_KV_LORA), F32)])
    return pl.pallas_call(
        kern,
        grid_spec=grid_spec,
        out_shape=jax.ShapeDtypeStruct((q2d.shape[0], MLA_KV_LORA), BF16),
        compiler_params=_params(1),
        name="mla_prompt_attn",
    )(b_tab, qi_tab, ki_tab, q2d, kcat)


def _mla_sample_kernel(pt_ref, q_ref, knew_ref, *refs, n_groups, seq_len):
    kv_refs = refs[:PAGES_PER_STEP]
    kr_refs = refs[PAGES_PER_STEP:2 * PAGES_PER_STEP]
    o_ref, m_sc, l_sc, acc_sc = refs[2 * PAGES_PER_STEP:]
    g = pl.program_id(1)
    r = MLA_KV_LORA
    nt = (((1,), (1,)), ((), ()))

    @pl.when(g == 0)
    def _():
        _flash_init(m_sc, l_sc, acc_sc)

    q = q_ref[...]
    q_lat = q[:, 0:r]
    q_rope = q[:, r:r + MLA_ROPE]
    vs = [kv_refs[i][...].astype(BF16) for i in range(PAGES_PER_STEP)]
    ss = [lax.dot_general(q_lat, vs[i], nt, preferred_element_type=F32)
          + lax.dot_general(q_rope, kr_refs[i][...].astype(BF16), nt, preferred_element_type=F32)
          for i in range(PAGES_PER_STEP)]
    m_prev = m_sc[...]
    m_cur = functools.reduce(jnp.maximum, ss)
    m_new = jnp.maximum(m_prev, jnp.max(m_cur, axis=1, keepdims=True))
    alpha = jnp.exp(m_prev - m_new)
    ps = [jnp.exp(s - m_new) for s in ss]
    l_sc[...] = alpha * l_sc[...] + jnp.sum(functools.reduce(jnp.add, ps), axis=1, keepdims=True)
    pv = functools.reduce(jnp.add, [jnp.dot(p.astype(BF16), v, preferred_element_type=F32)
                                    for p, v in zip(ps, vs)])
    acc_sc[...] = acc_sc[...] * _lane_tile(alpha, r // LANES) + pv
    m_sc[...] = m_new

    @pl.when(g == n_groups - 1)
    def _():
        knew = knew_ref[...]
        n_new = knew.shape[0]
        s_new = lax.dot_general(q, knew, nt, preferred_element_type=F32)
        rows = s_new.shape[0]
        q_t = _shr(lax.broadcasted_iota(jnp.int32, (rows, n_new), 0), MLA_HEADS)
        k_t = lax.broadcasted_iota(jnp.int32, (rows, n_new), 1)
        s_new = jnp.where((k_t <= q_t) & (k_t < seq_len), s_new, NEG_INF)
        m_old = m_sc[...]
        m_fin = jnp.maximum(m_old, jnp.max(s_new, axis=1, keepdims=True))
        a_fin = jnp.exp(m_old - m_fin)
        p_new = jnp.exp(s_new - m_fin[:, 0:n_new])
        l_sc[...] = a_fin * l_sc[...] + jnp.sum(p_new, axis=1, keepdims=True)
        acc_sc[...] = acc_sc[...] * _lane_tile(a_fin, r // LANES) + jnp.dot(
            p_new.astype(BF16), knew[:, 0:r], preferred_element_type=F32)
        _flash_finish(o_ref, l_sc, acc_sc)


def _mla_sample_attn(q2d, knew, cache_kv, cache_kr, page_table, layer_j, seq_len):
    b, n_pages = page_table.shape
    n_groups = n_pages // PAGES_PER_STEP
    rows = seq_len * MLA_HEADS
    pt_flat = page_table.reshape(-1)

    def page_spec(i, width):
        return pl.BlockSpec(
            (None, None, PAGE_SIZE, width),
            lambda bi, gi, pt: (layer_j, pt[bi * n_pages + gi * PAGES_PER_STEP + i], 0, 0))

    kern = functools.partial(_mla_sample_kernel, n_groups=n_groups, seq_len=seq_len)
    grid_spec = pltpu.PrefetchScalarGridSpec(
        num_scalar_prefetch=1,
        grid=(b, n_groups),
        in_specs=([pl.BlockSpec((rows, MLA_QK), lambda bi, gi, pt: (bi, 0)),
                   pl.BlockSpec((None, knew.shape[1], MLA_QK), lambda bi, gi, pt: (bi, 0, 0))]
                  + [page_spec(i, MLA_KV_LORA) for i in range(PAGES_PER_STEP)]
                  + [page_spec(i, MLA_ROPE) for i in range(PAGES_PER_STEP)]),
        out_specs=pl.BlockSpec((rows, MLA_KV_LORA), lambda bi, gi, pt: (bi, 0)),
        scratch_shapes=[pltpu.VMEM((rows, LANES), F32), pltpu.VMEM((rows, LANES), F32),
                        pltpu.VMEM((rows, MLA_KV_LORA), F32)])
    return pl.pallas_call(
        kern,
        grid_spec=grid_spec,
        out_shape=jax.ShapeDtypeStruct((q2d.shape[0], MLA_KV_LORA), BF16),
        compiler_params=_params(2),
        name="mla_sample_attn",
    )(pt_flat, q2d, knew, *([cache_kv] * PAGES_PER_STEP), *([cache_kr] * PAGES_PER_STEP))


def _mla_out_kernel(ol_ref, wuv_ref, wo_ref, x_ref, gt_ref, gpost_ref, o_ref, o_sc, *, grp):
    r = MLA_KV_LORA
    for h in range(MLA_HEADS):
        o_h = jnp.dot(ol_ref[:, h * r:(h + 1) * r], wuv_ref[h], preferred_element_type=F32)
        o_sc[:, h * MLA_V:(h + 1) * MLA_V] = o_h.astype(BF16)
    out = jnp.dot(o_sc[...], wo_ref[...], preferred_element_type=F32)
    o_ref[...] = x_ref[...] + grp.read(gt_ref, grp.row()) * _rms(out, gpost_ref[...])


def _mla_out(o_lat, w_uv, w_o, x, grp, layer, g_post, tm=256):
    t, d = x.shape
    sub = _Group(grp.mod, tm, None if grp.tiles_per_mod_row is None else grp.tiles_per_mod_row * grp.tm // tm)
    kern = functools.partial(_mla_out_kernel, grp=sub)
    return pl.pallas_call(
        kern,
        grid=(t // tm,),
        in_specs=[pl.BlockSpec((tm, o_lat.shape[1]), lambda i: (i, 0)),
                  pl.BlockSpec(w_uv.shape, lambda i: (0, 0, 0)),
                  pl.BlockSpec(w_o.shape, lambda i: (0, 0)),
                  pl.BlockSpec((tm, d), lambda i: (i, 0)),
                  sub.mod_spec(layer, 3 + 2),
                  pl.BlockSpec((1, d), lambda i: (0, 0))],
        out_specs=pl.BlockSpec((tm, d), lambda i: (i, 0)),
        out_shape=jax.ShapeDtypeStruct((t, d), F32),
        scratch_shapes=[pltpu.VMEM((tm, MLA_HEADS * MLA_V), BF16)],
        compiler_params=_params(1),
        name="mla_out",
    )(o_lat, w_uv, w_o, x, sub.mod, g_post.reshape(1, d))


def _t5_buckets(delta):
    n = np.maximum(delta, 0)
    max_exact = N_BUCKETS // 2
    log_ratio = np.log(np.maximum(n, 1).astype(np.float64) / max_exact) / math.log(BUCKET_MAX_DIST / max_exact)
    large = np.minimum(max_exact + (log_ratio * (N_BUCKETS - max_exact)).astype(np.int64), N_BUCKETS - 1)
    return np.where(n < max_exact, n, large).astype(np.int32)


def _swa_bias_table(rel_bias, delta, valid):
    lq, lk = delta.shape
    bias = rel_bias.astype(F32).T[:, _t5_buckets(delta)]
    bias = jnp.where(jnp.asarray(valid)[None], bias, NEG_INF)
    return bias.reshape(SWA_KV_HEADS, SWA_GROUP * lq, lk)


def _swa_q_perm():
    perm = np.zeros(SWA_HEADS * SWA_HEAD_DIM, np.int32)
    for m in range(SWA_KV_HEADS // 2):
        for g in range(SWA_GROUP):
            for p in range(2):
                src = ((2 * m + p) * SWA_GROUP + g) * SWA_HEAD_DIM
                dst = (4 * m + g) * LANES + p * SWA_HEAD_DIM
                perm[dst:dst + SWA_HEAD_DIM] = np.arange(src, src + SWA_HEAD_DIM)
    return perm


def _rope_tables(pos):
    half = MLA_ROPE // 2
    inv = ROPE_THETA ** (-jnp.arange(half, dtype=F32) / half)
    ang = pos.astype(F32)[:, None] * inv[None, :]
    cos, sin = jnp.cos(ang), jnp.sin(ang)
    zeros = jnp.zeros((pos.shape[0], LANES - MLA_ROPE), F32)
    return (jnp.concatenate([cos, cos, zeros], axis=1), jnp.concatenate([-sin, sin, zeros], axis=1))


def _pad_rope_cols(w):
    half = MLA_ROPE // 2
    z = jnp.zeros((w.shape[0], LANES - MLA_ROPE), w.dtype)
    return (jnp.concatenate([w, z], axis=1),
            jnp.concatenate([w[:, half:], w[:, :half], z], axis=1))


def kernel(x_prompt, x_sample, state_swa_k, state_swa_v, state_conv, cache_mla_kv, cache_mla_kr, page_table,
           c_prompt, c_sample, ada_w, ada_b, norm_pre, norm_post, ffn_w_in, ffn_w_out,
           gm_w_in, gm_ln_g, gm_ln_b, gm_w_s, gm_b_s, gm_w_out,
           swa_w_qkv, swa_w_o, swa_sinks, rel_bias,
           sc_w_in, sc_conv, sc_w_out,
           mla_w_qa, mla_qa_norm, mla_w_qb, mla_w_kva, mla_kva_norm, mla_w_kvb, mla_w_o):
    n_seq, seq_len, d = x_prompt.shape
    n_dec, dec_len, _ = x_sample.shape
    depth = ada_w.shape[0]
    past_len = page_table.shape[1] * PAGE_SIZE
    t_p, t_s = n_seq * seq_len, n_dec * dec_len

    c_all = jnp.concatenate([jnp.repeat(c_sample, dec_len, axis=0), c_prompt,
                             jnp.zeros((8 - n_seq, d), F32)], axis=0)
    mod = _ada(c_all, ada_w, ada_b)
    tm_p, tm_s = 512, 256
    grp_p = _Group(mod[:, t_s:t_s + 8], tm_p, seq_len // tm_p)
    grp_s = _Group(mod, tm_s, None)
    groups = (grp_p, grp_s)

    xs = [x_prompt.reshape(t_p, d), x_sample.reshape(t_s, d)]
    outs = {}

    for i in range(depth):
        kind, j = i % 4, i // 4
        w_in0, w_out0 = ffn_w_in[i, 0].astype(BF16), ffn_w_out[i, 0].astype(BF16)
        xs = [_half_ffn(x, g, i, 0, norm_pre[i, 0], norm_post[i, 0], w_in0, w_out0) for x, g in zip(xs, groups)]

        if kind == 0:
            w_in = gm_w_in[j].astype(BF16)
            w_out = gm_w_out[j].astype(BF16)
            new = []
            for x, g, chunk, sl, cps in ((xs[0], grp_p, GM_CHUNK, GM_CHUNK, seq_len // GM_CHUNK),
                                         (xs[1], grp_s, GM_CHUNK, dec_len, 1)):
                lc = min(sl, GM_CHUNK)
                m = jnp.tile(gm_w_s[j][:, :lc, :lc], (1, chunk // lc, chunk // lc))
                bias = jnp.broadcast_to(jnp.tile(gm_b_s[j][:, :lc], (1, chunk // lc))[:, :, None],
                                        (GM_GROUPS, chunk, LANES))
                uv = _norm_mod_matmul(x, g, i, norm_pre[i, 1], w_in, act="gelu")
                mixed, st = _gm_core(uv, gm_ln_g[j], gm_ln_b[j], m, bias, chunk, lc, cps)
                new.append(_matmul_residual(mixed, w_out, x, g, i, norm_post[i, 1]))
                outs.setdefault("gm", []).append(st)
            xs = new

        elif kind == 1:
            perm = _swa_q_perm()
            nq = SWA_HEADS * SWA_HEAD_DIM
            nkv = SWA_KV_HEADS * SWA_HEAD_DIM
            w_qkv = jnp.concatenate([swa_w_qkv[j][:, :nq][:, perm], swa_w_qkv[j][:, nq:]], axis=1).astype(BF16)
            w_o = swa_w_o[j][perm, :].astype(BF16)
            w = SWA_WINDOW
            i_q, i_k = np.arange(w), np.arange(2 * w)
            delta = w + i_q[:, None] - i_k[None, :]
            bias_p = _swa_bias_table(rel_bias, delta, (delta >= 0) & (delta < w))
            sink_p = jnp.repeat(swa_sinks[j].astype(F32).reshape(SWA_KV_HEADS, SWA_GROUP), w, axis=1)[:, :, None]
            qkv_p = _norm_mod_matmul(xs[0], grp_p, i, norm_pre[i, 1], w_qkv)
            o_p = _swa_prompt_core(qkv_p, bias_p, sink_p, seq_len)
            x_p = _matmul_residual(o_p, w_o, xs[0], grp_p, i, norm_post[i, 1])
            kv_p = qkv_p.reshape(n_seq, seq_len, -1)[:, seq_len - w:, nq:]
            outs.setdefault("swa_kp", []).append(kv_p[..., :nkv].reshape(n_seq, w, SWA_KV_HEADS, SWA_HEAD_DIM))
            outs.setdefault("swa_vp", []).append(kv_p[..., nkv:].reshape(n_seq, w, SWA_KV_HEADS, SWA_HEAD_DIM))
            lb = state_swa_k.shape[2]
            n_keys = lb + dec_len
            pad = (-n_keys) % 8
            i_q, i_k = np.arange(dec_len), np.arange(n_keys + pad)
            delta = lb + i_q[:, None] - i_k[None, :]
            valid = (delta >= 0) & (delta < w) & (i_k[None, :] < n_keys)
            bias_s = _swa_bias_table(rel_bias, delta, valid)
            sink_s = jnp.repeat(swa_sinks[j].astype(F32).reshape(SWA_KV_HEADS, SWA_GROUP), dec_len, axis=1)[:, :, None]
            qkv_s = _norm_mod_matmul(xs[1], grp_s, i, norm_pre[i, 1], w_qkv).reshape(n_dec, dec_len, -1)
            zpad = jnp.zeros((n_dec, pad, nkv), F32)
            kk = jnp.concatenate([state_swa_k[j].reshape(n_dec, lb, nkv), qkv_s[..., nq:nq + nkv], zpad], axis=1)
            vv = jnp.concatenate([state_swa_v[j].reshape(n_dec, lb, nkv), qkv_s[..., nq + nkv:], zpad], axis=1)
            q_s = qkv_s[..., :nq].reshape(n_dec, dec_len, SWA_KV_HEADS // 2, SWA_GROUP, LANES)
            q_s = q_s.transpose(0, 2, 3, 1, 4).reshape(n_dec, SWA_KV_HEADS // 2, SWA_GROUP * dec_len, LANES)
            o_s = _swa_sample_core(q_s, kk, vv, bias_s, sink_s)
            o_s = o_s.reshape(n_dec, SWA_KV_HEADS // 2, SWA_GROUP, dec_len, LANES).transpose(0, 3, 1, 2, 4)
            x_s = _matmul_residual(o_s.reshape(t_s, nq), w_o, xs[1], grp_s, i, norm_post[i, 1])
            outs.setdefault("swa_ks", []).append(kk[:, n_keys - lb:n_keys].reshape(n_dec, lb, SWA_KV_HEADS, SWA_HEAD_DIM))
            outs.setdefault("swa_vs", []).append(vv[:, n_keys - lb:n_keys].reshape(n_dec, lb, SWA_KV_HEADS, SWA_HEAD_DIM))
            xs = [x_p, x_s]

        elif kind == 2:
            w_in = sc_w_in[j].astype(BF16)
            w_out = sc_w_out[j].astype(BF16)
            c = sc_w_out.shape[1]
            g3_p = _norm_mod_matmul(xs[0], grp_p, i, norm_pre[i, 1], w_in)
            y_p, st_p = _conv_prompt_core(g3_p, sc_conv[j], seq_len)
            x_p = _matmul_residual(y_p, w_out, xs[0], grp_p, i, norm_post[i, 1])
            outs.setdefault("conv_p", []).append(st_p[:, 8 - (CONV_WIDTH - 1):])
            g3_s = _norm_mod_matmul(xs[1], grp_s, i, norm_pre[i, 1], w_in)
            y_s, st_s = _conv_sample_core(g3_s.reshape(n_dec, dec_len * 3 * c),
                                          state_conv[j].reshape(n_dec, (CONV_WIDTH - 1) * c), sc_conv[j], dec_len)
            x_s = _matmul_residual(y_s.reshape(t_s, c), w_out, xs[1], grp_s, i, norm_post[i, 1])
            outs.setdefault("conv_s", []).append(st_s.reshape(n_dec, CONV_WIDTH - 1, c))
            xs = [x_p, x_s]

        else:
            r = MLA_KV_LORA
            kr_pad, kr_sw = _pad_rope_cols(mla_w_kva[j][:, r:])
            w_p = jnp.concatenate([mla_w_qa[j], mla_w_kva[j][:, :r], kr_pad, kr_sw], axis=1).astype(BF16)
            w_qb = mla_w_qb[j].reshape(MLA_Q_LORA, MLA_HEADS, MLA_NOPE + MLA_ROPE)
            w_nope = w_qb[:, :, :MLA_NOPE].reshape(MLA_Q_LORA, -1).astype(BF16)
            rope_pairs = [_pad_rope_cols(w_qb[:, h, MLA_NOPE:]) for h in range(MLA_HEADS)]
            w_rope = jnp.concatenate([p[0] for p in rope_pairs], axis=1).astype(BF16)
            w_rope_sw = jnp.concatenate([p[1] for p in rope_pairs], axis=1).astype(BF16)
            w_kvb = mla_w_kvb[j].reshape(r, MLA_HEADS, MLA_NOPE + MLA_V)
            w_uk_t = w_kvb[:, :, :MLA_NOPE].transpose(1, 2, 0).astype(BF16)
            w_uv = w_kvb[:, :, MLA_NOPE:].transpose(1, 0, 2).astype(BF16)
            w_o = mla_w_o[j].astype(BF16)
            tm = 256
            cos_p, sin_p = _rope_tables(jnp.arange(seq_len))
            proj_p = _norm_mod_matmul(xs[0], grp_p, i, norm_pre[i, 1], w_p, tn=w_p.shape[1])
            q_p, ckv_p, kr_p, kcat_p = _mla_proj(proj_p, cos_p, sin_p, seq_len // tm, mla_qa_norm[j],
                                                 mla_kva_norm[j], w_nope, w_rope, w_rope_sw, w_uk_t, tm=tm)
            ol_p = _mla_prompt_attn(q_p.reshape(t_p * MLA_HEADS, MLA_QK), kcat_p, n_seq, seq_len)
            x_p = _mla_out(ol_p.reshape(t_p, MLA_HEADS * r), w_uv, w_o, xs[0], grp_p, i, norm_post[i, 1])
            outs.setdefault("mla_kvp", []).append(ckv_p.reshape(n_seq, seq_len, r))
            outs.setdefault("mla_krp", []).append(kr_p[:, :MLA_ROPE].reshape(n_seq, seq_len, MLA_ROPE))
            cos_s, sin_s = _rope_tables(past_len + jnp.arange(dec_len))
            cos_s, sin_s = jnp.tile(cos_s, (tm // dec_len, 1)), jnp.tile(sin_s, (tm // dec_len, 1))
            proj_s = _norm_mod_matmul(xs[1], grp_s, i, norm_pre[i, 1], w_p, tn=w_p.shape[1])
            q_s, ckv_s, kr_s, kcat_s = _mla_proj(proj_s, cos_s, sin_s, 1, mla_qa_norm[j],
                                                 mla_kva_norm[j], w_nope, w_rope, w_rope_sw, w_uk_t, tm=tm)
            knew = jnp.concatenate([kcat_s.reshape(n_dec, dec_len, MLA_QK),
                                    jnp.zeros((n_dec, 16 - dec_len, MLA_QK), BF16)], axis=1)
            ol_s = _mla_sample_attn(q_s.reshape(t_s * MLA_HEADS, MLA_QK), knew, cache_mla_kv, cache_mla_kr,
                                    page_table, j, dec_len)
            x_s = _mla_out(ol_s.reshape(t_s, MLA_HEADS * r), w_uv, w_o, xs[1], grp_s, i, norm_post[i, 1])
            outs.setdefault("mla_kvs", []).append(ckv_s.reshape(n_dec, dec_len, r))
            outs.setdefault("mla_krs", []).append(kr_s[:, :MLA_ROPE].reshape(n_dec, dec_len, MLA_ROPE))
            xs = [x_p, x_s]

        w_in1, w_out1 = ffn_w_in[i, 1].astype(BF16), ffn_w_out[i, 1].astype(BF16)
        xs = [_half_ffn(x, g, i, 2, norm_pre[i, 2], norm_post[i, 2], w_in1, w_out1) for x, g in zip(xs, groups)]

    gm_p, gm_s = outs["gm"][0::2], outs["gm"][1::2]
    return (xs[0].reshape(n_seq, seq_len, d), xs[1].reshape(n_dec, dec_len, d),
            jnp.stack(gm_p),
            jnp.stack([s.reshape(n_dec, dec_len, -1) for s in gm_s]),
            jnp.stack(outs["swa_kp"]), jnp.stack(outs["swa_vp"]),
            jnp.stack(outs["swa_ks"]), jnp.stack(outs["swa_vs"]),
            jnp.stack(outs["conv_p"]), jnp.stack(outs["conv_s"]),
            jnp.stack(outs["mla_kvp"]), jnp.stack(outs["mla_krp"]),
            jnp.stack(outs["mla_kvs"]), jnp.stack(outs["mla_krs"]))
```

```python
import functools
import math

import numpy as np
import jax
import jax.numpy as jnp
from jax import lax
from jax.experimental import pallas as pl
from jax.experimental.pallas import tpu as pltpu

F32 = jnp.float32
BF16 = jnp.bfloat16

VMEM_LIMIT_BYTES = 56 * 1024 * 1024
LANES = 128

RMS_EPS = 1e-6
NEG_INF = -1e30
FFN_RES_W = 0.5

D_MODEL = 2048
GM_GROUPS = 8
GM_CHUNK = 128
SWA_WINDOW = 128
SWA_HEAD_DIM = 64
SWA_HEADS = 32
SWA_KV_HEADS = 8
SWA_GROUP = 4
SWA_SCALE = SWA_HEAD_DIM ** -0.5
N_BUCKETS = 32
BUCKET_MAX_DIST = 128
CONV_WIDTH = 3
MLA_HEADS = 16
MLA_Q_LORA = 512
MLA_KV_LORA = 512
MLA_NOPE = 128
MLA_ROPE = 64
MLA_V = 128
MLA_SCALE = (MLA_NOPE + MLA_ROPE) ** -0.5
MLA_QK = MLA_KV_LORA + LANES
ROPE_THETA = 10000.0
PAGE_SIZE = 128
PAGES_PER_STEP = 32


def _params(n_axes):
    return pltpu.CompilerParams(dimension_semantics=("arbitrary",) * n_axes,
                                vmem_limit_bytes=VMEM_LIMIT_BYTES)


def _rms(x, g):
    return x * lax.rsqrt(jnp.mean(x * x, axis=-1, keepdims=True) + RMS_EPS) * g


def _lane_tile(x, n):
    return x if n == 1 else jnp.concatenate([x] * n, axis=1)


class _Group:
    def __init__(self, mod, tm, tiles_per_mod_row):
        self.mod = mod
        self.tm = tm
        self.tiles_per_mod_row = tiles_per_mod_row

    def mod_spec(self, layer, col):
        if self.tiles_per_mod_row is None:
            return pl.BlockSpec((None, self.tm, D_MODEL), lambda i, *_: (layer, i, col))
        return pl.BlockSpec((None, 8, D_MODEL), lambda i, *_: (layer, 0, col))

    def row(self):
        if self.tiles_per_mod_row is None:
            return None
        return pl.program_id(0) // self.tiles_per_mod_row

    @staticmethod
    def read(ref, row):
        return ref[...] if row is None else ref[pl.ds(row, 1), :]


def _shr(x, divisor):
    shift = divisor.bit_length() - 1
    assert 1 << shift == divisor
    return lax.shift_right_logical(x, shift)


def _ada_kernel(c_ref, w_ref, b_ref, o_ref, cs_sc):
    @pl.when((pl.program_id(0) == 0) & (pl.program_id(1) == 0))
    def _():
        c = c_ref[...]
        cs_sc[...] = (c * jax.nn.sigmoid(c)).astype(BF16)

    o_ref[...] = jnp.dot(cs_sc[...], w_ref[...].astype(BF16),
                         preferred_element_type=F32) + b_ref[...]


def _ada(c_all, ada_w, ada_b, tn=1024):
    n_layers, d, n = ada_w.shape
    rows = c_all.shape[0]
    return pl.pallas_call(
        _ada_kernel,
        grid=(n_layers, n // tn),
        in_specs=[pl.BlockSpec((rows, d), lambda l, j: (0, 0)),
                  pl.BlockSpec((None, d, tn), lambda l, j: (l, 0, j)),
                  pl.BlockSpec((None, 1, tn), lambda l, j: (l, 0, j))],
        out_specs=pl.BlockSpec((None, rows, tn), lambda l, j: (l, 0, j)),
        out_shape=jax.ShapeDtypeStruct((n_layers, rows, n), F32),
        scratch_shapes=[pltpu.VMEM((rows, d), BF16)],
        compiler_params=_params(2),
        name="ada_modulation",
    )(c_all, ada_w, ada_b.reshape(n_layers, 1, n))


def _ffn_kernel(x_ref, sh_ref, sc_ref, gt_ref, gpre_ref, gpost_ref, wg_ref, wu_ref, wo_ref,
                o_ref, h_sc, acc_sc, *, grp, n_f):
    j = pl.program_id(1)
    row = grp.row()

    @pl.when(j == 0)
    def _():
        h = _rms(x_ref[...], gpre_ref[...]) * (1.0 + grp.read(sc_ref, row)) + grp.read(sh_ref, row)
        h_sc[...] = h.astype(BF16)
        acc_sc[...] = jnp.zeros_like(acc_sc)

    h = h_sc[...]
    g = jnp.dot(h, wg_ref[...], preferred_element_type=F32)
    u = jnp.dot(h, wu_ref[...], preferred_element_type=F32)
    a = (g * jax.nn.sigmoid(g) * u).astype(BF16)
    acc_sc[...] += jnp.dot(a, wo_ref[...], preferred_element_type=F32)

    @pl.when(j == n_f - 1)
    def _():
        o_ref[...] = x_ref[...] + FFN_RES_W * grp.read(gt_ref, row) * _rms(acc_sc[...], gpost_ref[...])


def _half_ffn(x, grp, layer, which, g_pre, g_post, w_in, w_out, tf=512):
    t, d = x.shape
    f = w_out.shape[2]
    tm, n_f = grp.tm, f // tf
    sub = 2 * which
    kern = functools.partial(_ffn_kernel, grp=grp, n_f=n_f)
    return pl.pallas_call(
        kern,
        grid=(t // tm, n_f),
        in_specs=[pl.BlockSpec((tm, d), lambda i, j: (i, 0)),
                  grp.mod_spec(layer, sub * 3 + 0),
                  grp.mod_spec(layer, sub * 3 + 1),
                  grp.mod_spec(layer, sub * 3 + 2),
                  pl.BlockSpec((1, d), lambda i, j: (0, 0)),
                  pl.BlockSpec((1, d), lambda i, j: (0, 0)),
                  pl.BlockSpec((None, None, d, tf), lambda i, j: (layer, which, 0, j)),
                  pl.BlockSpec((None, None, d, tf), lambda i, j: (layer, which, 0, n_f + j)),
                  pl.BlockSpec((None, None, tf, d), lambda i, j: (layer, which, j, 0))],
        out_specs=pl.BlockSpec((tm, d), lambda i, j: (i, 0)),
        out_shape=jax.ShapeDtypeStruct((t, d), F32),
        scratch_shapes=[pltpu.VMEM((tm, d), BF16), pltpu.VMEM((tm, d), F32)],
        compiler_params=_params(2),
        name="half_ffn",
    )(x, grp.mod, grp.mod, grp.mod, g_pre.reshape(1, d), g_post.reshape(1, d), w_in, w_in, w_out)


def _nmm_kernel(x_ref, sh_ref, sc_ref, gpre_ref, w_ref, o_ref, h_sc, *, grp, act):
    row = grp.row()

    @pl.when(pl.program_id(1) == 0)
    def _():
        h = _rms(x_ref[...], gpre_ref[...]) * (1.0 + grp.read(sc_ref, row)) + grp.read(sh_ref, row)
        h_sc[...] = h.astype(BF16)

    y = jnp.dot(h_sc[...], w_ref[...], preferred_element_type=F32)
    if act == "gelu":
        y = jax.nn.gelu(y, approximate=True)
    o_ref[...] = y.astype(o_ref.dtype)


def _norm_mod_matmul(x, grp, layer, g_pre, w, act=None, tn=1024):
    t, d = x.shape
    n = w.shape[1]
    tn = min(tn, n)
    tm = grp.tm
    kern = functools.partial(_nmm_kernel, grp=grp, act=act)
    return pl.pallas_call(
        kern,
        grid=(t // tm, n // tn),
        in_specs=[pl.BlockSpec((tm, d), lambda i, j: (i, 0)),
                  grp.mod_spec(layer, 3 + 0),
                  grp.mod_spec(layer, 3 + 1),
                  pl.BlockSpec((1, d), lambda i, j: (0, 0)),
                  pl.BlockSpec((d, tn), lambda i, j: (0, j))],
        out_specs=pl.BlockSpec((tm, tn), lambda i, j: (i, j)),
        out_shape=jax.ShapeDtypeStruct((t, n), F32),
        scratch_shapes=[pltpu.VMEM((tm, d), BF16)],
        compiler_params=_params(2),
        name="norm_mod_matmul",
    )(x, grp.mod, grp.mod, g_pre.reshape(1, d), w)


def _mres_kernel(a_ref, w_ref, x_ref, gt_ref, gpost_ref, o_ref, acc_sc, *, grp, n_k):
    k = pl.program_id(1)
    row = grp.row()

    @pl.when(k == 0)
    def _():
        acc_sc[...] = jnp.zeros_like(acc_sc)

    acc_sc[...] += jnp.dot(a_ref[...].astype(BF16), w_ref[...], preferred_element_type=F32)

    @pl.when(k == n_k - 1)
    def _():
        o_ref[...] = x_ref[...] + grp.read(gt_ref, row) * _rms(acc_sc[...], gpost_ref[...])


def _matmul_residual(a, w, x, grp, layer, g_post, tk=1024):
    t, kdim = a.shape
    d = w.shape[1]
    tm, n_k = grp.tm, kdim // tk
    kern = functools.partial(_mres_kernel, grp=grp, n_k=n_k)
    return pl.pallas_call(
        kern,
        grid=(t // tm, n_k),
        in_specs=[pl.BlockSpec((tm, tk), lambda i, k: (i, k)),
                  pl.BlockSpec((tk, d), lambda i, k: (k, 0)),
                  pl.BlockSpec((tm, d), lambda i, k: (i, 0)),
                  grp.mod_spec(layer, 3 + 2),
                  pl.BlockSpec((1, d), lambda i, k: (0, 0))],
        out_specs=pl.BlockSpec((tm, d), lambda i, k: (i, 0)),
        out_shape=jax.ShapeDtypeStruct((t, d), F32),
        scratch_shapes=[pltpu.VMEM((tm, d), F32)],
        compiler_params=_params(2),
        name="matmul_residual",
    )(a, w, x, grp.mod, g_post.reshape(1, d))


def _gm_kernel(u_ref, v_ref, lng_ref, lnb_ref, m_ref, b_ref, o_ref, st_ref, *, seq_len):
    v = v_ref[...]
    vc = v - jnp.mean(v, axis=-1, keepdims=True)
    vn = vc * lax.rsqrt(jnp.mean(vc * vc, axis=-1, keepdims=True) + RMS_EPS) * lng_ref[...] + lnb_ref[...]
    st_ref[...] = vn
    vnb = vn.astype(BF16)
    c = v.shape[0]
    gw = v.shape[1] // GM_GROUPS
    row = lax.broadcasted_iota(jnp.int32, (c, c), 0)
    col = lax.broadcasted_iota(jnp.int32, (c, c), 1)
    keep = (col <= row) & (_shr(row, seq_len) == _shr(col, seq_len))
    for g in range(GM_GROUPS):
        mg = jnp.where(keep, m_ref[g], 0.0).astype(BF16)
        mixed = jnp.dot(mg, vnb[:, g * gw:(g + 1) * gw], preferred_element_type=F32)
        mixed = mixed + _lane_tile(b_ref[g], gw // LANES)
        o_ref[:, g * gw:(g + 1) * gw] = (u_ref[:, g * gw:(g + 1) * gw] * mixed).astype(o_ref.dtype)


def _gm_core(uv, ln_g, ln_b, m, bias, chunk, seq_len, chunks_per_state):
    t = uv.shape[0]
    w = uv.shape[1] // 2
    n_chunks = t // chunk
    n_states = n_chunks // chunks_per_state
    kern = functools.partial(_gm_kernel, seq_len=seq_len)
    return pl.pallas_call(
        kern,
        grid=(n_chunks,),
        in_specs=[pl.BlockSpec((chunk, w), lambda c: (c, 0)),
                  pl.BlockSpec((chunk, w), lambda c: (c, 1)),
                  pl.BlockSpec((1, w), lambda c: (0, 0)),
                  pl.BlockSpec((1, w), lambda c: (0, 0)),
                  pl.BlockSpec((GM_GROUPS, chunk, chunk), lambda c: (0, 0, 0)),
                  pl.BlockSpec((GM_GROUPS, chunk, LANES), lambda c: (0, 0, 0))],
        out_specs=[pl.BlockSpec((chunk, w), lambda c: (c, 0)),
                   pl.BlockSpec((None, chunk, w), lambda c: (c // chunks_per_state, 0, 0))],
        out_shape=[jax.ShapeDtypeStruct((t, w), BF16),
                   jax.ShapeDtypeStruct((n_states, chunk, w), F32)],
        compiler_params=_params(1),
        name="gm_core",
    )(uv, uv, ln_g.reshape(1, w), ln_b.reshape(1, w), m, bias)


def _softmax_sink(s, sink):
    mx = jnp.maximum(jnp.max(s, axis=-1, keepdims=True), sink)
    p = jnp.exp(s - mx)
    denom = jnp.sum(p, axis=-1, keepdims=True) + jnp.exp(sink - mx)
    return p / denom


def _half_lane_mask(rows, parity):
    lane = lax.broadcasted_iota(jnp.int32, (rows, LANES), 1)
    return (lane >= SWA_HEAD_DIM) if parity else (lane < SWA_HEAD_DIM)


def _swa_prompt_kernel(q_ref, kp_ref, ko_ref, vp_ref, vo_ref, bias_ref, sink_ref, o_ref, *, blocks_per_seq):
    w = SWA_WINDOW
    first = (pl.program_id(0) % blocks_per_seq) == 0
    q = (q_ref[...] * SWA_SCALE).astype(BF16)
    k = jnp.concatenate([kp_ref[...], ko_ref[...]], axis=0)
    v = jnp.concatenate([vp_ref[...], vo_ref[...]], axis=0)
    key_col = lax.broadcasted_iota(jnp.int32, (SWA_GROUP * w, 2 * w), 1)
    no_prev = key_col < jnp.where(first, w, 0)
    for m in range(SWA_KV_HEADS // 2):
        qs = jnp.concatenate([q[:, (4 * m + g) * LANES:(4 * m + g + 1) * LANES] for g in range(SWA_GROUP)],
                             axis=0)
        k2 = k[:, m * LANES:(m + 1) * LANES]
        v2 = v[:, m * LANES:(m + 1) * LANES]
        o_m = None
        for parity in range(2):
            keep = _half_lane_mask(2 * w, parity)
            kx = jnp.where(keep, k2, 0.0).astype(BF16)
            vx = jnp.where(keep, v2, 0.0).astype(BF16)
            s = lax.dot_general(qs, kx, (((1,), (1,)), ((), ())), preferred_element_type=F32)
            s = s + bias_ref[2 * m + parity]
            s = jnp.where(no_prev, NEG_INF, s)
            p = _softmax_sink(s, sink_ref[2 * m + parity])
            o_p = jnp.dot(p.astype(BF16), vx, preferred_element_type=F32)
            o_m = o_p if o_m is None else o_m + o_p
        for g in range(SWA_GROUP):
            o_ref[:, (4 * m + g) * LANES:(4 * m + g + 1) * LANES] = o_m[g * w:(g + 1) * w].astype(o_ref.dtype)


def _swa_prompt_core(qkv, bias, sink, seq_len):
    t = qkv.shape[0]
    w = SWA_WINDOW
    nq = SWA_HEADS * SWA_HEAD_DIM
    nkv = SWA_KV_HEADS * SWA_HEAD_DIM
    bps = seq_len // w
    kcol, vcol = nq // nkv, nq // nkv + 1

    def prev(i):
        return jnp.maximum(i - 1, 0)

    kern = functools.partial(_swa_prompt_kernel, blocks_per_seq=bps)
    return pl.pallas_call(
        kern,
        grid=(t // w,),
        in_specs=[pl.BlockSpec((w, nq), lambda i: (i, 0)),
                  pl.BlockSpec((w, nkv), lambda i: (prev(i), kcol)),
                  pl.BlockSpec((w, nkv), lambda i: (i, kcol)),
                  pl.BlockSpec((w, nkv), lambda i: (prev(i), vcol)),
                  pl.BlockSpec((w, nkv), lambda i: (i, vcol)),
                  pl.BlockSpec(bias.shape, lambda i: (0, 0, 0)),
                  pl.BlockSpec(sink.shape, lambda i: (0, 0, 0))],
        out_specs=pl.BlockSpec((w, nq), lambda i: (i, 0)),
        out_shape=jax.ShapeDtypeStruct((t, nq), BF16),
        compiler_params=_params(1),
        name="swa_prompt_core",
    )(qkv, qkv, qkv, qkv, qkv, bias, sink)


def _swa_sample_kernel(q_ref, k_ref, v_ref, bias_ref, sink_ref, o_ref, *, bb):
    n_keys = k_ref.shape[1]

    def body(b, carry):
        k = k_ref[b]
        v = v_ref[b]
        for m in range(SWA_KV_HEADS // 2):
            qs = (q_ref[b, m] * SWA_SCALE).astype(BF16)
            k2 = k[:, m * LANES:(m + 1) * LANES]
            v2 = v[:, m * LANES:(m + 1) * LANES]
            o_m = None
            for parity in range(2):
                keep = _half_lane_mask(n_keys, parity)
                kx = jnp.where(keep, k2, 0.0).astype(BF16)
                vx = jnp.where(keep, v2, 0.0).astype(BF16)
                s = lax.dot_general(qs, kx, (((1,), (1,)), ((), ())), preferred_element_type=F32)
                p = _softmax_sink(s + bias_ref[2 * m + parity], sink_ref[2 * m + parity])
                o_p = jnp.dot(p.astype(BF16), vx, preferred_element_type=F32)
                o_m = o_p if o_m is None else o_m + o_p
            o_ref[b, m] = o_m.astype(o_ref.dtype)
        return carry

    lax.fori_loop(0, bb, body, 0)


def _swa_sample_core(q, kk, vv, bias, sink, bb=8):
    b, n_pair, rows, _ = q.shape
    n_keys = kk.shape[1]
    nkv = kk.shape[2]
    kern = functools.partial(_swa_sample_kernel, bb=bb)
    return pl.pallas_call(
        kern,
        grid=(b // bb,),
        in_specs=[pl.BlockSpec((bb, n_pair, rows, LANES), lambda i: (i, 0, 0, 0)),
                  pl.BlockSpec((bb, n_keys, nkv), lambda i: (i, 0, 0)),
                  pl.BlockSpec((bb, n_keys, nkv), lambda i: (i, 0, 0)),
                  pl.BlockSpec(bias.shape, lambda i: (0, 0, 0)),
                  pl.BlockSpec(sink.shape, lambda i: (0, 0, 0))],
        out_specs=pl.BlockSpec((bb, n_pair, rows, LANES), lambda i: (i, 0, 0, 0)),
        out_shape=jax.ShapeDtypeStruct(q.shape, BF16),
        compiler_params=_params(1),
        name="swa_sample_core",
    )(q, kk, vv, bias, sink)


def _conv_prompt_kernel(gb_ref, gc_ref, z_ref, hc_ref, hz_ref, cw_ref, o_ref, st_ref, zp_sc, *, tiles_per_seq):
    tm = gb_ref.shape[0]
    first = (pl.program_id(0) % tiles_per_seq) == 0
    zz = gc_ref[...] * z_ref[...]
    halo = jnp.where(first, 0.0, hc_ref[...] * hz_ref[...])
    zp_sc[0:8, :] = halo
    zp_sc[8:8 + tm, :] = zz
    y = cw_ref[2:3, :] * zz + cw_ref[1:2, :] * zp_sc[7:7 + tm, :] + cw_ref[0:1, :] * zp_sc[6:6 + tm, :]
    o_ref[...] = (gb_ref[...] * y).astype(o_ref.dtype)
    st_ref[...] = zz[tm - 8:tm]


def _conv_prompt_core(g3, conv_w, seq_len, tm=256):
    t = g3.shape[0]
    c = g3.shape[1] // 3
    tps = seq_len // tm
    n_seq = t // seq_len

    def halo(col):
        return pl.BlockSpec((8, c), lambda i: (jnp.maximum(i * (tm // 8) - 1, 0), col))

    kern = functools.partial(_conv_prompt_kernel, tiles_per_seq=tps)
    return pl.pallas_call(
        kern,
        grid=(t // tm,),
        in_specs=[pl.BlockSpec((tm, c), lambda i: (i, 0)),
                  pl.BlockSpec((tm, c), lambda i: (i, 1)),
                  pl.BlockSpec((tm, c), lambda i: (i, 2)),
                  halo(1), halo(2),
                  pl.BlockSpec((CONV_WIDTH, c), lambda i: (0, 0))],
        out_specs=[pl.BlockSpec((tm, c), lambda i: (i, 0)),
                   pl.BlockSpec((None, 8, c), lambda i: (i // tps, 0, 0))],
        out_shape=[jax.ShapeDtypeStruct((t, c), BF16),
                   jax.ShapeDtypeStruct((n_seq, 8, c), F32)],
        scratch_shapes=[pltpu.VMEM((tm + 8, c), F32)],
        compiler_params=_params(1),
        name="conv_prompt_core",
    )(g3, g3, g3, g3, g3, conv_w)


def _conv_sample_kernel(g_ref, prev_ref, cw_ref, o_ref, st_ref, *, seq_len, c):
    zz = [prev_ref[:, 0:c], prev_ref[:, c:2 * c]]
    for t in range(seq_len):
        base = t * 3 * c
        zz.append(g_ref[:, base + c:base + 2 * c] * g_ref[:, base + 2 * c:base + 3 * c])
    for t in range(seq_len):
        y = cw_ref[2:3, :] * zz[t + 2] + cw_ref[1:2, :] * zz[t + 1] + cw_ref[0:1, :] * zz[t]
        o_ref[:, t * c:(t + 1) * c] = (g_ref[:, t * 3 * c:t * 3 * c + c] * y).astype(o_ref.dtype)
    st_ref[:, 0:c] = zz[seq_len]
    st_ref[:, c:2 * c] = zz[seq_len + 1]


def _conv_sample_core(g3, prev, conv_w, seq_len):
    b = g3.shape[0]
    c = g3.shape[1] // (3 * seq_len)
    kern = functools.partial(_conv_sample_kernel, seq_len=seq_len, c=c)
    return pl.pallas_call(
        kern,
        grid=(1,),
        in_specs=[pl.BlockSpec(g3.shape, lambda i: (0, 0)),
                  pl.BlockSpec(prev.shape, lambda i: (0, 0)),
                  pl.BlockSpec((CONV_WIDTH, c), lambda i: (0, 0))],
        out_specs=[pl.BlockSpec((b, seq_len * c), lambda i: (0, 0)),
                   pl.BlockSpec((b, 2 * c), lambda i: (0, 0))],
        out_shape=[jax.ShapeDtypeStruct((b, seq_len * c), BF16),
                   jax.ShapeDtypeStruct((b, 2 * c), F32)],
        compiler_params=_params(1),
        name="conv_sample_core",
    )(g3, prev, conv_w)


def _mla_proj_kernel(p_ref, cos_ref, sin_ref, qan_ref, kvn_ref, wn_ref, wr_ref, wrs_ref, wuk_ref,
                     q_ref, ckv_ref, kr_ref, kcat_ref, *, head_major):
    r = MLA_KV_LORA
    cos = cos_ref[...]
    sin = sin_ref[...]
    qa = _rms(p_ref[:, 0:MLA_Q_LORA], qan_ref[...]).astype(BF16)
    qn = jnp.dot(qa, wn_ref[...], preferred_element_type=F32)
    qr = jnp.dot(qa, wr_ref[...], preferred_element_type=F32)
    qrs = jnp.dot(qa, wrs_ref[...], preferred_element_type=F32)
    for h in range(MLA_HEADS):
        sl = slice(h * LANES, (h + 1) * LANES)
        q_lat = jnp.dot(qn[:, sl].astype(BF16), wuk_ref[h], preferred_element_type=F32)
        q_rope = qr[:, sl] * cos + qrs[:, sl] * sin
        q_lat = (q_lat * MLA_SCALE).astype(q_ref.dtype)
        q_rope = (q_rope * MLA_SCALE).astype(q_ref.dtype)
        if head_major:
            q_ref[h, :, 0:r] = q_lat
            q_ref[h, :, r:MLA_QK] = q_rope
        else:
            q_ref[:, h * MLA_QK:h * MLA_QK + r] = q_lat
            q_ref[:, h * MLA_QK + r:(h + 1) * MLA_QK] = q_rope
    off = MLA_Q_LORA
    ckv = _rms(p_ref[:, off:off + r], kvn_ref[...])
    kr = p_ref[:, off + r:off + r + LANES] * cos + p_ref[:, off + r + LANES:off + r + 2 * LANES] * sin
    ckv_ref[...] = ckv
    kr_ref[...] = kr
    kcat_ref[:, 0:r] = ckv.astype(kcat_ref.dtype)
    kcat_ref[:, r:r + LANES] = kr.astype(kcat_ref.dtype)


def _mla_proj(p, cos, sin, pos_blocks, qa_norm, kva_norm, w_nope, w_rope, w_rope_sw, w_uk_t, head_major, tm=256):
    t = p.shape[0]
    hq = MLA_HEADS * MLA_QK
    const2 = lambda i: (0, 0)
    if head_major:
        q_spec = pl.BlockSpec((MLA_HEADS, tm, MLA_QK), lambda i: (0, i, 0))
        q_shape = jax.ShapeDtypeStruct((MLA_HEADS, t, MLA_QK), BF16)
    else:
        q_spec = pl.BlockSpec((tm, hq), lambda i: (i, 0))
        q_shape = jax.ShapeDtypeStruct((t, hq), BF16)
    return pl.pallas_call(
        functools.partial(_mla_proj_kernel, head_major=head_major),
        grid=(t // tm,),
        in_specs=[pl.BlockSpec((tm, p.shape[1]), lambda i: (i, 0)),
                  pl.BlockSpec((tm, LANES), lambda i: (i % pos_blocks, 0)),
                  pl.BlockSpec((tm, LANES), lambda i: (i % pos_blocks, 0)),
                  pl.BlockSpec((1, MLA_Q_LORA), const2),
                  pl.BlockSpec((1, MLA_KV_LORA), const2),
                  pl.BlockSpec(w_nope.shape, const2),
                  pl.BlockSpec(w_rope.shape, const2),
                  pl.BlockSpec(w_rope_sw.shape, const2),
                  pl.BlockSpec(w_uk_t.shape, lambda i: (0, 0, 0))],
        out_specs=[q_spec,
                   pl.BlockSpec((tm, MLA_KV_LORA), lambda i: (i, 0)),
                   pl.BlockSpec((tm, LANES), lambda i: (i, 0)),
                   pl.BlockSpec((tm, MLA_QK), lambda i: (i, 0))],
        out_shape=[q_shape,
                   jax.ShapeDtypeStruct((t, MLA_KV_LORA), F32),
                   jax.ShapeDtypeStruct((t, LANES), F32),
                   jax.ShapeDtypeStruct((t, MLA_QK), BF16)],
        compiler_params=_params(1),
        name="mla_proj",
    )(p, cos, sin, qa_norm.reshape(1, -1), kva_norm.reshape(1, -1), w_nope, w_rope, w_rope_sw, w_uk_t)


def _flash_update(s, v, m_ref, l_ref, acc_ref):
    m_prev = m_ref[...]
    m_new = jnp.maximum(m_prev, jnp.max(s, axis=1, keepdims=True))
    alpha = jnp.exp(m_prev - m_new)
    p = jnp.exp(s - _lane_tile(m_new, s.shape[1] // LANES))
    l_ref[...] = alpha * l_ref[...] + jnp.sum(p, axis=1, keepdims=True)
    acc_ref[...] = acc_ref[...] * _lane_tile(alpha, acc_ref.shape[1] // LANES) + jnp.dot(
        p.astype(BF16), v, preferred_element_type=F32)
    m_ref[...] = m_new


def _flash_init(m_sc, l_sc, acc_sc):
    m_sc[...] = jnp.full_like(m_sc, NEG_INF)
    l_sc[...] = jnp.zeros_like(l_sc)
    acc_sc[...] = jnp.zeros_like(acc_sc)


def _flash_result(l_ref, acc_ref):
    return acc_ref[...] / _lane_tile(l_ref[...], acc_ref.shape[1] // LANES)


def _mla_prompt_kernel(b_tab, qi_tab, ki_tab, q_ref, k_ref, o_ref, m_sc, l_sc, acc_sc, *, tq, tk, hb):
    step = pl.program_id(0)
    qi = qi_tab[step]
    ki = ki_tab[step]
    last = ki == (qi * tq) // tk
    rows = hb * tq

    @pl.when(ki == 0)
    def _():
        _flash_init(m_sc, l_sc, acc_sc)

    def run(masked):
        k = k_ref[...]
        v = k[:, 0:MLA_KV_LORA]
        if masked:
            q_pos = qi * tq + (lax.broadcasted_iota(jnp.int32, (rows, tk), 0) & (tq - 1))
            k_pos = ki * tk + lax.broadcasted_iota(jnp.int32, (rows, tk), 1)
            visible = k_pos <= q_pos
        for g in range(MLA_HEADS // hb):
            sl = pl.ds(g * rows, rows)
            q = q_ref[g * hb:(g + 1) * hb].reshape(rows, MLA_QK)
            s = lax.dot_general(q, k, (((1,), (1,)), ((), ())), preferred_element_type=F32)
            if masked:
                s = jnp.where(visible, s, NEG_INF)
            _flash_update(s, v, m_sc.at[sl], l_sc.at[sl], acc_sc.at[sl])
            if masked:
                o = _flash_result(l_sc.at[sl], acc_sc.at[sl])
                o_ref[g * hb:(g + 1) * hb] = o.reshape(hb, tq, MLA_KV_LORA).astype(o_ref.dtype)

    pl.when(last)(lambda: run(True))
    pl.when(jnp.logical_not(last))(lambda: run(False))


def _mla_prompt_attn(q3, kcat, n_seq, seq_len, tq=256, tk=512, hb=2):
    assert tq & (tq - 1) == 0
    rows = tq * MLA_HEADS
    nq, nk = seq_len // tq, seq_len // tk
    steps = [(b, qi, ki) for b in range(n_seq) for qi in range(nq) for ki in range((qi * tq) // tk + 1)]
    b_tab, qi_tab, ki_tab = (jnp.asarray(np.array(col, np.int32)) for col in zip(*steps))
    kern = functools.partial(_mla_prompt_kernel, tq=tq, tk=tk, hb=hb)
    grid_spec = pltpu.PrefetchScalarGridSpec(
        num_scalar_prefetch=3,
        grid=(len(steps),),
        in_specs=[pl.BlockSpec((MLA_HEADS, tq, MLA_QK), lambda s, bt, qt, kt: (0, bt[s] * nq + qt[s], 0)),
                  pl.BlockSpec((tk, MLA_QK), lambda s, bt, qt, kt: (bt[s] * nk + kt[s], 0))],
        out_specs=pl.BlockSpec((MLA_HEADS, tq, MLA_KV_LORA), lambda s, bt, qt, kt: (0, bt[s] * nq + qt[s], 0)),
        scratch_shapes=[pltpu.VMEM((rows, LANES), F32), pltpu.VMEM((rows, LANES), F32),
                        pltpu.VMEM((rows, MLA_KV_LORA), F32)])
    return pl.pallas_call(
        kern,
        grid_spec=grid_spec,
        out_shape=jax.ShapeDtypeStruct((MLA_HEADS, q3.shape[1], MLA_KV_LORA), BF16),
        compiler_params=_params(1),
        name="mla_prompt_attn",
    )(b_tab, qi_tab, ki_tab, q3, kcat)


def _mla_sample_kernel(pt_ref, q_ref, knew_ref, *refs, n_groups, seq_len):
    kv_refs = refs[:PAGES_PER_STEP]
    krt_refs = refs[PAGES_PER_STEP:2 * PAGES_PER_STEP]
    o_ref, kv_sc, krt_sc, m_sc, l_sc, acc_sc = refs[2 * PAGES_PER_STEP:]
    g = pl.program_id(1)
    r = MLA_KV_LORA
    nt = (((1,), (1,)), ((), ()))

    @pl.when(g == 0)
    def _():
        _flash_init(m_sc, l_sc, acc_sc)

    for i in range(PAGES_PER_STEP):
        kv_sc[i * PAGE_SIZE:(i + 1) * PAGE_SIZE, :] = kv_refs[i][...].astype(BF16)
        krt_sc[:, i * PAGE_SIZE:(i + 1) * PAGE_SIZE] = krt_refs[i][...].astype(BF16)
    q = q_ref[...]
    kv = kv_sc[...]
    s = (lax.dot_general(q[:, 0:r], kv, nt, preferred_element_type=F32)
         + jnp.dot(q[:, r:r + MLA_ROPE], krt_sc[...], preferred_element_type=F32))
    _flash_update(s, kv, m_sc, l_sc, acc_sc)

    @pl.when(g == n_groups - 1)
    def _():
        knew = knew_ref[...]
        n_new = knew.shape[0]
        s_new = lax.dot_general(q, knew, nt, preferred_element_type=F32)
        rows = s_new.shape[0]
        q_t = _shr(lax.broadcasted_iota(jnp.int32, (rows, n_new), 0), MLA_HEADS)
        k_t = lax.broadcasted_iota(jnp.int32, (rows, n_new), 1)
        s_new = jnp.where((k_t <= q_t) & (k_t < seq_len), s_new, NEG_INF)
        m_old = m_sc[...]
        m_fin = jnp.maximum(m_old, jnp.max(s_new, axis=1, keepdims=True))
        a_fin = jnp.exp(m_old - m_fin)
        p_new = jnp.exp(s_new - m_fin[:, 0:n_new])
        l_sc[...] = a_fin * l_sc[...] + jnp.sum(p_new, axis=1, keepdims=True)
        acc_sc[...] = acc_sc[...] * _lane_tile(a_fin, r // LANES) + jnp.dot(
            p_new.astype(BF16), knew[:, 0:r], preferred_element_type=F32)
        o_ref[...] = _flash_result(l_sc, acc_sc).astype(o_ref.dtype)


def _mla_sample_attn(q2d, knew, cache_kv, cache_krt, page_table, layer_j, seq_len):
    b, n_pages = page_table.shape
    n_groups = n_pages // PAGES_PER_STEP
    rows = seq_len * MLA_HEADS
    keys = PAGES_PER_STEP * PAGE_SIZE
    pt_flat = page_table.reshape(-1)

    def page_spec(i, shape):
        return pl.BlockSpec(
            (None, None) + shape,
            lambda bi, gi, pt: (layer_j, pt[bi * n_pages + gi * PAGES_PER_STEP + i], 0, 0))

    kern = functools.partial(_mla_sample_kernel, n_groups=n_groups, seq_len=seq_len)
    grid_spec = pltpu.PrefetchScalarGridSpec(
        num_scalar_prefetch=1,
        grid=(b, n_groups),
        in_specs=([pl.BlockSpec((rows, MLA_QK), lambda bi, gi, pt: (bi, 0)),
                   pl.BlockSpec((None, knew.shape[1], MLA_QK), lambda bi, gi, pt: (bi, 0, 0))]
                  + [page_spec(i, (PAGE_SIZE, MLA_KV_LORA)) for i in range(PAGES_PER_STEP)]
                  + [page_spec(i, (MLA_ROPE, PAGE_SIZE)) for i in range(PAGES_PER_STEP)]),
        out_specs=pl.BlockSpec((rows, MLA_KV_LORA), lambda bi, gi, pt: (bi, 0)),
        scratch_shapes=[pltpu.VMEM((keys, MLA_KV_LORA), BF16), pltpu.VMEM((MLA_ROPE, keys), BF16),
                        pltpu.VMEM((rows, LANES), F32), pltpu.VMEM((rows, LANES), F32),
                        pltpu.VMEM((rows, MLA_KV_LORA), F32)])
    return pl.pallas_call(
        kern,
        grid_spec=grid_spec,
        out_shape=jax.ShapeDtypeStruct((q2d.shape[0], MLA_KV_LORA), BF16),
        compiler_params=_params(2),
        name="mla_sample_attn",
    )(pt_flat, q2d, knew, *([cache_kv] * PAGES_PER_STEP), *([cache_krt] * PAGES_PER_STEP))


def _mla_out_kernel(ol_ref, wuv_ref, wo_ref, x_ref, gt_ref, gpost_ref, o_ref, o_sc, *, grp, head_major):
    r = MLA_KV_LORA
    for h in range(MLA_HEADS):
        o_lat = ol_ref[h] if head_major else ol_ref[:, h * r:(h + 1) * r]
        o_h = jnp.dot(o_lat, wuv_ref[h], preferred_element_type=F32)
        o_sc[:, h * MLA_V:(h + 1) * MLA_V] = o_h.astype(BF16)
    out = jnp.dot(o_sc[...], wo_ref[...], preferred_element_type=F32)
    o_ref[...] = x_ref[...] + grp.read(gt_ref, grp.row()) * _rms(out, gpost_ref[...])


def _mla_out(o_lat, w_uv, w_o, x, grp, layer, g_post, tm=256):
    t, d = x.shape
    head_major = o_lat.ndim == 3
    sub = _Group(grp.mod, tm, None if grp.tiles_per_mod_row is None else grp.tiles_per_mod_row * grp.tm // tm)
    kern = functools.partial(_mla_out_kernel, grp=sub, head_major=head_major)
    if head_major:
        ol_spec = pl.BlockSpec((MLA_HEADS, tm, MLA_KV_LORA), lambda i: (0, i, 0))
    else:
        ol_spec = pl.BlockSpec((tm, o_lat.shape[1]), lambda i: (i, 0))
    return pl.pallas_call(
        kern,
        grid=(t // tm,),
        in_specs=[ol_spec,
                  pl.BlockSpec(w_uv.shape, lambda i: (0, 0, 0)),
                  pl.BlockSpec(w_o.shape, lambda i: (0, 0)),
                  pl.BlockSpec((tm, d), lambda i: (i, 0)),
                  sub.mod_spec(layer, 3 + 2),
                  pl.BlockSpec((1, d), lambda i: (0, 0))],
        out_specs=pl.BlockSpec((tm, d), lambda i: (i, 0)),
        out_shape=jax.ShapeDtypeStruct((t, d), F32),
        scratch_shapes=[pltpu.VMEM((tm, MLA_HEADS * MLA_V), BF16)],
        compiler_params=_params(1),
        name="mla_out",
    )(o_lat, w_uv, w_o, x, sub.mod, g_post.reshape(1, d))


def _t5_buckets(delta):
    n = np.maximum(delta, 0)
    max_exact = N_BUCKETS // 2
    log_ratio = np.log(np.maximum(n, 1).astype(np.float64) / max_exact) / math.log(BUCKET_MAX_DIST / max_exact)
    large = np.minimum(max_exact + (log_ratio * (N_BUCKETS - max_exact)).astype(np.int64), N_BUCKETS - 1)
    return np.where(n < max_exact, n, large).astype(np.int32)


def _swa_bias_table(rel_bias, delta, valid):
    lq, lk = delta.shape
    one_hot = (jnp.asarray(_t5_buckets(delta))[None] == jnp.arange(N_BUCKETS)[:, None, None]).astype(F32)
    bias = jnp.einsum("nh,nqk->hqk", rel_bias.astype(F32), one_hot, precision=lax.Precision.HIGHEST)
    bias = jnp.where(jnp.asarray(valid)[None], bias, NEG_INF)
    return bias.reshape(SWA_KV_HEADS, SWA_GROUP * lq, lk)


def _swa_q_perm():
    perm = np.zeros(SWA_HEADS * SWA_HEAD_DIM, np.int32)
    for m in range(SWA_KV_HEADS // 2):
        for g in range(SWA_GROUP):
            for p in range(2):
                src = ((2 * m + p) * SWA_GROUP + g) * SWA_HEAD_DIM
                dst = (4 * m + g) * LANES + p * SWA_HEAD_DIM
                perm[dst:dst + SWA_HEAD_DIM] = np.arange(src, src + SWA_HEAD_DIM)
    return perm


def _rope_tables(pos):
    half = MLA_ROPE // 2
    inv = ROPE_THETA ** (-jnp.arange(half, dtype=F32) / half)
    ang = pos.astype(F32)[:, None] * inv[None, :]
    cos, sin = jnp.cos(ang), jnp.sin(ang)
    zeros = jnp.zeros((pos.shape[0], LANES - MLA_ROPE), F32)
    return (jnp.concatenate([cos, cos, zeros], axis=1), jnp.concatenate([-sin, sin, zeros], axis=1))


def _pad_rope_cols(w):
    half = MLA_ROPE // 2
    z = jnp.zeros((w.shape[0], LANES - MLA_ROPE), w.dtype)
    return (jnp.concatenate([w, z], axis=1),
            jnp.concatenate([w[:, half:], w[:, :half], z], axis=1))


def kernel(x_prompt, x_sample, state_swa_k, state_swa_v, state_conv, cache_mla_kv, cache_mla_kr, page_table,
           c_prompt, c_sample, ada_w, ada_b, norm_pre, norm_post, ffn_w_in, ffn_w_out,
           gm_w_in, gm_ln_g, gm_ln_b, gm_w_s, gm_b_s, gm_w_out,
           swa_w_qkv, swa_w_o, swa_sinks, rel_bias,
           sc_w_in, sc_conv, sc_w_out,
           mla_w_qa, mla_qa_norm, mla_w_qb, mla_w_kva, mla_kva_norm, mla_w_kvb, mla_w_o):
    n_seq, seq_len, d = x_prompt.shape
    n_dec, dec_len, _ = x_sample.shape
    depth = ada_w.shape[0]
    past_len = page_table.shape[1] * PAGE_SIZE
    t_p, t_s = n_seq * seq_len, n_dec * dec_len

    c_all = jnp.concatenate([jnp.repeat(c_sample, dec_len, axis=0), c_prompt,
                             jnp.zeros((8 - n_seq, d), F32)], axis=0)
    mod = _ada(c_all, ada_w, ada_b)
    tm_p, tm_s = 512, 256
    grp_p = _Group(mod[:, t_s:t_s + 8], tm_p, seq_len // tm_p)
    grp_s = _Group(mod, tm_s, None)
    groups = (grp_p, grp_s)

    xs = [x_prompt.reshape(t_p, d), x_sample.reshape(t_s, d)]
    outs = {}
    ffn_w_in_b, ffn_w_out_b = ffn_w_in.astype(BF16), ffn_w_out.astype(BF16)

    for i in range(depth):
        kind, j = i % 4, i // 4
        xs = [_half_ffn(x, g, i, 0, norm_pre[i, 0], norm_post[i, 0], ffn_w_in_b, ffn_w_out_b)
              for x, g in zip(xs, groups)]

        if kind == 0:
            w_in = gm_w_in[j].astype(BF16)
            w_out = gm_w_out[j].astype(BF16)
            new = []
            for x, g, chunk, sl, cps in ((xs[0], grp_p, GM_CHUNK, GM_CHUNK, seq_len // GM_CHUNK),
                                         (xs[1], grp_s, GM_CHUNK, dec_len, 1)):
                lc = min(sl, GM_CHUNK)
                m = jnp.tile(gm_w_s[j][:, :lc, :lc], (1, chunk // lc, chunk // lc))
                bias = jnp.broadcast_to(jnp.tile(gm_b_s[j][:, :lc], (1, chunk // lc))[:, :, None],
                                        (GM_GROUPS, chunk, LANES))
                uv = _norm_mod_matmul(x, g, i, norm_pre[i, 1], w_in, act="gelu")
                mixed, st = _gm_core(uv, gm_ln_g[j], gm_ln_b[j], m, bias, chunk, lc, cps)
                new.append(_matmul_residual(mixed, w_out, x, g, i, norm_post[i, 1]))
                outs.setdefault("gm", []).append(st)
            xs = new

        elif kind == 1:
            perm = _swa_q_perm()
            nq = SWA_HEADS * SWA_HEAD_DIM
            nkv = SWA_KV_HEADS * SWA_HEAD_DIM
            w_qkv = jnp.concatenate([swa_w_qkv[j][:, :nq][:, perm], swa_w_qkv[j][:, nq:]], axis=1).astype(BF16)
            w_o = swa_w_o[j][perm, :].astype(BF16)
            w = SWA_WINDOW
            i_q, i_k = np.arange(w), np.arange(2 * w)
            delta = w + i_q[:, None] - i_k[None, :]
            bias_p = _swa_bias_table(rel_bias, delta, (delta >= 0) & (delta < w))
            sink_p = jnp.repeat(swa_sinks[j].astype(F32).reshape(SWA_KV_HEADS, SWA_GROUP), w, axis=1)[:, :, None]
            qkv_p = _norm_mod_matmul(xs[0], grp_p, i, norm_pre[i, 1], w_qkv)
            o_p = _swa_prompt_core(qkv_p, bias_p, sink_p, seq_len)
            x_p = _matmul_residual(o_p, w_o, xs[0], grp_p, i, norm_post[i, 1])
            kv_p = qkv_p.reshape(n_seq, seq_len, -1)[:, seq_len - w:, nq:]
            outs.setdefault("swa_kp", []).append(kv_p[..., :nkv].reshape(n_seq, w, SWA_KV_HEADS, SWA_HEAD_DIM))
            outs.setdefault("swa_vp", []).append(kv_p[..., nkv:].reshape(n_seq, w, SWA_KV_HEADS, SWA_HEAD_DIM))
            lb = state_swa_k.shape[2]
            n_keys = lb + dec_len
            pad = (-n_keys) % 8
            i_q, i_k = np.arange(dec_len), np.arange(n_keys + pad)
            delta = lb + i_q[:, None] - i_k[None, :]
            valid = (delta >= 0) & (delta < w) & (i_k[None, :] < n_keys)
            bias_s = _swa_bias_table(rel_bias, delta, valid)
            sink_s = jnp.repeat(swa_sinks[j].astype(F32).reshape(SWA_KV_HEADS, SWA_GROUP), dec_len, axis=1)[:, :, None]
            qkv_s = _norm_mod_matmul(xs[1], grp_s, i, norm_pre[i, 1], w_qkv).reshape(n_dec, dec_len, -1)
            zpad = jnp.zeros((n_dec, pad, nkv), F32)
            kk = jnp.concatenate([state_swa_k[j].reshape(n_dec, lb, nkv), qkv_s[..., nq:nq + nkv], zpad], axis=1)
            vv = jnp.concatenate([state_swa_v[j].reshape(n_dec, lb, nkv), qkv_s[..., nq + nkv:], zpad], axis=1)
            q_s = qkv_s[..., :nq].reshape(n_dec, dec_len, SWA_KV_HEADS // 2, SWA_GROUP, LANES)
            q_s = q_s.transpose(0, 2, 3, 1, 4).reshape(n_dec, SWA_KV_HEADS // 2, SWA_GROUP * dec_len, LANES)
            o_s = _swa_sample_core(q_s, kk, vv, bias_s, sink_s)
            o_s = o_s.reshape(n_dec, SWA_KV_HEADS // 2, SWA_GROUP, dec_len, LANES).transpose(0, 3, 1, 2, 4)
            x_s = _matmul_residual(o_s.reshape(t_s, nq), w_o, xs[1], grp_s, i, norm_post[i, 1])
            outs.setdefault("swa_ks", []).append(kk[:, n_keys - lb:n_keys].reshape(n_dec, lb, SWA_KV_HEADS, SWA_HEAD_DIM))
            outs.setdefault("swa_vs", []).append(vv[:, n_keys - lb:n_keys].reshape(n_dec, lb, SWA_KV_HEADS, SWA_HEAD_DIM))
            xs = [x_p, x_s]

        elif kind == 2:
            w_in = sc_w_in[j].astype(BF16)
            w_out = sc_w_out[j].astype(BF16)
            c = sc_w_out.shape[1]
            g3_p = _norm_mod_matmul(xs[0], grp_p, i, norm_pre[i, 1], w_in)
            y_p, st_p = _conv_prompt_core(g3_p, sc_conv[j], seq_len)
            x_p = _matmul_residual(y_p, w_out, xs[0], grp_p, i, norm_post[i, 1])
            outs.setdefault("conv_p", []).append(st_p[:, 8 - (CONV_WIDTH - 1):])
            g3_s = _norm_mod_matmul(xs[1], grp_s, i, norm_pre[i, 1], w_in)
            y_s, st_s = _conv_sample_core(g3_s.reshape(n_dec, dec_len * 3 * c),
                                          state_conv[j].reshape(n_dec, (CONV_WIDTH - 1) * c), sc_conv[j], dec_len)
            x_s = _matmul_residual(y_s.reshape(t_s, c), w_out, xs[1], grp_s, i, norm_post[i, 1])
            outs.setdefault("conv_s", []).append(st_s.reshape(n_dec, CONV_WIDTH - 1, c))
            xs = [x_p, x_s]

        else:
            r = MLA_KV_LORA
            kr_pad, kr_sw = _pad_rope_cols(mla_w_kva[j][:, r:])
            w_p = jnp.concatenate([mla_w_qa[j], mla_w_kva[j][:, :r], kr_pad, kr_sw], axis=1).astype(BF16)
            w_qb = mla_w_qb[j].reshape(MLA_Q_LORA, MLA_HEADS, MLA_NOPE + MLA_ROPE)
            w_nope = w_qb[:, :, :MLA_NOPE].reshape(MLA_Q_LORA, -1).astype(BF16)
            rope_pairs = [_pad_rope_cols(w_qb[:, h, MLA_NOPE:]) for h in range(MLA_HEADS)]
            w_rope = jnp.concatenate([p[0] for p in rope_pairs], axis=1).astype(BF16)
            w_rope_sw = jnp.concatenate([p[1] for p in rope_pairs], axis=1).astype(BF16)
            w_kvb = mla_w_kvb[j].reshape(r, MLA_HEADS, MLA_NOPE + MLA_V)
            w_uk_t = w_kvb[:, :, :MLA_NOPE].transpose(1, 2, 0).astype(BF16)
            w_uv = w_kvb[:, :, MLA_NOPE:].transpose(1, 0, 2).astype(BF16)
            w_o = mla_w_o[j].astype(BF16)
            tm = 256
            cos_p, sin_p = _rope_tables(jnp.arange(seq_len))
            proj_p = _norm_mod_matmul(xs[0], grp_p, i, norm_pre[i, 1], w_p, tn=w_p.shape[1])
            q_p, ckv_p, kr_p, kcat_p = _mla_proj(proj_p, cos_p, sin_p, seq_len // tm, mla_qa_norm[j],
                                                 mla_kva_norm[j], w_nope, w_rope, w_rope_sw, w_uk_t, True, tm=tm)
            ol_p = _mla_prompt_attn(q_p, kcat_p, n_seq, seq_len)
            x_p = _mla_out(ol_p, w_uv, w_o, xs[0], grp_p, i, norm_post[i, 1])
            outs.setdefault("mla_kvp", []).append(ckv_p.reshape(n_seq, seq_len, r))
            outs.setdefault("mla_krp", []).append(kr_p[:, :MLA_ROPE].reshape(n_seq, seq_len, MLA_ROPE))
            cos_s, sin_s = _rope_tables(past_len + jnp.arange(dec_len))
            cos_s, sin_s = jnp.tile(cos_s, (tm // dec_len, 1)), jnp.tile(sin_s, (tm // dec_len, 1))
            proj_s = _norm_mod_matmul(xs[1], grp_s, i, norm_pre[i, 1], w_p, tn=w_p.shape[1])
            q_s, ckv_s, kr_s, kcat_s = _mla_proj(proj_s, cos_s, sin_s, 1, mla_qa_norm[j],
                                                 mla_kva_norm[j], w_nope, w_rope, w_rope_sw, w_uk_t, False, tm=tm)
            knew = jnp.concatenate([kcat_s.reshape(n_dec, dec_len, MLA_QK),
                                    jnp.zeros((n_dec, 16 - dec_len, MLA_QK), BF16)], axis=1)
            ol_s = _mla_sample_attn(q_s.reshape(t_s * MLA_HEADS, MLA_QK), knew, cache_mla_kv,
                                    jnp.swapaxes(cache_mla_kr, 2, 3), page_table, j, dec_len)
            x_s = _mla_out(ol_s.reshape(t_s, MLA_HEADS * r), w_uv, w_o, xs[1], grp_s, i, norm_post[i, 1])
            outs.setdefault("mla_kvs", []).append(ckv_s.reshape(n_dec, dec_len, r))
            outs.setdefault("mla_krs", []).append(kr_s[:, :MLA_ROPE].reshape(n_dec, dec_len, MLA_ROPE))
            xs = [x_p, x_s]

        xs = [_half_ffn(x, g, i, 1, norm_pre[i, 2], norm_post[i, 2], ffn_w_in_b, ffn_w_out_b)
              for x, g in zip(xs, groups)]

    gm_p, gm_s = outs["gm"][0::2], outs["gm"][1::2]
    return (xs[0].reshape(n_seq, seq_len, d), xs[1].reshape(n_dec, dec_len, d),
            jnp.stack(gm_p),
            jnp.stack([s.reshape(n_dec, dec_len, -1) for s in gm_s]),
            jnp.stack(outs["swa_kp"]), jnp.stack(outs["swa_vp"]),
            jnp.stack(outs["swa_ks"]), jnp.stack(outs["swa_vs"]),
            jnp.stack(outs["conv_p"]), jnp.stack(outs["conv_s"]),
            jnp.stack(outs["mla_kvp"]), jnp.stack(outs["mla_krp"]),
            jnp.stack(outs["mla_kvs"]), jnp.stack(outs["mla_krs"]))
```

```python
import functools
import math

import numpy as np
import jax
import jax.numpy as jnp
from jax import lax
from jax.experimental import pallas as pl
from jax.experimental.pallas import tpu as pltpu

F32 = jnp.float32
BF16 = jnp.bfloat16

VMEM_LIMIT_BYTES = 56 * 1024 * 1024
LANES = 128

RMS_EPS = 1e-6
NEG_INF = -1e30
FFN_RES_W = 0.5

D_MODEL = 2048
GM_GROUPS = 8
GM_CHUNK = 128
SWA_WINDOW = 128
SWA_HEAD_DIM = 64
SWA_HEADS = 32
SWA_KV_HEADS = 8
SWA_GROUP = 4
SWA_SCALE = SWA_HEAD_DIM ** -0.5
N_BUCKETS = 32
BUCKET_MAX_DIST = 128
CONV_WIDTH = 3
MLA_HEADS = 16
MLA_Q_LORA = 512
MLA_KV_LORA = 512
MLA_NOPE = 128
MLA_ROPE = 64
MLA_V = 128
MLA_SCALE = (MLA_NOPE + MLA_ROPE) ** -0.5
MLA_QK = MLA_KV_LORA + LANES
ROPE_THETA = 10000.0
PAGE_SIZE = 128
PAGES_PER_STEP = 32


def _params(n_axes):
    return pltpu.CompilerParams(dimension_semantics=("arbitrary",) * n_axes,
                                vmem_limit_bytes=VMEM_LIMIT_BYTES)


def _rms(x, g):
    return x * lax.rsqrt(jnp.mean(x * x, axis=-1, keepdims=True) + RMS_EPS) * g


def _lane_tile(x, n):
    return x if n == 1 else jnp.concatenate([x] * n, axis=1)


class _Group:
    def __init__(self, mod, tm, tiles_per_mod_row):
        self.mod = mod
        self.tm = tm
        self.tiles_per_mod_row = tiles_per_mod_row

    def mod_spec(self, layer, col):
        if self.tiles_per_mod_row is None:
            return pl.BlockSpec((None, self.tm, D_MODEL), lambda i, *_: (layer, i, col))
        return pl.BlockSpec((None, 8, D_MODEL), lambda i, *_: (layer, 0, col))

    def row(self):
        if self.tiles_per_mod_row is None:
            return None
        return pl.program_id(0) // self.tiles_per_mod_row

    @staticmethod
    def read(ref, row):
        return ref[...] if row is None else ref[pl.ds(row, 1), :]


ROW_CHUNK = 16


def _for_row_chunks(n_rows, body):
    def step(c, carry):
        body(pl.ds(pl.multiple_of(c * ROW_CHUNK, ROW_CHUNK), ROW_CHUNK))
        return carry
    lax.fori_loop(0, n_rows // ROW_CHUNK, step, 0, unroll=2)


def _ada_in_rows(x_ref, gpre_ref, sc_ref, sh_ref, row, h_ref):
    if row is not None:
        gain = gpre_ref[...] * (1.0 + sc_ref[pl.ds(row, 1), :])
        shift = sh_ref[pl.ds(row, 1), :]

    def body(rs):
        x = x_ref[rs, :]
        xn = x * lax.rsqrt(jnp.mean(x * x, axis=-1, keepdims=True) + RMS_EPS)
        if row is None:
            h = xn * (gpre_ref[...] * (1.0 + sc_ref[rs, :])) + sh_ref[rs, :]
        else:
            h = xn * gain + shift
        h_ref[rs, :] = h.astype(h_ref.dtype)

    _for_row_chunks(x_ref.shape[0], body)


def _ada_out_rows(x_ref, val_ref, gpost_ref, gt_ref, row, res_w, o_ref):
    if row is not None:
        gain = gpost_ref[...] * (res_w * gt_ref[pl.ds(row, 1), :])

    def body(rs):
        v = val_ref[rs, :]
        vn = v * lax.rsqrt(jnp.mean(v * v, axis=-1, keepdims=True) + RMS_EPS)
        if row is None:
            o_ref[rs, :] = x_ref[rs, :] + vn * (gpost_ref[...] * (res_w * gt_ref[rs, :]))
        else:
            o_ref[rs, :] = x_ref[rs, :] + vn * gain

    _for_row_chunks(x_ref.shape[0], body)


def _shr(x, divisor):
    shift = divisor.bit_length() - 1
    assert 1 << shift == divisor
    return lax.shift_right_logical(x, shift)


def _ada_kernel(c_ref, w_ref, b_ref, o_ref, cs_sc):
    @pl.when((pl.program_id(0) == 0) & (pl.program_id(1) == 0))
    def _():
        c = c_ref[...]
        cs_sc[...] = (c * jax.nn.sigmoid(c)).astype(BF16)

    o_ref[...] = jnp.dot(cs_sc[...], w_ref[...].astype(BF16),
                         preferred_element_type=F32) + b_ref[...]


def _ada(c_all, ada_w, ada_b, tn=1024):
    n_layers, d, n = ada_w.shape
    rows = c_all.shape[0]
    return pl.pallas_call(
        _ada_kernel,
        grid=(n_layers, n // tn),
        in_specs=[pl.BlockSpec((rows, d), lambda l, j: (0, 0)),
                  pl.BlockSpec((None, d, tn), lambda l, j: (l, 0, j)),
                  pl.BlockSpec((None, 1, tn), lambda l, j: (l, 0, j))],
        out_specs=pl.BlockSpec((None, rows, tn), lambda l, j: (l, 0, j)),
        out_shape=jax.ShapeDtypeStruct((n_layers, rows, n), F32),
        scratch_shapes=[pltpu.VMEM((rows, d), BF16)],
        compiler_params=_params(2),
        name="ada_modulation",
    )(c_all, ada_w, ada_b.reshape(n_layers, 1, n))


def _ffn_kernel(x_ref, sh_ref, sc_ref, gt_ref, gpre_ref, gpost_ref, wg_ref, wu_ref, wo_ref, *rest, grp, n_f):
    if len(rest) == 7:
        nwi_ref, nwo_ref, o_ref, nwi_b_ref, nwo_b_ref, h_sc, acc_sc = rest
        nwi_b_ref[...] = nwi_ref[...].astype(BF16)
        nwo_b_ref[...] = nwo_ref[...].astype(BF16)
    else:
        o_ref, h_sc, acc_sc = rest
    j = pl.program_id(1)
    row = grp.row()

    @pl.when(j == 0)
    def _():
        _ada_in_rows(x_ref, gpre_ref, sc_ref, sh_ref, row, h_sc)
        acc_sc[...] = jnp.zeros_like(acc_sc)

    h = h_sc[...]
    g = jnp.dot(h, wg_ref[...], preferred_element_type=F32)
    u = jnp.dot(h, wu_ref[...], preferred_element_type=F32)
    a = (g * jax.nn.sigmoid(g) * u).astype(BF16)
    acc_sc[...] += jnp.dot(a, wo_ref[...], preferred_element_type=F32)

    @pl.when(j == n_f - 1)
    def _():
        _ada_out_rows(x_ref, acc_sc, gpost_ref, gt_ref, row, FFN_RES_W, o_ref)


def _half_ffn(x, grp, layer, which, g_pre, g_post, w_in, w_out, next_w=None, tf=512):
    t, d = x.shape
    f = w_out.shape[0]
    tm, n_f = grp.tm, f // tf
    n_i = t // tm
    sub = 2 * which
    kern = functools.partial(_ffn_kernel, grp=grp, n_f=n_f)
    in_specs = [pl.BlockSpec((tm, d), lambda i, j: (i, 0)),
                grp.mod_spec(layer, sub * 3 + 0),
                grp.mod_spec(layer, sub * 3 + 1),
                grp.mod_spec(layer, sub * 3 + 2),
                pl.BlockSpec((1, d), lambda i, j: (0, 0)),
                pl.BlockSpec((1, d), lambda i, j: (0, 0)),
                pl.BlockSpec((d, tf), lambda i, j: (0, j)),
                pl.BlockSpec((d, tf), lambda i, j: (0, n_f + j)),
                pl.BlockSpec((tf, d), lambda i, j: (j, 0))]
    args = [x, grp.mod, grp.mod, grp.mod, g_pre.reshape(1, d), g_post.reshape(1, d), w_in, w_in, w_out]
    out_specs = [pl.BlockSpec((tm, d), lambda i, j: (i, 0))]
    out_shape = [jax.ShapeDtypeStruct((t, d), F32)]
    if next_w is not None:
        nw_in, nw_out, nl, nwh = next_w
        bi = (d // n_i, 2 * f // n_f)
        bo = (f // n_f, d // n_i)
        in_specs += [pl.BlockSpec((None, None) + bi, lambda i, j: (nl, nwh, i, j)),
                     pl.BlockSpec((None, None) + bo, lambda i, j: (nl, nwh, j, i))]
        args += [nw_in, nw_out]
        out_specs += [pl.BlockSpec(bi, lambda i, j: (i, j)), pl.BlockSpec(bo, lambda i, j: (j, i))]
        out_shape += [jax.ShapeDtypeStruct((d, 2 * f), BF16), jax.ShapeDtypeStruct((f, d), BF16)]
    res = pl.pallas_call(
        kern,
        grid=(n_i, n_f),
        in_specs=in_specs,
        out_specs=out_specs,
        out_shape=out_shape,
        scratch_shapes=[pltpu.VMEM((tm, d), BF16), pltpu.VMEM((tm, d), F32)],
        compiler_params=_params(2),
        name="half_ffn",
    )(*args)
    return res if next_w is not None else res[0]


def _nmm_kernel(x_ref, sh_ref, sc_ref, gpre_ref, w_ref, o_ref, h_sc, *, grp, act):
    row = grp.row()

    @pl.when(pl.program_id(1) == 0)
    def _():
        _ada_in_rows(x_ref, gpre_ref, sc_ref, sh_ref, row, h_sc)

    y = jnp.dot(h_sc[...], w_ref[...], preferred_element_type=F32)
    if act == "gelu":
        y = jax.nn.gelu(y, approximate=True)
    o_ref[...] = y.astype(o_ref.dtype)


def _norm_mod_matmul(x, grp, layer, g_pre, w, act=None, tn=1024):
    t, d = x.shape
    n = w.shape[1]
    tn = min(tn, n)
    tm = grp.tm
    kern = functools.partial(_nmm_kernel, grp=grp, act=act)
    return pl.pallas_call(
        kern,
        grid=(t // tm, n // tn),
        in_specs=[pl.BlockSpec((tm, d), lambda i, j: (i, 0)),
                  grp.mod_spec(layer, 3 + 0),
                  grp.mod_spec(layer, 3 + 1),
                  pl.BlockSpec((1, d), lambda i, j: (0, 0)),
                  pl.BlockSpec((d, tn), lambda i, j: (0, j))],
        out_specs=pl.BlockSpec((tm, tn), lambda i, j: (i, j)),
        out_shape=jax.ShapeDtypeStruct((t, n), F32),
        scratch_shapes=[pltpu.VMEM((tm, d), BF16)],
        compiler_params=_params(2),
        name="norm_mod_matmul",
    )(x, grp.mod, grp.mod, g_pre.reshape(1, d), w)


def _mres_kernel(a_ref, w_ref, x_ref, gt_ref, gpost_ref, o_ref, acc_sc, *, grp, n_k):
    k = pl.program_id(1)
    row = grp.row()

    @pl.when(k == 0)
    def _():
        acc_sc[...] = jnp.zeros_like(acc_sc)

    acc_sc[...] += jnp.dot(a_ref[...].astype(BF16), w_ref[...], preferred_element_type=F32)

    @pl.when(k == n_k - 1)
    def _():
        _ada_out_rows(x_ref, acc_sc, gpost_ref, gt_ref, row, 1.0, o_ref)


def _matmul_residual(a, w, x, grp, layer, g_post, tk=1024):
    t, kdim = a.shape
    d = w.shape[1]
    tm, n_k = grp.tm, kdim // tk
    kern = functools.partial(_mres_kernel, grp=grp, n_k=n_k)
    return pl.pallas_call(
        kern,
        grid=(t // tm, n_k),
        in_specs=[pl.BlockSpec((tm, tk), lambda i, k: (i, k)),
                  pl.BlockSpec((tk, d), lambda i, k: (k, 0)),
                  pl.BlockSpec((tm, d), lambda i, k: (i, 0)),
                  grp.mod_spec(layer, 3 + 2),
                  pl.BlockSpec((1, d), lambda i, k: (0, 0))],
        out_specs=pl.BlockSpec((tm, d), lambda i, k: (i, 0)),
        out_shape=jax.ShapeDtypeStruct((t, d), F32),
        scratch_shapes=[pltpu.VMEM((tm, d), F32)],
        compiler_params=_params(2),
        name="matmul_residual",
    )(a, w, x, grp.mod, g_post.reshape(1, d))


def _gm_kernel(u_ref, v_ref, lng_ref, lnb_ref, m_ref, b_ref, o_ref, st_ref, *, seq_len):
    v = v_ref[...]
    vc = v - jnp.mean(v, axis=-1, keepdims=True)
    vn = vc * lax.rsqrt(jnp.mean(vc * vc, axis=-1, keepdims=True) + RMS_EPS) * lng_ref[...] + lnb_ref[...]
    st_ref[...] = vn
    vnb = vn.astype(BF16)
    c = v.shape[0]
    gw = v.shape[1] // GM_GROUPS
    row = lax.broadcasted_iota(jnp.int32, (c, c), 0)
    col = lax.broadcasted_iota(jnp.int32, (c, c), 1)
    keep = (col <= row) & (_shr(row, seq_len) == _shr(col, seq_len))
    for g in range(GM_GROUPS):
        mg = jnp.where(keep, m_ref[g], 0.0).astype(BF16)
        mixed = jnp.dot(mg, vnb[:, g * gw:(g + 1) * gw], preferred_element_type=F32)
        mixed = mixed + _lane_tile(b_ref[g], gw // LANES)
        o_ref[:, g * gw:(g + 1) * gw] = (u_ref[:, g * gw:(g + 1) * gw] * mixed).astype(o_ref.dtype)


def _gm_core(uv, ln_g, ln_b, m, bias, chunk, seq_len, chunks_per_state):
    t = uv.shape[0]
    w = uv.shape[1] // 2
    n_chunks = t // chunk
    n_states = n_chunks // chunks_per_state
    kern = functools.partial(_gm_kernel, seq_len=seq_len)
    return pl.pallas_call(
        kern,
        grid=(n_chunks,),
        in_specs=[pl.BlockSpec((chunk, w), lambda c: (c, 0)),
                  pl.BlockSpec((chunk, w), lambda c: (c, 1)),
                  pl.BlockSpec((1, w), lambda c: (0, 0)),
                  pl.BlockSpec((1, w), lambda c: (0, 0)),
                  pl.BlockSpec((GM_GROUPS, chunk, chunk), lambda c: (0, 0, 0)),
                  pl.BlockSpec((GM_GROUPS, chunk, LANES), lambda c: (0, 0, 0))],
        out_specs=[pl.BlockSpec((chunk, w), lambda c: (c, 0)),
                   pl.BlockSpec((None, chunk, w), lambda c: (c // chunks_per_state, 0, 0))],
        out_shape=[jax.ShapeDtypeStruct((t, w), BF16),
                   jax.ShapeDtypeStruct((n_states, chunk, w), F32)],
        compiler_params=_params(1),
        name="gm_core",
    )(uv, uv, ln_g.reshape(1, w), ln_b.reshape(1, w), m, bias)


def _softmax_sink(s, sink):
    mx = jnp.maximum(jnp.max(s, axis=-1, keepdims=True), sink)
    p = jnp.exp(s - mx)
    denom = jnp.sum(p, axis=-1, keepdims=True) + jnp.exp(sink - mx)
    return p / denom


def _half_lane_mask(rows, parity):
    lane = lax.broadcasted_iota(jnp.int32, (rows, LANES), 1)
    return (lane >= SWA_HEAD_DIM) if parity else (lane < SWA_HEAD_DIM)


def _swa_attend(q_blocks, k, v, bias_ref, sink_ref, masked_keys=None):
    rows = q_blocks[0].shape[0]
    n_keys = k.shape[0]
    logits, values = [], []
    for m in range(SWA_KV_HEADS // 2):
        k2 = k[:, m * LANES:(m + 1) * LANES]
        v2 = v[:, m * LANES:(m + 1) * LANES]
        for parity in range(2):
            keep = _half_lane_mask(n_keys, parity)
            kx = jnp.where(keep, k2, 0.0).astype(BF16)
            values.append(jnp.where(keep, v2, 0.0).astype(BF16))
            logits.append(lax.dot_general(q_blocks[m], kx, (((1,), (1,)), ((), ())), preferred_element_type=F32))
    s = jnp.concatenate(logits, axis=0) + bias_ref[...]
    if masked_keys is not None:
        key_col = lax.broadcasted_iota(jnp.int32, s.shape, 1)
        s = jnp.where(key_col < masked_keys, NEG_INF, s)
    p = _softmax_sink(s, sink_ref[...]).astype(BF16)
    outs = []
    for m in range(SWA_KV_HEADS // 2):
        o_m = None
        for parity in range(2):
            idx = 2 * m + parity
            o_p = jnp.dot(p[idx * rows:(idx + 1) * rows], values[idx], preferred_element_type=F32)
            o_m = o_p if o_m is None else o_m + o_p
        outs.append(o_m)
    return outs


def _swa_prompt_kernel(q_ref, kp_ref, ko_ref, vp_ref, vo_ref, bias_ref, sink_ref, o_ref, *, blocks_per_seq):
    w = SWA_WINDOW
    first = (pl.program_id(0) % blocks_per_seq) == 0
    q = (q_ref[...] * SWA_SCALE).astype(BF16)
    k = jnp.concatenate([kp_ref[...], ko_ref[...]], axis=0)
    v = jnp.concatenate([vp_ref[...], vo_ref[...]], axis=0)
    q_blocks = [jnp.concatenate([q[:, (4 * m + g) * LANES:(4 * m + g + 1) * LANES] for g in range(SWA_GROUP)],
                                axis=0) for m in range(SWA_KV_HEADS // 2)]
    outs = _swa_attend(q_blocks, k, v, bias_ref, sink_ref, masked_keys=jnp.where(first, w, 0))
    for m, o_m in enumerate(outs):
        for g in range(SWA_GROUP):
            o_ref[:, (4 * m + g) * LANES:(4 * m + g + 1) * LANES] = o_m[g * w:(g + 1) * w].astype(o_ref.dtype)


def _swa_prompt_core(qkv, bias, sink, seq_len):
    t = qkv.shape[0]
    w = SWA_WINDOW
    nq = SWA_HEADS * SWA_HEAD_DIM
    nkv = SWA_KV_HEADS * SWA_HEAD_DIM
    bps = seq_len // w
    kcol, vcol = nq // nkv, nq // nkv + 1

    def prev(i):
        return jnp.maximum(i - 1, 0)

    kern = functools.partial(_swa_prompt_kernel, blocks_per_seq=bps)
    return pl.pallas_call(
        kern,
        grid=(t // w,),
        in_specs=[pl.BlockSpec((w, nq), lambda i: (i, 0)),
                  pl.BlockSpec((w, nkv), lambda i: (prev(i), kcol)),
                  pl.BlockSpec((w, nkv), lambda i: (i, kcol)),
                  pl.BlockSpec((w, nkv), lambda i: (prev(i), vcol)),
                  pl.BlockSpec((w, nkv), lambda i: (i, vcol)),
                  pl.BlockSpec(bias.shape, lambda i: (0, 0)),
                  pl.BlockSpec(sink.shape, lambda i: (0, 0))],
        out_specs=pl.BlockSpec((w, nq), lambda i: (i, 0)),
        out_shape=jax.ShapeDtypeStruct((t, nq), BF16),
        compiler_params=_params(1),
        name="swa_prompt_core",
    )(qkv, qkv, qkv, qkv, qkv, bias, sink)


def _swa_sample_kernel(q_ref, k_ref, v_ref, bias_ref, sink_ref, o_ref, *, bb):
    def body(b, carry):
        q_blocks = [(q_ref[b, m] * SWA_SCALE).astype(BF16) for m in range(SWA_KV_HEADS // 2)]
        outs = _swa_attend(q_blocks, k_ref[b], v_ref[b], bias_ref, sink_ref)
        for m, o_m in enumerate(outs):
            o_ref[b, m] = o_m.astype(o_ref.dtype)
        return carry

    lax.fori_loop(0, bb, body, 0, unroll=2)


def _swa_sample_core(q, kk, vv, bias, sink, bb=8):
    b, n_pair, rows, _ = q.shape
    n_keys = kk.shape[1]
    nkv = kk.shape[2]
    kern = functools.partial(_swa_sample_kernel, bb=bb)
    return pl.pallas_call(
        kern,
        grid=(b // bb,),
        in_specs=[pl.BlockSpec((bb, n_pair, rows, LANES), lambda i: (i, 0, 0, 0)),
                  pl.BlockSpec((bb, n_keys, nkv), lambda i: (i, 0, 0)),
                  pl.BlockSpec((bb, n_keys, nkv), lambda i: (i, 0, 0)),
                  pl.BlockSpec(bias.shape, lambda i: (0, 0)),
                  pl.BlockSpec(sink.shape, lambda i: (0, 0))],
        out_specs=pl.BlockSpec((bb, n_pair, rows, LANES), lambda i: (i, 0, 0, 0)),
        out_shape=jax.ShapeDtypeStruct(q.shape, BF16),
        compiler_params=_params(1),
        name="swa_sample_core",
    )(q, kk, vv, bias, sink)


def _conv_prompt_kernel(gb_ref, gc_ref, z_ref, hc_ref, hz_ref, cw_ref, o_ref, st_ref, zp_sc, *, tiles_per_seq):
    tm = gb_ref.shape[0]
    first = (pl.program_id(0) % tiles_per_seq) == 0
    zz = gc_ref[...] * z_ref[...]
    halo = jnp.where(first, 0.0, hc_ref[...] * hz_ref[...])
    zp_sc[0:8, :] = halo
    zp_sc[8:8 + tm, :] = zz
    y = cw_ref[2:3, :] * zz + cw_ref[1:2, :] * zp_sc[7:7 + tm, :] + cw_ref[0:1, :] * zp_sc[6:6 + tm, :]
    o_ref[...] = (gb_ref[...] * y).astype(o_ref.dtype)
    st_ref[...] = zz[tm - 8:tm]


def _conv_prompt_core(g3, conv_w, seq_len, tm=256):
    t = g3.shape[0]
    c = g3.shape[1] // 3
    tps = seq_len // tm
    n_seq = t // seq_len

    def halo(col):
        return pl.BlockSpec((8, c), lambda i: (jnp.maximum(i * (tm // 8) - 1, 0), col))

    kern = functools.partial(_conv_prompt_kernel, tiles_per_seq=tps)
    return pl.pallas_call(
        kern,
        grid=(t // tm,),
        in_specs=[pl.BlockSpec((tm, c), lambda i: (i, 0)),
                  pl.BlockSpec((tm, c), lambda i: (i, 1)),
                  pl.BlockSpec((tm, c), lambda i: (i, 2)),
                  halo(1), halo(2),
                  pl.BlockSpec((CONV_WIDTH, c), lambda i: (0, 0))],
        out_specs=[pl.BlockSpec((tm, c), lambda i: (i, 0)),
                   pl.BlockSpec((None, 8, c), lambda i: (i // tps, 0, 0))],
        out_shape=[jax.ShapeDtypeStruct((t, c), BF16),
                   jax.ShapeDtypeStruct((n_seq, 8, c), F32)],
        scratch_shapes=[pltpu.VMEM((tm + 8, c), F32)],
        compiler_params=_params(1),
        name="conv_prompt_core",
    )(g3, g3, g3, g3, g3, conv_w)


def _conv_sample_kernel(g_ref, prev_ref, cw_ref, o_ref, st_ref, *, seq_len, c):
    zz = [prev_ref[:, 0:c], prev_ref[:, c:2 * c]]
    for t in range(seq_len):
        base = t * 3 * c
        zz.append(g_ref[:, base + c:base + 2 * c] * g_ref[:, base + 2 * c:base + 3 * c])
    for t in range(seq_len):
        y = cw_ref[2:3, :] * zz[t + 2] + cw_ref[1:2, :] * zz[t + 1] + cw_ref[0:1, :] * zz[t]
        o_ref[:, t * c:(t + 1) * c] = (g_ref[:, t * 3 * c:t * 3 * c + c] * y).astype(o_ref.dtype)
    st_ref[:, 0:c] = zz[seq_len]
    st_ref[:, c:2 * c] = zz[seq_len + 1]


def _conv_sample_core(g3, prev, conv_w, seq_len):
    b = g3.shape[0]
    c = g3.shape[1] // (3 * seq_len)
    kern = functools.partial(_conv_sample_kernel, seq_len=seq_len, c=c)
    return pl.pallas_call(
        kern,
        grid=(1,),
        in_specs=[pl.BlockSpec(g3.shape, lambda i: (0, 0)),
                  pl.BlockSpec(prev.shape, lambda i: (0, 0)),
                  pl.BlockSpec((CONV_WIDTH, c), lambda i: (0, 0))],
        out_specs=[pl.BlockSpec((b, seq_len * c), lambda i: (0, 0)),
                   pl.BlockSpec((b, 2 * c), lambda i: (0, 0))],
        out_shape=[jax.ShapeDtypeStruct((b, seq_len * c), BF16),
                   jax.ShapeDtypeStruct((b, 2 * c), F32)],
        compiler_params=_params(1),
        name="conv_sample_core",
    )(g3, prev, conv_w)


def _mla_proj_kernel(p_ref, cos_ref, sin_ref, qan_ref, kvn_ref, wn_ref, wr_ref, wrs_ref, wuk_ref,
                     q_ref, ckv_ref, kr_ref, kcat_ref, *, head_major):
    r = MLA_KV_LORA
    cos = cos_ref[...]
    sin = sin_ref[...]
    qa = _rms(p_ref[:, 0:MLA_Q_LORA], qan_ref[...]).astype(BF16)
    qn = jnp.dot(qa, wn_ref[...], preferred_element_type=F32)
    qr = jnp.dot(qa, wr_ref[...], preferred_element_type=F32)
    qrs = jnp.dot(qa, wrs_ref[...], preferred_element_type=F32)
    for h in range(MLA_HEADS):
        sl = slice(h * LANES, (h + 1) * LANES)
        q_lat = jnp.dot(qn[:, sl].astype(BF16), wuk_ref[h], preferred_element_type=F32)
        q_rope = qr[:, sl] * cos + qrs[:, sl] * sin
        q_lat = (q_lat * MLA_SCALE).astype(q_ref.dtype)
        q_rope = (q_rope * MLA_SCALE).astype(q_ref.dtype)
        if head_major:
            q_ref[h, :, 0:r] = q_lat
            q_ref[h, :, r:MLA_QK] = q_rope
        else:
            q_ref[:, h * MLA_QK:h * MLA_QK + r] = q_lat
            q_ref[:, h * MLA_QK + r:(h + 1) * MLA_QK] = q_rope
    off = MLA_Q_LORA
    ckv = _rms(p_ref[:, off:off + r], kvn_ref[...])
    kr = p_ref[:, off + r:off + r + LANES] * cos + p_ref[:, off + r + LANES:off + r + 2 * LANES] * sin
    ckv_ref[...] = ckv
    kr_ref[...] = kr
    kcat_ref[:, 0:r] = ckv.astype(kcat_ref.dtype)
    kcat_ref[:, r:r + LANES] = kr.astype(kcat_ref.dtype)


def _mla_proj(p, cos, sin, pos_blocks, qa_norm, kva_norm, w_nope, w_rope, w_rope_sw, w_uk_t, head_major, tm=256):
    t = p.shape[0]
    hq = MLA_HEADS * MLA_QK
    const2 = lambda i: (0, 0)
    if head_major:
        q_spec = pl.BlockSpec((MLA_HEADS, tm, MLA_QK), lambda i: (0, i, 0))
        q_shape = jax.ShapeDtypeStruct((MLA_HEADS, t, MLA_QK), BF16)
    else:
        q_spec = pl.BlockSpec((tm, hq), lambda i: (i, 0))
        q_shape = jax.ShapeDtypeStruct((t, hq), BF16)
    return pl.pallas_call(
        functools.partial(_mla_proj_kernel, head_major=head_major),
        grid=(t // tm,),
        in_specs=[pl.BlockSpec((tm, p.shape[1]), lambda i: (i, 0)),
                  pl.BlockSpec((tm, LANES), lambda i: (i % pos_blocks, 0)),
                  pl.BlockSpec((tm, LANES), lambda i: (i % pos_blocks, 0)),
                  pl.BlockSpec((1, MLA_Q_LORA), const2),
                  pl.BlockSpec((1, MLA_KV_LORA), const2),
                  pl.BlockSpec(w_nope.shape, const2),
                  pl.BlockSpec(w_rope.shape, const2),
                  pl.BlockSpec(w_rope_sw.shape, const2),
                  pl.BlockSpec(w_uk_t.shape, lambda i: (0, 0, 0))],
        out_specs=[q_spec,
                   pl.BlockSpec((tm, MLA_KV_LORA), lambda i: (i, 0)),
                   pl.BlockSpec((tm, LANES), lambda i: (i, 0)),
                   pl.BlockSpec((tm, MLA_QK), lambda i: (i, 0))],
        out_shape=[q_shape,
                   jax.ShapeDtypeStruct((t, MLA_KV_LORA), F32),
                   jax.ShapeDtypeStruct((t, LANES), F32),
                   jax.ShapeDtypeStruct((t, MLA_QK), BF16)],
        compiler_params=_params(1),
        name="mla_proj",
    )(p, cos, sin, qa_norm.reshape(1, -1), kva_norm.reshape(1, -1), w_nope, w_rope, w_rope_sw, w_uk_t)


def _flash_update(s, v, m_ref, l_ref, acc_ref):
    m_prev = m_ref[...]
    m_new = jnp.maximum(m_prev, jnp.max(s, axis=1, keepdims=True))
    alpha = jnp.exp(m_prev - m_new)
    p = jnp.exp(s - _lane_tile(m_new, s.shape[1] // LANES))
    l_ref[...] = alpha * l_ref[...] + jnp.sum(p, axis=1, keepdims=True)
    acc_ref[...] = acc_ref[...] * _lane_tile(alpha, acc_ref.shape[1] // LANES) + jnp.dot(
        p.astype(BF16), v, preferred_element_type=F32)
    m_ref[...] = m_new


def _flash_init(m_sc, l_sc, acc_sc):
    m_sc[...] = jnp.full_like(m_sc, NEG_INF)
    l_sc[...] = jnp.zeros_like(l_sc)
    acc_sc[...] = jnp.zeros_like(acc_sc)


def _flash_result(l_ref, acc_ref):
    return acc_ref[...] / _lane_tile(l_ref[...], acc_ref.shape[1] // LANES)


def _mla_prompt_kernel(b_tab, qi_tab, ki_tab, q_ref, k_ref, o_ref, m_sc, l_sc, acc_sc, *, tq, tk, hb):
    step = pl.program_id(0)
    qi = qi_tab[step]
    ki = ki_tab[step]
    last = ki == (qi * tq) // tk
    rows = hb * tq

    @pl.when(ki == 0)
    def _():
        _flash_init(m_sc, l_sc, acc_sc)

    def run(masked):
        k = k_ref[...]
        v = k[:, 0:MLA_KV_LORA]
        if masked:
            q_pos = qi * tq + (lax.broadcasted_iota(jnp.int32, (rows, tk), 0) & (tq - 1))
            k_pos = ki * tk + lax.broadcasted_iota(jnp.int32, (rows, tk), 1)
            visible = k_pos <= q_pos
        for g in range(MLA_HEADS // hb):
            sl = pl.ds(g * rows, rows)
            q = q_ref[g * hb:(g + 1) * hb].reshape(rows, MLA_QK)
            s = lax.dot_general(q, k, (((1,), (1,)), ((), ())), preferred_element_type=F32)
            if masked:
                s = jnp.where(visible, s, NEG_INF)
            _flash_update(s, v, m_sc.at[sl], l_sc.at[sl], acc_sc.at[sl])
            if masked:
                o = _flash_result(l_sc.at[sl], acc_sc.at[sl])
                o_ref[g * hb:(g + 1) * hb] = o.reshape(hb, tq, MLA_KV_LORA).astype(o_ref.dtype)

    pl.when(last)(lambda: run(True))
    pl.when(jnp.logical_not(last))(lambda: run(False))


def _mla_prompt_attn(q3, kcat, n_seq, seq_len, tq=256, tk=512, hb=2):
    assert tq & (tq - 1) == 0
    rows = tq * MLA_HEADS
    nq, nk = seq_len // tq, seq_len // tk
    steps = [(b, qi, ki) for b in range(n_seq) for qi in range(nq) for ki in range((qi * tq) // tk + 1)]
    b_tab, qi_tab, ki_tab = (jnp.asarray(np.array(col, np.int32)) for col in zip(*steps))
    kern = functools.partial(_mla_prompt_kernel, tq=tq, tk=tk, hb=hb)
    grid_spec = pltpu.PrefetchScalarGridSpec(
        num_scalar_prefetch=3,
        grid=(len(steps),),
        in_specs=[pl.BlockSpec((MLA_HEADS, tq, MLA_QK), lambda s, bt, qt, kt: (0, bt[s] * nq + qt[s], 0)),
                  pl.BlockSpec((tk, MLA_QK), lambda s, bt, qt, kt: (bt[s] * nk + kt[s], 0))],
        out_specs=pl.BlockSpec((MLA_HEADS, tq, MLA_KV_LORA), lambda s, bt, qt, kt: (0, bt[s] * nq + qt[s], 0)),
        scratch_shapes=[pltpu.VMEM((rows, LANES), F32), pltpu.VMEM((rows, LANES), F32),
                        pltpu.VMEM((rows, MLA_KV_LORA), F32)])
    return pl.pallas_call(
        kern,
        grid_spec=grid_spec,
        out_shape=jax.ShapeDtypeStruct((MLA_HEADS, q3.shape[1], MLA_KV_LORA), BF16),
        compiler_params=_params(1),
        name="mla_prompt_attn",
    )(b_tab, qi_tab, ki_tab, q3, kcat)


def _mla_sample_kernel(pt_ref, q_ref, knew_ref, *refs, n_groups, seq_len):
    kv_refs = refs[:PAGES_PER_STEP]
    krt_refs = refs[PAGES_PER_STEP:2 * PAGES_PER_STEP]
    o_ref, kv_sc, krt_sc, m_sc, l_sc, acc_sc = refs[2 * PAGES_PER_STEP:]
    g = pl.program_id(1)
    r = MLA_KV_LORA
    nt = (((1,), (1,)), ((), ()))

    @pl.when(g == 0)
    def _():
        _flash_init(m_sc, l_sc, acc_sc)

    for i in range(PAGES_PER_STEP):
        kv_sc[i * PAGE_SIZE:(i + 1) * PAGE_SIZE, :] = kv_refs[i][...].astype(BF16)
        krt_sc[:, i * PAGE_SIZE:(i + 1) * PAGE_SIZE] = krt_refs[i][...].astype(BF16)
    q = q_ref[...]
    kv = kv_sc[...]
    s = (lax.dot_general(q[:, 0:r], kv, nt, preferred_element_type=F32)
         + jnp.dot(q[:, r:r + MLA_ROPE], krt_sc[...], preferred_element_type=F32))
    _flash_update(s, kv, m_sc, l_sc, acc_sc)

    @pl.when(g == n_groups - 1)
    def _():
        knew = knew_ref[...]
        n_new = knew.shape[0]
        s_new = lax.dot_general(q, knew, nt, preferred_element_type=F32)
        rows = s_new.shape[0]
        q_t = _shr(lax.broadcasted_iota(jnp.int32, (rows, n_new), 0), MLA_HEADS)
        k_t = lax.broadcasted_iota(jnp.int32, (rows, n_new), 1)
        s_new = jnp.where((k_t <= q_t) & (k_t < seq_len), s_new, NEG_INF)
        m_old = m_sc[...]
        m_fin = jnp.maximum(m_old, jnp.max(s_new, axis=1, keepdims=True))
        a_fin = jnp.exp(m_old - m_fin)
        p_new = jnp.exp(s_new - m_fin[:, 0:n_new])
        l_sc[...] = a_fin * l_sc[...] + jnp.sum(p_new, axis=1, keepdims=True)
        acc_sc[...] = acc_sc[...] * _lane_tile(a_fin, r // LANES) + jnp.dot(
            p_new.astype(BF16), knew[:, 0:r], preferred_element_type=F32)
        o_ref[...] = _flash_result(l_sc, acc_sc).astype(o_ref.dtype)


def _mla_sample_attn(q2d, knew, cache_kv, cache_krt, page_table, layer_j, seq_len):
    b, n_pages = page_table.shape
    n_groups = n_pages // PAGES_PER_STEP
    rows = seq_len * MLA_HEADS
    keys = PAGES_PER_STEP * PAGE_SIZE
    pt_flat = page_table.reshape(-1)

    def page_spec(i, shape):
        return pl.BlockSpec(
            (None, None) + shape,
            lambda bi, gi, pt: (layer_j, pt[bi * n_pages + gi * PAGES_PER_STEP + i], 0, 0))

    kern = functools.partial(_mla_sample_kernel, n_groups=n_groups, seq_len=seq_len)
    grid_spec = pltpu.PrefetchScalarGridSpec(
        num_scalar_prefetch=1,
        grid=(b, n_groups),
        in_specs=([pl.BlockSpec((rows, MLA_QK), lambda bi, gi, pt: (bi, 0)),
                   pl.BlockSpec((None, knew.shape[1], MLA_QK), lambda bi, gi, pt: (bi, 0, 0))]
                  + [page_spec(i, (PAGE_SIZE, MLA_KV_LORA)) for i in range(PAGES_PER_STEP)]
                  + [page_spec(i, (MLA_ROPE, PAGE_SIZE)) for i in range(PAGES_PER_STEP)]),
        out_specs=pl.BlockSpec((rows, MLA_KV_LORA), lambda bi, gi, pt: (bi, 0)),
        scratch_shapes=[pltpu.VMEM((keys, MLA_KV_LORA), BF16), pltpu.VMEM((MLA_ROPE, keys), BF16),
                        pltpu.VMEM((rows, LANES), F32), pltpu.VMEM((rows, LANES), F32),
                        pltpu.VMEM((rows, MLA_KV_LORA), F32)])
    return pl.pallas_call(
        kern,
        grid_spec=grid_spec,
        out_shape=jax.ShapeDtypeStruct((q2d.shape[0], MLA_KV_LORA), BF16),
        compiler_params=_params(2),
        name="mla_sample_attn",
    )(pt_flat, q2d, knew, *([cache_kv] * PAGES_PER_STEP), *([cache_krt] * PAGES_PER_STEP))


def _mla_out_kernel(ol_ref, wuv_ref, wo_ref, x_ref, gt_ref, gpost_ref, o_ref, o_sc, out_sc, *, grp, head_major):
    r = MLA_KV_LORA
    for h in range(MLA_HEADS):
        o_lat = ol_ref[h] if head_major else ol_ref[:, h * r:(h + 1) * r]
        o_h = jnp.dot(o_lat, wuv_ref[h], preferred_element_type=F32)
        o_sc[:, h * MLA_V:(h + 1) * MLA_V] = o_h.astype(BF16)
    out_sc[...] = jnp.dot(o_sc[...], wo_ref[...], preferred_element_type=F32)
    _ada_out_rows(x_ref, out_sc, gpost_ref, gt_ref, grp.row(), 1.0, o_ref)


def _mla_out(o_lat, w_uv, w_o, x, grp, layer, g_post, tm=256):
    t, d = x.shape
    head_major = o_lat.ndim == 3
    sub = _Group(grp.mod, tm, None if grp.tiles_per_mod_row is None else grp.tiles_per_mod_row * grp.tm // tm)
    kern = functools.partial(_mla_out_kernel, grp=sub, head_major=head_major)
    if head_major:
        ol_spec = pl.BlockSpec((MLA_HEADS, tm, MLA_KV_LORA), lambda i: (0, i, 0))
    else:
        ol_spec = pl.BlockSpec((tm, o_lat.shape[1]), lambda i: (i, 0))
    return pl.pallas_call(
        kern,
        grid=(t // tm,),
        in_specs=[ol_spec,
                  pl.BlockSpec(w_uv.shape, lambda i: (0, 0, 0)),
                  pl.BlockSpec(w_o.shape, lambda i: (0, 0)),
                  pl.BlockSpec((tm, d), lambda i: (i, 0)),
                  sub.mod_spec(layer, 3 + 2),
                  pl.BlockSpec((1, d), lambda i: (0, 0))],
        out_specs=pl.BlockSpec((tm, d), lambda i: (i, 0)),
        out_shape=jax.ShapeDtypeStruct((t, d), F32),
        scratch_shapes=[pltpu.VMEM((tm, MLA_HEADS * MLA_V), BF16), pltpu.VMEM((tm, d), F32)],
        compiler_params=_params(1),
        name="mla_out",
    )(o_lat, w_uv, w_o, x, sub.mod, g_post.reshape(1, d))


def _t5_buckets(delta):
    n = np.maximum(delta, 0)
    max_exact = N_BUCKETS // 2
    log_ratio = np.log(np.maximum(n, 1).astype(np.float64) / max_exact) / math.log(BUCKET_MAX_DIST / max_exact)
    large = np.minimum(max_exact + (log_ratio * (N_BUCKETS - max_exact)).astype(np.int64), N_BUCKETS - 1)
    return np.where(n < max_exact, n, large).astype(np.int32)


def _swa_bias_table(rel_bias, delta, valid):
    lq, lk = delta.shape
    one_hot = (jnp.asarray(_t5_buckets(delta))[None] == jnp.arange(N_BUCKETS)[:, None, None]).astype(F32)
    bias = jnp.einsum("nh,nqk->hqk", rel_bias.astype(F32), one_hot, precision=lax.Precision.HIGHEST)
    bias = jnp.where(jnp.asarray(valid)[None], bias, NEG_INF)
    return bias.reshape(SWA_HEADS * lq, lk)


def _swa_sink_table(sinks, lq):
    return jnp.repeat(sinks.astype(F32), lq)[:, None]


def _swa_q_perm():
    perm = np.zeros(SWA_HEADS * SWA_HEAD_DIM, np.int32)
    for m in range(SWA_KV_HEADS // 2):
        for g in range(SWA_GROUP):
            for p in range(2):
                src = ((2 * m + p) * SWA_GROUP + g) * SWA_HEAD_DIM
                dst = (4 * m + g) * LANES + p * SWA_HEAD_DIM
                perm[dst:dst + SWA_HEAD_DIM] = np.arange(src, src + SWA_HEAD_DIM)
    return perm


def _rope_tables(pos):
    half = MLA_ROPE // 2
    inv = ROPE_THETA ** (-jnp.arange(half, dtype=F32) / half)
    ang = pos.astype(F32)[:, None] * inv[None, :]
    cos, sin = jnp.cos(ang), jnp.sin(ang)
    zeros = jnp.zeros((pos.shape[0], LANES - MLA_ROPE), F32)
    return (jnp.concatenate([cos, cos, zeros], axis=1), jnp.concatenate([-sin, sin, zeros], axis=1))


def _pad_rope_cols(w):
    half = MLA_ROPE // 2
    z = jnp.zeros((w.shape[0], LANES - MLA_ROPE), w.dtype)
    return (jnp.concatenate([w, z], axis=1),
            jnp.concatenate([w[:, half:], w[:, :half], z], axis=1))


def kernel(x_prompt, x_sample, state_swa_k, state_swa_v, state_conv, cache_mla_kv, cache_mla_kr, page_table,
           c_prompt, c_sample, ada_w, ada_b, norm_pre, norm_post, ffn_w_in, ffn_w_out,
           gm_w_in, gm_ln_g, gm_ln_b, gm_w_s, gm_b_s, gm_w_out,
           swa_w_qkv, swa_w_o, swa_sinks, rel_bias,
           sc_w_in, sc_conv, sc_w_out,
           mla_w_qa, mla_qa_norm, mla_w_qb, mla_w_kva, mla_kva_norm, mla_w_kvb, mla_w_o):
    n_seq, seq_len, d = x_prompt.shape
    n_dec, dec_len, _ = x_sample.shape
    depth = ada_w.shape[0]
    past_len = page_table.shape[1] * PAGE_SIZE
    t_p, t_s = n_seq * seq_len, n_dec * dec_len

    c_all = jnp.concatenate([jnp.repeat(c_sample, dec_len, axis=0), c_prompt,
                             jnp.zeros((8 - n_seq, d), F32)], axis=0)
    mod = _ada(c_all, ada_w, ada_b)
    tm_p, tm_s = 512, 256
    grp_p = _Group(mod[:, t_s:t_s + 8], tm_p, seq_len // tm_p)
    grp_s = _Group(mod, tm_s, None)
    groups = (grp_p, grp_s)

    xs = [x_prompt.reshape(t_p, d), x_sample.reshape(t_s, d)]
    outs = {}
    ffn_w = (ffn_w_in[0, 0].astype(BF16), ffn_w_out[0, 0].astype(BF16))

    def ffn_pair(xs, ffn_w, layer, which):
        nxt = (layer, 1) if which == 0 else (layer + 1, 0)
        next_w = (ffn_w_in, ffn_w_out) + nxt if nxt[0] < depth else None
        sub = 2 * which
        res = _half_ffn(xs[0], grp_p, layer, which, norm_pre[layer, sub], norm_post[layer, sub], *ffn_w,
                        next_w=next_w)
        x_s = _half_ffn(xs[1], grp_s, layer, which, norm_pre[layer, sub], norm_post[layer, sub], *ffn_w)
        if next_w is None:
            return [res, x_s], None
        return [res[0], x_s], (res[1], res[2])

    for i in range(depth):
        kind, j = i % 4, i // 4
        xs, ffn_w = ffn_pair(xs, ffn_w, i, 0)

        if kind == 0:
            w_in = gm_w_in[j].astype(BF16)
            w_out = gm_w_out[j].astype(BF16)
            new = []
            for x, g, chunk, sl, cps in ((xs[0], grp_p, GM_CHUNK, GM_CHUNK, seq_len // GM_CHUNK),
                                         (xs[1], grp_s, GM_CHUNK, dec_len, 1)):
                lc = min(sl, GM_CHUNK)
                m = jnp.tile(gm_w_s[j][:, :lc, :lc], (1, chunk // lc, chunk // lc))
                bias = jnp.broadcast_to(jnp.tile(gm_b_s[j][:, :lc], (1, chunk // lc))[:, :, None],
                                        (GM_GROUPS, chunk, LANES))
                uv = _norm_mod_matmul(x, g, i, norm_pre[i, 1], w_in, act="gelu")
                mixed, st = _gm_core(uv, gm_ln_g[j], gm_ln_b[j], m, bias, chunk, lc, cps)
                new.append(_matmul_residual(mixed, w_out, x, g, i, norm_post[i, 1]))
                outs.setdefault("gm", []).append(st)
            xs = new

        elif kind == 1:
            perm = _swa_q_perm()
            nq = SWA_HEADS * SWA_HEAD_DIM
            nkv = SWA_KV_HEADS * SWA_HEAD_DIM
            w_qkv = jnp.concatenate([swa_w_qkv[j][:, :nq][:, perm], swa_w_qkv[j][:, nq:]], axis=1).astype(BF16)
            w_o = swa_w_o[j][perm, :].astype(BF16)
            w = SWA_WINDOW
            i_q, i_k = np.arange(w), np.arange(2 * w)
            delta = w + i_q[:, None] - i_k[None, :]
            bias_p = _swa_bias_table(rel_bias, delta, (delta >= 0) & (delta < w))
            sink_p = _swa_sink_table(swa_sinks[j], w)
            qkv_p = _norm_mod_matmul(xs[0], grp_p, i, norm_pre[i, 1], w_qkv)
            o_p = _swa_prompt_core(qkv_p, bias_p, sink_p, seq_len)
            x_p = _matmul_residual(o_p, w_o, xs[0], grp_p, i, norm_post[i, 1])
            kv_p = qkv_p.reshape(n_seq, seq_len, -1)[:, seq_len - w:, nq:]
            outs.setdefault("swa_kp", []).append(kv_p[..., :nkv].reshape(n_seq, w, SWA_KV_HEADS, SWA_HEAD_DIM))
            outs.setdefault("swa_vp", []).append(kv_p[..., nkv:].reshape(n_seq, w, SWA_KV_HEADS, SWA_HEAD_DIM))
            lb = state_swa_k.shape[2]
            n_keys = lb + dec_len
            pad = (-n_keys) % 8
            i_q, i_k = np.arange(dec_len), np.arange(n_keys + pad)
            delta = lb + i_q[:, None] - i_k[None, :]
            valid = (delta >= 0) & (delta < w) & (i_k[None, :] < n_keys)
            bias_s = _swa_bias_table(rel_bias, delta, valid)
            sink_s = _swa_sink_table(swa_sinks[j], dec_len)
            qkv_s = _norm_mod_matmul(xs[1], grp_s, i, norm_pre[i, 1], w_qkv).reshape(n_dec, dec_len, -1)
            zpad = jnp.zeros((n_dec, pad, nkv), F32)
            kk = jnp.concatenate([state_swa_k[j].reshape(n_dec, lb, nkv), qkv_s[..., nq:nq + nkv], zpad], axis=1)
            vv = jnp.concatenate([state_swa_v[j].reshape(n_dec, lb, nkv), qkv_s[..., nq + nkv:], zpad], axis=1)
            q_s = qkv_s[..., :nq].reshape(n_dec, dec_len, SWA_KV_HEADS // 2, SWA_GROUP, LANES)
            q_s = q_s.transpose(0, 2, 3, 1, 4).reshape(n_dec, SWA_KV_HEADS // 2, SWA_GROUP * dec_len, LANES)
            o_s = _swa_sample_core(q_s, kk, vv, bias_s, sink_s)
            o_s = o_s.reshape(n_dec, SWA_KV_HEADS // 2, SWA_GROUP, dec_len, LANES).transpose(0, 3, 1, 2, 4)
            x_s = _matmul_residual(o_s.reshape(t_s, nq), w_o, xs[1], grp_s, i, norm_post[i, 1])
            outs.setdefault("swa_ks", []).append(kk[:, n_keys - lb:n_keys].reshape(n_dec, lb, SWA_KV_HEADS, SWA_HEAD_DIM))
            outs.setdefault("swa_vs", []).append(vv[:, n_keys - lb:n_keys].reshape(n_dec, lb, SWA_KV_HEADS, SWA_HEAD_DIM))
            xs = [x_p, x_s]

        elif kind == 2:
            w_in = sc_w_in[j].astype(BF16)
            w_out = sc_w_out[j].astype(BF16)
            c = sc_w_out.shape[1]
            g3_p = _norm_mod_matmul(xs[0], grp_p, i, norm_pre[i, 1], w_in)
            y_p, st_p = _conv_prompt_core(g3_p, sc_conv[j], seq_len)
            x_p = _matmul_residual(y_p, w_out, xs[0], grp_p, i, norm_post[i, 1])
            outs.setdefault("conv_p", []).append(st_p[:, 8 - (CONV_WIDTH - 1):])
            g3_s = _norm_mod_matmul(xs[1], grp_s, i, norm_pre[i, 1], w_in)
            y_s, st_s = _conv_sample_core(g3_s.reshape(n_dec, dec_len * 3 * c),
                                          state_conv[j].reshape(n_dec, (CONV_WIDTH - 1) * c), sc_conv[j], dec_len)
            x_s = _matmul_residual(y_s.reshape(t_s, c), w_out, xs[1], grp_s, i, norm_post[i, 1])
            outs.setdefault("conv_s", []).append(st_s.reshape(n_dec, CONV_WIDTH - 1, c))
            xs = [x_p, x_s]

        else:
            r = MLA_KV_LORA
            kr_pad, kr_sw = _pad_rope_cols(mla_w_kva[j][:, r:])
            w_p = jnp.concatenate([mla_w_qa[j], mla_w_kva[j][:, :r], kr_pad, kr_sw], axis=1).astype(BF16)
            w_qb = mla_w_qb[j].reshape(MLA_Q_LORA, MLA_HEADS, MLA_NOPE + MLA_ROPE)
            w_nope = w_qb[:, :, :MLA_NOPE].reshape(MLA_Q_LORA, -1).astype(BF16)
            rope_pairs = [_pad_rope_cols(w_qb[:, h, MLA_NOPE:]) for h in range(MLA_HEADS)]
            w_rope = jnp.concatenate([p[0] for p in rope_pairs], axis=1).astype(BF16)
            w_rope_sw = jnp.concatenate([p[1] for p in rope_pairs], axis=1).astype(BF16)
            w_kvb = mla_w_kvb[j].reshape(r, MLA_HEADS, MLA_NOPE + MLA_V)
            w_uk_t = w_kvb[:, :, :MLA_NOPE].transpose(1, 2, 0).astype(BF16)
            w_uv = w_kvb[:, :, MLA_NOPE:].transpose(1, 0, 2).astype(BF16)
            w_o = mla_w_o[j].astype(BF16)
            tm = 256
            cos_p, sin_p = _rope_tables(jnp.arange(seq_len))
            proj_p = _norm_mod_matmul(xs[0], grp_p, i, norm_pre[i, 1], w_p, tn=w_p.shape[1])
            q_p, ckv_p, kr_p, kcat_p = _mla_proj(proj_p, cos_p, sin_p, seq_len // tm, mla_qa_norm[j],
                                                 mla_kva_norm[j], w_nope, w_rope, w_rope_sw, w_uk_t, True, tm=tm)
            ol_p = _mla_prompt_attn(q_p, kcat_p, n_seq, seq_len)
            x_p = _mla_out(ol_p, w_uv, w_o, xs[0], grp_p, i, norm_post[i, 1])
            outs.setdefault("mla_kvp", []).append(ckv_p.reshape(n_seq, seq_len, r))
            outs.setdefault("mla_krp", []).append(kr_p[:, :MLA_ROPE].reshape(n_seq, seq_len, MLA_ROPE))
            cos_s, sin_s = _rope_tables(past_len + jnp.arange(dec_len))
            cos_s, sin_s = jnp.tile(cos_s, (tm // dec_len, 1)), jnp.tile(sin_s, (tm // dec_len, 1))
            proj_s = _norm_mod_matmul(xs[1], grp_s, i, norm_pre[i, 1], w_p, tn=w_p.shape[1])
            q_s, ckv_s, kr_s, kcat_s = _mla_proj(proj_s, cos_s, sin_s, 1, mla_qa_norm[j],
                                                 mla_kva_norm[j], w_nope, w_rope, w_rope_sw, w_uk_t, False, tm=tm)
            knew = jnp.concatenate([kcat_s.reshape(n_dec, dec_len, MLA_QK),
                                    jnp.zeros((n_dec, 16 - dec_len, MLA_QK), BF16)], axis=1)
            ol_s = _mla_sample_attn(q_s.reshape(t_s * MLA_HEADS, MLA_QK), knew, cache_mla_kv,
                                    jnp.swapaxes(cache_mla_kr, 2, 3), page_table, j, dec_len)
            x_s = _mla_out(ol_s.reshape(t_s, MLA_HEADS * r), w_uv, w_o, xs[1], grp_s, i, norm_post[i, 1])
            outs.setdefault("mla_kvs", []).append(ckv_s.reshape(n_dec, dec_len, r))
            outs.setdefault("mla_krs", []).append(kr_s[:, :MLA_ROPE].reshape(n_dec, dec_len, MLA_ROPE))
            xs = [x_p, x_s]

        xs, ffn_w = ffn_pair(xs, ffn_w, i, 1)

    gm_p, gm_s = outs["gm"][0::2], outs["gm"][1::2]
    return (xs[0].reshape(n_seq, seq_len, d), xs[1].reshape(n_dec, dec_len, d),
            jnp.stack(gm_p),
            jnp.stack([s.reshape(n_dec, dec_len, -1) for s in gm_s]),
            jnp.stack(outs["swa_kp"]), jnp.stack(outs["swa_vp"]),
            jnp.stack(outs["swa_ks"]), jnp.stack(outs["swa_vs"]),
            jnp.stack(outs["conv_p"]), jnp.stack(outs["conv_s"]),
            jnp.stack(outs["mla_kvp"]), jnp.stack(outs["mla_krp"]),
            jnp.stack(outs["mla_kvs"]), jnp.stack(outs["mla_krs"]))
```

```python
import functools
import math

import numpy as np
import jax
import jax.numpy as jnp
from jax import lax
from jax.experimental import pallas as pl
from jax.experimental.pallas import tpu as pltpu

F32 = jnp.float32
BF16 = jnp.bfloat16

VMEM_LIMIT_BYTES = 56 * 1024 * 1024
LANES = 128

RMS_EPS = 1e-6
NEG_INF = -1e30
FFN_RES_W = 0.5

D_MODEL = 2048
GM_GROUPS = 8
GM_CHUNK = 128
SWA_WINDOW = 128
SWA_HEAD_DIM = 64
SWA_HEADS = 32
SWA_KV_HEADS = 8
SWA_GROUP = 4
SWA_SCALE = SWA_HEAD_DIM ** -0.5
N_BUCKETS = 32
BUCKET_MAX_DIST = 128
CONV_WIDTH = 3
MLA_HEADS = 16
MLA_Q_LORA = 512
MLA_KV_LORA = 512
MLA_NOPE = 128
MLA_ROPE = 64
MLA_V = 128
MLA_SCALE = (MLA_NOPE + MLA_ROPE) ** -0.5
MLA_QK = MLA_KV_LORA + LANES
ROPE_THETA = 10000.0
PAGE_SIZE = 128
PAGES_PER_STEP = 32


def _params(n_axes):
    return pltpu.CompilerParams(dimension_semantics=("arbitrary",) * n_axes,
                                vmem_limit_bytes=VMEM_LIMIT_BYTES)


def _rms(x, g):
    return x * lax.rsqrt(jnp.mean(x * x, axis=-1, keepdims=True) + RMS_EPS) * g


def _lane_tile(x, n):
    return x if n == 1 else jnp.concatenate([x] * n, axis=1)


class _Group:
    def __init__(self, mod, tm, tiles_per_mod_row):
        self.mod = mod
        self.tm = tm
        self.tiles_per_mod_row = tiles_per_mod_row

    def mod_spec(self, layer, col):
        if self.tiles_per_mod_row is None:
            return pl.BlockSpec((None, self.tm, D_MODEL), lambda i, *_: (layer, i, col))
        return pl.BlockSpec((None, 8, D_MODEL), lambda i, *_: (layer, 0, col))

    def row(self):
        if self.tiles_per_mod_row is None:
            return None
        return pl.program_id(0) // self.tiles_per_mod_row

    @staticmethod
    def read(ref, row):
        return ref[...] if row is None else ref[pl.ds(row, 1), :]


ROW_CHUNK = 16


def _for_row_chunks(n_rows, body, unroll):
    def step(c, carry):
        body(pl.ds(pl.multiple_of(c * ROW_CHUNK, ROW_CHUNK), ROW_CHUNK))
        return carry
    lax.fori_loop(0, n_rows // ROW_CHUNK, step, 0, unroll=unroll)


def _row_rms_scale(val_ref, stat_ref):
    def body(rs):
        v = val_ref[rs, :]
        scale = lax.rsqrt(jnp.mean(v * v, axis=-1, keepdims=True) + RMS_EPS)
        stat_ref[rs, :] = jnp.broadcast_to(scale, (ROW_CHUNK, LANES))

    _for_row_chunks(val_ref.shape[0], body, unroll=8)


def _ada_in_rows(x_ref, gpre_ref, sc_ref, sh_ref, row, stat_ref, h_ref):
    n_tile = x_ref.shape[1] // LANES
    _row_rms_scale(x_ref, stat_ref)
    if row is not None:
        gain = gpre_ref[...] * (1.0 + sc_ref[pl.ds(row, 1), :])
        shift = sh_ref[pl.ds(row, 1), :]

    def body(rs):
        xn = x_ref[rs, :] * _lane_tile(stat_ref[rs, :], n_tile)
        if row is None:
            h = xn * (gpre_ref[...] * (1.0 + sc_ref[rs, :])) + sh_ref[rs, :]
        else:
            h = xn * gain + shift
        h_ref[rs, :] = h.astype(h_ref.dtype)

    _for_row_chunks(x_ref.shape[0], body, unroll=2)


def _ada_out_rows(x_ref, val_ref, gpost_ref, gt_ref, row, res_w, stat_ref, o_ref):
    n_tile = x_ref.shape[1] // LANES
    _row_rms_scale(val_ref, stat_ref)
    if row is not None:
        gain = gpost_ref[...] * (res_w * gt_ref[pl.ds(row, 1), :])

    def body(rs):
        vn = val_ref[rs, :] * _lane_tile(stat_ref[rs, :], n_tile)
        if row is None:
            o_ref[rs, :] = x_ref[rs, :] + vn * (gpost_ref[...] * (res_w * gt_ref[rs, :]))
        else:
            o_ref[rs, :] = x_ref[rs, :] + vn * gain

    _for_row_chunks(x_ref.shape[0], body, unroll=2)


def _shr(x, divisor):
    shift = divisor.bit_length() - 1
    assert 1 << shift == divisor
    return lax.shift_right_logical(x, shift)


def _ada_kernel(c_ref, w_ref, b_ref, o_ref, cs_sc):
    @pl.when((pl.program_id(0) == 0) & (pl.program_id(1) == 0))
    def _():
        c = c_ref[...]
        cs_sc[...] = (c * jax.nn.sigmoid(c)).astype(BF16)

    o_ref[...] = jnp.dot(cs_sc[...], w_ref[...].astype(BF16),
                         preferred_element_type=F32) + b_ref[...]


def _ada(c_all, ada_w, ada_b, tn=1024):
    n_layers, d, n = ada_w.shape
    rows = c_all.shape[0]
    return pl.pallas_call(
        _ada_kernel,
        grid=(n_layers, n // tn),
        in_specs=[pl.BlockSpec((rows, d), lambda l, j: (0, 0)),
                  pl.BlockSpec((None, d, tn), lambda l, j: (l, 0, j)),
                  pl.BlockSpec((None, 1, tn), lambda l, j: (l, 0, j))],
        out_specs=pl.BlockSpec((None, rows, tn), lambda l, j: (l, 0, j)),
        out_shape=jax.ShapeDtypeStruct((n_layers, rows, n), F32),
        scratch_shapes=[pltpu.VMEM((rows, d), BF16)],
        compiler_params=_params(2),
        name="ada_modulation",
    )(c_all, ada_w, ada_b.reshape(n_layers, 1, n))


def _ffn_kernel(x_ref, sh_ref, sc_ref, gt_ref, gpre_ref, gpost_ref, wg_ref, wu_ref, wo_ref, *rest, grp, n_f):
    if len(rest) == 8:
        nwi_ref, nwo_ref, o_ref, nwi_b_ref, nwo_b_ref, h_sc, acc_sc, stat_sc = rest
        nwi_b_ref[...] = nwi_ref[...].astype(BF16)
        nwo_b_ref[...] = nwo_ref[...].astype(BF16)
    else:
        o_ref, h_sc, acc_sc, stat_sc = rest
    j = pl.program_id(1)
    row = grp.row()

    @pl.when(j == 0)
    def _():
        _ada_in_rows(x_ref, gpre_ref, sc_ref, sh_ref, row, stat_sc, h_sc)
        acc_sc[...] = jnp.zeros_like(acc_sc)

    h = h_sc[...]
    tf = wg_ref.shape[1]
    acts = []
    for c in range(2):
        cols = slice(c * tf // 2, (c + 1) * tf // 2)
        g = jnp.dot(h, wg_ref[:, cols], preferred_element_type=F32)
        u = jnp.dot(h, wu_ref[:, cols], preferred_element_type=F32)
        acts.append((g * jax.nn.sigmoid(g) * u).astype(BF16))
    acc_sc[...] += jnp.dot(jnp.concatenate(acts, axis=1), wo_ref[...], preferred_element_type=F32)

    @pl.when(j == n_f - 1)
    def _():
        _ada_out_rows(x_ref, acc_sc, gpost_ref, gt_ref, row, FFN_RES_W, stat_sc, o_ref)


def _half_ffn(x, grp, layer, which, g_pre, g_post, w_in, w_out, next_w=None, tf=512):
    t, d = x.shape
    f = w_out.shape[0]
    tm, n_f = grp.tm, f // tf
    n_i = t // tm
    sub = 2 * which
    kern = functools.partial(_ffn_kernel, grp=grp, n_f=n_f)
    in_specs = [pl.BlockSpec((tm, d), lambda i, j: (i, 0)),
                grp.mod_spec(layer, sub * 3 + 0),
                grp.mod_spec(layer, sub * 3 + 1),
                grp.mod_spec(layer, sub * 3 + 2),
                pl.BlockSpec((1, d), lambda i, j: (0, 0)),
                pl.BlockSpec((1, d), lambda i, j: (0, 0)),
                pl.BlockSpec((d, tf), lambda i, j: (0, j)),
                pl.BlockSpec((d, tf), lambda i, j: (0, n_f + j)),
                pl.BlockSpec((tf, d), lambda i, j: (j, 0))]
    args = [x, grp.mod, grp.mod, grp.mod, g_pre.reshape(1, d), g_post.reshape(1, d), w_in, w_in, w_out]
    out_specs = [pl.BlockSpec((tm, d), lambda i, j: (i, 0))]
    out_shape = [jax.ShapeDtypeStruct((t, d), F32)]
    if next_w is not None:
        nw_in, nw_out, nl, nwh = next_w
        assert n_i % 2 == 0
        bi = (2 * d // n_i, 2 * f // n_f)
        bo = (f // n_f, 2 * d // n_i)

        def blk(i, j):
            s = (i * n_f + j) // 2
            return s // n_f, s % n_f

        in_specs += [pl.BlockSpec((None, None) + bi, lambda i, j: (nl, nwh) + blk(i, j)),
                     pl.BlockSpec((None, None) + bo, lambda i, j: (nl, nwh) + blk(i, j)[::-1])]
        args += [nw_in, nw_out]
        out_specs += [pl.BlockSpec(bi, blk), pl.BlockSpec(bo, lambda i, j: blk(i, j)[::-1])]
        out_shape += [jax.ShapeDtypeStruct((d, 2 * f), BF16), jax.ShapeDtypeStruct((f, d), BF16)]
    res = pl.pallas_call(
        kern,
        grid=(n_i, n_f),
        in_specs=in_specs,
        out_specs=out_specs,
        out_shape=out_shape,
        scratch_shapes=[pltpu.VMEM((tm, d), BF16), pltpu.VMEM((tm, d), F32), pltpu.VMEM((tm, LANES), F32)],
        compiler_params=_params(2),
        name="half_ffn",
    )(*args)
    return res if next_w is not None else res[0]


def _nmm_kernel(x_ref, sh_ref, sc_ref, gpre_ref, w_ref, o_ref, h_sc, stat_sc, *, grp, act):
    row = grp.row()

    @pl.when(pl.program_id(1) == 0)
    def _():
        _ada_in_rows(x_ref, gpre_ref, sc_ref, sh_ref, row, stat_sc, h_sc)

    y = jnp.dot(h_sc[...], w_ref[...], preferred_element_type=F32)
    if act == "gelu":
        y = jax.nn.gelu(y, approximate=True)
    o_ref[...] = y.astype(o_ref.dtype)


def _norm_mod_matmul(x, grp, layer, g_pre, w, act=None, tn=1024, out_dtype=F32):
    t, d = x.shape
    n = w.shape[1]
    tn = min(tn, n)
    tm = grp.tm
    kern = functools.partial(_nmm_kernel, grp=grp, act=act)
    return pl.pallas_call(
        kern,
        grid=(t // tm, n // tn),
        in_specs=[pl.BlockSpec((tm, d), lambda i, j: (i, 0)),
                  grp.mod_spec(layer, 3 + 0),
                  grp.mod_spec(layer, 3 + 1),
                  pl.BlockSpec((1, d), lambda i, j: (0, 0)),
                  pl.BlockSpec((d, tn), lambda i, j: (0, j))],
        out_specs=pl.BlockSpec((tm, tn), lambda i, j: (i, j)),
        out_shape=jax.ShapeDtypeStruct((t, n), out_dtype),
        scratch_shapes=[pltpu.VMEM((tm, d), BF16), pltpu.VMEM((tm, LANES), F32)],
        compiler_params=_params(2),
        name="norm_mod_matmul",
    )(x, grp.mod, grp.mod, g_pre.reshape(1, d), w)


def _mres_kernel(a_ref, w_ref, x_ref, gt_ref, gpost_ref, o_ref, acc_sc, stat_sc, *, grp, n_k):
    k = pl.program_id(1)
    row = grp.row()

    @pl.when(k == 0)
    def _():
        acc_sc[...] = jnp.zeros_like(acc_sc)

    acc_sc[...] += jnp.dot(a_ref[...].astype(BF16), w_ref[...], preferred_element_type=F32)

    @pl.when(k == n_k - 1)
    def _():
        _ada_out_rows(x_ref, acc_sc, gpost_ref, gt_ref, row, 1.0, stat_sc, o_ref)


def _matmul_residual(a, w, x, grp, layer, g_post, tk=1024):
    t, kdim = a.shape
    d = w.shape[1]
    tm, n_k = grp.tm, kdim // tk
    kern = functools.partial(_mres_kernel, grp=grp, n_k=n_k)
    return pl.pallas_call(
        kern,
        grid=(t // tm, n_k),
        in_specs=[pl.BlockSpec((tm, tk), lambda i, k: (i, k)),
                  pl.BlockSpec((tk, d), lambda i, k: (k, 0)),
                  pl.BlockSpec((tm, d), lambda i, k: (i, 0)),
                  grp.mod_spec(layer, 3 + 2),
                  pl.BlockSpec((1, d), lambda i, k: (0, 0))],
        out_specs=pl.BlockSpec((tm, d), lambda i, k: (i, 0)),
        out_shape=jax.ShapeDtypeStruct((t, d), F32),
        scratch_shapes=[pltpu.VMEM((tm, d), F32), pltpu.VMEM((tm, LANES), F32)],
        compiler_params=_params(2),
        name="matmul_residual",
    )(a, w, x, grp.mod, g_post.reshape(1, d))


def _gm_kernel(u_ref, v_ref, lng_ref, lnb_ref, m_ref, b_ref, o_ref, st_ref, *, seq_len):
    v = v_ref[...].astype(F32)
    vc = v - jnp.mean(v, axis=-1, keepdims=True)
    vn = vc * lax.rsqrt(jnp.mean(vc * vc, axis=-1, keepdims=True) + RMS_EPS) * lng_ref[...] + lnb_ref[...]
    st_ref[...] = vn
    vnb = vn.astype(BF16)
    c = v.shape[0]
    gw = v.shape[1] // GM_GROUPS
    row = lax.broadcasted_iota(jnp.int32, (c, c), 0)
    col = lax.broadcasted_iota(jnp.int32, (c, c), 1)
    keep = (col <= row) & (_shr(row, seq_len) == _shr(col, seq_len))
    for g in range(GM_GROUPS):
        mg = jnp.where(keep, m_ref[g], 0.0).astype(BF16)
        mixed = jnp.dot(mg, vnb[:, g * gw:(g + 1) * gw], preferred_element_type=F32)
        mixed = mixed + _lane_tile(b_ref[g], gw // LANES)
        o_ref[:, g * gw:(g + 1) * gw] = (u_ref[:, g * gw:(g + 1) * gw].astype(F32) * mixed).astype(o_ref.dtype)


def _gm_core(uv, ln_g, ln_b, m, bias, chunk, seq_len, chunks_per_state):
    t = uv.shape[0]
    w = uv.shape[1] // 2
    n_chunks = t // chunk
    n_states = n_chunks // chunks_per_state
    kern = functools.partial(_gm_kernel, seq_len=seq_len)
    return pl.pallas_call(
        kern,
        grid=(n_chunks,),
        in_specs=[pl.BlockSpec((chunk, w), lambda c: (c, 0)),
                  pl.BlockSpec((chunk, w), lambda c: (c, 1)),
                  pl.BlockSpec((1, w), lambda c: (0, 0)),
                  pl.BlockSpec((1, w), lambda c: (0, 0)),
                  pl.BlockSpec((GM_GROUPS, chunk, chunk), lambda c: (0, 0, 0)),
                  pl.BlockSpec((GM_GROUPS, chunk, LANES), lambda c: (0, 0, 0))],
        out_specs=[pl.BlockSpec((chunk, w), lambda c: (c, 0)),
                   pl.BlockSpec((None, chunk, w), lambda c: (c // chunks_per_state, 0, 0))],
        out_shape=[jax.ShapeDtypeStruct((t, w), BF16),
                   jax.ShapeDtypeStruct((n_states, chunk, w), F32)],
        compiler_params=_params(1),
        name="gm_core",
    )(uv, uv, ln_g.reshape(1, w), ln_b.reshape(1, w), m, bias)


def _softmax_sink(s, sink):
    mx = jnp.maximum(jnp.max(s, axis=-1, keepdims=True), sink)
    p = jnp.exp(s - mx)
    denom = jnp.sum(p, axis=-1, keepdims=True) + jnp.exp(sink - mx)
    return p / denom


def _half_lane_mask(rows, parity):
    lane = lax.broadcasted_iota(jnp.int32, (rows, LANES), 1)
    return (lane >= SWA_HEAD_DIM) if parity else (lane < SWA_HEAD_DIM)


def _swa_attend(q_blocks, k, v, bias_ref, sink_ref, masked_keys=None):
    rows = q_blocks[0].shape[0]
    n_keys = k.shape[0]
    logits, values = [], []
    for m in range(SWA_KV_HEADS // 2):
        k2 = k[:, m * LANES:(m + 1) * LANES]
        v2 = v[:, m * LANES:(m + 1) * LANES]
        for parity in range(2):
            keep = _half_lane_mask(n_keys, parity)
            kx = jnp.where(keep, k2, 0.0).astype(BF16)
            values.append(jnp.where(keep, v2, 0.0).astype(BF16))
            logits.append(lax.dot_general(q_blocks[m], kx, (((1,), (1,)), ((), ())), preferred_element_type=F32))
    s = jnp.concatenate(logits, axis=0) + bias_ref[...]
    if masked_keys is not None:
        key_col = lax.broadcasted_iota(jnp.int32, s.shape, 1)
        s = jnp.where(key_col < masked_keys, NEG_INF, s)
    p = _softmax_sink(s, sink_ref[...]).astype(BF16)
    outs = []
    for m in range(SWA_KV_HEADS // 2):
        o_m = None
        for parity in range(2):
            idx = 2 * m + parity
            o_p = jnp.dot(p[idx * rows:(idx + 1) * rows], values[idx], preferred_element_type=F32)
            o_m = o_p if o_m is None else o_m + o_p
        outs.append(o_m)
    return outs


def _swa_prompt_kernel(q_ref, kp_ref, ko_ref, vp_ref, vo_ref, bias_ref, sink_ref, o_ref, *, blocks_per_seq):
    w = SWA_WINDOW
    first = (pl.program_id(0) % blocks_per_seq) == 0
    q = (q_ref[...] * SWA_SCALE).astype(BF16)
    k = jnp.concatenate([kp_ref[...], ko_ref[...]], axis=0)
    v = jnp.concatenate([vp_ref[...], vo_ref[...]], axis=0)
    q_blocks = [jnp.concatenate([q[:, (4 * m + g) * LANES:(4 * m + g + 1) * LANES] for g in range(SWA_GROUP)],
                                axis=0) for m in range(SWA_KV_HEADS // 2)]
    outs = _swa_attend(q_blocks, k, v, bias_ref, sink_ref, masked_keys=jnp.where(first, w, 0))
    for m, o_m in enumerate(outs):
        for g in range(SWA_GROUP):
            o_ref[:, (4 * m + g) * LANES:(4 * m + g + 1) * LANES] = o_m[g * w:(g + 1) * w].astype(o_ref.dtype)


def _swa_prompt_core(qkv, bias, sink, seq_len):
    t = qkv.shape[0]
    w = SWA_WINDOW
    nq = SWA_HEADS * SWA_HEAD_DIM
    nkv = SWA_KV_HEADS * SWA_HEAD_DIM
    bps = seq_len // w
    kcol, vcol = nq // nkv, nq // nkv + 1

    def prev(i):
        return jnp.maximum(i - 1, 0)

    kern = functools.partial(_swa_prompt_kernel, blocks_per_seq=bps)
    return pl.pallas_call(
        kern,
        grid=(t // w,),
        in_specs=[pl.BlockSpec((w, nq), lambda i: (i, 0)),
                  pl.BlockSpec((w, nkv), lambda i: (prev(i), kcol)),
                  pl.BlockSpec((w, nkv), lambda i: (i, kcol)),
                  pl.BlockSpec((w, nkv), lambda i: (prev(i), vcol)),
                  pl.BlockSpec((w, nkv), lambda i: (i, vcol)),
                  pl.BlockSpec(bias.shape, lambda i: (0, 0)),
                  pl.BlockSpec(sink.shape, lambda i: (0, 0))],
        out_specs=pl.BlockSpec((w, nq), lambda i: (i, 0)),
        out_shape=jax.ShapeDtypeStruct((t, nq), BF16),
        compiler_params=_params(1),
        name="swa_prompt_core",
    )(qkv, qkv, qkv, qkv, qkv, bias, sink)


def _swa_sample_kernel(q_ref, k_ref, v_ref, bias_ref, sink_ref, o_ref, *, bb):
    def body(b, carry):
        q_blocks = [(q_ref[b, m] * SWA_SCALE).astype(BF16) for m in range(SWA_KV_HEADS // 2)]
        outs = _swa_attend(q_blocks, k_ref[b], v_ref[b], bias_ref, sink_ref)
        for m, o_m in enumerate(outs):
            o_ref[b, m] = o_m.astype(o_ref.dtype)
        return carry

    lax.fori_loop(0, bb, body, 0, unroll=2)


def _swa_sample_core(q, kk, vv, bias, sink, bb=8):
    b, n_pair, rows, _ = q.shape
    n_keys = kk.shape[1]
    nkv = kk.shape[2]
    kern = functools.partial(_swa_sample_kernel, bb=bb)
    return pl.pallas_call(
        kern,
        grid=(b // bb,),
        in_specs=[pl.BlockSpec((bb, n_pair, rows, LANES), lambda i: (i, 0, 0, 0)),
                  pl.BlockSpec((bb, n_keys, nkv), lambda i: (i, 0, 0)),
                  pl.BlockSpec((bb, n_keys, nkv), lambda i: (i, 0, 0)),
                  pl.BlockSpec(bias.shape, lambda i: (0, 0)),
                  pl.BlockSpec(sink.shape, lambda i: (0, 0))],
        out_specs=pl.BlockSpec((bb, n_pair, rows, LANES), lambda i: (i, 0, 0, 0)),
        out_shape=jax.ShapeDtypeStruct(q.shape, BF16),
        compiler_params=_params(1),
        name="swa_sample_core",
    )(q, kk, vv, bias, sink)


def _conv_prompt_kernel(gb_ref, gc_ref, z_ref, hc_ref, hz_ref, cw_ref, o_ref, st_ref, zp_sc, *, tiles_per_seq):
    tm = gb_ref.shape[0]
    hr = hc_ref.shape[0]
    first = (pl.program_id(0) % tiles_per_seq) == 0
    zz = gc_ref[...].astype(F32) * z_ref[...].astype(F32)
    halo = jnp.where(first, 0.0, hc_ref[...].astype(F32) * hz_ref[...].astype(F32))
    zp_sc[0:8, :] = halo[hr - 8:hr]
    zp_sc[8:8 + tm, :] = zz
    y = cw_ref[2:3, :] * zz + cw_ref[1:2, :] * zp_sc[7:7 + tm, :] + cw_ref[0:1, :] * zp_sc[6:6 + tm, :]
    o_ref[...] = (gb_ref[...].astype(F32) * y).astype(o_ref.dtype)
    st_ref[...] = zz[tm - 8:tm]


def _conv_prompt_core(g3, conv_w, seq_len, tm=256):
    t = g3.shape[0]
    c = g3.shape[1] // 3
    tps = seq_len // tm
    n_seq = t // seq_len

    hr = 16

    def halo(col):
        return pl.BlockSpec((hr, c), lambda i: (jnp.maximum(i * (tm // hr) - 1, 0), col))

    kern = functools.partial(_conv_prompt_kernel, tiles_per_seq=tps)
    return pl.pallas_call(
        kern,
        grid=(t // tm,),
        in_specs=[pl.BlockSpec((tm, c), lambda i: (i, 0)),
                  pl.BlockSpec((tm, c), lambda i: (i, 1)),
                  pl.BlockSpec((tm, c), lambda i: (i, 2)),
                  halo(1), halo(2),
                  pl.BlockSpec((CONV_WIDTH, c), lambda i: (0, 0))],
        out_specs=[pl.BlockSpec((tm, c), lambda i: (i, 0)),
                   pl.BlockSpec((None, 8, c), lambda i: (i // tps, 0, 0))],
        out_shape=[jax.ShapeDtypeStruct((t, c), BF16),
                   jax.ShapeDtypeStruct((n_seq, 8, c), F32)],
        scratch_shapes=[pltpu.VMEM((tm + 8, c), F32)],
        compiler_params=_params(1),
        name="conv_prompt_core",
    )(g3, g3, g3, g3, g3, conv_w)


def _conv_sample_kernel(g_ref, prev_ref, cw_ref, o_ref, st_ref, *, seq_len, c):
    zz = [prev_ref[:, 0:c], prev_ref[:, c:2 * c]]
    for t in range(seq_len):
        base = t * 3 * c
        zz.append(g_ref[:, base + c:base + 2 * c].astype(F32) * g_ref[:, base + 2 * c:base + 3 * c].astype(F32))
    for t in range(seq_len):
        y = cw_ref[2:3, :] * zz[t + 2] + cw_ref[1:2, :] * zz[t + 1] + cw_ref[0:1, :] * zz[t]
        o_ref[:, t * c:(t + 1) * c] = (g_ref[:, t * 3 * c:t * 3 * c + c].astype(F32) * y).astype(o_ref.dtype)
    st_ref[:, 0:c] = zz[seq_len]
    st_ref[:, c:2 * c] = zz[seq_len + 1]


def _conv_sample_core(g3, prev, conv_w, seq_len):
    b = g3.shape[0]
    c = g3.shape[1] // (3 * seq_len)
    kern = functools.partial(_conv_sample_kernel, seq_len=seq_len, c=c)
    return pl.pallas_call(
        kern,
        grid=(1,),
        in_specs=[pl.BlockSpec(g3.shape, lambda i: (0, 0)),
                  pl.BlockSpec(prev.shape, lambda i: (0, 0)),
                  pl.BlockSpec((CONV_WIDTH, c), lambda i: (0, 0))],
        out_specs=[pl.BlockSpec((b, seq_len * c), lambda i: (0, 0)),
                   pl.BlockSpec((b, 2 * c), lambda i: (0, 0))],
        out_shape=[jax.ShapeDtypeStruct((b, seq_len * c), BF16),
                   jax.ShapeDtypeStruct((b, 2 * c), F32)],
        compiler_params=_params(1),
        name="conv_sample_core",
    )(g3, prev, conv_w)


def _mla_proj_kernel(p_ref, cos_ref, sin_ref, qan_ref, kvn_ref, wn_ref, wr_ref, wrs_ref, wuk_ref,
                     q_ref, ckv_ref, kr_ref, kcat_ref, *, head_major):
    r = MLA_KV_LORA
    cos = cos_ref[...]
    sin = sin_ref[...]
    qa = _rms(p_ref[:, 0:MLA_Q_LORA], qan_ref[...]).astype(BF16)
    qn = jnp.dot(qa, wn_ref[...], preferred_element_type=F32)
    qr = jnp.dot(qa, wr_ref[...], preferred_element_type=F32)
    qrs = jnp.dot(qa, wrs_ref[...], preferred_element_type=F32)
    for h in range(MLA_HEADS):
        sl = slice(h * LANES, (h + 1) * LANES)
        q_lat = jnp.dot(qn[:, sl].astype(BF16), wuk_ref[h], preferred_element_type=F32)
        q_rope = qr[:, sl] * cos + qrs[:, sl] * sin
        q_lat = (q_lat * MLA_SCALE).astype(q_ref.dtype)
        q_rope = (q_rope * MLA_SCALE).astype(q_ref.dtype)
        if head_major:
            q_ref[h, :, 0:r] = q_lat
            q_ref[h, :, r:MLA_QK] = q_rope
        else:
            q_ref[:, h * MLA_QK:h * MLA_QK + r] = q_lat
            q_ref[:, h * MLA_QK + r:(h + 1) * MLA_QK] = q_rope
    off = MLA_Q_LORA
    ckv = _rms(p_ref[:, off:off + r], kvn_ref[...])
    kr = p_ref[:, off + r:off + r + LANES] * cos + p_ref[:, off + r + LANES:off + r + 2 * LANES] * sin
    ckv_ref[...] = ckv
    kr_ref[...] = kr
    kcat_ref[:, 0:r] = ckv.astype(kcat_ref.dtype)
    kcat_ref[:, r:r + LANES] = kr.astype(kcat_ref.dtype)


def _mla_proj(p, cos, sin, pos_blocks, qa_norm, kva_norm, w_nope, w_rope, w_rope_sw, w_uk_t, head_major, tm=256):
    t = p.shape[0]
    hq = MLA_HEADS * MLA_QK
    const2 = lambda i: (0, 0)
    if head_major:
        q_spec = pl.BlockSpec((MLA_HEADS, tm, MLA_QK), lambda i: (0, i, 0))
        q_shape = jax.ShapeDtypeStruct((MLA_HEADS, t, MLA_QK), BF16)
    else:
        q_spec = pl.BlockSpec((tm, hq), lambda i: (i, 0))
        q_shape = jax.ShapeDtypeStruct((t, hq), BF16)
    return pl.pallas_call(
        functools.partial(_mla_proj_kernel, head_major=head_major),
        grid=(t // tm,),
        in_specs=[pl.BlockSpec((tm, p.shape[1]), lambda i: (i, 0)),
                  pl.BlockSpec((tm, LANES), lambda i: (i % pos_blocks, 0)),
                  pl.BlockSpec((tm, LANES), lambda i: (i % pos_blocks, 0)),
                  pl.BlockSpec((1, MLA_Q_LORA), const2),
                  pl.BlockSpec((1, MLA_KV_LORA), const2),
                  pl.BlockSpec(w_nope.shape, const2),
                  pl.BlockSpec(w_rope.shape, const2),
                  pl.BlockSpec(w_rope_sw.shape, const2),
                  pl.BlockSpec(w_uk_t.shape, lambda i: (0, 0, 0))],
        out_specs=[q_spec,
                   pl.BlockSpec((tm, MLA_KV_LORA), lambda i: (i, 0)),
                   pl.BlockSpec((tm, LANES), lambda i: (i, 0)),
                   pl.BlockSpec((tm, MLA_QK), lambda i: (i, 0))],
        out_shape=[q_shape,
                   jax.ShapeDtypeStruct((t, MLA_KV_LORA), F32),
                   jax.ShapeDtypeStruct((t, LANES), F32),
                   jax.ShapeDtypeStruct((t, MLA_QK), BF16)],
        compiler_params=_params(1),
        name="mla_proj",
    )(p, cos, sin, qa_norm.reshape(1, -1), kva_norm.reshape(1, -1), w_nope, w_rope, w_rope_sw, w_uk_t)


def _flash_update(s, v, m_ref, l_ref, acc_ref):
    m_prev = m_ref[...]
    m_new = jnp.maximum(m_prev, jnp.max(s, axis=1, keepdims=True))
    alpha = jnp.exp(m_prev - m_new)
    p = jnp.exp(s - _lane_tile(m_new, s.shape[1] // LANES))
    l_ref[...] = alpha * l_ref[...] + jnp.sum(p, axis=1, keepdims=True)
    acc_ref[...] = acc_ref[...] * _lane_tile(alpha, acc_ref.shape[1] // LANES) + jnp.dot(
        p.astype(BF16), v, preferred_element_type=F32)
    m_ref[...] = m_new


def _flash_init(m_sc, l_sc, acc_sc):
    m_sc[...] = jnp.full_like(m_sc, NEG_INF)
    l_sc[...] = jnp.zeros_like(l_sc)
    acc_sc[...] = jnp.zeros_like(acc_sc)


def _flash_result(l_ref, acc_ref):
    return acc_ref[...] / _lane_tile(l_ref[...], acc_ref.shape[1] // LANES)


def _mla_prompt_kernel(b_tab, qi_tab, ki_tab, q_ref, k_ref, o_ref, m_sc, l_sc, acc_sc, *, tq, tk, hb):
    step = pl.program_id(0)
    qi = qi_tab[step]
    ki = ki_tab[step]
    last = ki == (qi * tq) // tk
    rows = hb * tq

    @pl.when(ki == 0)
    def _():
        _flash_init(m_sc, l_sc, acc_sc)

    def run(masked):
        k = k_ref[...]
        v = k[:, 0:MLA_KV_LORA]
        if masked:
            q_pos = qi * tq + (lax.broadcasted_iota(jnp.int32, (rows, tk), 0) & (tq - 1))
            k_pos = ki * tk + lax.broadcasted_iota(jnp.int32, (rows, tk), 1)
            visible = k_pos <= q_pos
        for g in range(MLA_HEADS // hb):
            sl = pl.ds(g * rows, rows)
            q = q_ref[g * hb:(g + 1) * hb].reshape(rows, MLA_QK)
            s = lax.dot_general(q, k, (((1,), (1,)), ((), ())), preferred_element_type=F32)
            if masked:
                s = jnp.where(visible, s, NEG_INF)
            _flash_update(s, v, m_sc.at[sl], l_sc.at[sl], acc_sc.at[sl])
            if masked:
                o = _flash_result(l_sc.at[sl], acc_sc.at[sl])
                o_ref[g * hb:(g + 1) * hb] = o.reshape(hb, tq, MLA_KV_LORA).astype(o_ref.dtype)

    pl.when(last)(lambda: run(True))
    pl.when(jnp.logical_not(last))(lambda: run(False))


def _mla_prompt_attn(q3, kcat, n_seq, seq_len, tq=256, tk=512, hb=2):
    assert tq & (tq - 1) == 0
    rows = tq * MLA_HEADS
    nq, nk = seq_len // tq, seq_len // tk
    steps = [(b, qi, ki) for b in range(n_seq) for qi in range(nq) for ki in range((qi * tq) // tk + 1)]
    b_tab, qi_tab, ki_tab = (jnp.asarray(np.array(col, np.int32)) for col in zip(*steps))
    kern = functools.partial(_mla_prompt_kernel, tq=tq, tk=tk, hb=hb)
    grid_spec = pltpu.PrefetchScalarGridSpec(
        num_scalar_prefetch=3,
        grid=(len(steps),),
        in_specs=[pl.BlockSpec((MLA_HEADS, tq, MLA_QK), lambda s, bt, qt, kt: (0, bt[s] * nq + qt[s], 0)),
                  pl.BlockSpec((tk, MLA_QK), lambda s, bt, qt, kt: (bt[s] * nk + kt[s], 0))],
        out_specs=pl.BlockSpec((MLA_HEADS, tq, MLA_KV_LORA), lambda s, bt, qt, kt: (0, bt[s] * nq + qt[s], 0)),
        scratch_shapes=[pltpu.VMEM((rows, LANES), F32), pltpu.VMEM((rows, LANES), F32),
                        pltpu.VMEM((rows, MLA_KV_LORA), F32)])
    return pl.pallas_call(
        kern,
        grid_spec=grid_spec,
        out_shape=jax.ShapeDtypeStruct((MLA_HEADS, q3.shape[1], MLA_KV_LORA), BF16),
        compiler_params=_params(1),
        name="mla_prompt_attn",
    )(b_tab, qi_tab, ki_tab, q3, kcat)


def _mla_sample_kernel(pt_ref, q_ref, knew_ref, *refs, n_groups, seq_len):
    kv_refs = refs[:PAGES_PER_STEP]
    krt_refs = refs[PAGES_PER_STEP:2 * PAGES_PER_STEP]
    o_ref, kv_sc, krt_sc, m_sc, l_sc, acc_sc = refs[2 * PAGES_PER_STEP:]
    g = pl.program_id(1)
    r = MLA_KV_LORA
    nt = (((1,), (1,)), ((), ()))

    @pl.when(g == 0)
    def _():
        _flash_init(m_sc, l_sc, acc_sc)

    for i in range(PAGES_PER_STEP):
        kv_sc[i * PAGE_SIZE:(i + 1) * PAGE_SIZE, :] = kv_refs[i][...].astype(BF16)
        krt_sc[:, i * PAGE_SIZE:(i + 1) * PAGE_SIZE] = krt_refs[i][...].astype(BF16)
    q = q_ref[...]
    kv = kv_sc[...]
    s = (lax.dot_general(q[:, 0:r], kv, nt, preferred_element_type=F32)
         + jnp.dot(q[:, r:r + MLA_ROPE], krt_sc[...], preferred_element_type=F32))
    _flash_update(s, kv, m_sc, l_sc, acc_sc)

    @pl.when(g == n_groups - 1)
    def _():
        knew = knew_ref[...]
        n_new = knew.shape[0]
        s_new = lax.dot_general(q, knew, nt, preferred_element_type=F32)
        rows = s_new.shape[0]
        q_t = _shr(lax.broadcasted_iota(jnp.int32, (rows, n_new), 0), MLA_HEADS)
        k_t = lax.broadcasted_iota(jnp.int32, (rows, n_new), 1)
        s_new = jnp.where((k_t <= q_t) & (k_t < seq_len), s_new, NEG_INF)
        m_old = m_sc[...]
        m_fin = jnp.maximum(m_old, jnp.max(s_new, axis=1, keepdims=True))
        a_fin = jnp.exp(m_old - m_fin)
        p_new = jnp.exp(s_new - m_fin[:, 0:n_new])
        l_sc[...] = a_fin * l_sc[...] + jnp.sum(p_new, axis=1, keepdims=True)
        acc_sc[...] = acc_sc[...] * _lane_tile(a_fin, r // LANES) + jnp.dot(
            p_new.astype(BF16), knew[:, 0:r], preferred_element_type=F32)
        o_ref[...] = _flash_result(l_sc, acc_sc).astype(o_ref.dtype)


def _mla_sample_attn(q2d, knew, cache_kv, cache_krt, page_table, layer_j, seq_len):
    b, n_pages = page_table.shape
    n_groups = n_pages // PAGES_PER_STEP
    rows = seq_len * MLA_HEADS
    keys = PAGES_PER_STEP * PAGE_SIZE
    pt_flat = page_table.reshape(-1)

    def page_spec(i, shape):
        return pl.BlockSpec(
            (None, None) + shape,
            lambda bi, gi, pt: (layer_j, pt[bi * n_pages + gi * PAGES_PER_STEP + i], 0, 0))

    kern = functools.partial(_mla_sample_kernel, n_groups=n_groups, seq_len=seq_len)
    grid_spec = pltpu.PrefetchScalarGridSpec(
        num_scalar_prefetch=1,
        grid=(b, n_groups),
        in_specs=([pl.BlockSpec((rows, MLA_QK), lambda bi, gi, pt: (bi, 0)),
                   pl.BlockSpec((None, knew.shape[1], MLA_QK), lambda bi, gi, pt: (bi, 0, 0))]
                  + [page_spec(i, (PAGE_SIZE, MLA_KV_LORA)) for i in range(PAGES_PER_STEP)]
                  + [page_spec(i, (MLA_ROPE, PAGE_SIZE)) for i in range(PAGES_PER_STEP)]),
        out_specs=pl.BlockSpec((rows, MLA_KV_LORA), lambda bi, gi, pt: (bi, 0)),
        scratch_shapes=[pltpu.VMEM((keys, MLA_KV_LORA), BF16), pltpu.VMEM((MLA_ROPE, keys), BF16),
                        pltpu.VMEM((rows, LANES), F32), pltpu.VMEM((rows, LANES), F32),
                        pltpu.VMEM((rows, MLA_KV_LORA), F32)])
    return pl.pallas_call(
        kern,
        grid_spec=grid_spec,
        out_shape=jax.ShapeDtypeStruct((q2d.shape[0], MLA_KV_LORA), BF16),
        compiler_params=_params(2),
        name="mla_sample_attn",
    )(pt_flat, q2d, knew, *([cache_kv] * PAGES_PER_STEP), *([cache_krt] * PAGES_PER_STEP))


def _mla_out_kernel(ol_ref, wuv_ref, wo_ref, x_ref, gt_ref, gpost_ref, o_ref, o_sc, out_sc, stat_sc, *, grp,
                    head_major):
    r = MLA_KV_LORA
    for h in range(MLA_HEADS):
        o_lat = ol_ref[h] if head_major else ol_ref[:, h * r:(h + 1) * r]
        o_h = jnp.dot(o_lat, wuv_ref[h], preferred_element_type=F32)
        o_sc[:, h * MLA_V:(h + 1) * MLA_V] = o_h.astype(BF16)
    out_sc[...] = jnp.dot(o_sc[...], wo_ref[...], preferred_element_type=F32)
    _ada_out_rows(x_ref, out_sc, gpost_ref, gt_ref, grp.row(), 1.0, stat_sc, o_ref)


def _mla_out(o_lat, w_uv, w_o, x, grp, layer, g_post, tm=256):
    t, d = x.shape
    head_major = o_lat.ndim == 3
    sub = _Group(grp.mod, tm, None if grp.tiles_per_mod_row is None else grp.tiles_per_mod_row * grp.tm // tm)
    kern = functools.partial(_mla_out_kernel, grp=sub, head_major=head_major)
    if head_major:
        ol_spec = pl.BlockSpec((MLA_HEADS, tm, MLA_KV_LORA), lambda i: (0, i, 0))
    else:
        ol_spec = pl.BlockSpec((tm, o_lat.shape[1]), lambda i: (i, 0))
    return pl.pallas_call(
        kern,
        grid=(t // tm,),
        in_specs=[ol_spec,
                  pl.BlockSpec(w_uv.shape, lambda i: (0, 0, 0)),
                  pl.BlockSpec(w_o.shape, lambda i: (0, 0)),
                  pl.BlockSpec((tm, d), lambda i: (i, 0)),
                  sub.mod_spec(layer, 3 + 2),
                  pl.BlockSpec((1, d), lambda i: (0, 0))],
        out_specs=pl.BlockSpec((tm, d), lambda i: (i, 0)),
        out_shape=jax.ShapeDtypeStruct((t, d), F32),
        scratch_shapes=[pltpu.VMEM((tm, MLA_HEADS * MLA_V), BF16), pltpu.VMEM((tm, d), F32),
                        pltpu.VMEM((tm, LANES), F32)],
        compiler_params=_params(1),
        name="mla_out",
    )(o_lat, w_uv, w_o, x, sub.mod, g_post.reshape(1, d))


def _t5_buckets(delta):
    n = np.maximum(delta, 0)
    max_exact = N_BUCKETS // 2
    log_ratio = np.log(np.maximum(n, 1).astype(np.float64) / max_exact) / math.log(BUCKET_MAX_DIST / max_exact)
    large = np.minimum(max_exact + (log_ratio * (N_BUCKETS - max_exact)).astype(np.int64), N_BUCKETS - 1)
    return np.where(n < max_exact, n, large).astype(np.int32)


def _swa_bias_table(rel_bias, delta, valid):
    lq, lk = delta.shape
    one_hot = (jnp.asarray(_t5_buckets(delta))[None] == jnp.arange(N_BUCKETS)[:, None, None]).astype(F32)
    bias = jnp.einsum("nh,nqk->hqk", rel_bias.astype(F32), one_hot, precision=lax.Precision.HIGHEST)
    bias = jnp.where(jnp.asarray(valid)[None], bias, NEG_INF)
    return bias.reshape(SWA_HEADS * lq, lk)


def _swa_sink_table(sinks, lq):
    return jnp.repeat(sinks.astype(F32), lq)[:, None]


def _swa_q_perm():
    perm = np.zeros(SWA_HEADS * SWA_HEAD_DIM, np.int32)
    for m in range(SWA_KV_HEADS // 2):
        for g in range(SWA_GROUP):
            for p in range(2):
                src = ((2 * m + p) * SWA_GROUP + g) * SWA_HEAD_DIM
                dst = (4 * m + g) * LANES + p * SWA_HEAD_DIM
                perm[dst:dst + SWA_HEAD_DIM] = np.arange(src, src + SWA_HEAD_DIM)
    return perm


def _rope_tables(pos):
    half = MLA_ROPE // 2
    inv = ROPE_THETA ** (-jnp.arange(half, dtype=F32) / half)
    ang = pos.astype(F32)[:, None] * inv[None, :]
    cos, sin = jnp.cos(ang), jnp.sin(ang)
    zeros = jnp.zeros((pos.shape[0], LANES - MLA_ROPE), F32)
    return (jnp.concatenate([cos, cos, zeros], axis=1), jnp.concatenate([-sin, sin, zeros], axis=1))


def _pad_rope_cols(w):
    half = MLA_ROPE // 2
    z = jnp.zeros((w.shape[0], LANES - MLA_ROPE), w.dtype)
    return (jnp.concatenate([w, z], axis=1),
            jnp.concatenate([w[:, half:], w[:, :half], z], axis=1))


def kernel(x_prompt, x_sample, state_swa_k, state_swa_v, state_conv, cache_mla_kv, cache_mla_kr, page_table,
           c_prompt, c_sample, ada_w, ada_b, norm_pre, norm_post, ffn_w_in, ffn_w_out,
           gm_w_in, gm_ln_g, gm_ln_b, gm_w_s, gm_b_s, gm_w_out,
           swa_w_qkv, swa_w_o, swa_sinks, rel_bias,
           sc_w_in, sc_conv, sc_w_out,
           mla_w_qa, mla_qa_norm, mla_w_qb, mla_w_kva, mla_kva_norm, mla_w_kvb, mla_w_o):
    n_seq, seq_len, d = x_prompt.shape
    n_dec, dec_len, _ = x_sample.shape
    depth = ada_w.shape[0]
    past_len = page_table.shape[1] * PAGE_SIZE
    t_p, t_s = n_seq * seq_len, n_dec * dec_len

    c_all = jnp.concatenate([jnp.repeat(c_sample, dec_len, axis=0), c_prompt,
                             jnp.zeros((8 - n_seq, d), F32)], axis=0)
    mod = _ada(c_all, ada_w, ada_b)
    tm_p, tm_s = 512, 256
    grp_p = _Group(mod[:, t_s:t_s + 8], tm_p, seq_len // tm_p)
    grp_s = _Group(mod, tm_s, None)
    groups = (grp_p, grp_s)

    xs = [x_prompt.reshape(t_p, d), x_sample.reshape(t_s, d)]
    outs = {}
    ffn_w = (ffn_w_in[0, 0].astype(BF16), ffn_w_out[0, 0].astype(BF16))

    def ffn_pair(xs, ffn_w, layer, which):
        nxt = (layer, 1) if which == 0 else (layer + 1, 0)
        next_w = (ffn_w_in, ffn_w_out) + nxt if nxt[0] < depth else None
        sub = 2 * which
        res = _half_ffn(xs[0], grp_p, layer, which, norm_pre[layer, sub], norm_post[layer, sub], *ffn_w,
                        next_w=next_w)
        x_s = _half_ffn(xs[1], grp_s, layer, which, norm_pre[layer, sub], norm_post[layer, sub], *ffn_w)
        if next_w is None:
            return [res, x_s], None
        return [res[0], x_s], (res[1], res[2])

    for i in range(depth):
        kind, j = i % 4, i // 4
        xs, ffn_w = ffn_pair(xs, ffn_w, i, 0)

        if kind == 0:
            w_in = gm_w_in[j].astype(BF16)
            w_out = gm_w_out[j].astype(BF16)
            new = []
            for x, g, chunk, sl, cps in ((xs[0], grp_p, GM_CHUNK, GM_CHUNK, seq_len // GM_CHUNK),
                                         (xs[1], grp_s, GM_CHUNK, dec_len, 1)):
                lc = min(sl, GM_CHUNK)
                m = jnp.tile(gm_w_s[j][:, :lc, :lc], (1, chunk // lc, chunk // lc))
                bias = jnp.broadcast_to(jnp.tile(gm_b_s[j][:, :lc], (1, chunk // lc))[:, :, None],
                                        (GM_GROUPS, chunk, LANES))
                uv = _norm_mod_matmul(x, g, i, norm_pre[i, 1], w_in, act="gelu", out_dtype=BF16)
                mixed, st = _gm_core(uv, gm_ln_g[j], gm_ln_b[j], m, bias, chunk, lc, cps)
                new.append(_matmul_residual(mixed, w_out, x, g, i, norm_post[i, 1]))
                outs.setdefault("gm", []).append(st)
            xs = new

        elif kind == 1:
            perm = _swa_q_perm()
            nq = SWA_HEADS * SWA_HEAD_DIM
            nkv = SWA_KV_HEADS * SWA_HEAD_DIM
            w_qkv = jnp.concatenate([swa_w_qkv[j][:, :nq][:, perm], swa_w_qkv[j][:, nq:]], axis=1).astype(BF16)
            w_o = swa_w_o[j][perm, :].astype(BF16)
            w = SWA_WINDOW
            i_q, i_k = np.arange(w), np.arange(2 * w)
            delta = w + i_q[:, None] - i_k[None, :]
            bias_p = _swa_bias_table(rel_bias, delta, (delta >= 0) & (delta < w))
            sink_p = _swa_sink_table(swa_sinks[j], w)
            qkv_p = _norm_mod_matmul(xs[0], grp_p, i, norm_pre[i, 1], w_qkv)
            o_p = _swa_prompt_core(qkv_p, bias_p, sink_p, seq_len)
            x_p = _matmul_residual(o_p, w_o, xs[0], grp_p, i, norm_post[i, 1])
            kv_p = qkv_p.reshape(n_seq, seq_len, -1)[:, seq_len - w:, nq:]
            outs.setdefault("swa_kp", []).append(kv_p[..., :nkv].reshape(n_seq, w, SWA_KV_HEADS, SWA_HEAD_DIM))
            outs.setdefault("swa_vp", []).append(kv_p[..., nkv:].reshape(n_seq, w, SWA_KV_HEADS, SWA_HEAD_DIM))
            lb = state_swa_k.shape[2]
            n_keys = lb + dec_len
            pad = (-n_keys) % 8
            i_q, i_k = np.arange(dec_len), np.arange(n_keys + pad)
            delta = lb + i_q[:, None] - i_k[None, :]
            valid = (delta >= 0) & (delta < w) & (i_k[None, :] < n_keys)
            bias_s = _swa_bias_table(rel_bias, delta, valid)
            sink_s = _swa_sink_table(swa_sinks[j], dec_len)
            qkv_s = _norm_mod_matmul(xs[1], grp_s, i, norm_pre[i, 1], w_qkv).reshape(n_dec, dec_len, -1)
            zpad = jnp.zeros((n_dec, pad, nkv), F32)
            kk = jnp.concatenate([state_swa_k[j].reshape(n_dec, lb, nkv), qkv_s[..., nq:nq + nkv], zpad], axis=1)
            vv = jnp.concatenate([state_swa_v[j].reshape(n_dec, lb, nkv), qkv_s[..., nq + nkv:], zpad], axis=1)
            q_s = qkv_s[..., :nq].reshape(n_dec, dec_len, SWA_KV_HEADS // 2, SWA_GROUP, LANES)
            q_s = q_s.transpose(0, 2, 3, 1, 4).reshape(n_dec, SWA_KV_HEADS // 2, SWA_GROUP * dec_len, LANES)
            o_s = _swa_sample_core(q_s, kk, vv, bias_s, sink_s)
            o_s = o_s.reshape(n_dec, SWA_KV_HEADS // 2, SWA_GROUP, dec_len, LANES).transpose(0, 3, 1, 2, 4)
            x_s = _matmul_residual(o_s.reshape(t_s, nq), w_o, xs[1], grp_s, i, norm_post[i, 1])
            outs.setdefault("swa_ks", []).append(kk[:, n_keys - lb:n_keys].reshape(n_dec, lb, SWA_KV_HEADS, SWA_HEAD_DIM))
            outs.setdefault("swa_vs", []).append(vv[:, n_keys - lb:n_keys].reshape(n_dec, lb, SWA_KV_HEADS, SWA_HEAD_DIM))
            xs = [x_p, x_s]

        elif kind == 2:
            w_in = sc_w_in[j].astype(BF16)
            w_out = sc_w_out[j].astype(BF16)
            c = sc_w_out.shape[1]
            g3_p = _norm_mod_matmul(xs[0], grp_p, i, norm_pre[i, 1], w_in, out_dtype=BF16)
            y_p, st_p = _conv_prompt_core(g3_p, sc_conv[j], seq_len)
            x_p = _matmul_residual(y_p, w_out, xs[0], grp_p, i, norm_post[i, 1])
            outs.setdefault("conv_p", []).append(st_p[:, 8 - (CONV_WIDTH - 1):])
            g3_s = _norm_mod_matmul(xs[1], grp_s, i, norm_pre[i, 1], w_in, out_dtype=BF16)
            y_s, st_s = _conv_sample_core(g3_s.reshape(n_dec, dec_len * 3 * c),
                                          state_conv[j].reshape(n_dec, (CONV_WIDTH - 1) * c), sc_conv[j], dec_len)
            x_s = _matmul_residual(y_s.reshape(t_s, c), w_out, xs[1], grp_s, i, norm_post[i, 1])
            outs.setdefault("conv_s", []).append(st_s.reshape(n_dec, CONV_WIDTH - 1, c))
            xs = [x_p, x_s]

        else:
            r = MLA_KV_LORA
            kr_pad, kr_sw = _pad_rope_cols(mla_w_kva[j][:, r:])
            w_p = jnp.concatenate([mla_w_qa[j], mla_w_kva[j][:, :r], kr_pad, kr_sw], axis=1).astype(BF16)
            w_qb = mla_w_qb[j].reshape(MLA_Q_LORA, MLA_HEADS, MLA_NOPE + MLA_ROPE)
            w_nope = w_qb[:, :, :MLA_NOPE].reshape(MLA_Q_LORA, -1).astype(BF16)
            rope_pairs = [_pad_rope_cols(w_qb[:, h, MLA_NOPE:]) for h in range(MLA_HEADS)]
            w_rope = jnp.concatenate([p[0] for p in rope_pairs], axis=1).astype(BF16)
            w_rope_sw = jnp.concatenate([p[1] for p in rope_pairs], axis=1).astype(BF16)
            w_kvb = mla_w_kvb[j].reshape(r, MLA_HEADS, MLA_NOPE + MLA_V)
            w_uk_t = w_kvb[:, :, :MLA_NOPE].transpose(1, 2, 0).astype(BF16)
            w_uv = w_kvb[:, :, MLA_NOPE:].transpose(1, 0, 2).astype(BF16)
            w_o = mla_w_o[j].astype(BF16)
            tm = 256
            cos_p, sin_p = _rope_tables(jnp.arange(seq_len))
            proj_p = _norm_mod_matmul(xs[0], grp_p, i, norm_pre[i, 1], w_p, tn=w_p.shape[1])
            q_p, ckv_p, kr_p, kcat_p = _mla_proj(proj_p, cos_p, sin_p, seq_len // tm, mla_qa_norm[j],
                                                 mla_kva_norm[j], w_nope, w_rope, w_rope_sw, w_uk_t, True, tm=tm)
            ol_p = _mla_prompt_attn(q_p, kcat_p, n_seq, seq_len)
            x_p = _mla_out(ol_p, w_uv, w_o, xs[0], grp_p, i, norm_post[i, 1])
            outs.setdefault("mla_kvp", []).append(ckv_p.reshape(n_seq, seq_len, r))
            outs.setdefault("mla_krp", []).append(kr_p[:, :MLA_ROPE].reshape(n_seq, seq_len, MLA_ROPE))
            cos_s, sin_s = _rope_tables(past_len + jnp.arange(dec_len))
            cos_s, sin_s = jnp.tile(cos_s, (tm // dec_len, 1)), jnp.tile(sin_s, (tm // dec_len, 1))
            proj_s = _norm_mod_matmul(xs[1], grp_s, i, norm_pre[i, 1], w_p, tn=w_p.shape[1])
            q_s, ckv_s, kr_s, kcat_s = _mla_proj(proj_s, cos_s, sin_s, 1, mla_qa_norm[j],
                                                 mla_kva_norm[j], w_nope, w_rope, w_rope_sw, w_uk_t, False, tm=tm)
            knew = jnp.concatenate([kcat_s.reshape(n_dec, dec_len, MLA_QK),
                                    jnp.zeros((n_dec, 16 - dec_len, MLA_QK), BF16)], axis=1)
            ol_s = _mla_sample_attn(q_s.reshape(t_s * MLA_HEADS, MLA_QK), knew, cache_mla_kv,
                                    jnp.swapaxes(cache_mla_kr, 2, 3), page_table, j, dec_len)
            x_s = _mla_out(ol_s.reshape(t_s, MLA_HEADS * r), w_uv, w_o, xs[1], grp_s, i, norm_post[i, 1])
            outs.setdefault("mla_kvs", []).append(ckv_s.reshape(n_dec, dec_len, r))
            outs.setdefault("mla_krs", []).append(kr_s[:, :MLA_ROPE].reshape(n_dec, dec_len, MLA_ROPE))
            xs = [x_p, x_s]

        xs, ffn_w = ffn_pair(xs, ffn_w, i, 1)

    gm_p, gm_s = outs["gm"][0::2], outs["gm"][1::2]
    return (xs[0].reshape(n_seq, seq_len, d), xs[1].reshape(n_dec, dec_len, d),
            jnp.stack(gm_p),
            jnp.stack([s.reshape(n_dec, dec_len, -1) for s in gm_s]),
            jnp.stack(outs["swa_kp"]), jnp.stack(outs["swa_vp"]),
            jnp.stack(outs["swa_ks"]), jnp.stack(outs["swa_vs"]),
            jnp.stack(outs["conv_p"]), jnp.stack(outs["conv_s"]),
            jnp.stack(outs["mla_kvp"]), jnp.stack(outs["mla_krp"]),
            jnp.stack(outs["mla_kvs"]), jnp.stack(outs["mla_krs"]))
```

```python
import functools
import math

import numpy as np
import jax
import jax.numpy as jnp
from jax import lax
from jax.experimental import pallas as pl
from jax.experimental.pallas import tpu as pltpu

F32 = jnp.float32
BF16 = jnp.bfloat16

VMEM_LIMIT_BYTES = 56 * 1024 * 1024
LANES = 128

RMS_EPS = 1e-6
NEG_INF = -1e30
FFN_RES_W = 0.5

D_MODEL = 2048
GM_GROUPS = 8
GM_CHUNK = 128
SWA_WINDOW = 128
SWA_HEAD_DIM = 64
SWA_HEADS = 32
SWA_KV_HEADS = 8
SWA_GROUP = 4
SWA_SCALE = SWA_HEAD_DIM ** -0.5
N_BUCKETS = 32
BUCKET_MAX_DIST = 128
CONV_WIDTH = 3
MLA_HEADS = 16
MLA_Q_LORA = 512
MLA_KV_LORA = 512
MLA_NOPE = 128
MLA_ROPE = 64
MLA_V = 128
MLA_SCALE = (MLA_NOPE + MLA_ROPE) ** -0.5
MLA_QK = MLA_KV_LORA + LANES
ROPE_THETA = 10000.0
PAGE_SIZE = 128
PAGES_PER_STEP = 32


def _params(n_axes):
    return pltpu.CompilerParams(dimension_semantics=("arbitrary",) * n_axes,
                                vmem_limit_bytes=VMEM_LIMIT_BYTES)


def _rms(x, g):
    return x * lax.rsqrt(jnp.mean(x * x, axis=-1, keepdims=True) + RMS_EPS) * g


def _lane_tile(x, n):
    return x if n == 1 else jnp.concatenate([x] * n, axis=1)


class _Group:
    def __init__(self, mod, tm, tiles_per_mod_row):
        self.mod = mod
        self.tm = tm
        self.tiles_per_mod_row = tiles_per_mod_row

    def mod_spec(self, layer, col):
        if self.tiles_per_mod_row is None:
            return pl.BlockSpec((None, self.tm, D_MODEL), lambda i, *_: (layer, i, col))
        return pl.BlockSpec((None, 8, D_MODEL), lambda i, *_: (layer, 0, col))

    def row(self):
        if self.tiles_per_mod_row is None:
            return None
        return pl.program_id(0) // self.tiles_per_mod_row

    @staticmethod
    def read(ref, row):
        return ref[...] if row is None else ref[pl.ds(row, 1), :]


ROW_CHUNK = 16


def _for_row_chunks(n_rows, body, unroll):
    def step(c, carry):
        body(pl.ds(pl.multiple_of(c * ROW_CHUNK, ROW_CHUNK), ROW_CHUNK))
        return carry
    lax.fori_loop(0, n_rows // ROW_CHUNK, step, 0, unroll=unroll)


def _row_rms_scale(val_ref, stat_ref):
    def body(rs):
        v = val_ref[rs, :]
        scale = lax.rsqrt(jnp.mean(v * v, axis=-1, keepdims=True) + RMS_EPS)
        stat_ref[rs, :] = jnp.broadcast_to(scale, (ROW_CHUNK, LANES))

    _for_row_chunks(val_ref.shape[0], body, unroll=8)


def _ada_in_rows(x_ref, gpre_ref, sc_ref, sh_ref, row, stat_ref, h_ref):
    n_tile = x_ref.shape[1] // LANES
    _row_rms_scale(x_ref, stat_ref)
    if row is not None:
        gain = gpre_ref[...] * (1.0 + sc_ref[pl.ds(row, 1), :])
        shift = sh_ref[pl.ds(row, 1), :]

    def body(rs):
        xn = x_ref[rs, :] * _lane_tile(stat_ref[rs, :], n_tile)
        if row is None:
            h = xn * (gpre_ref[...] * (1.0 + sc_ref[rs, :])) + sh_ref[rs, :]
        else:
            h = xn * gain + shift
        h_ref[rs, :] = h.astype(h_ref.dtype)

    _for_row_chunks(x_ref.shape[0], body, unroll=2)


def _ada_out_rows(x_ref, val_ref, gpost_ref, gt_ref, row, res_w, stat_ref, o_ref):
    n_tile = x_ref.shape[1] // LANES
    _row_rms_scale(val_ref, stat_ref)
    if row is not None:
        gain = gpost_ref[...] * (res_w * gt_ref[pl.ds(row, 1), :])

    def body(rs):
        vn = val_ref[rs, :] * _lane_tile(stat_ref[rs, :], n_tile)
        if row is None:
            o_ref[rs, :] = x_ref[rs, :] + vn * (gpost_ref[...] * (res_w * gt_ref[rs, :]))
        else:
            o_ref[rs, :] = x_ref[rs, :] + vn * gain

    _for_row_chunks(x_ref.shape[0], body, unroll=2)


def _shr(x, divisor):
    shift = divisor.bit_length() - 1
    assert 1 << shift == divisor
    return lax.shift_right_logical(x, shift)


def _ada_kernel(c_ref, w_ref, b_ref, o_ref, cs_sc):
    @pl.when((pl.program_id(0) == 0) & (pl.program_id(1) == 0))
    def _():
        c = c_ref[...]
        cs_sc[...] = (c * jax.nn.sigmoid(c)).astype(BF16)

    o_ref[...] = jnp.dot(cs_sc[...], w_ref[...].astype(BF16),
                         preferred_element_type=F32) + b_ref[...]


def _ada(c_all, ada_w, ada_b, tn=1024):
    n_layers, d, n = ada_w.shape
    rows = c_all.shape[0]
    return pl.pallas_call(
        _ada_kernel,
        grid=(n_layers, n // tn),
        in_specs=[pl.BlockSpec((rows, d), lambda l, j: (0, 0)),
                  pl.BlockSpec((None, d, tn), lambda l, j: (l, 0, j)),
                  pl.BlockSpec((None, 1, tn), lambda l, j: (l, 0, j))],
        out_specs=pl.BlockSpec((None, rows, tn), lambda l, j: (l, 0, j)),
        out_shape=jax.ShapeDtypeStruct((n_layers, rows, n), F32),
        scratch_shapes=[pltpu.VMEM((rows, d), BF16)],
        compiler_params=_params(2),
        name="ada_modulation",
    )(c_all, ada_w, ada_b.reshape(n_layers, 1, n))


def _ffn_kernel(x_ref, sh_ref, sc_ref, gt_ref, gpre_ref, gpost_ref, wg_ref, wu_ref, wo_ref, *rest, grp, n_f):
    if len(rest) == 8:
        nwi_ref, nwo_ref, o_ref, nwi_b_ref, nwo_b_ref, h_sc, acc_sc, stat_sc = rest
        nwi_b_ref[...] = nwi_ref[...].astype(BF16)
        nwo_b_ref[...] = nwo_ref[...].astype(BF16)
    else:
        o_ref, h_sc, acc_sc, stat_sc = rest
    j = pl.program_id(1)
    row = grp.row()

    @pl.when(j == 0)
    def _():
        _ada_in_rows(x_ref, gpre_ref, sc_ref, sh_ref, row, stat_sc, h_sc)
        acc_sc[...] = jnp.zeros_like(acc_sc)

    h = h_sc[...]
    tf = wg_ref.shape[1]
    acts = []
    for c in range(2):
        cols = slice(c * tf // 2, (c + 1) * tf // 2)
        g = jnp.dot(h, wg_ref[:, cols], preferred_element_type=F32)
        u = jnp.dot(h, wu_ref[:, cols], preferred_element_type=F32)
        acts.append((g * jax.nn.sigmoid(g) * u).astype(BF16))
    acc_sc[...] += jnp.dot(jnp.concatenate(acts, axis=1), wo_ref[...], preferred_element_type=F32)

    @pl.when(j == n_f - 1)
    def _():
        _ada_out_rows(x_ref, acc_sc, gpost_ref, gt_ref, row, FFN_RES_W, stat_sc, o_ref)


def _half_ffn(x, grp, layer, which, g_pre, g_post, w_in, w_out, next_w=None, tf=512):
    t, d = x.shape
    f = w_out.shape[0]
    tm, n_f = grp.tm, f // tf
    n_i = t // tm
    sub = 2 * which
    kern = functools.partial(_ffn_kernel, grp=grp, n_f=n_f)
    in_specs = [pl.BlockSpec((tm, d), lambda i, j: (i, 0)),
                grp.mod_spec(layer, sub * 3 + 0),
                grp.mod_spec(layer, sub * 3 + 1),
                grp.mod_spec(layer, sub * 3 + 2),
                pl.BlockSpec((1, d), lambda i, j: (0, 0)),
                pl.BlockSpec((1, d), lambda i, j: (0, 0)),
                pl.BlockSpec((d, tf), lambda i, j: (0, j)),
                pl.BlockSpec((d, tf), lambda i, j: (0, n_f + j)),
                pl.BlockSpec((tf, d), lambda i, j: (j, 0))]
    args = [x, grp.mod, grp.mod, grp.mod, g_pre.reshape(1, d), g_post.reshape(1, d), w_in, w_in, w_out]
    out_specs = [pl.BlockSpec((tm, d), lambda i, j: (i, 0))]
    out_shape = [jax.ShapeDtypeStruct((t, d), F32)]
    if next_w is not None:
        nw_in, nw_out, nl, nwh = next_w
        assert n_i % 2 == 0
        bi = (2 * d // n_i, 2 * f // n_f)
        bo = (f // n_f, 2 * d // n_i)

        def blk(i, j):
            s = (i * n_f + j) // 2
            return s // n_f, s % n_f

        in_specs += [pl.BlockSpec((None, None) + bi, lambda i, j: (nl, nwh) + blk(i, j)),
                     pl.BlockSpec((None, None) + bo, lambda i, j: (nl, nwh) + blk(i, j)[::-1])]
        args += [nw_in, nw_out]
        out_specs += [pl.BlockSpec(bi, blk), pl.BlockSpec(bo, lambda i, j: blk(i, j)[::-1])]
        out_shape += [jax.ShapeDtypeStruct((d, 2 * f), BF16), jax.ShapeDtypeStruct((f, d), BF16)]
    res = pl.pallas_call(
        kern,
        grid=(n_i, n_f),
        in_specs=in_specs,
        out_specs=out_specs,
        out_shape=out_shape,
        scratch_shapes=[pltpu.VMEM((tm, d), BF16), pltpu.VMEM((tm, d), F32), pltpu.VMEM((tm, LANES), F32)],
        compiler_params=_params(2),
        name="half_ffn",
    )(*args)
    return res if next_w is not None else res[0]


def _nmm_kernel(x_ref, sh_ref, sc_ref, gpre_ref, w_ref, o_ref, h_sc, stat_sc, *, grp, act):
    row = grp.row()

    @pl.when(pl.program_id(1) == 0)
    def _():
        _ada_in_rows(x_ref, gpre_ref, sc_ref, sh_ref, row, stat_sc, h_sc)

    y = jnp.dot(h_sc[...], w_ref[...], preferred_element_type=F32)
    if act == "gelu":
        y = jax.nn.gelu(y, approximate=True)
    o_ref[...] = y.astype(o_ref.dtype)


def _norm_mod_matmul(x, grp, layer, g_pre, w, act=None, tn=1024, out_dtype=F32):
    t, d = x.shape
    n = w.shape[1]
    tn = min(tn, n)
    tm = grp.tm
    kern = functools.partial(_nmm_kernel, grp=grp, act=act)
    return pl.pallas_call(
        kern,
        grid=(t // tm, n // tn),
        in_specs=[pl.BlockSpec((tm, d), lambda i, j: (i, 0)),
                  grp.mod_spec(layer, 3 + 0),
                  grp.mod_spec(layer, 3 + 1),
                  pl.BlockSpec((1, d), lambda i, j: (0, 0)),
                  pl.BlockSpec((d, tn), lambda i, j: (0, j))],
        out_specs=pl.BlockSpec((tm, tn), lambda i, j: (i, j)),
        out_shape=jax.ShapeDtypeStruct((t, n), out_dtype),
        scratch_shapes=[pltpu.VMEM((tm, d), BF16), pltpu.VMEM((tm, LANES), F32)],
        compiler_params=_params(2),
        name="norm_mod_matmul",
    )(x, grp.mod, grp.mod, g_pre.reshape(1, d), w)


def _mres_kernel(a_ref, w_ref, x_ref, gt_ref, gpost_ref, o_ref, acc_sc, stat_sc, *, grp, n_k):
    k = pl.program_id(1)
    row = grp.row()

    @pl.when(k == 0)
    def _():
        acc_sc[...] = jnp.zeros_like(acc_sc)

    acc_sc[...] += jnp.dot(a_ref[...].astype(BF16), w_ref[...], preferred_element_type=F32)

    @pl.when(k == n_k - 1)
    def _():
        _ada_out_rows(x_ref, acc_sc, gpost_ref, gt_ref, row, 1.0, stat_sc, o_ref)


def _matmul_residual(a, w, x, grp, layer, g_post, tk=1024):
    t, kdim = a.shape
    d = w.shape[1]
    tm, n_k = grp.tm, kdim // tk
    kern = functools.partial(_mres_kernel, grp=grp, n_k=n_k)
    return pl.pallas_call(
        kern,
        grid=(t // tm, n_k),
        in_specs=[pl.BlockSpec((tm, tk), lambda i, k: (i, k)),
                  pl.BlockSpec((tk, d), lambda i, k: (k, 0)),
                  pl.BlockSpec((tm, d), lambda i, k: (i, 0)),
                  grp.mod_spec(layer, 3 + 2),
                  pl.BlockSpec((1, d), lambda i, k: (0, 0))],
        out_specs=pl.BlockSpec((tm, d), lambda i, k: (i, 0)),
        out_shape=jax.ShapeDtypeStruct((t, d), F32),
        scratch_shapes=[pltpu.VMEM((tm, d), F32), pltpu.VMEM((tm, LANES), F32)],
        compiler_params=_params(2),
        name="matmul_residual",
    )(a, w, x, grp.mod, g_post.reshape(1, d))


def _gm_kernel(u_ref, v_ref, lng_ref, lnb_ref, m_ref, b_ref, o_ref, st_ref, *, seq_len):
    v = v_ref[...].astype(F32)
    vc = v - jnp.mean(v, axis=-1, keepdims=True)
    vn = vc * lax.rsqrt(jnp.mean(vc * vc, axis=-1, keepdims=True) + RMS_EPS) * lng_ref[...] + lnb_ref[...]
    st_ref[...] = vn
    vnb = vn.astype(BF16)
    c = v.shape[0]
    gw = v.shape[1] // GM_GROUPS
    row = lax.broadcasted_iota(jnp.int32, (c, c), 0)
    col = lax.broadcasted_iota(jnp.int32, (c, c), 1)
    keep = (col <= row) & (_shr(row, seq_len) == _shr(col, seq_len))
    for g in range(GM_GROUPS):
        mg = jnp.where(keep, m_ref[g], 0.0).astype(BF16)
        mixed = jnp.dot(mg, vnb[:, g * gw:(g + 1) * gw], preferred_element_type=F32)
        mixed = mixed + _lane_tile(b_ref[g], gw // LANES)
        o_ref[:, g * gw:(g + 1) * gw] = (u_ref[:, g * gw:(g + 1) * gw].astype(F32) * mixed).astype(o_ref.dtype)


def _gm_core(uv, ln_g, ln_b, m, bias, chunk, seq_len, chunks_per_state):
    t = uv.shape[0]
    w = uv.shape[1] // 2
    n_chunks = t // chunk
    n_states = n_chunks // chunks_per_state
    kern = functools.partial(_gm_kernel, seq_len=seq_len)
    return pl.pallas_call(
        kern,
        grid=(n_chunks,),
        in_specs=[pl.BlockSpec((chunk, w), lambda c: (c, 0)),
                  pl.BlockSpec((chunk, w), lambda c: (c, 1)),
                  pl.BlockSpec((1, w), lambda c: (0, 0)),
                  pl.BlockSpec((1, w), lambda c: (0, 0)),
                  pl.BlockSpec((GM_GROUPS, chunk, chunk), lambda c: (0, 0, 0)),
                  pl.BlockSpec((GM_GROUPS, chunk, LANES), lambda c: (0, 0, 0))],
        out_specs=[pl.BlockSpec((chunk, w), lambda c: (c, 0)),
                   pl.BlockSpec((None, chunk, w), lambda c: (c // chunks_per_state, 0, 0))],
        out_shape=[jax.ShapeDtypeStruct((t, w), BF16),
                   jax.ShapeDtypeStruct((n_states, chunk, w), F32)],
        compiler_params=_params(1),
        name="gm_core",
    )(uv, uv, ln_g.reshape(1, w), ln_b.reshape(1, w), m, bias)


def _softmax_sink(s, sink):
    mx = jnp.maximum(jnp.max(s, axis=-1, keepdims=True), sink)
    p = jnp.exp(s - mx)
    denom = jnp.sum(p, axis=-1, keepdims=True) + jnp.exp(sink - mx)
    return p / denom


def _half_lane_mask(rows, parity):
    lane = lax.broadcasted_iota(jnp.int32, (rows, LANES), 1)
    return (lane >= SWA_HEAD_DIM) if parity else (lane < SWA_HEAD_DIM)


def _swa_attend(q_blocks, k, v, bias_ref, sink_ref, masked_keys=None):
    rows = q_blocks[0].shape[0]
    n_keys = k.shape[0]
    logits, values = [], []
    for m in range(SWA_KV_HEADS // 2):
        k2 = k[:, m * LANES:(m + 1) * LANES]
        v2 = v[:, m * LANES:(m + 1) * LANES]
        for parity in range(2):
            keep = _half_lane_mask(n_keys, parity)
            kx = jnp.where(keep, k2, 0.0).astype(BF16)
            values.append(jnp.where(keep, v2, 0.0).astype(BF16))
            logits.append(lax.dot_general(q_blocks[m], kx, (((1,), (1,)), ((), ())), preferred_element_type=F32))
    s = jnp.concatenate(logits, axis=0) + bias_ref[...]
    if masked_keys is not None:
        key_col = lax.broadcasted_iota(jnp.int32, s.shape, 1)
        s = jnp.where(key_col < masked_keys, NEG_INF, s)
    p = _softmax_sink(s, sink_ref[...]).astype(BF16)
    outs = []
    for m in range(SWA_KV_HEADS // 2):
        o_m = None
        for parity in range(2):
            idx = 2 * m + parity
            o_p = jnp.dot(p[idx * rows:(idx + 1) * rows], values[idx], preferred_element_type=F32)
            o_m = o_p if o_m is None else o_m + o_p
        outs.append(o_m)
    return outs


def _swa_prompt_kernel(q_ref, kp_ref, ko_ref, vp_ref, vo_ref, bias_ref, sink_ref, o_ref, *, blocks_per_seq):
    w = SWA_WINDOW
    first = (pl.program_id(0) % blocks_per_seq) == 0
    q = (q_ref[...] * SWA_SCALE).astype(BF16)
    k = jnp.concatenate([kp_ref[...], ko_ref[...]], axis=0)
    v = jnp.concatenate([vp_ref[...], vo_ref[...]], axis=0)
    q_blocks = [jnp.concatenate([q[:, (4 * m + g) * LANES:(4 * m + g + 1) * LANES] for g in range(SWA_GROUP)],
                                axis=0) for m in range(SWA_KV_HEADS // 2)]
    outs = _swa_attend(q_blocks, k, v, bias_ref, sink_ref, masked_keys=jnp.where(first, w, 0))
    for m, o_m in enumerate(outs):
        for g in range(SWA_GROUP):
            o_ref[:, (4 * m + g) * LANES:(4 * m + g + 1) * LANES] = o_m[g * w:(g + 1) * w].astype(o_ref.dtype)


def _swa_prompt_core(qkv, bias, sink, seq_len):
    t = qkv.shape[0]
    w = SWA_WINDOW
    nq = SWA_HEADS * SWA_HEAD_DIM
    nkv = SWA_KV_HEADS * SWA_HEAD_DIM
    bps = seq_len // w
    kcol, vcol = nq // nkv, nq // nkv + 1

    def prev(i):
        return jnp.maximum(i - 1, 0)

    kern = functools.partial(_swa_prompt_kernel, blocks_per_seq=bps)
    return pl.pallas_call(
        kern,
        grid=(t // w,),
        in_specs=[pl.BlockSpec((w, nq), lambda i: (i, 0)),
                  pl.BlockSpec((w, nkv), lambda i: (prev(i), kcol)),
                  pl.BlockSpec((w, nkv), lambda i: (i, kcol)),
                  pl.BlockSpec((w, nkv), lambda i: (prev(i), vcol)),
                  pl.BlockSpec((w, nkv), lambda i: (i, vcol)),
                  pl.BlockSpec(bias.shape, lambda i: (0, 0)),
                  pl.BlockSpec(sink.shape, lambda i: (0, 0))],
        out_specs=pl.BlockSpec((w, nq), lambda i: (i, 0)),
        out_shape=jax.ShapeDtypeStruct((t, nq), BF16),
        compiler_params=_params(1),
        name="swa_prompt_core",
    )(qkv, qkv, qkv, qkv, qkv, bias, sink)


def _swa_sample_kernel(q_ref, k_ref, v_ref, bias_ref, sink_ref, o_ref, *, bb):
    def body(b, carry):
        q_blocks = [(q_ref[b, m] * SWA_SCALE).astype(BF16) for m in range(SWA_KV_HEADS // 2)]
        outs = _swa_attend(q_blocks, k_ref[b], v_ref[b], bias_ref, sink_ref)
        for m, o_m in enumerate(outs):
            o_ref[b, m] = o_m.astype(o_ref.dtype)
        return carry

    lax.fori_loop(0, bb, body, 0, unroll=2)


def _swa_sample_core(q, kk, vv, bias, sink, bb=8):
    b, n_pair, rows, _ = q.shape
    n_keys = kk.shape[1]
    nkv = kk.shape[2]
    kern = functools.partial(_swa_sample_kernel, bb=bb)
    return pl.pallas_call(
        kern,
        grid=(b // bb,),
        in_specs=[pl.BlockSpec((bb, n_pair, rows, LANES), lambda i: (i, 0, 0, 0)),
                  pl.BlockSpec((bb, n_keys, nkv), lambda i: (i, 0, 0)),
                  pl.BlockSpec((bb, n_keys, nkv), lambda i: (i, 0, 0)),
                  pl.BlockSpec(bias.shape, lambda i: (0, 0)),
                  pl.BlockSpec(sink.shape, lambda i: (0, 0))],
        out_specs=pl.BlockSpec((bb, n_pair, rows, LANES), lambda i: (i, 0, 0, 0)),
        out_shape=jax.ShapeDtypeStruct(q.shape, BF16),
        compiler_params=_params(1),
        name="swa_sample_core",
    )(q, kk, vv, bias, sink)


def _conv_prompt_kernel(gb_ref, gc_ref, z_ref, hc_ref, hz_ref, cw_ref, o_ref, st_ref, zp_sc, *, tiles_per_seq):
    tm = gb_ref.shape[0]
    hr = hc_ref.shape[0]
    first = (pl.program_id(0) % tiles_per_seq) == 0
    zz = gc_ref[...].astype(F32) * z_ref[...].astype(F32)
    halo = jnp.where(first, 0.0, hc_ref[...].astype(F32) * hz_ref[...].astype(F32))
    zp_sc[0:8, :] = halo[hr - 8:hr]
    zp_sc[8:8 + tm, :] = zz
    y = cw_ref[2:3, :] * zz + cw_ref[1:2, :] * zp_sc[7:7 + tm, :] + cw_ref[0:1, :] * zp_sc[6:6 + tm, :]
    o_ref[...] = (gb_ref[...].astype(F32) * y).astype(o_ref.dtype)
    st_ref[...] = zz[tm - 8:tm]


def _conv_prompt_core(g3, conv_w, seq_len, tm=256):
    t = g3.shape[0]
    c = g3.shape[1] // 3
    tps = seq_len // tm
    n_seq = t // seq_len

    hr = 16

    def halo(col):
        return pl.BlockSpec((hr, c), lambda i: (jnp.maximum(i * (tm // hr) - 1, 0), col))

    kern = functools.partial(_conv_prompt_kernel, tiles_per_seq=tps)
    return pl.pallas_call(
        kern,
        grid=(t // tm,),
        in_specs=[pl.BlockSpec((tm, c), lambda i: (i, 0)),
                  pl.BlockSpec((tm, c), lambda i: (i, 1)),
                  pl.BlockSpec((tm, c), lambda i: (i, 2)),
                  halo(1), halo(2),
                  pl.BlockSpec((CONV_WIDTH, c), lambda i: (0, 0))],
        out_specs=[pl.BlockSpec((tm, c), lambda i: (i, 0)),
                   pl.BlockSpec((None, 8, c), lambda i: (i // tps, 0, 0))],
        out_shape=[jax.ShapeDtypeStruct((t, c), BF16),
                   jax.ShapeDtypeStruct((n_seq, 8, c), F32)],
        scratch_shapes=[pltpu.VMEM((tm + 8, c), F32)],
        compiler_params=_params(1),
        name="conv_prompt_core",
    )(g3, g3, g3, g3, g3, conv_w)


def _conv_sample_kernel(g_ref, prev_ref, cw_ref, o_ref, st_ref, *, seq_len, c):
    zz = [prev_ref[:, 0:c], prev_ref[:, c:2 * c]]
    for t in range(seq_len):
        base = t * 3 * c
        zz.append(g_ref[:, base + c:base + 2 * c].astype(F32) * g_ref[:, base + 2 * c:base + 3 * c].astype(F32))
    for t in range(seq_len):
        y = cw_ref[2:3, :] * zz[t + 2] + cw_ref[1:2, :] * zz[t + 1] + cw_ref[0:1, :] * zz[t]
        o_ref[:, t * c:(t + 1) * c] = (g_ref[:, t * 3 * c:t * 3 * c + c].astype(F32) * y).astype(o_ref.dtype)
    st_ref[:, 0:c] = zz[seq_len]
    st_ref[:, c:2 * c] = zz[seq_len + 1]


def _conv_sample_core(g3, prev, conv_w, seq_len):
    b = g3.shape[0]
    c = g3.shape[1] // (3 * seq_len)
    kern = functools.partial(_conv_sample_kernel, seq_len=seq_len, c=c)
    return pl.pallas_call(
        kern,
        grid=(1,),
        in_specs=[pl.BlockSpec(g3.shape, lambda i: (0, 0)),
                  pl.BlockSpec(prev.shape, lambda i: (0, 0)),
                  pl.BlockSpec((CONV_WIDTH, c), lambda i: (0, 0))],
        out_specs=[pl.BlockSpec((b, seq_len * c), lambda i: (0, 0)),
                   pl.BlockSpec((b, 2 * c), lambda i: (0, 0))],
        out_shape=[jax.ShapeDtypeStruct((b, seq_len * c), BF16),
                   jax.ShapeDtypeStruct((b, 2 * c), F32)],
        compiler_params=_params(1),
        name="conv_sample_core",
    )(g3, prev, conv_w)


def _mla_proj_kernel(p_ref, cos_ref, sin_ref, qan_ref, kvn_ref, wn_ref, wr_ref, wrs_ref, wkv_ref, *out_refs, per_head):
    r = MLA_KV_LORA
    cos = cos_ref[...]
    sin = sin_ref[...]
    qa = _rms(p_ref[:, 0:MLA_Q_LORA], qan_ref[...]).astype(BF16)
    qn = jnp.dot(qa, wn_ref[...], preferred_element_type=F32)
    qr = jnp.dot(qa, wr_ref[...], preferred_element_type=F32)
    qrs = jnp.dot(qa, wrs_ref[...], preferred_element_type=F32)
    off = MLA_Q_LORA
    ckv = _rms(p_ref[:, off:off + r], kvn_ref[...])
    kr = p_ref[:, off + r:off + r + LANES] * cos + p_ref[:, off + r + LANES:off + r + 2 * LANES] * sin
    if per_head:
        q_ref, ckv_ref, kr_ref, k_ref, v_ref = out_refs
        kv_up = jnp.dot(ckv.astype(BF16), wkv_ref[...], preferred_element_type=F32)
        kr_b = kr.astype(k_ref.dtype)
    else:
        q_ref, ckv_ref, kr_ref, kcat_ref = out_refs
    for h in range(MLA_HEADS):
        sl = slice(h * LANES, (h + 1) * LANES)
        q_rope = ((qr[:, sl] * cos + qrs[:, sl] * sin) * MLA_SCALE).astype(q_ref.dtype)
        if per_head:
            q_ref[h, :, 0:MLA_NOPE] = (qn[:, sl] * MLA_SCALE).astype(q_ref.dtype)
            q_ref[h, :, MLA_NOPE:MLA_NOPE + LANES] = q_rope
            base = h * (MLA_NOPE + MLA_V)
            k_ref[h, :, 0:MLA_NOPE] = kv_up[:, base:base + MLA_NOPE].astype(k_ref.dtype)
            k_ref[h, :, MLA_NOPE:MLA_NOPE + LANES] = kr_b
            v_ref[h] = kv_up[:, base + MLA_NOPE:base + MLA_NOPE + MLA_V].astype(v_ref.dtype)
        else:
            q_lat = jnp.dot(qn[:, sl].astype(BF16), wkv_ref[h], preferred_element_type=F32)
            q_ref[:, h * MLA_QK:h * MLA_QK + r] = (q_lat * MLA_SCALE).astype(q_ref.dtype)
            q_ref[:, h * MLA_QK + r:(h + 1) * MLA_QK] = q_rope
    ckv_ref[...] = ckv
    kr_ref[...] = kr
    if not per_head:
        kcat_ref[:, 0:r] = ckv.astype(kcat_ref.dtype)
        kcat_ref[:, r:r + LANES] = kr.astype(kcat_ref.dtype)


def _mla_proj(p, cos, sin, pos_blocks, qa_norm, kva_norm, w_nope, w_rope, w_rope_sw, w_kv, per_head, tm=256):
    t = p.shape[0]
    hq = MLA_HEADS * MLA_QK
    dqk = MLA_NOPE + LANES
    const2 = lambda i: (0, 0)
    row_block = lambda width: pl.BlockSpec((tm, width), lambda i: (i, 0))
    head_block = lambda width: pl.BlockSpec((MLA_HEADS, tm, width), lambda i: (0, i, 0))
    out_specs = [None, row_block(MLA_KV_LORA), row_block(LANES)]
    out_shape = [None, jax.ShapeDtypeStruct((t, MLA_KV_LORA), F32), jax.ShapeDtypeStruct((t, LANES), F32)]
    if per_head:
        out_specs[0] = head_block(dqk)
        out_shape[0] = jax.ShapeDtypeStruct((MLA_HEADS, t, dqk), BF16)
        out_specs += [head_block(dqk), head_block(MLA_V)]
        out_shape += [jax.ShapeDtypeStruct((MLA_HEADS, t, dqk), BF16), jax.ShapeDtypeStruct((MLA_HEADS, t, MLA_V), BF16)]
    else:
        out_specs[0] = row_block(hq)
        out_shape[0] = jax.ShapeDtypeStruct((t, hq), BF16)
        out_specs += [row_block(MLA_QK)]
        out_shape += [jax.ShapeDtypeStruct((t, MLA_QK), BF16)]
    return pl.pallas_call(
        functools.partial(_mla_proj_kernel, per_head=per_head),
        grid=(t // tm,),
        in_specs=[pl.BlockSpec((tm, p.shape[1]), lambda i: (i, 0)),
                  pl.BlockSpec((tm, LANES), lambda i: (i % pos_blocks, 0)),
                  pl.BlockSpec((tm, LANES), lambda i: (i % pos_blocks, 0)),
                  pl.BlockSpec((1, MLA_Q_LORA), const2),
                  pl.BlockSpec((1, MLA_KV_LORA), const2),
                  pl.BlockSpec(w_nope.shape, const2),
                  pl.BlockSpec(w_rope.shape, const2),
                  pl.BlockSpec(w_rope_sw.shape, const2),
                  pl.BlockSpec(w_kv.shape, lambda i: (0,) * w_kv.ndim)],
        out_specs=out_specs,
        out_shape=out_shape,
        compiler_params=_params(1),
        name="mla_proj",
    )(p, cos, sin, qa_norm.reshape(1, -1), kva_norm.reshape(1, -1), w_nope, w_rope, w_rope_sw, w_kv)


def _flash_update(s, v, m_ref, l_ref, acc_ref):
    m_prev = m_ref[...]
    m_new = jnp.maximum(m_prev, jnp.max(s, axis=1, keepdims=True))
    alpha = jnp.exp(m_prev - m_new)
    p = jnp.exp(s - _lane_tile(m_new, s.shape[1] // LANES))
    l_ref[...] = alpha * l_ref[...] + jnp.sum(p, axis=1, keepdims=True)
    acc_ref[...] = acc_ref[...] * _lane_tile(alpha, acc_ref.shape[1] // LANES) + jnp.dot(
        p.astype(BF16), v, preferred_element_type=F32)
    m_ref[...] = m_new


def _flash_init(m_sc, l_sc, acc_sc):
    m_sc[...] = jnp.full_like(m_sc, NEG_INF)
    l_sc[...] = jnp.zeros_like(l_sc)
    acc_sc[...] = jnp.zeros_like(acc_sc)


def _flash_result(l_ref, acc_ref):
    return acc_ref[...] / _lane_tile(l_ref[...], acc_ref.shape[1] // LANES)


def _mla_prompt_kernel(b_tab, qi_tab, ki_tab, q_ref, k_ref, v_ref, o_ref, m_sc, l_sc, acc_sc, *, tq, tk):
    step = pl.program_id(0)
    qi = qi_tab[step]
    ki = ki_tab[step]
    last = ki == (qi * tq) // tk

    @pl.when(ki == 0)
    def _():
        _flash_init(m_sc, l_sc, acc_sc)

    def run(masked):
        if masked:
            q_pos = qi * tq + lax.broadcasted_iota(jnp.int32, (tq, tk), 0)
            k_pos = ki * tk + lax.broadcasted_iota(jnp.int32, (tq, tk), 1)
            visible = k_pos <= q_pos
        for h in range(MLA_HEADS):
            s = lax.dot_general(q_ref[h], k_ref[h], (((1,), (1,)), ((), ())), preferred_element_type=F32)
            if masked:
                s = jnp.where(visible, s, NEG_INF)
            _flash_update(s, v_ref[h], m_sc.at[h], l_sc.at[h], acc_sc.at[h])
            if masked:
                o_ref[:, h * MLA_V:(h + 1) * MLA_V] = _flash_result(l_sc.at[h], acc_sc.at[h]).astype(o_ref.dtype)

    pl.when(last)(lambda: run(True))
    pl.when(jnp.logical_not(last))(lambda: run(False))


def _mla_prompt_attn(q3, k3, v3, n_seq, seq_len, tq=512, tk=512):
    dqk = q3.shape[2]
    nq, nk = seq_len // tq, seq_len // tk
    steps = [(b, qi, ki) for b in range(n_seq) for qi in range(nq) for ki in range((qi * tq) // tk + 1)]
    b_tab, qi_tab, ki_tab = (jnp.asarray(np.array(col, np.int32)) for col in zip(*steps))
    kern = functools.partial(_mla_prompt_kernel, tq=tq, tk=tk)
    grid_spec = pltpu.PrefetchScalarGridSpec(
        num_scalar_prefetch=3,
        grid=(len(steps),),
        in_specs=[pl.BlockSpec((MLA_HEADS, tq, dqk), lambda s, bt, qt, kt: (0, bt[s] * nq + qt[s], 0)),
                  pl.BlockSpec((MLA_HEADS, tk, dqk), lambda s, bt, qt, kt: (0, bt[s] * nk + kt[s], 0)),
                  pl.BlockSpec((MLA_HEADS, tk, MLA_V), lambda s, bt, qt, kt: (0, bt[s] * nk + kt[s], 0))],
        out_specs=pl.BlockSpec((tq, MLA_HEADS * MLA_V), lambda s, bt, qt, kt: (bt[s] * nq + qt[s], 0)),
        scratch_shapes=[pltpu.VMEM((MLA_HEADS, tq, LANES), F32), pltpu.VMEM((MLA_HEADS, tq, LANES), F32),
                        pltpu.VMEM((MLA_HEADS, tq, MLA_V), F32)])
    return pl.pallas_call(
        kern,
        grid_spec=grid_spec,
        out_shape=jax.ShapeDtypeStruct((q3.shape[1], MLA_HEADS * MLA_V), BF16),
        compiler_params=_params(1),
        name="mla_prompt_attn",
    )(b_tab, qi_tab, ki_tab, q3, k3, v3)


def _mla_sample_kernel(pt_ref, q_ref, knew_ref, *refs, n_groups, seq_len):
    kv_refs = refs[:PAGES_PER_STEP]
    krt_refs = refs[PAGES_PER_STEP:2 * PAGES_PER_STEP]
    o_ref, kv_sc, krt_sc, m_sc, l_sc, acc_sc = refs[2 * PAGES_PER_STEP:]
    g = pl.program_id(1)
    r = MLA_KV_LORA
    nt = (((1,), (1,)), ((), ()))

    @pl.when(g == 0)
    def _():
        _flash_init(m_sc, l_sc, acc_sc)

    for i in range(PAGES_PER_STEP):
        kv_sc[i * PAGE_SIZE:(i + 1) * PAGE_SIZE, :] = kv_refs[i][...].astype(BF16)
        krt_sc[:, i * PAGE_SIZE:(i + 1) * PAGE_SIZE] = krt_refs[i][...].astype(BF16)
    q = q_ref[...]
    kv = kv_sc[...]
    s = (lax.dot_general(q[:, 0:r], kv, nt, preferred_element_type=F32)
         + jnp.dot(q[:, r:r + MLA_ROPE], krt_sc[...], preferred_element_type=F32))
    _flash_update(s, kv, m_sc, l_sc, acc_sc)

    @pl.when(g == n_groups - 1)
    def _():
        knew = knew_ref[...]
        n_new = knew.shape[0]
        s_new = lax.dot_general(q, knew, nt, preferred_element_type=F32)
        rows = s_new.shape[0]
        q_t = _shr(lax.broadcasted_iota(jnp.int32, (rows, n_new), 0), MLA_HEADS)
        k_t = lax.broadcasted_iota(jnp.int32, (rows, n_new), 1)
        s_new = jnp.where((k_t <= q_t) & (k_t < seq_len), s_new, NEG_INF)
        m_old = m_sc[...]
        m_fin = jnp.maximum(m_old, jnp.max(s_new, axis=1, keepdims=True))
        a_fin = jnp.exp(m_old - m_fin)
        p_new = jnp.exp(s_new - m_fin[:, 0:n_new])
        l_sc[...] = a_fin * l_sc[...] + jnp.sum(p_new, axis=1, keepdims=True)
        acc_sc[...] = acc_sc[...] * _lane_tile(a_fin, r // LANES) + jnp.dot(
            p_new.astype(BF16), knew[:, 0:r], preferred_element_type=F32)
        o_ref[...] = _flash_result(l_sc, acc_sc).astype(o_ref.dtype)


def _mla_sample_attn(q2d, knew, cache_kv, cache_krt, page_table, layer_j, seq_len):
    b, n_pages = page_table.shape
    n_groups = n_pages // PAGES_PER_STEP
    rows = seq_len * MLA_HEADS
    keys = PAGES_PER_STEP * PAGE_SIZE
    pt_flat = page_table.reshape(-1)

    def page_spec(i, shape):
        return pl.BlockSpec(
            (None, None) + shape,
            lambda bi, gi, pt: (layer_j, pt[bi * n_pages + gi * PAGES_PER_STEP + i], 0, 0))

    kern = functools.partial(_mla_sample_kernel, n_groups=n_groups, seq_len=seq_len)
    grid_spec = pltpu.PrefetchScalarGridSpec(
        num_scalar_prefetch=1,
        grid=(b, n_groups),
        in_specs=([pl.BlockSpec((rows, MLA_QK), lambda bi, gi, pt: (bi, 0)),
                   pl.BlockSpec((None, knew.shape[1], MLA_QK), lambda bi, gi, pt: (bi, 0, 0))]
                  + [page_spec(i, (PAGE_SIZE, MLA_KV_LORA)) for i in range(PAGES_PER_STEP)]
                  + [page_spec(i, (MLA_ROPE, PAGE_SIZE)) for i in range(PAGES_PER_STEP)]),
        out_specs=pl.BlockSpec((rows, MLA_KV_LORA), lambda bi, gi, pt: (bi, 0)),
        scratch_shapes=[pltpu.VMEM((keys, MLA_KV_LORA), BF16), pltpu.VMEM((MLA_ROPE, keys), BF16),
                        pltpu.VMEM((rows, LANES), F32), pltpu.VMEM((rows, LANES), F32),
                        pltpu.VMEM((rows, MLA_KV_LORA), F32)])
    return pl.pallas_call(
        kern,
        grid_spec=grid_spec,
        out_shape=jax.ShapeDtypeStruct((q2d.shape[0], MLA_KV_LORA), BF16),
        compiler_params=_params(2),
        name="mla_sample_attn",
    )(pt_flat, q2d, knew, *([cache_kv] * PAGES_PER_STEP), *([cache_krt] * PAGES_PER_STEP))


def _mla_out_kernel(ol_ref, wuv_ref, wo_ref, x_ref, gt_ref, gpost_ref, o_ref, o_sc, out_sc, stat_sc, *, grp,
                    head_major):
    r = MLA_KV_LORA
    for h in range(MLA_HEADS):
        o_lat = ol_ref[h] if head_major else ol_ref[:, h * r:(h + 1) * r]
        o_h = jnp.dot(o_lat, wuv_ref[h], preferred_element_type=F32)
        o_sc[:, h * MLA_V:(h + 1) * MLA_V] = o_h.astype(BF16)
    out_sc[...] = jnp.dot(o_sc[...], wo_ref[...], preferred_element_type=F32)
    _ada_out_rows(x_ref, out_sc, gpost_ref, gt_ref, grp.row(), 1.0, stat_sc, o_ref)


def _mla_out(o_lat, w_uv, w_o, x, grp, layer, g_post, tm=256):
    t, d = x.shape
    head_major = o_lat.ndim == 3
    sub = _Group(grp.mod, tm, None if grp.tiles_per_mod_row is None else grp.tiles_per_mod_row * grp.tm // tm)
    kern = functools.partial(_mla_out_kernel, grp=sub, head_major=head_major)
    if head_major:
        ol_spec = pl.BlockSpec((MLA_HEADS, tm, MLA_KV_LORA), lambda i: (0, i, 0))
    else:
        ol_spec = pl.BlockSpec((tm, o_lat.shape[1]), lambda i: (i, 0))
    return pl.pallas_call(
        kern,
        grid=(t // tm,),
        in_specs=[ol_spec,
                  pl.BlockSpec(w_uv.shape, lambda i: (0, 0, 0)),
                  pl.BlockSpec(w_o.shape, lambda i: (0, 0)),
                  pl.BlockSpec((tm, d), lambda i: (i, 0)),
                  sub.mod_spec(layer, 3 + 2),
                  pl.BlockSpec((1, d), lambda i: (0, 0))],
        out_specs=pl.BlockSpec((tm, d), lambda i: (i, 0)),
        out_shape=jax.ShapeDtypeStruct((t, d), F32),
        scratch_shapes=[pltpu.VMEM((tm, MLA_HEADS * MLA_V), BF16), pltpu.VMEM((tm, d), F32),
                        pltpu.VMEM((tm, LANES), F32)],
        compiler_params=_params(1),
        name="mla_out",
    )(o_lat, w_uv, w_o, x, sub.mod, g_post.reshape(1, d))


def _t5_buckets(delta):
    n = np.maximum(delta, 0)
    max_exact = N_BUCKETS // 2
    log_ratio = np.log(np.maximum(n, 1).astype(np.float64) / max_exact) / math.log(BUCKET_MAX_DIST / max_exact)
    large = np.minimum(max_exact + (log_ratio * (N_BUCKETS - max_exact)).astype(np.int64), N_BUCKETS - 1)
    return np.where(n < max_exact, n, large).astype(np.int32)


def _swa_bias_table(rel_bias, delta, valid):
    lq, lk = delta.shape
    one_hot = (jnp.asarray(_t5_buckets(delta))[None] == jnp.arange(N_BUCKETS)[:, None, None]).astype(F32)
    bias = jnp.einsum("nh,nqk->hqk", rel_bias.astype(F32), one_hot, precision=lax.Precision.HIGHEST)
    bias = jnp.where(jnp.asarray(valid)[None], bias, NEG_INF)
    return bias.reshape(SWA_HEADS * lq, lk)


def _swa_sink_table(sinks, lq):
    return jnp.repeat(sinks.astype(F32), lq)[:, None]


def _swa_q_perm():
    perm = np.zeros(SWA_HEADS * SWA_HEAD_DIM, np.int32)
    for m in range(SWA_KV_HEADS // 2):
        for g in range(SWA_GROUP):
            for p in range(2):
                src = ((2 * m + p) * SWA_GROUP + g) * SWA_HEAD_DIM
                dst = (4 * m + g) * LANES + p * SWA_HEAD_DIM
                perm[dst:dst + SWA_HEAD_DIM] = np.arange(src, src + SWA_HEAD_DIM)
    return perm


def _rope_tables(pos):
    half = MLA_ROPE // 2
    inv = ROPE_THETA ** (-jnp.arange(half, dtype=F32) / half)
    ang = pos.astype(F32)[:, None] * inv[None, :]
    cos, sin = jnp.cos(ang), jnp.sin(ang)
    zeros = jnp.zeros((pos.shape[0], LANES - MLA_ROPE), F32)
    return (jnp.concatenate([cos, cos, zeros], axis=1), jnp.concatenate([-sin, sin, zeros], axis=1))


def _pad_rope_cols(w):
    half = MLA_ROPE // 2
    z = jnp.zeros((w.shape[0], LANES - MLA_ROPE), w.dtype)
    return (jnp.concatenate([w, z], axis=1),
            jnp.concatenate([w[:, half:], w[:, :half], z], axis=1))


def kernel(x_prompt, x_sample, state_swa_k, state_swa_v, state_conv, cache_mla_kv, cache_mla_kr, page_table,
           c_prompt, c_sample, ada_w, ada_b, norm_pre, norm_post, ffn_w_in, ffn_w_out,
           gm_w_in, gm_ln_g, gm_ln_b, gm_w_s, gm_b_s, gm_w_out,
           swa_w_qkv, swa_w_o, swa_sinks, rel_bias,
           sc_w_in, sc_conv, sc_w_out,
           mla_w_qa, mla_qa_norm, mla_w_qb, mla_w_kva, mla_kva_norm, mla_w_kvb, mla_w_o):
    n_seq, seq_len, d = x_prompt.shape
    n_dec, dec_len, _ = x_sample.shape
    depth = ada_w.shape[0]
    past_len = page_table.shape[1] * PAGE_SIZE
    t_p, t_s = n_seq * seq_len, n_dec * dec_len

    c_all = jnp.concatenate([jnp.repeat(c_sample, dec_len, axis=0), c_prompt,
                             jnp.zeros((8 - n_seq, d), F32)], axis=0)
    mod = _ada(c_all, ada_w, ada_b)
    tm_p, tm_s = 512, 256
    grp_p = _Group(mod[:, t_s:t_s + 8], tm_p, seq_len // tm_p)
    grp_s = _Group(mod, tm_s, None)
    groups = (grp_p, grp_s)

    xs = [x_prompt.reshape(t_p, d), x_sample.reshape(t_s, d)]
    outs = {}
    ffn_w = (ffn_w_in[0, 0].astype(BF16), ffn_w_out[0, 0].astype(BF16))

    def ffn_pair(xs, ffn_w, layer, which):
        nxt = (layer, 1) if which == 0 else (layer + 1, 0)
        next_w = (ffn_w_in, ffn_w_out) + nxt if nxt[0] < depth else None
        sub = 2 * which
        res = _half_ffn(xs[0], grp_p, layer, which, norm_pre[layer, sub], norm_post[layer, sub], *ffn_w,
                        next_w=next_w)
        x_s = _half_ffn(xs[1], grp_s, layer, which, norm_pre[layer, sub], norm_post[layer, sub], *ffn_w)
        if next_w is None:
            return [res, x_s], None
        return [res[0], x_s], (res[1], res[2])

    for i in range(depth):
        kind, j = i % 4, i // 4
        xs, ffn_w = ffn_pair(xs, ffn_w, i, 0)

        if kind == 0:
            w_in = gm_w_in[j].astype(BF16)
            w_out = gm_w_out[j].astype(BF16)
            new = []
            for x, g, chunk, sl, cps in ((xs[0], grp_p, GM_CHUNK, GM_CHUNK, seq_len // GM_CHUNK),
                                         (xs[1], grp_s, GM_CHUNK, dec_len, 1)):
                lc = min(sl, GM_CHUNK)
                m = jnp.tile(gm_w_s[j][:, :lc, :lc], (1, chunk // lc, chunk // lc))
                bias = jnp.broadcast_to(jnp.tile(gm_b_s[j][:, :lc], (1, chunk // lc))[:, :, None],
                                        (GM_GROUPS, chunk, LANES))
                uv = _norm_mod_matmul(x, g, i, norm_pre[i, 1], w_in, act="gelu", out_dtype=BF16)
                mixed, st = _gm_core(uv, gm_ln_g[j], gm_ln_b[j], m, bias, chunk, lc, cps)
                new.append(_matmul_residual(mixed, w_out, x, g, i, norm_post[i, 1]))
                outs.setdefault("gm", []).append(st)
            xs = new

        elif kind == 1:
            perm = _swa_q_perm()
            nq = SWA_HEADS * SWA_HEAD_DIM
            nkv = SWA_KV_HEADS * SWA_HEAD_DIM
            w_qkv = jnp.concatenate([swa_w_qkv[j][:, :nq][:, perm], swa_w_qkv[j][:, nq:]], axis=1).astype(BF16)
            w_o = swa_w_o[j][perm, :].astype(BF16)
            w = SWA_WINDOW
            i_q, i_k = np.arange(w), np.arange(2 * w)
            delta = w + i_q[:, None] - i_k[None, :]
            bias_p = _swa_bias_table(rel_bias, delta, (delta >= 0) & (delta < w))
            sink_p = _swa_sink_table(swa_sinks[j], w)
            qkv_p = _norm_mod_matmul(xs[0], grp_p, i, norm_pre[i, 1], w_qkv)
            o_p = _swa_prompt_core(qkv_p, bias_p, sink_p, seq_len)
            x_p = _matmul_residual(o_p, w_o, xs[0], grp_p, i, norm_post[i, 1])
            kv_p = qkv_p.reshape(n_seq, seq_len, -1)[:, seq_len - w:, nq:]
            outs.setdefault("swa_kp", []).append(kv_p[..., :nkv].reshape(n_seq, w, SWA_KV_HEADS, SWA_HEAD_DIM))
            outs.setdefault("swa_vp", []).append(kv_p[..., nkv:].reshape(n_seq, w, SWA_KV_HEADS, SWA_HEAD_DIM))
            lb = state_swa_k.shape[2]
            n_keys = lb + dec_len
            pad = (-n_keys) % 8
            i_q, i_k = np.arange(dec_len), np.arange(n_keys + pad)
            delta = lb + i_q[:, None] - i_k[None, :]
            valid = (delta >= 0) & (delta < w) & (i_k[None, :] < n_keys)
            bias_s = _swa_bias_table(rel_bias, delta, valid)
            sink_s = _swa_sink_table(swa_sinks[j], dec_len)
            qkv_s = _norm_mod_matmul(xs[1], grp_s, i, norm_pre[i, 1], w_qkv).reshape(n_dec, dec_len, -1)
            zpad = jnp.zeros((n_dec, pad, nkv), F32)
            kk = jnp.concatenate([state_swa_k[j].reshape(n_dec, lb, nkv), qkv_s[..., nq:nq + nkv], zpad], axis=1)
            vv = jnp.concatenate([state_swa_v[j].reshape(n_dec, lb, nkv), qkv_s[..., nq + nkv:], zpad], axis=1)
            q_s = qkv_s[..., :nq].reshape(n_dec, dec_len, SWA_KV_HEADS // 2, SWA_GROUP, LANES)
            q_s = q_s.transpose(0, 2, 3, 1, 4).reshape(n_dec, SWA_KV_HEADS // 2, SWA_GROUP * dec_len, LANES)
            o_s = _swa_sample_core(q_s, kk, vv, bias_s, sink_s)
            o_s = o_s.reshape(n_dec, SWA_KV_HEADS // 2, SWA_GROUP, dec_len, LANES).transpose(0, 3, 1, 2, 4)
            x_s = _matmul_residual(o_s.reshape(t_s, nq), w_o, xs[1], grp_s, i, norm_post[i, 1])
            outs.setdefault("swa_ks", []).append(kk[:, n_keys - lb:n_keys].reshape(n_dec, lb, SWA_KV_HEADS, SWA_HEAD_DIM))
            outs.setdefault("swa_vs", []).append(vv[:, n_keys - lb:n_keys].reshape(n_dec, lb, SWA_KV_HEADS, SWA_HEAD_DIM))
            xs = [x_p, x_s]

        elif kind == 2:
            w_in = sc_w_in[j].astype(BF16)
            w_out = sc_w_out[j].astype(BF16)
            c = sc_w_out.shape[1]
            g3_p = _norm_mod_matmul(xs[0], grp_p, i, norm_pre[i, 1], w_in, out_dtype=BF16)
            y_p, st_p = _conv_prompt_core(g3_p, sc_conv[j], seq_len)
            x_p = _matmul_residual(y_p, w_out, xs[0], grp_p, i, norm_post[i, 1])
            outs.setdefault("conv_p", []).append(st_p[:, 8 - (CONV_WIDTH - 1):])
            g3_s = _norm_mod_matmul(xs[1], grp_s, i, norm_pre[i, 1], w_in, out_dtype=BF16)
            y_s, st_s = _conv_sample_core(g3_s.reshape(n_dec, dec_len * 3 * c),
                                          state_conv[j].reshape(n_dec, (CONV_WIDTH - 1) * c), sc_conv[j], dec_len)
            x_s = _matmul_residual(y_s.reshape(t_s, c), w_out, xs[1], grp_s, i, norm_post[i, 1])
            outs.setdefault("conv_s", []).append(st_s.reshape(n_dec, CONV_WIDTH - 1, c))
            xs = [x_p, x_s]

        else:
            r = MLA_KV_LORA
            kr_pad, kr_sw = _pad_rope_cols(mla_w_kva[j][:, r:])
            w_p = jnp.concatenate([mla_w_qa[j], mla_w_kva[j][:, :r], kr_pad, kr_sw], axis=1).astype(BF16)
            w_qb = mla_w_qb[j].reshape(MLA_Q_LORA, MLA_HEADS, MLA_NOPE + MLA_ROPE)
            w_nope = w_qb[:, :, :MLA_NOPE].reshape(MLA_Q_LORA, -1).astype(BF16)
            rope_pairs = [_pad_rope_cols(w_qb[:, h, MLA_NOPE:]) for h in range(MLA_HEADS)]
            w_rope = jnp.concatenate([p[0] for p in rope_pairs], axis=1).astype(BF16)
            w_rope_sw = jnp.concatenate([p[1] for p in rope_pairs], axis=1).astype(BF16)
            w_kvb = mla_w_kvb[j].reshape(r, MLA_HEADS, MLA_NOPE + MLA_V)
            w_uk_t = w_kvb[:, :, :MLA_NOPE].transpose(1, 2, 0).astype(BF16)
            w_uv = w_kvb[:, :, MLA_NOPE:].transpose(1, 0, 2).astype(BF16)
            w_o = mla_w_o[j].astype(BF16)
            tm = 256
            cos_p, sin_p = _rope_tables(jnp.arange(seq_len))
            proj_p = _norm_mod_matmul(xs[0], grp_p, i, norm_pre[i, 1], w_p, tn=w_p.shape[1])
            q_p, ckv_p, kr_p, k_p, v_p = _mla_proj(proj_p, cos_p, sin_p, seq_len // tm, mla_qa_norm[j],
                                                   mla_kva_norm[j], w_nope, w_rope, w_rope_sw,
                                                   mla_w_kvb[j].astype(BF16), True, tm=tm)
            o_p = _mla_prompt_attn(q_p, k_p, v_p, n_seq, seq_len)
            x_p = _matmul_residual(o_p, w_o, xs[0], grp_p, i, norm_post[i, 1])
            outs.setdefault("mla_kvp", []).append(ckv_p.reshape(n_seq, seq_len, r))
            outs.setdefault("mla_krp", []).append(kr_p[:, :MLA_ROPE].reshape(n_seq, seq_len, MLA_ROPE))
            cos_s, sin_s = _rope_tables(past_len + jnp.arange(dec_len))
            cos_s, sin_s = jnp.tile(cos_s, (tm // dec_len, 1)), jnp.tile(sin_s, (tm // dec_len, 1))
            proj_s = _norm_mod_matmul(xs[1], grp_s, i, norm_pre[i, 1], w_p, tn=w_p.shape[1])
            q_s, ckv_s, kr_s, kcat_s = _mla_proj(proj_s, cos_s, sin_s, 1, mla_qa_norm[j],
                                                 mla_kva_norm[j], w_nope, w_rope, w_rope_sw, w_uk_t, False, tm=tm)
            knew = jnp.concatenate([kcat_s.reshape(n_dec, dec_len, MLA_QK),
                                    jnp.zeros((n_dec, 16 - dec_len, MLA_QK), BF16)], axis=1)
            ol_s = _mla_sample_attn(q_s.reshape(t_s * MLA_HEADS, MLA_QK), knew, cache_mla_kv,
                                    jnp.swapaxes(cache_mla_kr, 2, 3), page_table, j, dec_len)
            x_s = _mla_out(ol_s.reshape(t_s, MLA_HEADS * r), w_uv, w_o, xs[1], grp_s, i, norm_post[i, 1])
            outs.setdefault("mla_kvs", []).append(ckv_s.reshape(n_dec, dec_len, r))
            outs.setdefault("mla_krs", []).append(kr_s[:, :MLA_ROPE].reshape(n_dec, dec_len, MLA_ROPE))
            xs = [x_p, x_s]

        xs, ffn_w = ffn_pair(xs, ffn_w, i, 1)

    gm_p, gm_s = outs["gm"][0::2], outs["gm"][1::2]
    return (xs[0].reshape(n_seq, seq_len, d), xs[1].reshape(n_dec, dec_len, d),
            jnp.stack(gm_p),
            jnp.stack([s.reshape(n_dec, dec_len, -1) for s in gm_s]),
            jnp.stack(outs["swa_kp"]), jnp.stack(outs["swa_vp"]),
            jnp.stack(outs["swa_ks"]), jnp.stack(outs["swa_vs"]),
            jnp.stack(outs["conv_p"]), jnp.stack(outs["conv_s"]),
            jnp.stack(outs["mla_kvp"]), jnp.stack(outs["mla_krp"]),
            jnp.stack(outs["mla_kvs"]), jnp.stack(outs["mla_krs"]))
```

```python
import functools
import math

import numpy as np
import jax
import jax.numpy as jnp
from jax import lax
from jax.experimental import pallas as pl
from jax.experimental.pallas import tpu as pltpu

F32 = jnp.float32
BF16 = jnp.bfloat16

VMEM_LIMIT_BYTES = 56 * 1024 * 1024
LANES = 128

RMS_EPS = 1e-6
NEG_INF = -1e30
FFN_RES_W = 0.5

D_MODEL = 2048
GM_GROUPS = 8
GM_CHUNK = 128
SWA_WINDOW = 128
SWA_HEAD_DIM = 64
SWA_HEADS = 32
SWA_KV_HEADS = 8
SWA_GROUP = 4
SWA_SCALE = SWA_HEAD_DIM ** -0.5
N_BUCKETS = 32
BUCKET_MAX_DIST = 128
CONV_WIDTH = 3
MLA_HEADS = 16
MLA_Q_LORA = 512
MLA_KV_LORA = 512
MLA_NOPE = 128
MLA_ROPE = 64
MLA_V = 128
MLA_SCALE = (MLA_NOPE + MLA_ROPE) ** -0.5
MLA_QK = MLA_KV_LORA + LANES
ROPE_THETA = 10000.0
PAGE_SIZE = 128
PAGES_PER_STEP = 64


def _params(n_axes):
    return pltpu.CompilerParams(dimension_semantics=("arbitrary",) * n_axes,
                                vmem_limit_bytes=VMEM_LIMIT_BYTES)


def _rms(x, g):
    return x * lax.rsqrt(jnp.mean(x * x, axis=-1, keepdims=True) + RMS_EPS) * g


def _lane_tile(x, n):
    return x if n == 1 else jnp.concatenate([x] * n, axis=1)


class _Group:
    def __init__(self, mod, tm, tiles_per_mod_row):
        self.mod = mod
        self.tm = tm
        self.tiles_per_mod_row = tiles_per_mod_row

    def mod_spec(self, layer, col):
        if self.tiles_per_mod_row is None:
            return pl.BlockSpec((None, self.tm, D_MODEL), lambda i, *_: (layer, i, col))
        return pl.BlockSpec((None, 8, D_MODEL), lambda i, *_: (layer, 0, col))

    def row(self):
        if self.tiles_per_mod_row is None:
            return None
        return pl.program_id(0) // self.tiles_per_mod_row

    @staticmethod
    def read(ref, row):
        return ref[...] if row is None else ref[pl.ds(row, 1), :]


ROW_CHUNK = 16


def _for_row_chunks(n_rows, body, unroll):
    def step(c, carry):
        body(pl.ds(pl.multiple_of(c * ROW_CHUNK, ROW_CHUNK), ROW_CHUNK))
        return carry
    lax.fori_loop(0, n_rows // ROW_CHUNK, step, 0, unroll=unroll)


def _row_rms_scale(val_ref, stat_ref):
    width = val_ref.shape[1]

    def body(rs):
        parts = [val_ref[rs, i * LANES:(i + 1) * LANES] for i in range(width // LANES)]
        parts = [p * p for p in parts]
        while len(parts) > 1:
            parts = [a + b for a, b in zip(parts[0::2], parts[1::2])] + ([parts[-1]] if len(parts) % 2 else [])
        scale = lax.rsqrt(jnp.sum(parts[0], axis=-1, keepdims=True) * (1.0 / width) + RMS_EPS)
        stat_ref[rs, :] = jnp.broadcast_to(scale, (ROW_CHUNK, LANES))

    _for_row_chunks(val_ref.shape[0], body, unroll=8)


def _ada_in_rows(x_ref, gpre_ref, sc_ref, sh_ref, row, stat_ref, h_ref):
    n_tile = x_ref.shape[1] // LANES
    _row_rms_scale(x_ref, stat_ref)
    if row is not None:
        gain = gpre_ref[...] * (1.0 + sc_ref[pl.ds(row, 1), :])
        shift = sh_ref[pl.ds(row, 1), :]

    def body(rs):
        xn = x_ref[rs, :] * _lane_tile(stat_ref[rs, :], n_tile)
        if row is None:
            h = xn * (gpre_ref[...] * (1.0 + sc_ref[rs, :])) + sh_ref[rs, :]
        else:
            h = xn * gain + shift
        h_ref[rs, :] = h.astype(h_ref.dtype)

    _for_row_chunks(x_ref.shape[0], body, unroll=2)


def _ada_out_rows(x_ref, val_ref, gpost_ref, gt_ref, row, res_w, stat_ref, o_ref):
    n_tile = x_ref.shape[1] // LANES
    _row_rms_scale(val_ref, stat_ref)
    if row is not None:
        gain = gpost_ref[...] * (res_w * gt_ref[pl.ds(row, 1), :])

    def body(rs):
        vn = val_ref[rs, :] * _lane_tile(stat_ref[rs, :], n_tile)
        if row is None:
            o_ref[rs, :] = x_ref[rs, :] + vn * (gpost_ref[...] * (res_w * gt_ref[rs, :]))
        else:
            o_ref[rs, :] = x_ref[rs, :] + vn * gain

    _for_row_chunks(x_ref.shape[0], body, unroll=2)


def _shr(x, divisor):
    shift = divisor.bit_length() - 1
    assert 1 << shift == divisor
    return lax.shift_right_logical(x, shift)


def _ada_kernel(c_ref, w_ref, b_ref, o_ref, cs_sc):
    @pl.when((pl.program_id(0) == 0) & (pl.program_id(1) == 0))
    def _():
        c = c_ref[...]
        cs_sc[...] = (c * jax.nn.sigmoid(c)).astype(BF16)

    o_ref[...] = jnp.dot(cs_sc[...], w_ref[...].astype(BF16),
                         preferred_element_type=F32) + b_ref[...]


def _ada(c_all, ada_w, ada_b, tn=1024):
    n_layers, d, n = ada_w.shape
    rows = c_all.shape[0]
    return pl.pallas_call(
        _ada_kernel,
        grid=(n_layers, n // tn),
        in_specs=[pl.BlockSpec((rows, d), lambda l, j: (0, 0)),
                  pl.BlockSpec((None, d, tn), lambda l, j: (l, 0, j)),
                  pl.BlockSpec((None, 1, tn), lambda l, j: (l, 0, j))],
        out_specs=pl.BlockSpec((None, rows, tn), lambda l, j: (l, 0, j)),
        out_shape=jax.ShapeDtypeStruct((n_layers, rows, n), F32),
        scratch_shapes=[pltpu.VMEM((rows, d), BF16)],
        compiler_params=_params(2),
        name="ada_modulation",
    )(c_all, ada_w, ada_b.reshape(n_layers, 1, n))


def _ffn_kernel(x_ref, sh_ref, sc_ref, gt_ref, gpre_ref, gpost_ref, wg_ref, wu_ref, wo_ref, *rest, grp, n_f):
    if len(rest) == 7:
        nwi_ref, nwo_ref, o_ref, nwi_b_ref, nwo_b_ref, h_sc, stat_sc = rest
        nwi_b_ref[...] = nwi_ref[...].astype(BF16)
        nwo_b_ref[...] = nwo_ref[...].astype(BF16)
    else:
        o_ref, h_sc, stat_sc = rest
    j = pl.program_id(1)
    row = grp.row()
    acc_sc = o_ref

    @pl.when(j == 0)
    def _():
        _ada_in_rows(x_ref, gpre_ref, sc_ref, sh_ref, row, stat_sc, h_sc)
        acc_sc[...] = jnp.zeros_like(acc_sc)

    h = h_sc[...]
    tf = wg_ref.shape[1]
    acts = []
    for c in range(2):
        cols = slice(c * tf // 2, (c + 1) * tf // 2)
        g = jnp.dot(h, wg_ref[:, cols], preferred_element_type=F32)
        u = jnp.dot(h, wu_ref[:, cols], preferred_element_type=F32)
        acts.append((g * jax.nn.sigmoid(g) * u).astype(BF16))
    acc_sc[...] += jnp.dot(jnp.concatenate(acts, axis=1), wo_ref[...], preferred_element_type=F32)

    @pl.when(j == n_f - 1)
    def _():
        _ada_out_rows(x_ref, acc_sc, gpost_ref, gt_ref, row, FFN_RES_W, stat_sc, o_ref)


def _half_ffn(x, grp, layer, which, g_pre, g_post, w_in, w_out, next_w=None, tf=512):
    t, d = x.shape
    f = w_out.shape[0]
    tm, n_f = grp.tm, f // tf
    n_i = t // tm
    sub = 2 * which
    kern = functools.partial(_ffn_kernel, grp=grp, n_f=n_f)
    in_specs = [pl.BlockSpec((tm, d), lambda i, j: (i, 0)),
                grp.mod_spec(layer, sub * 3 + 0),
                grp.mod_spec(layer, sub * 3 + 1),
                grp.mod_spec(layer, sub * 3 + 2),
                pl.BlockSpec((1, d), lambda i, j: (0, 0)),
                pl.BlockSpec((1, d), lambda i, j: (0, 0)),
                pl.BlockSpec((d, tf), lambda i, j: (0, j)),
                pl.BlockSpec((d, tf), lambda i, j: (0, n_f + j)),
                pl.BlockSpec((tf, d), lambda i, j: (j, 0))]
    args = [x, grp.mod, grp.mod, grp.mod, g_pre.reshape(1, d), g_post.reshape(1, d), w_in, w_in, w_out]
    out_specs = [pl.BlockSpec((tm, d), lambda i, j: (i, 0))]
    out_shape = [jax.ShapeDtypeStruct((t, d), F32)]
    if next_w is not None:
        nw_in, nw_out, nl, nwh = next_w
        share = 2 if n_i * n_f > 100 else 1
        assert n_i % share == 0
        bi = (share * d // n_i, 2 * f // n_f)
        bo = (f // n_f, share * d // n_i)

        def blk(i, j):
            s = (i * n_f + j) // share
            return s // n_f, s % n_f

        in_specs += [pl.BlockSpec((None, None) + bi, lambda i, j: (nl, nwh) + blk(i, j)),
                     pl.BlockSpec((None, None) + bo, lambda i, j: (nl, nwh) + blk(i, j)[::-1])]
        args += [nw_in, nw_out]
        out_specs += [pl.BlockSpec(bi, blk), pl.BlockSpec(bo, lambda i, j: blk(i, j)[::-1])]
        out_shape += [jax.ShapeDtypeStruct((d, 2 * f), BF16), jax.ShapeDtypeStruct((f, d), BF16)]
    res = pl.pallas_call(
        kern,
        grid=(n_i, n_f),
        in_specs=in_specs,
        out_specs=out_specs,
        out_shape=out_shape,
        scratch_shapes=[pltpu.VMEM((tm, d), BF16), pltpu.VMEM((tm, LANES), F32)],
        compiler_params=_params(2),
        name="half_ffn",
    )(*args)
    return res if next_w is not None else res[0]


def _nmm_kernel(x_ref, sh_ref, sc_ref, gpre_ref, w_ref, o_ref, h_sc, stat_sc, *, grp, act):
    row = grp.row()

    @pl.when(pl.program_id(1) == 0)
    def _():
        _ada_in_rows(x_ref, gpre_ref, sc_ref, sh_ref, row, stat_sc, h_sc)

    y = jnp.dot(h_sc[...], w_ref[...], preferred_element_type=F32)
    if act == "gelu":
        y = jax.nn.gelu(y, approximate=True)
    o_ref[...] = y.astype(o_ref.dtype)


def _norm_mod_matmul(x, grp, layer, g_pre, w, act=None, tn=None, out_dtype=F32):
    t, d = x.shape
    n = w.shape[1]
    if tn is None:
        tn = max(c for c in range(LANES, min(n, 2048) + 1, LANES) if n % c == 0)
    tm = grp.tm
    kern = functools.partial(_nmm_kernel, grp=grp, act=act)
    return pl.pallas_call(
        kern,
        grid=(t // tm, n // tn),
        in_specs=[pl.BlockSpec((tm, d), lambda i, j: (i, 0)),
                  grp.mod_spec(layer, 3 + 0),
                  grp.mod_spec(layer, 3 + 1),
                  pl.BlockSpec((1, d), lambda i, j: (0, 0)),
                  pl.BlockSpec((d, tn), lambda i, j: (0, j))],
        out_specs=pl.BlockSpec((tm, tn), lambda i, j: (i, j)),
        out_shape=jax.ShapeDtypeStruct((t, n), out_dtype),
        scratch_shapes=[pltpu.VMEM((tm, d), BF16), pltpu.VMEM((tm, LANES), F32)],
        compiler_params=_params(2),
        name="norm_mod_matmul",
    )(x, grp.mod, grp.mod, g_pre.reshape(1, d), w)


def _mres_kernel(a_ref, w_ref, x_ref, gt_ref, gpost_ref, o_ref, acc_sc, stat_sc, *, grp, n_k):
    k = pl.program_id(1)
    row = grp.row()

    @pl.when(k == 0)
    def _():
        acc_sc[...] = jnp.zeros_like(acc_sc)

    acc_sc[...] += jnp.dot(a_ref[...].astype(BF16), w_ref[...], preferred_element_type=F32)

    @pl.when(k == n_k - 1)
    def _():
        _ada_out_rows(x_ref, acc_sc, gpost_ref, gt_ref, row, 1.0, stat_sc, o_ref)


def _matmul_residual(a, w, x, grp, layer, g_post, tk=2048):
    t, kdim = a.shape
    d = w.shape[1]
    tm, n_k = grp.tm, kdim // tk
    kern = functools.partial(_mres_kernel, grp=grp, n_k=n_k)
    return pl.pallas_call(
        kern,
        grid=(t // tm, n_k),
        in_specs=[pl.BlockSpec((tm, tk), lambda i, k: (i, k)),
                  pl.BlockSpec((tk, d), lambda i, k: (k, 0)),
                  pl.BlockSpec((tm, d), lambda i, k: (i, 0)),
                  grp.mod_spec(layer, 3 + 2),
                  pl.BlockSpec((1, d), lambda i, k: (0, 0))],
        out_specs=pl.BlockSpec((tm, d), lambda i, k: (i, 0)),
        out_shape=jax.ShapeDtypeStruct((t, d), F32),
        scratch_shapes=[pltpu.VMEM((tm, d), F32), pltpu.VMEM((tm, LANES), F32)],
        compiler_params=_params(2),
        name="matmul_residual",
    )(a, w, x, grp.mod, g_post.reshape(1, d))


def _gm_kernel(u_ref, v_ref, lng_ref, lnb_ref, m_ref, b_ref, o_ref, st_ref, *, seq_len):
    v = v_ref[...].astype(F32)
    vc = v - jnp.mean(v, axis=-1, keepdims=True)
    vn = vc * lax.rsqrt(jnp.mean(vc * vc, axis=-1, keepdims=True) + RMS_EPS) * lng_ref[...] + lnb_ref[...]
    st_ref[...] = vn
    vnb = vn.astype(BF16)
    c = v.shape[0]
    gw = v.shape[1] // GM_GROUPS
    row = lax.broadcasted_iota(jnp.int32, (c, c), 0)
    col = lax.broadcasted_iota(jnp.int32, (c, c), 1)
    keep = (col <= row) & (_shr(row, seq_len) == _shr(col, seq_len))
    for g in range(GM_GROUPS):
        mg = jnp.where(keep, m_ref[g], 0.0).astype(BF16)
        mixed = jnp.dot(mg, vnb[:, g * gw:(g + 1) * gw], preferred_element_type=F32)
        mixed = mixed + _lane_tile(b_ref[g], gw // LANES)
        o_ref[:, g * gw:(g + 1) * gw] = (u_ref[:, g * gw:(g + 1) * gw].astype(F32) * mixed).astype(o_ref.dtype)


def _gm_core(uv, ln_g, ln_b, m, bias, chunk, seq_len, chunks_per_state):
    t = uv.shape[0]
    w = uv.shape[1] // 2
    n_chunks = t // chunk
    n_states = n_chunks // chunks_per_state
    kern = functools.partial(_gm_kernel, seq_len=seq_len)
    return pl.pallas_call(
        kern,
        grid=(n_chunks,),
        in_specs=[pl.BlockSpec((chunk, w), lambda c: (c, 0)),
                  pl.BlockSpec((chunk, w), lambda c: (c, 1)),
                  pl.BlockSpec((1, w), lambda c: (0, 0)),
                  pl.BlockSpec((1, w), lambda c: (0, 0)),
                  pl.BlockSpec((GM_GROUPS, chunk, chunk), lambda c: (0, 0, 0)),
                  pl.BlockSpec((GM_GROUPS, chunk, LANES), lambda c: (0, 0, 0))],
        out_specs=[pl.BlockSpec((chunk, w), lambda c: (c, 0)),
                   pl.BlockSpec((None, chunk, w), lambda c: (c // chunks_per_state, 0, 0))],
        out_shape=[jax.ShapeDtypeStruct((t, w), BF16),
                   jax.ShapeDtypeStruct((n_states, chunk, w), F32)],
        compiler_params=_params(1),
        name="gm_core",
    )(uv, uv, ln_g.reshape(1, w), ln_b.reshape(1, w), m, bias)


def _softmax_sink(s, sink):
    mx = jnp.maximum(jnp.max(s, axis=-1, keepdims=True), sink)
    p = jnp.exp(s - mx)
    denom = jnp.sum(p, axis=-1, keepdims=True) + jnp.exp(sink - mx)
    return p / denom


def _half_lane_mask(rows, parity):
    lane = lax.broadcasted_iota(jnp.int32, (rows, LANES), 1)
    return (lane >= SWA_HEAD_DIM) if parity else (lane < SWA_HEAD_DIM)


def _swa_attend(q_blocks, k, v, bias_ref, sink_ref, masked_keys=None):
    rows = q_blocks[0].shape[0]
    n_keys = k.shape[0]
    logits, values = [], []
    for m in range(SWA_KV_HEADS // 2):
        k2 = k[:, m * LANES:(m + 1) * LANES]
        v2 = v[:, m * LANES:(m + 1) * LANES]
        for parity in range(2):
            keep = _half_lane_mask(n_keys, parity)
            kx = jnp.where(keep, k2, 0.0).astype(BF16)
            values.append(jnp.where(keep, v2, 0.0).astype(BF16))
            logits.append(lax.dot_general(q_blocks[m], kx, (((1,), (1,)), ((), ())), preferred_element_type=F32))
    s = jnp.concatenate(logits, axis=0) + bias_ref[...]
    if masked_keys is not None:
        key_col = lax.broadcasted_iota(jnp.int32, s.shape, 1)
        s = jnp.where(key_col < masked_keys, NEG_INF, s)
    p = _softmax_sink(s, sink_ref[...]).astype(BF16)
    outs = []
    for m in range(SWA_KV_HEADS // 2):
        o_m = None
        for parity in range(2):
            idx = 2 * m + parity
            o_p = jnp.dot(p[idx * rows:(idx + 1) * rows], values[idx], preferred_element_type=F32)
            o_m = o_p if o_m is None else o_m + o_p
        outs.append(o_m)
    return outs


def _swa_prompt_kernel(q_ref, kp_ref, ko_ref, vp_ref, vo_ref, bias_ref, sink_ref, o_ref, *, blocks_per_seq):
    w = SWA_WINDOW
    first = (pl.program_id(0) % blocks_per_seq) == 0
    q = (q_ref[...] * SWA_SCALE).astype(BF16)
    k = jnp.concatenate([kp_ref[...], ko_ref[...]], axis=0)
    v = jnp.concatenate([vp_ref[...], vo_ref[...]], axis=0)
    q_blocks = [jnp.concatenate([q[:, (4 * m + g) * LANES:(4 * m + g + 1) * LANES] for g in range(SWA_GROUP)],
                                axis=0) for m in range(SWA_KV_HEADS // 2)]
    outs = _swa_attend(q_blocks, k, v, bias_ref, sink_ref, masked_keys=jnp.where(first, w, 0))
    for m, o_m in enumerate(outs):
        for g in range(SWA_GROUP):
            o_ref[:, (4 * m + g) * LANES:(4 * m + g + 1) * LANES] = o_m[g * w:(g + 1) * w].astype(o_ref.dtype)


def _swa_prompt_core(qkv, bias, sink, seq_len):
    t = qkv.shape[0]
    w = SWA_WINDOW
    nq = SWA_HEADS * SWA_HEAD_DIM
    nkv = SWA_KV_HEADS * SWA_HEAD_DIM
    bps = seq_len // w
    kcol, vcol = nq // nkv, nq // nkv + 1

    def prev(i):
        return jnp.maximum(i - 1, 0)

    kern = functools.partial(_swa_prompt_kernel, blocks_per_seq=bps)
    return pl.pallas_call(
        kern,
        grid=(t // w,),
        in_specs=[pl.BlockSpec((w, nq), lambda i: (i, 0)),
                  pl.BlockSpec((w, nkv), lambda i: (prev(i), kcol)),
                  pl.BlockSpec((w, nkv), lambda i: (i, kcol)),
                  pl.BlockSpec((w, nkv), lambda i: (prev(i), vcol)),
                  pl.BlockSpec((w, nkv), lambda i: (i, vcol)),
                  pl.BlockSpec(bias.shape, lambda i: (0, 0)),
                  pl.BlockSpec(sink.shape, lambda i: (0, 0))],
        out_specs=pl.BlockSpec((w, nq), lambda i: (i, 0)),
        out_shape=jax.ShapeDtypeStruct((t, nq), BF16),
        compiler_params=_params(1),
        name="swa_prompt_core",
    )(qkv, qkv, qkv, qkv, qkv, bias, sink)


def _swa_sample_kernel(q_ref, k_ref, v_ref, bias_ref, sink_ref, o_ref, *, bb):
    def body(b, carry):
        q_blocks = [(q_ref[b, m] * SWA_SCALE).astype(BF16) for m in range(SWA_KV_HEADS // 2)]
        outs = _swa_attend(q_blocks, k_ref[b], v_ref[b], bias_ref, sink_ref)
        for m, o_m in enumerate(outs):
            o_ref[b, m] = o_m.astype(o_ref.dtype)
        return carry

    lax.fori_loop(0, bb, body, 0, unroll=2)


def _swa_sample_core(q, kk, vv, bias, sink, bb=8):
    b, n_pair, rows, _ = q.shape
    n_keys = kk.shape[1]
    nkv = kk.shape[2]
    kern = functools.partial(_swa_sample_kernel, bb=bb)
    return pl.pallas_call(
        kern,
        grid=(b // bb,),
        in_specs=[pl.BlockSpec((bb, n_pair, rows, LANES), lambda i: (i, 0, 0, 0)),
                  pl.BlockSpec((bb, n_keys, nkv), lambda i: (i, 0, 0)),
                  pl.BlockSpec((bb, n_keys, nkv), lambda i: (i, 0, 0)),
                  pl.BlockSpec(bias.shape, lambda i: (0, 0)),
                  pl.BlockSpec(sink.shape, lambda i: (0, 0))],
        out_specs=pl.BlockSpec((bb, n_pair, rows, LANES), lambda i: (i, 0, 0, 0)),
        out_shape=jax.ShapeDtypeStruct(q.shape, BF16),
        compiler_params=_params(1),
        name="swa_sample_core",
    )(q, kk, vv, bias, sink)


def _conv_prompt_kernel(gb_ref, gc_ref, z_ref, hc_ref, hz_ref, cw_ref, o_ref, st_ref, zp_sc, *, tiles_per_seq):
    tm = gb_ref.shape[0]
    hr = hc_ref.shape[0]
    first = (pl.program_id(0) % tiles_per_seq) == 0
    zz = gc_ref[...].astype(F32) * z_ref[...].astype(F32)
    halo = jnp.where(first, 0.0, hc_ref[...].astype(F32) * hz_ref[...].astype(F32))
    zp_sc[0:8, :] = halo[hr - 8:hr]
    zp_sc[8:8 + tm, :] = zz
    y = cw_ref[2:3, :] * zz + cw_ref[1:2, :] * zp_sc[7:7 + tm, :] + cw_ref[0:1, :] * zp_sc[6:6 + tm, :]
    o_ref[...] = (gb_ref[...].astype(F32) * y).astype(o_ref.dtype)
    st_ref[...] = zz[tm - 8:tm]


def _conv_prompt_core(g3, conv_w, seq_len, tm=256):
    t = g3.shape[0]
    c = g3.shape[1] // 3
    tps = seq_len // tm
    n_seq = t // seq_len

    hr = 16

    def halo(col):
        return pl.BlockSpec((hr, c), lambda i: (jnp.maximum(i * (tm // hr) - 1, 0), col))

    kern = functools.partial(_conv_prompt_kernel, tiles_per_seq=tps)
    return pl.pallas_call(
        kern,
        grid=(t // tm,),
        in_specs=[pl.BlockSpec((tm, c), lambda i: (i, 0)),
                  pl.BlockSpec((tm, c), lambda i: (i, 1)),
                  pl.BlockSpec((tm, c), lambda i: (i, 2)),
                  halo(1), halo(2),
                  pl.BlockSpec((CONV_WIDTH, c), lambda i: (0, 0))],
        out_specs=[pl.BlockSpec((tm, c), lambda i: (i, 0)),
                   pl.BlockSpec((None, 8, c), lambda i: (i // tps, 0, 0))],
        out_shape=[jax.ShapeDtypeStruct((t, c), BF16),
                   jax.ShapeDtypeStruct((n_seq, 8, c), F32)],
        scratch_shapes=[pltpu.VMEM((tm + 8, c), F32)],
        compiler_params=_params(1),
        name="conv_prompt_core",
    )(g3, g3, g3, g3, g3, conv_w)


def _conv_sample_kernel(g_ref, prev_ref, cw_ref, o_ref, st_ref, *, seq_len, c):
    zz = [prev_ref[:, 0:c], prev_ref[:, c:2 * c]]
    for t in range(seq_len):
        base = t * 3 * c
        zz.append(g_ref[:, base + c:base + 2 * c].astype(F32) * g_ref[:, base + 2 * c:base + 3 * c].astype(F32))
    for t in range(seq_len):
        y = cw_ref[2:3, :] * zz[t + 2] + cw_ref[1:2, :] * zz[t + 1] + cw_ref[0:1, :] * zz[t]
        o_ref[:, t * c:(t + 1) * c] = (g_ref[:, t * 3 * c:t * 3 * c + c].astype(F32) * y).astype(o_ref.dtype)
    st_ref[:, 0:c] = zz[seq_len]
    st_ref[:, c:2 * c] = zz[seq_len + 1]


def _conv_sample_core(g3, prev, conv_w, seq_len):
    b = g3.shape[0]
    c = g3.shape[1] // (3 * seq_len)
    kern = functools.partial(_conv_sample_kernel, seq_len=seq_len, c=c)
    return pl.pallas_call(
        kern,
        grid=(1,),
        in_specs=[pl.BlockSpec(g3.shape, lambda i: (0, 0)),
                  pl.BlockSpec(prev.shape, lambda i: (0, 0)),
                  pl.BlockSpec((CONV_WIDTH, c), lambda i: (0, 0))],
        out_specs=[pl.BlockSpec((b, seq_len * c), lambda i: (0, 0)),
                   pl.BlockSpec((b, 2 * c), lambda i: (0, 0))],
        out_shape=[jax.ShapeDtypeStruct((b, seq_len * c), BF16),
                   jax.ShapeDtypeStruct((b, 2 * c), F32)],
        compiler_params=_params(1),
        name="conv_sample_core",
    )(g3, prev, conv_w)


def _mla_proj_kernel(p_ref, cos_ref, sin_ref, qan_ref, kvn_ref, wn_ref, wr_ref, wrs_ref, wkv_ref, *out_refs, per_head):
    r = MLA_KV_LORA
    cos = cos_ref[...]
    sin = sin_ref[...]
    qa = _rms(p_ref[:, 0:MLA_Q_LORA], qan_ref[...]).astype(BF16)
    qn = jnp.dot(qa, wn_ref[...], preferred_element_type=F32)
    qr = jnp.dot(qa, wr_ref[...], preferred_element_type=F32)
    qrs = jnp.dot(qa, wrs_ref[...], preferred_element_type=F32)
    off = MLA_Q_LORA
    ckv = _rms(p_ref[:, off:off + r], kvn_ref[...])
    kr = p_ref[:, off + r:off + r + LANES] * cos + p_ref[:, off + r + LANES:off + r + 2 * LANES] * sin
    if per_head:
        q_ref, ckv_ref, kr_ref, k_ref, v_ref = out_refs
        kv_up = jnp.dot(ckv.astype(BF16), wkv_ref[...], preferred_element_type=F32)
        kr_b = kr.astype(k_ref.dtype)
    else:
        q_ref, ckv_ref, kr_ref, kcat_ref = out_refs
    for h in range(MLA_HEADS):
        sl = slice(h * LANES, (h + 1) * LANES)
        q_rope = ((qr[:, sl] * cos + qrs[:, sl] * sin) * MLA_SCALE).astype(q_ref.dtype)
        if per_head:
            q_ref[h, :, 0:MLA_NOPE] = (qn[:, sl] * MLA_SCALE).astype(q_ref.dtype)
            q_ref[h, :, MLA_NOPE:MLA_NOPE + LANES] = q_rope
            base = h * (MLA_NOPE + MLA_V)
            k_ref[h, :, 0:MLA_NOPE] = kv_up[:, base:base + MLA_NOPE].astype(k_ref.dtype)
            k_ref[h, :, MLA_NOPE:MLA_NOPE + LANES] = kr_b
            v_ref[h] = kv_up[:, base + MLA_NOPE:base + MLA_NOPE + MLA_V].astype(v_ref.dtype)
        else:
            q_lat = jnp.dot(qn[:, sl].astype(BF16), wkv_ref[h], preferred_element_type=F32)
            q_ref[:, h * MLA_QK:h * MLA_QK + r] = (q_lat * MLA_SCALE).astype(q_ref.dtype)
            q_ref[:, h * MLA_QK + r:(h + 1) * MLA_QK] = q_rope
    ckv_ref[...] = ckv
    kr_ref[...] = kr
    if not per_head:
        kcat_ref[:, 0:r] = ckv.astype(kcat_ref.dtype)
        kcat_ref[:, r:r + LANES] = kr.astype(kcat_ref.dtype)


def _mla_proj(p, cos, sin, pos_blocks, qa_norm, kva_norm, w_nope, w_rope, w_rope_sw, w_kv, per_head, tm=256):
    t = p.shape[0]
    hq = MLA_HEADS * MLA_QK
    dqk = MLA_NOPE + LANES
    const2 = lambda i: (0, 0)
    row_block = lambda width: pl.BlockSpec((tm, width), lambda i: (i, 0))
    head_block = lambda width: pl.BlockSpec((MLA_HEADS, tm, width), lambda i: (0, i, 0))
    out_specs = [None, row_block(MLA_KV_LORA), row_block(LANES)]
    out_shape = [None, jax.ShapeDtypeStruct((t, MLA_KV_LORA), F32), jax.ShapeDtypeStruct((t, LANES), F32)]
    if per_head:
        out_specs[0] = head_block(dqk)
        out_shape[0] = jax.ShapeDtypeStruct((MLA_HEADS, t, dqk), BF16)
        out_specs += [head_block(dqk), head_block(MLA_V)]
        out_shape += [jax.ShapeDtypeStruct((MLA_HEADS, t, dqk), BF16), jax.ShapeDtypeStruct((MLA_HEADS, t, MLA_V), BF16)]
    else:
        out_specs[0] = row_block(hq)
        out_shape[0] = jax.ShapeDtypeStruct((t, hq), BF16)
        out_specs += [row_block(MLA_QK)]
        out_shape += [jax.ShapeDtypeStruct((t, MLA_QK), BF16)]
    return pl.pallas_call(
        functools.partial(_mla_proj_kernel, per_head=per_head),
        grid=(t // tm,),
        in_specs=[pl.BlockSpec((tm, p.shape[1]), lambda i: (i, 0)),
                  pl.BlockSpec((tm, LANES), lambda i: (i % pos_blocks, 0)),
                  pl.BlockSpec((tm, LANES), lambda i: (i % pos_blocks, 0)),
                  pl.BlockSpec((1, MLA_Q_LORA), const2),
                  pl.BlockSpec((1, MLA_KV_LORA), const2),
                  pl.BlockSpec(w_nope.shape, const2),
                  pl.BlockSpec(w_rope.shape, const2),
                  pl.BlockSpec(w_rope_sw.shape, const2),
                  pl.BlockSpec(w_kv.shape, lambda i: (0,) * w_kv.ndim)],
        out_specs=out_specs,
        out_shape=out_shape,
        compiler_params=_params(1),
        name="mla_proj",
    )(p, cos, sin, qa_norm.reshape(1, -1), kva_norm.reshape(1, -1), w_nope, w_rope, w_rope_sw, w_kv)


def _flash_update(s, v, m_ref, l_ref, acc_ref):
    m_prev = m_ref[...]
    m_new = jnp.maximum(m_prev, jnp.max(s, axis=1, keepdims=True))
    alpha = jnp.exp(m_prev - m_new)
    p = jnp.exp(s - _lane_tile(m_new, s.shape[1] // LANES))
    l_ref[...] = alpha * l_ref[...] + jnp.sum(p, axis=1, keepdims=True)
    acc_ref[...] = acc_ref[...] * _lane_tile(alpha, acc_ref.shape[1] // LANES) + jnp.dot(
        p.astype(BF16), v, preferred_element_type=F32)
    m_ref[...] = m_new


def _flash_init(m_sc, l_sc, acc_sc):
    m_sc[...] = jnp.full_like(m_sc, NEG_INF)
    l_sc[...] = jnp.zeros_like(l_sc)
    acc_sc[...] = jnp.zeros_like(acc_sc)


def _flash_result(l_ref, acc_ref):
    return acc_ref[...] / _lane_tile(l_ref[...], acc_ref.shape[1] // LANES)


def _mla_prompt_kernel(b_tab, qi_tab, ki_tab, q_ref, k_ref, v_ref, o_ref, m_sc, l_sc, acc_sc, *, tq, tk):
    step = pl.program_id(0)
    qi = qi_tab[step]
    ki = ki_tab[step]
    last = ki == (qi * tq) // tk

    @pl.when(ki == 0)
    def _():
        _flash_init(m_sc, l_sc, acc_sc)

    def run(masked):
        if masked:
            q_pos = qi * tq + lax.broadcasted_iota(jnp.int32, (tq, tk), 0)
            k_pos = ki * tk + lax.broadcasted_iota(jnp.int32, (tq, tk), 1)
            visible = k_pos <= q_pos
        for h in range(MLA_HEADS):
            s = lax.dot_general(q_ref[h], k_ref[h], (((1,), (1,)), ((), ())), preferred_element_type=F32)
            if masked:
                s = jnp.where(visible, s, NEG_INF)
            _flash_update(s, v_ref[h], m_sc.at[h], l_sc.at[h], acc_sc.at[h])
            if masked:
                o_ref[:, h * MLA_V:(h + 1) * MLA_V] = _flash_result(l_sc.at[h], acc_sc.at[h]).astype(o_ref.dtype)

    pl.when(last)(lambda: run(True))
    pl.when(jnp.logical_not(last))(lambda: run(False))


def _mla_prompt_attn(q3, k3, v3, n_seq, seq_len, tq=512, tk=512):
    dqk = q3.shape[2]
    nq, nk = seq_len // tq, seq_len // tk
    steps = [(b, qi, ki) for b in range(n_seq) for qi in range(nq) for ki in range((qi * tq) // tk + 1)]
    b_tab, qi_tab, ki_tab = (jnp.asarray(np.array(col, np.int32)) for col in zip(*steps))
    kern = functools.partial(_mla_prompt_kernel, tq=tq, tk=tk)
    grid_spec = pltpu.PrefetchScalarGridSpec(
        num_scalar_prefetch=3,
        grid=(len(steps),),
        in_specs=[pl.BlockSpec((MLA_HEADS, tq, dqk), lambda s, bt, qt, kt: (0, bt[s] * nq + qt[s], 0)),
                  pl.BlockSpec((MLA_HEADS, tk, dqk), lambda s, bt, qt, kt: (0, bt[s] * nk + kt[s], 0)),
                  pl.BlockSpec((MLA_HEADS, tk, MLA_V), lambda s, bt, qt, kt: (0, bt[s] * nk + kt[s], 0))],
        out_specs=pl.BlockSpec((tq, MLA_HEADS * MLA_V), lambda s, bt, qt, kt: (bt[s] * nq + qt[s], 0)),
        scratch_shapes=[pltpu.VMEM((MLA_HEADS, tq, LANES), F32), pltpu.VMEM((MLA_HEADS, tq, LANES), F32),
                        pltpu.VMEM((MLA_HEADS, tq, MLA_V), F32)])
    return pl.pallas_call(
        kern,
        grid_spec=grid_spec,
        out_shape=jax.ShapeDtypeStruct((q3.shape[1], MLA_HEADS * MLA_V), BF16),
        compiler_params=_params(1),
        name="mla_prompt_attn",
    )(b_tab, qi_tab, ki_tab, q3, k3, v3)


def _mla_sample_kernel(pt_ref, q_ref, knew_ref, *refs, n_groups, seq_len):
    kv_refs = refs[:PAGES_PER_STEP]
    krt_refs = refs[PAGES_PER_STEP:2 * PAGES_PER_STEP]
    o_ref, kv_sc, krt_sc, m_sc, l_sc, acc_sc = refs[2 * PAGES_PER_STEP:]
    g = pl.program_id(1)
    r = MLA_KV_LORA
    nt = (((1,), (1,)), ((), ()))

    @pl.when(g == 0)
    def _():
        _flash_init(m_sc, l_sc, acc_sc)

    for i in range(PAGES_PER_STEP):
        kv_sc[i * PAGE_SIZE:(i + 1) * PAGE_SIZE, :] = kv_refs[i][...].astype(BF16)
        krt_sc[:, i * PAGE_SIZE:(i + 1) * PAGE_SIZE] = krt_refs[i][...].astype(BF16)
    q = q_ref[...]
    kv = kv_sc[...]
    s = (lax.dot_general(q[:, 0:r], kv, nt, preferred_element_type=F32)
         + jnp.dot(q[:, r:r + MLA_ROPE], krt_sc[...], preferred_element_type=F32))
    _flash_update(s, kv, m_sc, l_sc, acc_sc)

    @pl.when(g == n_groups - 1)
    def _():
        knew = knew_ref[...]
        n_new = knew.shape[0]
        s_new = lax.dot_general(q, knew, nt, preferred_element_type=F32)
        rows = s_new.shape[0]
        q_t = _shr(lax.broadcasted_iota(jnp.int32, (rows, n_new), 0), MLA_HEADS)
        k_t = lax.broadcasted_iota(jnp.int32, (rows, n_new), 1)
        s_new = jnp.where((k_t <= q_t) & (k_t < seq_len), s_new, NEG_INF)
        m_old = m_sc[...]
        m_fin = jnp.maximum(m_old, jnp.max(s_new, axis=1, keepdims=True))
        a_fin = jnp.exp(m_old - m_fin)
        p_new = jnp.exp(s_new - m_fin[:, 0:n_new])
        l_sc[...] = a_fin * l_sc[...] + jnp.sum(p_new, axis=1, keepdims=True)
        acc_sc[...] = acc_sc[...] * _lane_tile(a_fin, r // LANES) + jnp.dot(
            p_new.astype(BF16), knew[:, 0:r], preferred_element_type=F32)
        o_ref[...] = _flash_result(l_sc, acc_sc).astype(o_ref.dtype)


def _mla_sample_attn(q2d, knew, cache_kv, cache_krt, page_table, layer_j, seq_len):
    b, n_pages = page_table.shape
    n_groups = n_pages // PAGES_PER_STEP
    rows = seq_len * MLA_HEADS
    keys = PAGES_PER_STEP * PAGE_SIZE
    pt_flat = page_table.reshape(-1)

    def page_spec(i, shape):
        return pl.BlockSpec(
            (None, None) + shape,
            lambda bi, gi, pt: (layer_j, pt[bi * n_pages + gi * PAGES_PER_STEP + i], 0, 0))

    kern = functools.partial(_mla_sample_kernel, n_groups=n_groups, seq_len=seq_len)
    grid_spec = pltpu.PrefetchScalarGridSpec(
        num_scalar_prefetch=1,
        grid=(b, n_groups),
        in_specs=([pl.BlockSpec((rows, MLA_QK), lambda bi, gi, pt: (bi, 0)),
                   pl.BlockSpec((None, knew.shape[1], MLA_QK), lambda bi, gi, pt: (bi, 0, 0))]
                  + [page_spec(i, (PAGE_SIZE, MLA_KV_LORA)) for i in range(PAGES_PER_STEP)]
                  + [page_spec(i, (MLA_ROPE, PAGE_SIZE)) for i in range(PAGES_PER_STEP)]),
        out_specs=pl.BlockSpec((rows, MLA_KV_LORA), lambda bi, gi, pt: (bi, 0)),
        scratch_shapes=[pltpu.VMEM((keys, MLA_KV_LORA), BF16), pltpu.VMEM((MLA_ROPE, keys), BF16),
                        pltpu.VMEM((rows, LANES), F32), pltpu.VMEM((rows, LANES), F32),
                        pltpu.VMEM((rows, MLA_KV_LORA), F32)])
    return pl.pallas_call(
        kern,
        grid_spec=grid_spec,
        out_shape=jax.ShapeDtypeStruct((q2d.shape[0], MLA_KV_LORA), BF16),
        compiler_params=_params(2),
        name="mla_sample_attn",
    )(pt_flat, q2d, knew, *([cache_kv] * PAGES_PER_STEP), *([cache_krt] * PAGES_PER_STEP))


def _mla_out_kernel(ol_ref, wuv_ref, wo_ref, x_ref, gt_ref, gpost_ref, o_ref, o_sc, out_sc, stat_sc, *, grp,
                    head_major):
    r = MLA_KV_LORA
    for h in range(MLA_HEADS):
        o_lat = ol_ref[h] if head_major else ol_ref[:, h * r:(h + 1) * r]
        o_h = jnp.dot(o_lat, wuv_ref[h], preferred_element_type=F32)
        o_sc[:, h * MLA_V:(h + 1) * MLA_V] = o_h.astype(BF16)
    out_sc[...] = jnp.dot(o_sc[...], wo_ref[...], preferred_element_type=F32)
    _ada_out_rows(x_ref, out_sc, gpost_ref, gt_ref, grp.row(), 1.0, stat_sc, o_ref)


def _mla_out(o_lat, w_uv, w_o, x, grp, layer, g_post, tm=256):
    t, d = x.shape
    head_major = o_lat.ndim == 3
    sub = _Group(grp.mod, tm, None if grp.tiles_per_mod_row is None else grp.tiles_per_mod_row * grp.tm // tm)
    kern = functools.partial(_mla_out_kernel, grp=sub, head_major=head_major)
    if head_major:
        ol_spec = pl.BlockSpec((MLA_HEADS, tm, MLA_KV_LORA), lambda i: (0, i, 0))
    else:
        ol_spec = pl.BlockSpec((tm, o_lat.shape[1]), lambda i: (i, 0))
    return pl.pallas_call(
        kern,
        grid=(t // tm,),
        in_specs=[ol_spec,
                  pl.BlockSpec(w_uv.shape, lambda i: (0, 0, 0)),
                  pl.BlockSpec(w_o.shape, lambda i: (0, 0)),
                  pl.BlockSpec((tm, d), lambda i: (i, 0)),
                  sub.mod_spec(layer, 3 + 2),
                  pl.BlockSpec((1, d), lambda i: (0, 0))],
        out_specs=pl.BlockSpec((tm, d), lambda i: (i, 0)),
        out_shape=jax.ShapeDtypeStruct((t, d), F32),
        scratch_shapes=[pltpu.VMEM((tm, MLA_HEADS * MLA_V), BF16), pltpu.VMEM((tm, d), F32),
                        pltpu.VMEM((tm, LANES), F32)],
        compiler_params=_params(1),
        name="mla_out",
    )(o_lat, w_uv, w_o, x, sub.mod, g_post.reshape(1, d))


def _t5_buckets(delta):
    n = np.maximum(delta, 0)
    max_exact = N_BUCKETS // 2
    log_ratio = np.log(np.maximum(n, 1).astype(np.float64) / max_exact) / math.log(BUCKET_MAX_DIST / max_exact)
    large = np.minimum(max_exact + (log_ratio * (N_BUCKETS - max_exact)).astype(np.int64), N_BUCKETS - 1)
    return np.where(n < max_exact, n, large).astype(np.int32)


def _swa_bias_table(rel_bias, delta, valid):
    lq, lk = delta.shape
    one_hot = (jnp.asarray(_t5_buckets(delta))[None] == jnp.arange(N_BUCKETS)[:, None, None]).astype(F32)
    bias = jnp.einsum("nh,nqk->hqk", rel_bias.astype(F32), one_hot, precision=lax.Precision.HIGHEST)
    bias = jnp.where(jnp.asarray(valid)[None], bias, NEG_INF)
    return bias.reshape(SWA_HEADS * lq, lk)


def _swa_sink_table(sinks, lq):
    return jnp.repeat(sinks.astype(F32), lq)[:, None]


def _swa_q_perm():
    perm = np.zeros(SWA_HEADS * SWA_HEAD_DIM, np.int32)
    for m in range(SWA_KV_HEADS // 2):
        for g in range(SWA_GROUP):
            for p in range(2):
                src = ((2 * m + p) * SWA_GROUP + g) * SWA_HEAD_DIM
                dst = (4 * m + g) * LANES + p * SWA_HEAD_DIM
                perm[dst:dst + SWA_HEAD_DIM] = np.arange(src, src + SWA_HEAD_DIM)
    return perm


def _rope_tables(pos):
    half = MLA_ROPE // 2
    inv = ROPE_THETA ** (-jnp.arange(half, dtype=F32) / half)
    ang = pos.astype(F32)[:, None] * inv[None, :]
    cos, sin = jnp.cos(ang), jnp.sin(ang)
    zeros = jnp.zeros((pos.shape[0], LANES - MLA_ROPE), F32)
    return (jnp.concatenate([cos, cos, zeros], axis=1), jnp.concatenate([-sin, sin, zeros], axis=1))


def _pad_rope_cols(w):
    half = MLA_ROPE // 2
    z = jnp.zeros((w.shape[0], LANES - MLA_ROPE), w.dtype)
    return (jnp.concatenate([w, z], axis=1),
            jnp.concatenate([w[:, half:], w[:, :half], z], axis=1))


def kernel(x_prompt, x_sample, state_swa_k, state_swa_v, state_conv, cache_mla_kv, cache_mla_kr, page_table,
           c_prompt, c_sample, ada_w, ada_b, norm_pre, norm_post, ffn_w_in, ffn_w_out,
           gm_w_in, gm_ln_g, gm_ln_b, gm_w_s, gm_b_s, gm_w_out,
           swa_w_qkv, swa_w_o, swa_sinks, rel_bias,
           sc_w_in, sc_conv, sc_w_out,
           mla_w_qa, mla_qa_norm, mla_w_qb, mla_w_kva, mla_kva_norm, mla_w_kvb, mla_w_o):
    n_seq, seq_len, d = x_prompt.shape
    n_dec, dec_len, _ = x_sample.shape
    depth = ada_w.shape[0]
    past_len = page_table.shape[1] * PAGE_SIZE
    t_p, t_s = n_seq * seq_len, n_dec * dec_len

    c_all = jnp.concatenate([jnp.repeat(c_sample, dec_len, axis=0), c_prompt,
                             jnp.zeros((8 - n_seq, d), F32)], axis=0)
    mod = _ada(c_all, ada_w, ada_b)
    tm_p, tm_s = 512, 256
    mod_p = mod[:, t_s:t_s + 8]
    grp_p = _Group(mod_p, tm_p, seq_len // tm_p)
    grp_s = _Group(mod, tm_s, None)
    tm_ffn = 512
    grp_p_ffn = _Group(mod_p, tm_ffn, seq_len // tm_ffn)

    xs = [x_prompt.reshape(t_p, d), x_sample.reshape(t_s, d)]
    outs = {}
    ffn_w = (ffn_w_in[0, 0].astype(BF16), ffn_w_out[0, 0].astype(BF16))

    def ffn_pair(xs, ffn_w, layer, which):
        nxt = (layer, 1) if which == 0 else (layer + 1, 0)
        next_w = (ffn_w_in, ffn_w_out) + nxt if nxt[0] < depth else None
        sub = 2 * which
        res = _half_ffn(xs[0], grp_p_ffn, layer, which, norm_pre[layer, sub], norm_post[layer, sub], *ffn_w,
                        next_w=next_w)
        x_s = _half_ffn(xs[1], grp_s, layer, which, norm_pre[layer, sub], norm_post[layer, sub], *ffn_w)
        if next_w is None:
            return [res, x_s], None
        return [res[0], x_s], (res[1], res[2])

    for i in range(depth):
        kind, j = i % 4, i // 4
        xs, ffn_w = ffn_pair(xs, ffn_w, i, 0)

        if kind == 0:
            w_in = gm_w_in[j].astype(BF16)
            w_out = gm_w_out[j].astype(BF16)
            new = []
            for x, g, chunk, sl, cps in ((xs[0], grp_p, GM_CHUNK, GM_CHUNK, seq_len // GM_CHUNK),
                                         (xs[1], grp_s, GM_CHUNK, dec_len, 1)):
                lc = min(sl, GM_CHUNK)
                m = jnp.tile(gm_w_s[j][:, :lc, :lc], (1, chunk // lc, chunk // lc))
                bias = jnp.broadcast_to(jnp.tile(gm_b_s[j][:, :lc], (1, chunk // lc))[:, :, None],
                                        (GM_GROUPS, chunk, LANES))
                uv = _norm_mod_matmul(x, g, i, norm_pre[i, 1], w_in, act="gelu", out_dtype=BF16)
                mixed, st = _gm_core(uv, gm_ln_g[j], gm_ln_b[j], m, bias, chunk, lc, cps)
                new.append(_matmul_residual(mixed, w_out, x, g, i, norm_post[i, 1]))
                outs.setdefault("gm", []).append(st)
            xs = new

        elif kind == 1:
            perm = _swa_q_perm()
            nq = SWA_HEADS * SWA_HEAD_DIM
            nkv = SWA_KV_HEADS * SWA_HEAD_DIM
            w_qkv = jnp.concatenate([swa_w_qkv[j][:, :nq][:, perm], swa_w_qkv[j][:, nq:]], axis=1).astype(BF16)
            w_o = swa_w_o[j][perm, :].astype(BF16)
            w = SWA_WINDOW
            i_q, i_k = np.arange(w), np.arange(2 * w)
            delta = w + i_q[:, None] - i_k[None, :]
            bias_p = _swa_bias_table(rel_bias, delta, (delta >= 0) & (delta < w))
            sink_p = _swa_sink_table(swa_sinks[j], w)
            qkv_p = _norm_mod_matmul(xs[0], grp_p, i, norm_pre[i, 1], w_qkv)
            o_p = _swa_prompt_core(qkv_p, bias_p, sink_p, seq_len)
            x_p = _matmul_residual(o_p, w_o, xs[0], grp_p, i, norm_post[i, 1])
            kv_p = qkv_p.reshape(n_seq, seq_len, -1)[:, seq_len - w:, nq:]
            outs.setdefault("swa_kp", []).append(kv_p[..., :nkv].reshape(n_seq, w, SWA_KV_HEADS, SWA_HEAD_DIM))
            outs.setdefault("swa_vp", []).append(kv_p[..., nkv:].reshape(n_seq, w, SWA_KV_HEADS, SWA_HEAD_DIM))
            lb = state_swa_k.shape[2]
            n_keys = lb + dec_len
            pad = (-n_keys) % 8
            i_q, i_k = np.arange(dec_len), np.arange(n_keys + pad)
            delta = lb + i_q[:, None] - i_k[None, :]
            valid = (delta >= 0) & (delta < w) & (i_k[None, :] < n_keys)
            bias_s = _swa_bias_table(rel_bias, delta, valid)
            sink_s = _swa_sink_table(swa_sinks[j], dec_len)
            qkv_s = _norm_mod_matmul(xs[1], grp_s, i, norm_pre[i, 1], w_qkv).reshape(n_dec, dec_len, -1)
            zpad = jnp.zeros((n_dec, pad, nkv), F32)
            kk = jnp.concatenate([state_swa_k[j].reshape(n_dec, lb, nkv), qkv_s[..., nq:nq + nkv], zpad], axis=1)
            vv = jnp.concatenate([state_swa_v[j].reshape(n_dec, lb, nkv), qkv_s[..., nq + nkv:], zpad], axis=1)
            q_s = qkv_s[..., :nq].reshape(n_dec, dec_len, SWA_KV_HEADS // 2, SWA_GROUP, LANES)
            q_s = q_s.transpose(0, 2, 3, 1, 4).reshape(n_dec, SWA_KV_HEADS // 2, SWA_GROUP * dec_len, LANES)
            o_s = _swa_sample_core(q_s, kk, vv, bias_s, sink_s)
            o_s = o_s.reshape(n_dec, SWA_KV_HEADS // 2, SWA_GROUP, dec_len, LANES).transpose(0, 3, 1, 2, 4)
            x_s = _matmul_residual(o_s.reshape(t_s, nq), w_o, xs[1], grp_s, i, norm_post[i, 1])
            outs.setdefault("swa_ks", []).append(kk[:, n_keys - lb:n_keys].reshape(n_dec, lb, SWA_KV_HEADS, SWA_HEAD_DIM))
            outs.setdefault("swa_vs", []).append(vv[:, n_keys - lb:n_keys].reshape(n_dec, lb, SWA_KV_HEADS, SWA_HEAD_DIM))
            xs = [x_p, x_s]

        elif kind == 2:
            w_in = sc_w_in[j].astype(BF16)
            w_out = sc_w_out[j].astype(BF16)
            c = sc_w_out.shape[1]
            g3_p = _norm_mod_matmul(xs[0], grp_p, i, norm_pre[i, 1], w_in, out_dtype=BF16)
            y_p, st_p = _conv_prompt_core(g3_p, sc_conv[j], seq_len)
            x_p = _matmul_residual(y_p, w_out, xs[0], grp_p, i, norm_post[i, 1])
            outs.setdefault("conv_p", []).append(st_p[:, 8 - (CONV_WIDTH - 1):])
            g3_s = _norm_mod_matmul(xs[1], grp_s, i, norm_pre[i, 1], w_in, out_dtype=BF16)
            y_s, st_s = _conv_sample_core(g3_s.reshape(n_dec, dec_len * 3 * c),
                                          state_conv[j].reshape(n_dec, (CONV_WIDTH - 1) * c), sc_conv[j], dec_len)
            x_s = _matmul_residual(y_s.reshape(t_s, c), w_out, xs[1], grp_s, i, norm_post[i, 1])
            outs.setdefault("conv_s", []).append(st_s.reshape(n_dec, CONV_WIDTH - 1, c))
            xs = [x_p, x_s]

        else:
            r = MLA_KV_LORA
            kr_pad, kr_sw = _pad_rope_cols(mla_w_kva[j][:, r:])
            w_p = jnp.concatenate([mla_w_qa[j], mla_w_kva[j][:, :r], kr_pad, kr_sw], axis=1).astype(BF16)
            w_qb = mla_w_qb[j].reshape(MLA_Q_LORA, MLA_HEADS, MLA_NOPE + MLA_ROPE)
            w_nope = w_qb[:, :, :MLA_NOPE].reshape(MLA_Q_LORA, -1).astype(BF16)
            rope_pairs = [_pad_rope_cols(w_qb[:, h, MLA_NOPE:]) for h in range(MLA_HEADS)]
            w_rope = jnp.concatenate([p[0] for p in rope_pairs], axis=1).astype(BF16)
            w_rope_sw = jnp.concatenate([p[1] for p in rope_pairs], axis=1).astype(BF16)
            w_kvb = mla_w_kvb[j].reshape(r, MLA_HEADS, MLA_NOPE + MLA_V)
            w_uk_t = w_kvb[:, :, :MLA_NOPE].transpose(1, 2, 0).astype(BF16)
            w_uv = w_kvb[:, :, MLA_NOPE:].transpose(1, 0, 2).astype(BF16)
            w_o = mla_w_o[j].astype(BF16)
            tm = 256
            cos_p, sin_p = _rope_tables(jnp.arange(seq_len))
            proj_p = _norm_mod_matmul(xs[0], grp_p, i, norm_pre[i, 1], w_p, tn=w_p.shape[1])
            q_p, ckv_p, kr_p, k_p, v_p = _mla_proj(proj_p, cos_p, sin_p, seq_len // tm, mla_qa_norm[j],
                                                   mla_kva_norm[j], w_nope, w_rope, w_rope_sw,
                                                   mla_w_kvb[j].astype(BF16), True, tm=tm)
            o_p = _mla_prompt_attn(q_p, k_p, v_p, n_seq, seq_len)
            x_p = _matmul_residual(o_p, w_o, xs[0], grp_p, i, norm_post[i, 1])
            outs.setdefault("mla_kvp", []).append(ckv_p.reshape(n_seq, seq_len, r))
            outs.setdefault("mla_krp", []).append(kr_p[:, :MLA_ROPE].reshape(n_seq, seq_len, MLA_ROPE))
            cos_s, sin_s = _rope_tables(past_len + jnp.arange(dec_len))
            cos_s, sin_s = jnp.tile(cos_s, (tm // dec_len, 1)), jnp.tile(sin_s, (tm // dec_len, 1))
            proj_s = _norm_mod_matmul(xs[1], grp_s, i, norm_pre[i, 1], w_p, tn=w_p.shape[1])
            q_s, ckv_s, kr_s, kcat_s = _mla_proj(proj_s, cos_s, sin_s, 1, mla_qa_norm[j],
                                                 mla_kva_norm[j], w_nope, w_rope, w_rope_sw, w_uk_t, False, tm=tm)
            knew = jnp.concatenate([kcat_s.reshape(n_dec, dec_len, MLA_QK),
                                    jnp.zeros((n_dec, 16 - dec_len, MLA_QK), BF16)], axis=1)
            ol_s = _mla_sample_attn(q_s.reshape(t_s * MLA_HEADS, MLA_QK), knew, cache_mla_kv,
                                    jnp.swapaxes(cache_mla_kr, 2, 3), page_table, j, dec_len)
            x_s = _mla_out(ol_s.reshape(t_s, MLA_HEADS * r), w_uv, w_o, xs[1], grp_s, i, norm_post[i, 1])
            outs.setdefault("mla_kvs", []).append(ckv_s.reshape(n_dec, dec_len, r))
            outs.setdefault("mla_krs", []).append(kr_s[:, :MLA_ROPE].reshape(n_dec, dec_len, MLA_ROPE))
            xs = [x_p, x_s]

        xs, ffn_w = ffn_pair(xs, ffn_w, i, 1)

    gm_p, gm_s = outs["gm"][0::2], outs["gm"][1::2]
    return (xs[0].reshape(n_seq, seq_len, d), xs[1].reshape(n_dec, dec_len, d),
            jnp.stack(gm_p),
            jnp.stack([s.reshape(n_dec, dec_len, -1) for s in gm_s]),
            jnp.stack(outs["swa_kp"]), jnp.stack(outs["swa_vp"]),
            jnp.stack(outs["swa_ks"]), jnp.stack(outs["swa_vs"]),
            jnp.stack(outs["conv_p"]), jnp.stack(outs["conv_s"]),
            jnp.stack(outs["mla_kvp"]), jnp.stack(outs["mla_krp"]),
            jnp.stack(outs["mla_kvs"]), jnp.stack(outs["mla_krs"]))
```

```python
import functools
import math

import numpy as np
import jax
import jax.numpy as jnp
from jax import lax
from jax.experimental import pallas as pl
from jax.experimental.pallas import tpu as pltpu

F32 = jnp.float32
BF16 = jnp.bfloat16

VMEM_LIMIT_BYTES = 56 * 1024 * 1024
LANES = 128

RMS_EPS = 1e-6
NEG_INF = -1e30
FFN_RES_W = 0.5

D_MODEL = 2048
GM_GROUPS = 8
GM_CHUNK = 128
SWA_WINDOW = 128
SWA_HEAD_DIM = 64
SWA_HEADS = 32
SWA_KV_HEADS = 8
SWA_GROUP = 4
SWA_SCALE = SWA_HEAD_DIM ** -0.5
N_BUCKETS = 32
BUCKET_MAX_DIST = 128
CONV_WIDTH = 3
MLA_HEADS = 16
MLA_Q_LORA = 512
MLA_KV_LORA = 512
MLA_NOPE = 128
MLA_ROPE = 64
MLA_V = 128
MLA_SCALE = (MLA_NOPE + MLA_ROPE) ** -0.5
MLA_QK = MLA_KV_LORA + LANES
ROPE_THETA = 10000.0
PAGE_SIZE = 128
PAGES_PER_STEP = 64


def _params(n_axes):
    return pltpu.CompilerParams(dimension_semantics=("arbitrary",) * n_axes,
                                vmem_limit_bytes=VMEM_LIMIT_BYTES)


def _rms(x, g):
    return x * lax.rsqrt(jnp.mean(x * x, axis=-1, keepdims=True) + RMS_EPS) * g


def _lane_tile(x, n):
    return x if n == 1 else jnp.concatenate([x] * n, axis=1)


class _Group:
    def __init__(self, mod, tm, tiles_per_mod_row):
        self.mod = mod
        self.tm = tm
        self.tiles_per_mod_row = tiles_per_mod_row

    def mod_spec(self, layer, col):
        if self.tiles_per_mod_row is None:
            return pl.BlockSpec((None, self.tm, D_MODEL), lambda i, *_: (layer, i, col))
        return pl.BlockSpec((None, 8, D_MODEL), lambda i, *_: (layer, 0, col))

    def row(self):
        if self.tiles_per_mod_row is None:
            return None
        return pl.program_id(0) // self.tiles_per_mod_row

    @staticmethod
    def read(ref, row):
        return ref[...] if row is None else ref[pl.ds(row, 1), :]


SUBLANES = 8
ROW_CHUNK = 16


def _sublane_tile(x):
    return jnp.concatenate([x] * (ROW_CHUNK // SUBLANES), axis=0)


def _for_row_chunks(n_rows, body, unroll):
    def step(c, carry):
        body(pl.ds(pl.multiple_of(c * ROW_CHUNK, ROW_CHUNK), ROW_CHUNK))
        return carry
    lax.fori_loop(0, n_rows // ROW_CHUNK, step, 0, unroll=unroll)


def _row_rms_scale(val_ref, stat_ref):
    width = val_ref.shape[1]

    def body(rs):
        parts = [val_ref[rs, i * LANES:(i + 1) * LANES] for i in range(width // LANES)]
        parts = [p * p for p in parts]
        while len(parts) > 1:
            parts = [a + b for a, b in zip(parts[0::2], parts[1::2])] + ([parts[-1]] if len(parts) % 2 else [])
        scale = lax.rsqrt(jnp.sum(parts[0], axis=-1, keepdims=True) * (1.0 / width) + RMS_EPS)
        stat_ref[rs, :] = jnp.broadcast_to(scale, (ROW_CHUNK, LANES))

    _for_row_chunks(val_ref.shape[0], body, unroll=True)


def _ada_in_rows(x_ref, gpre_ref, sc_ref, sh_ref, row, stat_ref, h_ref):
    n_rows, width = x_ref.shape
    n_tile = width // LANES
    _row_rms_scale(x_ref, stat_ref)

    def per_row(rs, bc_ref):
        xn = x_ref[rs, :] * _lane_tile(stat_ref[rs, :], n_tile)
        h_ref[rs, :] = (xn * (gpre_ref[...] * (1.0 + sc_ref[rs, :])) + sh_ref[rs, :]).astype(h_ref.dtype)

    def per_tile(rs, bc_ref):
        xn = x_ref[rs, :] * _lane_tile(stat_ref[rs, :], n_tile)
        h_ref[rs, :] = (xn * _sublane_tile(bc_ref[0]) + _sublane_tile(bc_ref[1])).astype(h_ref.dtype)

    def run(bc_ref):
        if row is not None:
            bc_ref[0] = jnp.broadcast_to(gpre_ref[...] * (1.0 + sc_ref[pl.ds(row, 1), :]), (SUBLANES, width))
            bc_ref[1] = jnp.broadcast_to(sh_ref[pl.ds(row, 1), :], (SUBLANES, width))
        body = per_row if row is None else per_tile
        _for_row_chunks(n_rows, lambda rs: body(rs, bc_ref), unroll=2)

    pl.run_scoped(run, pltpu.VMEM((2, SUBLANES, width), F32))


def _ada_out_rows(x_ref, val_ref, gpost_ref, gt_ref, row, res_w, stat_ref, o_ref):
    n_rows, width = x_ref.shape
    n_tile = width // LANES
    _row_rms_scale(val_ref, stat_ref)

    def per_row(rs, bc_ref):
        vn = val_ref[rs, :] * _lane_tile(stat_ref[rs, :], n_tile)
        o_ref[rs, :] = x_ref[rs, :] + vn * (gpost_ref[...] * (res_w * gt_ref[rs, :]))

    def per_tile(rs, bc_ref):
        vn = val_ref[rs, :] * _lane_tile(stat_ref[rs, :], n_tile)
        o_ref[rs, :] = x_ref[rs, :] + vn * _sublane_tile(bc_ref[0])

    def run(bc_ref):
        if row is not None:
            bc_ref[0] = jnp.broadcast_to(gpost_ref[...] * (res_w * gt_ref[pl.ds(row, 1), :]), (SUBLANES, width))
        body = per_row if row is None else per_tile
        _for_row_chunks(n_rows, lambda rs: body(rs, bc_ref), unroll=2)

    pl.run_scoped(run, pltpu.VMEM((1, SUBLANES, width), F32))


def _shr(x, divisor):
    shift = divisor.bit_length() - 1
    assert 1 << shift == divisor
    return lax.shift_right_logical(x, shift)


def _ada_kernel(c_ref, w_ref, b_ref, o_ref, cs_sc):
    @pl.when((pl.program_id(0) == 0) & (pl.program_id(1) == 0))
    def _():
        c = c_ref[...]
        cs_sc[...] = (c * jax.nn.sigmoid(c)).astype(BF16)

    o_ref[...] = jnp.dot(cs_sc[...], w_ref[...].astype(BF16),
                         preferred_element_type=F32) + b_ref[...]


def _ada(c_all, ada_w, ada_b, tn=1024):
    n_layers, d, n = ada_w.shape
    rows = c_all.shape[0]
    return pl.pallas_call(
        _ada_kernel,
        grid=(n_layers, n // tn),
        in_specs=[pl.BlockSpec((rows, d), lambda l, j: (0, 0)),
                  pl.BlockSpec((None, d, tn), lambda l, j: (l, 0, j)),
                  pl.BlockSpec((None, 1, tn), lambda l, j: (l, 0, j))],
        out_specs=pl.BlockSpec((None, rows, tn), lambda l, j: (l, 0, j)),
        out_shape=jax.ShapeDtypeStruct((n_layers, rows, n), F32),
        scratch_shapes=[pltpu.VMEM((rows, d), BF16)],
        compiler_params=_params(2),
        name="ada_modulation",
    )(c_all, ada_w, ada_b.reshape(n_layers, 1, n))


def _ffn_kernel(x_ref, sh_ref, sc_ref, gt_ref, gpre_ref, gpost_ref, wg_ref, wu_ref, wo_ref, *rest, grp, n_f):
    if len(rest) == 7:
        nwi_ref, nwo_ref, o_ref, nwi_b_ref, nwo_b_ref, h_sc, stat_sc = rest
        nwi_b_ref[...] = nwi_ref[...].astype(BF16)
        nwo_b_ref[...] = nwo_ref[...].astype(BF16)
    else:
        o_ref, h_sc, stat_sc = rest
    j = pl.program_id(1)
    row = grp.row()
    acc_sc = o_ref

    @pl.when(j == 0)
    def _():
        _ada_in_rows(x_ref, gpre_ref, sc_ref, sh_ref, row, stat_sc, h_sc)
        acc_sc[...] = jnp.zeros_like(acc_sc)

    h = h_sc[...]
    tf = wg_ref.shape[1]
    acts = []
    for c in range(2):
        cols = slice(c * tf // 2, (c + 1) * tf // 2)
        g = jnp.dot(h, wg_ref[:, cols], preferred_element_type=F32)
        u = jnp.dot(h, wu_ref[:, cols], preferred_element_type=F32)
        acts.append((g * jax.nn.sigmoid(g) * u).astype(BF16))
    acc_sc[...] += jnp.dot(jnp.concatenate(acts, axis=1), wo_ref[...], preferred_element_type=F32)

    @pl.when(j == n_f - 1)
    def _():
        _ada_out_rows(x_ref, acc_sc, gpost_ref, gt_ref, row, FFN_RES_W, stat_sc, o_ref)


def _half_ffn(x, grp, layer, which, g_pre, g_post, w_in, w_out, next_w=None, tf=512):
    t, d = x.shape
    f = w_out.shape[0]
    tm, n_f = grp.tm, f // tf
    n_i = t // tm
    sub = 2 * which
    kern = functools.partial(_ffn_kernel, grp=grp, n_f=n_f)
    in_specs = [pl.BlockSpec((tm, d), lambda i, j: (i, 0)),
                grp.mod_spec(layer, sub * 3 + 0),
                grp.mod_spec(layer, sub * 3 + 1),
                grp.mod_spec(layer, sub * 3 + 2),
                pl.BlockSpec((1, d), lambda i, j: (0, 0)),
                pl.BlockSpec((1, d), lambda i, j: (0, 0)),
                pl.BlockSpec((d, tf), lambda i, j: (0, j)),
                pl.BlockSpec((d, tf), lambda i, j: (0, n_f + j)),
                pl.BlockSpec((tf, d), lambda i, j: (j, 0))]
    args = [x, grp.mod, grp.mod, grp.mod, g_pre.reshape(1, d), g_post.reshape(1, d), w_in, w_in, w_out]
    out_specs = [pl.BlockSpec((tm, d), lambda i, j: (i, 0))]
    out_shape = [jax.ShapeDtypeStruct((t, d), F32)]
    if next_w is not None:
        nw_in, nw_out, nl, nwh = next_w
        share = 4 if n_i * n_f > 100 else 1
        assert n_i % share == 0
        bi = (share * d // n_i, 2 * f // n_f)
        bo = (f // n_f, share * d // n_i)

        def blk(i, j):
            s = (i * n_f + j) // share
            return s // n_f, s % n_f

        in_specs += [pl.BlockSpec((None, None) + bi, lambda i, j: (nl, nwh) + blk(i, j)),
                     pl.BlockSpec((None, None) + bo, lambda i, j: (nl, nwh) + blk(i, j)[::-1])]
        args += [nw_in, nw_out]
        out_specs += [pl.BlockSpec(bi, blk), pl.BlockSpec(bo, lambda i, j: blk(i, j)[::-1])]
        out_shape += [jax.ShapeDtypeStruct((d, 2 * f), BF16), jax.ShapeDtypeStruct((f, d), BF16)]
    res = pl.pallas_call(
        kern,
        grid=(n_i, n_f),
        in_specs=in_specs,
        out_specs=out_specs,
        out_shape=out_shape,
        scratch_shapes=[pltpu.VMEM((tm, d), BF16), pltpu.VMEM((tm, LANES), F32)],
        compiler_params=_params(2),
        name="half_ffn",
    )(*args)
    return res if next_w is not None else res[0]


def _nmm_kernel(x_ref, sh_ref, sc_ref, gpre_ref, w_ref, o_ref, h_sc, stat_sc, *, grp, act):
    row = grp.row()

    @pl.when(pl.program_id(1) == 0)
    def _():
        _ada_in_rows(x_ref, gpre_ref, sc_ref, sh_ref, row, stat_sc, h_sc)

    y = jnp.dot(h_sc[...], w_ref[...], preferred_element_type=F32)
    if act == "gelu":
        y = jax.nn.gelu(y, approximate=True)
    o_ref[...] = y.astype(o_ref.dtype)


def _norm_mod_matmul(x, grp, layer, g_pre, w, act=None, tn=None, out_dtype=F32):
    t, d = x.shape
    n = w.shape[1]
    if tn is None:
        tn = max(c for c in range(LANES, min(n, 2048) + 1, LANES) if n % c == 0)
    tm = grp.tm
    kern = functools.partial(_nmm_kernel, grp=grp, act=act)
    return pl.pallas_call(
        kern,
        grid=(t // tm, n // tn),
        in_specs=[pl.BlockSpec((tm, d), lambda i, j: (i, 0)),
                  grp.mod_spec(layer, 3 + 0),
                  grp.mod_spec(layer, 3 + 1),
                  pl.BlockSpec((1, d), lambda i, j: (0, 0)),
                  pl.BlockSpec((d, tn), lambda i, j: (0, j))],
        out_specs=pl.BlockSpec((tm, tn), lambda i, j: (i, j)),
        out_shape=jax.ShapeDtypeStruct((t, n), out_dtype),
        scratch_shapes=[pltpu.VMEM((tm, d), BF16), pltpu.VMEM((tm, LANES), F32)],
        compiler_params=_params(2),
        name="norm_mod_matmul",
    )(x, grp.mod, grp.mod, g_pre.reshape(1, d), w)


def _mres_kernel(a_ref, w_ref, x_ref, gt_ref, gpost_ref, o_ref, acc_sc, stat_sc, *, grp, n_k):
    k = pl.program_id(1)
    row = grp.row()

    @pl.when(k == 0)
    def _():
        acc_sc[...] = jnp.zeros_like(acc_sc)

    acc_sc[...] += jnp.dot(a_ref[...].astype(BF16), w_ref[...], preferred_element_type=F32)

    @pl.when(k == n_k - 1)
    def _():
        _ada_out_rows(x_ref, acc_sc, gpost_ref, gt_ref, row, 1.0, stat_sc, o_ref)


def _matmul_residual(a, w, x, grp, layer, g_post, tk=2048):
    t, kdim = a.shape
    d = w.shape[1]
    tm, n_k = grp.tm, kdim // tk
    kern = functools.partial(_mres_kernel, grp=grp, n_k=n_k)
    return pl.pallas_call(
        kern,
        grid=(t // tm, n_k),
        in_specs=[pl.BlockSpec((tm, tk), lambda i, k: (i, k)),
                  pl.BlockSpec((tk, d), lambda i, k: (k, 0)),
                  pl.BlockSpec((tm, d), lambda i, k: (i, 0)),
                  grp.mod_spec(layer, 3 + 2),
                  pl.BlockSpec((1, d), lambda i, k: (0, 0))],
        out_specs=pl.BlockSpec((tm, d), lambda i, k: (i, 0)),
        out_shape=jax.ShapeDtypeStruct((t, d), F32),
        scratch_shapes=[pltpu.VMEM((tm, d), F32), pltpu.VMEM((tm, LANES), F32)],
        compiler_params=_params(2),
        name="matmul_residual",
    )(a, w, x, grp.mod, g_post.reshape(1, d))


def _gm_kernel(u_ref, v_ref, lng_ref, lnb_ref, m_ref, b_ref, o_ref, st_ref, *, seq_len):
    v = v_ref[...].astype(F32)
    vc = v - jnp.mean(v, axis=-1, keepdims=True)
    vn = vc * lax.rsqrt(jnp.mean(vc * vc, axis=-1, keepdims=True) + RMS_EPS) * lng_ref[...] + lnb_ref[...]
    st_ref[...] = vn
    vnb = vn.astype(BF16)
    c = v.shape[0]
    gw = v.shape[1] // GM_GROUPS
    row = lax.broadcasted_iota(jnp.int32, (c, c), 0)
    col = lax.broadcasted_iota(jnp.int32, (c, c), 1)
    keep = (col <= row) & (_shr(row, seq_len) == _shr(col, seq_len))
    for g in range(GM_GROUPS):
        mg = jnp.where(keep, m_ref[g], 0.0).astype(BF16)
        mixed = jnp.dot(mg, vnb[:, g * gw:(g + 1) * gw], preferred_element_type=F32)
        mixed = mixed + _lane_tile(b_ref[g], gw // LANES)
        o_ref[:, g * gw:(g + 1) * gw] = (u_ref[:, g * gw:(g + 1) * gw].astype(F32) * mixed).astype(o_ref.dtype)


def _gm_core(uv, ln_g, ln_b, m, bias, chunk, seq_len, chunks_per_state):
    t = uv.shape[0]
    w = uv.shape[1] // 2
    n_chunks = t // chunk
    n_states = n_chunks // chunks_per_state
    kern = functools.partial(_gm_kernel, seq_len=seq_len)
    return pl.pallas_call(
        kern,
        grid=(n_chunks,),
        in_specs=[pl.BlockSpec((chunk, w), lambda c: (c, 0)),
                  pl.BlockSpec((chunk, w), lambda c: (c, 1)),
                  pl.BlockSpec((1, w), lambda c: (0, 0)),
                  pl.BlockSpec((1, w), lambda c: (0, 0)),
                  pl.BlockSpec((GM_GROUPS, chunk, chunk), lambda c: (0, 0, 0)),
                  pl.BlockSpec((GM_GROUPS, chunk, LANES), lambda c: (0, 0, 0))],
        out_specs=[pl.BlockSpec((chunk, w), lambda c: (c, 0)),
                   pl.BlockSpec((None, chunk, w), lambda c: (c // chunks_per_state, 0, 0))],
        out_shape=[jax.ShapeDtypeStruct((t, w), BF16),
                   jax.ShapeDtypeStruct((n_states, chunk, w), F32)],
        compiler_params=_params(1),
        name="gm_core",
    )(uv, uv, ln_g.reshape(1, w), ln_b.reshape(1, w), m, bias)


def _softmax_sink_parts(s, sink_b):
    mx = jnp.maximum(jnp.broadcast_to(jnp.max(s, axis=-1, keepdims=True), sink_b.shape), sink_b)
    n_keys = s.shape[1]
    mx_keys = _lane_tile(mx, n_keys // LANES) if n_keys % LANES == 0 else mx[:, 0:1]
    return jnp.exp(s - mx_keys).astype(BF16), jnp.exp(sink_b - mx)


def _half_lane_mask(rows, parity):
    lane = lax.broadcasted_iota(jnp.int32, (rows, LANES), 1)
    return (lane >= SWA_HEAD_DIM) if parity else (lane < SWA_HEAD_DIM)


def _swa_attend(q_blocks, k, v, bias_ref, sink_ref, masked_keys=None):
    rows = q_blocks[0].shape[0]
    n_keys = k.shape[0]
    logits, values = [], []
    for m in range(SWA_KV_HEADS // 2):
        k2 = k[:, m * LANES:(m + 1) * LANES]
        v2 = v[:, m * LANES:(m + 1) * LANES]
        for parity in range(2):
            keep = _half_lane_mask(n_keys, parity)
            kx = jnp.where(keep, k2, 0.0).astype(BF16)
            values.append(jnp.where(keep, v2, 0.0).astype(BF16))
            logits.append(lax.dot_general(q_blocks[m], kx, (((1,), (1,)), ((), ())), preferred_element_type=F32))
    s = jnp.concatenate(logits, axis=0) + bias_ref[...]
    if masked_keys is not None:
        key_col = lax.broadcasted_iota(jnp.int32, s.shape, 1)
        s = jnp.where(key_col < masked_keys, NEG_INF, s)
    p, p_sink = _softmax_sink_parts(s, sink_ref[...])
    denom = jnp.dot(p, jnp.ones((n_keys, LANES), BF16), preferred_element_type=F32) + p_sink
    outs = []
    for m in range(SWA_KV_HEADS // 2):
        o_m = None
        for parity in range(2):
            idx = 2 * m + parity
            sl = slice(idx * rows, (idx + 1) * rows)
            o_p = jnp.dot(p[sl], values[idx], preferred_element_type=F32) / denom[sl]
            o_m = o_p if o_m is None else o_m + o_p
        outs.append(o_m)
    return outs


def _swa_prompt_kernel(q_ref, kp_ref, ko_ref, vp_ref, vo_ref, bias_ref, sink_ref, o_ref, *, blocks_per_seq):
    w = SWA_WINDOW
    first = (pl.program_id(0) % blocks_per_seq) == 0
    q = (q_ref[...] * SWA_SCALE).astype(BF16)
    k = jnp.concatenate([kp_ref[...], ko_ref[...]], axis=0)
    v = jnp.concatenate([vp_ref[...], vo_ref[...]], axis=0)
    q_blocks = [jnp.concatenate([q[:, (4 * m + g) * LANES:(4 * m + g + 1) * LANES] for g in range(SWA_GROUP)],
                                axis=0) for m in range(SWA_KV_HEADS // 2)]
    outs = _swa_attend(q_blocks, k, v, bias_ref, sink_ref, masked_keys=jnp.where(first, w, 0))
    for m, o_m in enumerate(outs):
        for g in range(SWA_GROUP):
            o_ref[:, (4 * m + g) * LANES:(4 * m + g + 1) * LANES] = o_m[g * w:(g + 1) * w].astype(o_ref.dtype)


def _swa_prompt_core(qkv, bias, sink, seq_len):
    t = qkv.shape[0]
    w = SWA_WINDOW
    nq = SWA_HEADS * SWA_HEAD_DIM
    nkv = SWA_KV_HEADS * SWA_HEAD_DIM
    bps = seq_len // w
    kcol, vcol = nq // nkv, nq // nkv + 1

    def prev(i):
        return jnp.maximum(i - 1, 0)

    kern = functools.partial(_swa_prompt_kernel, blocks_per_seq=bps)
    return pl.pallas_call(
        kern,
        grid=(t // w,),
        in_specs=[pl.BlockSpec((w, nq), lambda i: (i, 0)),
                  pl.BlockSpec((w, nkv), lambda i: (prev(i), kcol)),
                  pl.BlockSpec((w, nkv), lambda i: (i, kcol)),
                  pl.BlockSpec((w, nkv), lambda i: (prev(i), vcol)),
                  pl.BlockSpec((w, nkv), lambda i: (i, vcol)),
                  pl.BlockSpec(bias.shape, lambda i: (0, 0)),
                  pl.BlockSpec(sink.shape, lambda i: (0, 0))],
        out_specs=pl.BlockSpec((w, nq), lambda i: (i, 0)),
        out_shape=jax.ShapeDtypeStruct((t, nq), BF16),
        compiler_params=_params(1),
        name="swa_prompt_core",
    )(qkv, qkv, qkv, qkv, qkv, bias, sink)


def _swa_sample_kernel(q_ref, k_ref, v_ref, bias_ref, sink_ref, o_ref, *, bb):
    def body(b, carry):
        q_blocks = [(q_ref[b, m] * SWA_SCALE).astype(BF16) for m in range(SWA_KV_HEADS // 2)]
        outs = _swa_attend(q_blocks, k_ref[b], v_ref[b], bias_ref, sink_ref)
        for m, o_m in enumerate(outs):
            o_ref[b, m] = o_m.astype(o_ref.dtype)
        return carry

    lax.fori_loop(0, bb, body, 0, unroll=2)


def _swa_sample_core(q, kk, vv, bias, sink, bb=8):
    b, n_pair, rows, _ = q.shape
    n_keys = kk.shape[1]
    nkv = kk.shape[2]
    kern = functools.partial(_swa_sample_kernel, bb=bb)
    return pl.pallas_call(
        kern,
        grid=(b // bb,),
        in_specs=[pl.BlockSpec((bb, n_pair, rows, LANES), lambda i: (i, 0, 0, 0)),
                  pl.BlockSpec((bb, n_keys, nkv), lambda i: (i, 0, 0)),
                  pl.BlockSpec((bb, n_keys, nkv), lambda i: (i, 0, 0)),
                  pl.BlockSpec(bias.shape, lambda i: (0, 0)),
                  pl.BlockSpec(sink.shape, lambda i: (0, 0))],
        out_specs=pl.BlockSpec((bb, n_pair, rows, LANES), lambda i: (i, 0, 0, 0)),
        out_shape=jax.ShapeDtypeStruct(q.shape, BF16),
        compiler_params=_params(1),
        name="swa_sample_core",
    )(q, kk, vv, bias, sink)


def _conv_prompt_kernel(gb_ref, gc_ref, z_ref, hc_ref, hz_ref, cw_ref, o_ref, st_ref, zp_sc, *, tiles_per_seq):
    tm = gb_ref.shape[0]
    hr = hc_ref.shape[0]
    first = (pl.program_id(0) % tiles_per_seq) == 0
    zz = gc_ref[...].astype(F32) * z_ref[...].astype(F32)
    halo = jnp.where(first, 0.0, hc_ref[...].astype(F32) * hz_ref[...].astype(F32))
    zp_sc[0:8, :] = halo[hr - 8:hr]
    zp_sc[8:8 + tm, :] = zz
    y = cw_ref[2:3, :] * zz + cw_ref[1:2, :] * zp_sc[7:7 + tm, :] + cw_ref[0:1, :] * zp_sc[6:6 + tm, :]
    o_ref[...] = (gb_ref[...].astype(F32) * y).astype(o_ref.dtype)
    st_ref[...] = zz[tm - 8:tm]


def _conv_prompt_core(g3, conv_w, seq_len, tm=256):
    t = g3.shape[0]
    c = g3.shape[1] // 3
    tps = seq_len // tm
    n_seq = t // seq_len

    hr = 16

    def halo(col):
        return pl.BlockSpec((hr, c), lambda i: (jnp.maximum(i * (tm // hr) - 1, 0), col))

    kern = functools.partial(_conv_prompt_kernel, tiles_per_seq=tps)
    return pl.pallas_call(
        kern,
        grid=(t // tm,),
        in_specs=[pl.BlockSpec((tm, c), lambda i: (i, 0)),
                  pl.BlockSpec((tm, c), lambda i: (i, 1)),
                  pl.BlockSpec((tm, c), lambda i: (i, 2)),
                  halo(1), halo(2),
                  pl.BlockSpec((CONV_WIDTH, c), lambda i: (0, 0))],
        out_specs=[pl.BlockSpec((tm, c), lambda i: (i, 0)),
                   pl.BlockSpec((None, 8, c), lambda i: (i // tps, 0, 0))],
        out_shape=[jax.ShapeDtypeStruct((t, c), BF16),
                   jax.ShapeDtypeStruct((n_seq, 8, c), F32)],
        scratch_shapes=[pltpu.VMEM((tm + 8, c), F32)],
        compiler_params=_params(1),
        name="conv_prompt_core",
    )(g3, g3, g3, g3, g3, conv_w)


def _conv_sample_kernel(g_ref, prev_ref, cw_ref, o_ref, st_ref, *, seq_len, c):
    zz = [prev_ref[:, 0:c], prev_ref[:, c:2 * c]]
    for t in range(seq_len):
        base = t * 3 * c
        zz.append(g_ref[:, base + c:base + 2 * c].astype(F32) * g_ref[:, base + 2 * c:base + 3 * c].astype(F32))
    for t in range(seq_len):
        y = cw_ref[2:3, :] * zz[t + 2] + cw_ref[1:2, :] * zz[t + 1] + cw_ref[0:1, :] * zz[t]
        o_ref[:, t * c:(t + 1) * c] = (g_ref[:, t * 3 * c:t * 3 * c + c].astype(F32) * y).astype(o_ref.dtype)
    st_ref[:, 0:c] = zz[seq_len]
    st_ref[:, c:2 * c] = zz[seq_len + 1]


def _conv_sample_core(g3, prev, conv_w, seq_len):
    b = g3.shape[0]
    c = g3.shape[1] // (3 * seq_len)
    kern = functools.partial(_conv_sample_kernel, seq_len=seq_len, c=c)
    return pl.pallas_call(
        kern,
        grid=(1,),
        in_specs=[pl.BlockSpec(g3.shape, lambda i: (0, 0)),
                  pl.BlockSpec(prev.shape, lambda i: (0, 0)),
                  pl.BlockSpec((CONV_WIDTH, c), lambda i: (0, 0))],
        out_specs=[pl.BlockSpec((b, seq_len * c), lambda i: (0, 0)),
                   pl.BlockSpec((b, 2 * c), lambda i: (0, 0))],
        out_shape=[jax.ShapeDtypeStruct((b, seq_len * c), BF16),
                   jax.ShapeDtypeStruct((b, 2 * c), F32)],
        compiler_params=_params(1),
        name="conv_sample_core",
    )(g3, prev, conv_w)


def _mla_proj_kernel(p_ref, cos_ref, sin_ref, qan_ref, kvn_ref, wn_ref, wr_ref, wrs_ref, wkv_ref, *out_refs, per_head):
    r = MLA_KV_LORA
    cos = cos_ref[...]
    sin = sin_ref[...]
    qa = _rms(p_ref[:, 0:MLA_Q_LORA], qan_ref[...]).astype(BF16)
    qn = jnp.dot(qa, wn_ref[...], preferred_element_type=F32)
    qr = jnp.dot(qa, wr_ref[...], preferred_element_type=F32)
    qrs = jnp.dot(qa, wrs_ref[...], preferred_element_type=F32)
    off = MLA_Q_LORA
    ckv = _rms(p_ref[:, off:off + r], kvn_ref[...])
    kr = p_ref[:, off + r:off + r + LANES] * cos + p_ref[:, off + r + LANES:off + r + 2 * LANES] * sin
    if per_head:
        q_ref, ckv_ref, kr_ref, k_ref, v_ref = out_refs
        kv_up = jnp.dot(ckv.astype(BF16), wkv_ref[...], preferred_element_type=F32)
        kr_b = kr.astype(k_ref.dtype)
    else:
        q_ref, ckv_ref, kr_ref, kcat_ref = out_refs
    for h in range(MLA_HEADS):
        sl = slice(h * LANES, (h + 1) * LANES)
        q_rope = ((qr[:, sl] * cos + qrs[:, sl] * sin) * MLA_SCALE).astype(q_ref.dtype)
        if per_head:
            q_ref[h, :, 0:MLA_NOPE] = (qn[:, sl] * MLA_SCALE).astype(q_ref.dtype)
            q_ref[h, :, MLA_NOPE:MLA_NOPE + LANES] = q_rope
            base = h * (MLA_NOPE + MLA_V)
            k_ref[h, :, 0:MLA_NOPE] = kv_up[:, base:base + MLA_NOPE].astype(k_ref.dtype)
            k_ref[h, :, MLA_NOPE:MLA_NOPE + LANES] = kr_b
            v_ref[h] = kv_up[:, base + MLA_NOPE:base + MLA_NOPE + MLA_V].astype(v_ref.dtype)
        else:
            q_lat = jnp.dot(qn[:, sl].astype(BF16), wkv_ref[h], preferred_element_type=F32)
            q_ref[:, h * MLA_QK:h * MLA_QK + r] = (q_lat * MLA_SCALE).astype(q_ref.dtype)
            q_ref[:, h * MLA_QK + r:(h + 1) * MLA_QK] = q_rope
    ckv_ref[...] = ckv
    kr_ref[...] = kr
    if not per_head:
        kcat_ref[:, 0:r] = ckv.astype(kcat_ref.dtype)
        kcat_ref[:, r:r + LANES] = kr.astype(kcat_ref.dtype)


def _mla_proj(p, cos, sin, pos_blocks, qa_norm, kva_norm, w_nope, w_rope, w_rope_sw, w_kv, per_head, tm=256):
    t = p.shape[0]
    hq = MLA_HEADS * MLA_QK
    dqk = MLA_NOPE + LANES
    const2 = lambda i: (0, 0)
    row_block = lambda width: pl.BlockSpec((tm, width), lambda i: (i, 0))
    head_block = lambda width: pl.BlockSpec((MLA_HEADS, tm, width), lambda i: (0, i, 0))
    out_specs = [None, row_block(MLA_KV_LORA), row_block(LANES)]
    out_shape = [None, jax.ShapeDtypeStruct((t, MLA_KV_LORA), F32), jax.ShapeDtypeStruct((t, LANES), F32)]
    if per_head:
        out_specs[0] = head_block(dqk)
        out_shape[0] = jax.ShapeDtypeStruct((MLA_HEADS, t, dqk), BF16)
        out_specs += [head_block(dqk), head_block(MLA_V)]
        out_shape += [jax.ShapeDtypeStruct((MLA_HEADS, t, dqk), BF16), jax.ShapeDtypeStruct((MLA_HEADS, t, MLA_V), BF16)]
    else:
        out_specs[0] = row_block(hq)
        out_shape[0] = jax.ShapeDtypeStruct((t, hq), BF16)
        out_specs += [row_block(MLA_QK)]
        out_shape += [jax.ShapeDtypeStruct((t, MLA_QK), BF16)]
    return pl.pallas_call(
        functools.partial(_mla_proj_kernel, per_head=per_head),
        grid=(t // tm,),
        in_specs=[pl.BlockSpec((tm, p.shape[1]), lambda i: (i, 0)),
                  pl.BlockSpec((tm, LANES), lambda i: (i % pos_blocks, 0)),
                  pl.BlockSpec((tm, LANES), lambda i: (i % pos_blocks, 0)),
                  pl.BlockSpec((1, MLA_Q_LORA), const2),
                  pl.BlockSpec((1, MLA_KV_LORA), const2),
                  pl.BlockSpec(w_nope.shape, const2),
                  pl.BlockSpec(w_rope.shape, const2),
                  pl.BlockSpec(w_rope_sw.shape, const2),
                  pl.BlockSpec(w_kv.shape, lambda i: (0,) * w_kv.ndim)],
        out_specs=out_specs,
        out_shape=out_shape,
        compiler_params=_params(1),
        name="mla_proj",
    )(p, cos, sin, qa_norm.reshape(1, -1), kva_norm.reshape(1, -1), w_nope, w_rope, w_rope_sw, w_kv)


def _flash_update(s, v, m_ref, l_ref, acc_ref):
    m_prev = m_ref[...]
    m_new = jnp.maximum(m_prev, jnp.max(s, axis=1, keepdims=True))
    alpha = jnp.exp(m_prev - m_new)
    p = jnp.exp(s - _lane_tile(m_new, s.shape[1] // LANES))
    l_ref[...] = alpha * l_ref[...] + jnp.sum(p, axis=1, keepdims=True)
    acc_ref[...] = acc_ref[...] * _lane_tile(alpha, acc_ref.shape[1] // LANES) + jnp.dot(
        p.astype(BF16), v, preferred_element_type=F32)
    m_ref[...] = m_new


def _flash_init(m_sc, l_sc, acc_sc):
    m_sc[...] = jnp.full_like(m_sc, NEG_INF)
    l_sc[...] = jnp.zeros_like(l_sc)
    acc_sc[...] = jnp.zeros_like(acc_sc)


def _flash_result(l_ref, acc_ref):
    return acc_ref[...] / _lane_tile(l_ref[...], acc_ref.shape[1] // LANES)


def _mla_prompt_kernel(b_tab, qi_tab, ki_tab, q_ref, k_ref, v_ref, o_ref, m_sc, l_sc, acc_sc, *, tq, tk):
    step = pl.program_id(0)
    qi = qi_tab[step]
    ki = ki_tab[step]
    last = ki == (qi * tq) // tk

    @pl.when(ki == 0)
    def _():
        _flash_init(m_sc, l_sc, acc_sc)

    def run(masked):
        if masked:
            q_pos = qi * tq + lax.broadcasted_iota(jnp.int32, (tq, tk), 0)
            k_pos = ki * tk + lax.broadcasted_iota(jnp.int32, (tq, tk), 1)
            visible = k_pos <= q_pos
        for h in range(MLA_HEADS):
            s = lax.dot_general(q_ref[h], k_ref[h], (((1,), (1,)), ((), ())), preferred_element_type=F32)
            if masked:
                s = jnp.where(visible, s, NEG_INF)
            _flash_update(s, v_ref[h], m_sc.at[h], l_sc.at[h], acc_sc.at[h])
            if masked:
                o_ref[:, h * MLA_V:(h + 1) * MLA_V] = _flash_result(l_sc.at[h], acc_sc.at[h]).astype(o_ref.dtype)

    pl.when(last)(lambda: run(True))
    pl.when(jnp.logical_not(last))(lambda: run(False))


def _mla_prompt_attn(q3, k3, v3, n_seq, seq_len, tq=512, tk=512):
    dqk = q3.shape[2]
    nq, nk = seq_len // tq, seq_len // tk
    steps = [(b, qi, ki) for b in range(n_seq) for qi in range(nq) for ki in range((qi * tq) // tk + 1)]
    b_tab, qi_tab, ki_tab = (jnp.asarray(np.array(col, np.int32)) for col in zip(*steps))
    kern = functools.partial(_mla_prompt_kernel, tq=tq, tk=tk)
    grid_spec = pltpu.PrefetchScalarGridSpec(
        num_scalar_prefetch=3,
        grid=(len(steps),),
        in_specs=[pl.BlockSpec((MLA_HEADS, tq, dqk), lambda s, bt, qt, kt: (0, bt[s] * nq + qt[s], 0)),
                  pl.BlockSpec((MLA_HEADS, tk, dqk), lambda s, bt, qt, kt: (0, bt[s] * nk + kt[s], 0)),
                  pl.BlockSpec((MLA_HEADS, tk, MLA_V), lambda s, bt, qt, kt: (0, bt[s] * nk + kt[s], 0))],
        out_specs=pl.BlockSpec((tq, MLA_HEADS * MLA_V), lambda s, bt, qt, kt: (bt[s] * nq + qt[s], 0)),
        scratch_shapes=[pltpu.VMEM((MLA_HEADS, tq, LANES), F32), pltpu.VMEM((MLA_HEADS, tq, LANES), F32),
                        pltpu.VMEM((MLA_HEADS, tq, MLA_V), F32)])
    return pl.pallas_call(
        kern,
        grid_spec=grid_spec,
        out_shape=jax.ShapeDtypeStruct((q3.shape[1], MLA_HEADS * MLA_V), BF16),
        compiler_params=_params(1),
        name="mla_prompt_attn",
    )(b_tab, qi_tab, ki_tab, q3, k3, v3)


def _mla_sample_kernel(pt_ref, q_ref, knew_ref, *refs, n_groups, seq_len):
    kv_refs = refs[:PAGES_PER_STEP]
    krt_refs = refs[PAGES_PER_STEP:2 * PAGES_PER_STEP]
    o_ref, kv_sc, krt_sc, m_sc, l_sc, acc_sc = refs[2 * PAGES_PER_STEP:]
    g = pl.program_id(1)
    r = MLA_KV_LORA
    nt = (((1,), (1,)), ((), ()))

    @pl.when(g == 0)
    def _():
        _flash_init(m_sc, l_sc, acc_sc)

    for i in range(PAGES_PER_STEP):
        kv_sc[i * PAGE_SIZE:(i + 1) * PAGE_SIZE, :] = kv_refs[i][...].astype(BF16)
        krt_sc[:, i * PAGE_SIZE:(i + 1) * PAGE_SIZE] = krt_refs[i][...].astype(BF16)
    q = q_ref[...]
    kv = kv_sc[...]
    s = (lax.dot_general(q[:, 0:r], kv, nt, preferred_element_type=F32)
         + jnp.dot(q[:, r:r + MLA_ROPE], krt_sc[...], preferred_element_type=F32))
    _flash_update(s, kv, m_sc, l_sc, acc_sc)

    @pl.when(g == n_groups - 1)
    def _():
        knew = knew_ref[...]
        n_new = knew.shape[0]
        s_new = lax.dot_general(q, knew, nt, preferred_element_type=F32)
        rows = s_new.shape[0]
        q_t = _shr(lax.broadcasted_iota(jnp.int32, (rows, n_new), 0), MLA_HEADS)
        k_t = lax.broadcasted_iota(jnp.int32, (rows, n_new), 1)
        s_new = jnp.where((k_t <= q_t) & (k_t < seq_len), s_new, NEG_INF)
        m_old = m_sc[...]
        m_fin = jnp.maximum(m_old, jnp.max(s_new, axis=1, keepdims=True))
        a_fin = jnp.exp(m_old - m_fin)
        p_new = jnp.exp(s_new - m_fin[:, 0:n_new])
        l_sc[...] = a_fin * l_sc[...] + jnp.sum(p_new, axis=1, keepdims=True)
        acc_sc[...] = acc_sc[...] * _lane_tile(a_fin, r // LANES) + jnp.dot(
            p_new.astype(BF16), knew[:, 0:r], preferred_element_type=F32)
        o_ref[...] = _flash_result(l_sc, acc_sc).astype(o_ref.dtype)


def _mla_sample_attn(q2d, knew, cache_kv, cache_krt, page_table, layer_j, seq_len):
    b, n_pages = page_table.shape
    n_groups = n_pages // PAGES_PER_STEP
    rows = seq_len * MLA_HEADS
    keys = PAGES_PER_STEP * PAGE_SIZE
    pt_flat = page_table.reshape(-1)

    def page_spec(i, shape):
        return pl.BlockSpec(
            (None, None) + shape,
            lambda bi, gi, pt: (layer_j, pt[bi * n_pages + gi * PAGES_PER_STEP + i], 0, 0))

    kern = functools.partial(_mla_sample_kernel, n_groups=n_groups, seq_len=seq_len)
    grid_spec = pltpu.PrefetchScalarGridSpec(
        num_scalar_prefetch=1,
        grid=(b, n_groups),
        in_specs=([pl.BlockSpec((rows, MLA_QK), lambda bi, gi, pt: (bi, 0)),
                   pl.BlockSpec((None, knew.shape[1], MLA_QK), lambda bi, gi, pt: (bi, 0, 0))]
                  + [page_spec(i, (PAGE_SIZE, MLA_KV_LORA)) for i in range(PAGES_PER_STEP)]
                  + [page_spec(i, (MLA_ROPE, PAGE_SIZE)) for i in range(PAGES_PER_STEP)]),
        out_specs=pl.BlockSpec((rows, MLA_KV_LORA), lambda bi, gi, pt: (bi, 0)),
        scratch_shapes=[pltpu.VMEM((keys, MLA_KV_LORA), BF16), pltpu.VMEM((MLA_ROPE, keys), BF16),
                        pltpu.VMEM((rows, LANES), F32), pltpu.VMEM((rows, LANES), F32),
                        pltpu.VMEM((rows, MLA_KV_LORA), F32)])
    return pl.pallas_call(
        kern,
        grid_spec=grid_spec,
        out_shape=jax.ShapeDtypeStruct((q2d.shape[0], MLA_KV_LORA), BF16),
        compiler_params=_params(2),
        name="mla_sample_attn",
    )(pt_flat, q2d, knew, *([cache_kv] * PAGES_PER_STEP), *([cache_krt] * PAGES_PER_STEP))


def _mla_out_kernel(ol_ref, wuv_ref, wo_ref, x_ref, gt_ref, gpost_ref, o_ref, o_sc, out_sc, stat_sc, *, grp,
                    head_major):
    r = MLA_KV_LORA
    for h in range(MLA_HEADS):
        o_lat = ol_ref[h] if head_major else ol_ref[:, h * r:(h + 1) * r]
        o_h = jnp.dot(o_lat, wuv_ref[h], preferred_element_type=F32)
        o_sc[:, h * MLA_V:(h + 1) * MLA_V] = o_h.astype(BF16)
    out_sc[...] = jnp.dot(o_sc[...], wo_ref[...], preferred_element_type=F32)
    _ada_out_rows(x_ref, out_sc, gpost_ref, gt_ref, grp.row(), 1.0, stat_sc, o_ref)


def _mla_out(o_lat, w_uv, w_o, x, grp, layer, g_post, tm=256):
    t, d = x.shape
    head_major = o_lat.ndim == 3
    sub = _Group(grp.mod, tm, None if grp.tiles_per_mod_row is None else grp.tiles_per_mod_row * grp.tm // tm)
    kern = functools.partial(_mla_out_kernel, grp=sub, head_major=head_major)
    if head_major:
        ol_spec = pl.BlockSpec((MLA_HEADS, tm, MLA_KV_LORA), lambda i: (0, i, 0))
    else:
        ol_spec = pl.BlockSpec((tm, o_lat.shape[1]), lambda i: (i, 0))
    return pl.pallas_call(
        kern,
        grid=(t // tm,),
        in_specs=[ol_spec,
                  pl.BlockSpec(w_uv.shape, lambda i: (0, 0, 0)),
                  pl.BlockSpec(w_o.shape, lambda i: (0, 0)),
                  pl.BlockSpec((tm, d), lambda i: (i, 0)),
                  sub.mod_spec(layer, 3 + 2),
                  pl.BlockSpec((1, d), lambda i: (0, 0))],
        out_specs=pl.BlockSpec((tm, d), lambda i: (i, 0)),
        out_shape=jax.ShapeDtypeStruct((t, d), F32),
        scratch_shapes=[pltpu.VMEM((tm, MLA_HEADS * MLA_V), BF16), pltpu.VMEM((tm, d), F32),
                        pltpu.VMEM((tm, LANES), F32)],
        compiler_params=_params(1),
        name="mla_out",
    )(o_lat, w_uv, w_o, x, sub.mod, g_post.reshape(1, d))


def _t5_buckets(delta):
    n = np.maximum(delta, 0)
    max_exact = N_BUCKETS // 2
    log_ratio = np.log(np.maximum(n, 1).astype(np.float64) / max_exact) / math.log(BUCKET_MAX_DIST / max_exact)
    large = np.minimum(max_exact + (log_ratio * (N_BUCKETS - max_exact)).astype(np.int64), N_BUCKETS - 1)
    return np.where(n < max_exact, n, large).astype(np.int32)


def _swa_bias_table(rel_bias, delta, valid):
    lq, lk = delta.shape
    one_hot = (jnp.asarray(_t5_buckets(delta))[None] == jnp.arange(N_BUCKETS)[:, None, None]).astype(F32)
    bias = jnp.einsum("nh,nqk->hqk", rel_bias.astype(F32), one_hot, precision=lax.Precision.HIGHEST)
    bias = jnp.where(jnp.asarray(valid)[None], bias, NEG_INF)
    return bias.reshape(SWA_HEADS * lq, lk)


def _swa_sink_table(sinks, lq):
    return jnp.broadcast_to(jnp.repeat(sinks.astype(F32), lq)[:, None], (sinks.shape[0] * lq, LANES))


def _swa_q_perm():
    perm = np.zeros(SWA_HEADS * SWA_HEAD_DIM, np.int32)
    for m in range(SWA_KV_HEADS // 2):
        for g in range(SWA_GROUP):
            for p in range(2):
                src = ((2 * m + p) * SWA_GROUP + g) * SWA_HEAD_DIM
                dst = (4 * m + g) * LANES + p * SWA_HEAD_DIM
                perm[dst:dst + SWA_HEAD_DIM] = np.arange(src, src + SWA_HEAD_DIM)
    return perm


def _rope_tables(pos):
    half = MLA_ROPE // 2
    inv = ROPE_THETA ** (-jnp.arange(half, dtype=F32) / half)
    ang = pos.astype(F32)[:, None] * inv[None, :]
    cos, sin = jnp.cos(ang), jnp.sin(ang)
    zeros = jnp.zeros((pos.shape[0], LANES - MLA_ROPE), F32)
    return (jnp.concatenate([cos, cos, zeros], axis=1), jnp.concatenate([-sin, sin, zeros], axis=1))


def _pad_rope_cols(w):
    half = MLA_ROPE // 2
    z = jnp.zeros((w.shape[0], LANES - MLA_ROPE), w.dtype)
    return (jnp.concatenate([w, z], axis=1),
            jnp.concatenate([w[:, half:], w[:, :half], z], axis=1))


def kernel(x_prompt, x_sample, state_swa_k, state_swa_v, state_conv, cache_mla_kv, cache_mla_kr, page_table,
           c_prompt, c_sample, ada_w, ada_b, norm_pre, norm_post, ffn_w_in, ffn_w_out,
           gm_w_in, gm_ln_g, gm_ln_b, gm_w_s, gm_b_s, gm_w_out,
           swa_w_qkv, swa_w_o, swa_sinks, rel_bias,
           sc_w_in, sc_conv, sc_w_out,
           mla_w_qa, mla_qa_norm, mla_w_qb, mla_w_kva, mla_kva_norm, mla_w_kvb, mla_w_o):
    n_seq, seq_len, d = x_prompt.shape
    n_dec, dec_len, _ = x_sample.shape
    depth = ada_w.shape[0]
    past_len = page_table.shape[1] * PAGE_SIZE
    t_p, t_s = n_seq * seq_len, n_dec * dec_len

    c_all = jnp.concatenate([jnp.repeat(c_sample, dec_len, axis=0), c_prompt,
                             jnp.zeros((8 - n_seq, d), F32)], axis=0)
    mod = _ada(c_all, ada_w, ada_b)
    tm_p, tm_s = 512, 256
    mod_p = mod[:, t_s:t_s + 8]
    grp_p = _Group(mod_p, tm_p, seq_len // tm_p)
    grp_s = _Group(mod, tm_s, None)
    tm_ffn = 512
    grp_p_ffn = _Group(mod_p, tm_ffn, seq_len // tm_ffn)

    xs = [x_prompt.reshape(t_p, d), x_sample.reshape(t_s, d)]
    outs = {}
    ffn_w = (ffn_w_in[0, 0].astype(BF16), ffn_w_out[0, 0].astype(BF16))

    def ffn_pair(xs, ffn_w, layer, which):
        nxt = (layer, 1) if which == 0 else (layer + 1, 0)
        next_w = (ffn_w_in, ffn_w_out) + nxt if nxt[0] < depth else None
        sub = 2 * which
        res = _half_ffn(xs[0], grp_p_ffn, layer, which, norm_pre[layer, sub], norm_post[layer, sub], *ffn_w,
                        next_w=next_w)
        x_s = _half_ffn(xs[1], grp_s, layer, which, norm_pre[layer, sub], norm_post[layer, sub], *ffn_w)
        if next_w is None:
            return [res, x_s], None
        return [res[0], x_s], (res[1], res[2])

    for i in range(depth):
        kind, j = i % 4, i // 4
        xs, ffn_w = ffn_pair(xs, ffn_w, i, 0)

        if kind == 0:
            w_in = gm_w_in[j].astype(BF16)
            w_out = gm_w_out[j].astype(BF16)
            new = []
            for x, g, chunk, sl, cps in ((xs[0], grp_p, GM_CHUNK, GM_CHUNK, seq_len // GM_CHUNK),
                                         (xs[1], grp_s, GM_CHUNK, dec_len, 1)):
                lc = min(sl, GM_CHUNK)
                m = jnp.tile(gm_w_s[j][:, :lc, :lc], (1, chunk // lc, chunk // lc))
                bias = jnp.broadcast_to(jnp.tile(gm_b_s[j][:, :lc], (1, chunk // lc))[:, :, None],
                                        (GM_GROUPS, chunk, LANES))
                uv = _norm_mod_matmul(x, g, i, norm_pre[i, 1], w_in, act="gelu", out_dtype=BF16)
                mixed, st = _gm_core(uv, gm_ln_g[j], gm_ln_b[j], m, bias, chunk, lc, cps)
                new.append(_matmul_residual(mixed, w_out, x, g, i, norm_post[i, 1]))
                outs.setdefault("gm", []).append(st)
            xs = new

        elif kind == 1:
            perm = _swa_q_perm()
            nq = SWA_HEADS * SWA_HEAD_DIM
            nkv = SWA_KV_HEADS * SWA_HEAD_DIM
            w_qkv = jnp.concatenate([swa_w_qkv[j][:, :nq][:, perm], swa_w_qkv[j][:, nq:]], axis=1).astype(BF16)
            w_o = swa_w_o[j][perm, :].astype(BF16)
            w = SWA_WINDOW
            i_q, i_k = np.arange(w), np.arange(2 * w)
            delta = w + i_q[:, None] - i_k[None, :]
            bias_p = _swa_bias_table(rel_bias, delta, (delta >= 0) & (delta < w))
            sink_p = _swa_sink_table(swa_sinks[j], w)
            qkv_p = _norm_mod_matmul(xs[0], grp_p, i, norm_pre[i, 1], w_qkv)
            o_p = _swa_prompt_core(qkv_p, bias_p, sink_p, seq_len)
            x_p = _matmul_residual(o_p, w_o, xs[0], grp_p, i, norm_post[i, 1])
            kv_p = qkv_p.reshape(n_seq, seq_len, -1)[:, seq_len - w:, nq:]
            outs.setdefault("swa_kp", []).append(kv_p[..., :nkv].reshape(n_seq, w, SWA_KV_HEADS, SWA_HEAD_DIM))
            outs.setdefault("swa_vp", []).append(kv_p[..., nkv:].reshape(n_seq, w, SWA_KV_HEADS, SWA_HEAD_DIM))
            lb = state_swa_k.shape[2]
            n_keys = lb + dec_len
            pad = (-n_keys) % 8
            i_q, i_k = np.arange(dec_len), np.arange(n_keys + pad)
            delta = lb + i_q[:, None] - i_k[None, :]
            valid = (delta >= 0) & (delta < w) & (i_k[None, :] < n_keys)
            bias_s = _swa_bias_table(rel_bias, delta, valid)
            sink_s = _swa_sink_table(swa_sinks[j], dec_len)
            qkv_s = _norm_mod_matmul(xs[1], grp_s, i, norm_pre[i, 1], w_qkv).reshape(n_dec, dec_len, -1)
            zpad = jnp.zeros((n_dec, pad, nkv), F32)
            kk = jnp.concatenate([state_swa_k[j].reshape(n_dec, lb, nkv), qkv_s[..., nq:nq + nkv], zpad], axis=1)
            vv = jnp.concatenate([state_swa_v[j].reshape(n_dec, lb, nkv), qkv_s[..., nq + nkv:], zpad], axis=1)
            q_s = qkv_s[..., :nq].reshape(n_dec, dec_len, SWA_KV_HEADS // 2, SWA_GROUP, LANES)
            q_s = q_s.transpose(0, 2, 3, 1, 4).reshape(n_dec, SWA_KV_HEADS // 2, SWA_GROUP * dec_len, LANES)
            o_s = _swa_sample_core(q_s, kk, vv, bias_s, sink_s)
            o_s = o_s.reshape(n_dec, SWA_KV_HEADS // 2, SWA_GROUP, dec_len, LANES).transpose(0, 3, 1, 2, 4)
            x_s = _matmul_residual(o_s.reshape(t_s, nq), w_o, xs[1], grp_s, i, norm_post[i, 1])
            outs.setdefault("swa_ks", []).append(kk[:, n_keys - lb:n_keys].reshape(n_dec, lb, SWA_KV_HEADS, SWA_HEAD_DIM))
            outs.setdefault("swa_vs", []).append(vv[:, n_keys - lb:n_keys].reshape(n_dec, lb, SWA_KV_HEADS, SWA_HEAD_DIM))
            xs = [x_p, x_s]

        elif kind == 2:
            w_in = sc_w_in[j].astype(BF16)
            w_out = sc_w_out[j].astype(BF16)
            c = sc_w_out.shape[1]
            g3_p = _norm_mod_matmul(xs[0], grp_p, i, norm_pre[i, 1], w_in, out_dtype=BF16)
            y_p, st_p = _conv_prompt_core(g3_p, sc_conv[j], seq_len)
            x_p = _matmul_residual(y_p, w_out, xs[0], grp_p, i, norm_post[i, 1])
            outs.setdefault("conv_p", []).append(st_p[:, 8 - (CONV_WIDTH - 1):])
            g3_s = _norm_mod_matmul(xs[1], grp_s, i, norm_pre[i, 1], w_in, out_dtype=BF16)
            y_s, st_s = _conv_sample_core(g3_s.reshape(n_dec, dec_len * 3 * c),
                                          state_conv[j].reshape(n_dec, (CONV_WIDTH - 1) * c), sc_conv[j], dec_len)
            x_s = _matmul_residual(y_s.reshape(t_s, c), w_out, xs[1], grp_s, i, norm_post[i, 1])
            outs.setdefault("conv_s", []).append(st_s.reshape(n_dec, CONV_WIDTH - 1, c))
            xs = [x_p, x_s]

        else:
            r = MLA_KV_LORA
            kr_pad, kr_sw = _pad_rope_cols(mla_w_kva[j][:, r:])
            w_p = jnp.concatenate([mla_w_qa[j], mla_w_kva[j][:, :r], kr_pad, kr_sw], axis=1).astype(BF16)
            w_qb = mla_w_qb[j].reshape(MLA_Q_LORA, MLA_HEADS, MLA_NOPE + MLA_ROPE)
            w_nope = w_qb[:, :, :MLA_NOPE].reshape(MLA_Q_LORA, -1).astype(BF16)
            rope_pairs = [_pad_rope_cols(w_qb[:, h, MLA_NOPE:]) for h in range(MLA_HEADS)]
            w_rope = jnp.concatenate([p[0] for p in rope_pairs], axis=1).astype(BF16)
            w_rope_sw = jnp.concatenate([p[1] for p in rope_pairs], axis=1).astype(BF16)
            w_kvb = mla_w_kvb[j].reshape(r, MLA_HEADS, MLA_NOPE + MLA_V)
            w_uk_t = w_kvb[:, :, :MLA_NOPE].transpose(1, 2, 0).astype(BF16)
            w_uv = w_kvb[:, :, MLA_NOPE:].transpose(1, 0, 2).astype(BF16)
            w_o = mla_w_o[j].astype(BF16)
            tm = 256
            cos_p, sin_p = _rope_tables(jnp.arange(seq_len))
            proj_p = _norm_mod_matmul(xs[0], grp_p, i, norm_pre[i, 1], w_p, tn=w_p.shape[1])
            q_p, ckv_p, kr_p, k_p, v_p = _mla_proj(proj_p, cos_p, sin_p, seq_len // tm, mla_qa_norm[j],
                                                   mla_kva_norm[j], w_nope, w_rope, w_rope_sw,
                                                   mla_w_kvb[j].astype(BF16), True, tm=tm)
            o_p = _mla_prompt_attn(q_p, k_p, v_p, n_seq, seq_len)
            x_p = _matmul_residual(o_p, w_o, xs[0], grp_p, i, norm_post[i, 1])
            outs.setdefault("mla_kvp", []).append(ckv_p.reshape(n_seq, seq_len, r))
            outs.setdefault("mla_krp", []).append(kr_p[:, :MLA_ROPE].reshape(n_seq, seq_len, MLA_ROPE))
            cos_s, sin_s = _rope_tables(past_len + jnp.arange(dec_len))
            cos_s, sin_s = jnp.tile(cos_s, (tm // dec_len, 1)), jnp.tile(sin_s, (tm // dec_len, 1))
            proj_s = _norm_mod_matmul(xs[1], grp_s, i, norm_pre[i, 1], w_p, tn=w_p.shape[1])
            q_s, ckv_s, kr_s, kcat_s = _mla_proj(proj_s, cos_s, sin_s, 1, mla_qa_norm[j],
                                                 mla_kva_norm[j], w_nope, w_rope, w_rope_sw, w_uk_t, False, tm=tm)
            knew = jnp.concatenate([kcat_s.reshape(n_dec, dec_len, MLA_QK),
                                    jnp.zeros((n_dec, 16 - dec_len, MLA_QK), BF16)], axis=1)
            ol_s = _mla_sample_attn(q_s.reshape(t_s * MLA_HEADS, MLA_QK), knew, cache_mla_kv,
                                    jnp.swapaxes(cache_mla_kr, 2, 3), page_table, j, dec_len)
            x_s = _mla_out(ol_s.reshape(t_s, MLA_HEADS * r), w_uv, w_o, xs[1], grp_s, i, norm_post[i, 1])
            outs.setdefault("mla_kvs", []).append(ckv_s.reshape(n_dec, dec_len, r))
            outs.setdefault("mla_krs", []).append(kr_s[:, :MLA_ROPE].reshape(n_dec, dec_len, MLA_ROPE))
            xs = [x_p, x_s]

        xs, ffn_w = ffn_pair(xs, ffn_w, i, 1)

    gm_p, gm_s = outs["gm"][0::2], outs["gm"][1::2]
    return (xs[0].reshape(n_seq, seq_len, d), xs[1].reshape(n_dec, dec_len, d),
            jnp.stack(gm_p),
            jnp.stack([s.reshape(n_dec, dec_len, -1) for s in gm_s]),
            jnp.stack(outs["swa_kp"]), jnp.stack(outs["swa_vp"]),
            jnp.stack(outs["swa_ks"]), jnp.stack(outs["swa_vs"]),
            jnp.stack(outs["conv_p"]), jnp.stack(outs["conv_s"]),
            jnp.stack(outs["mla_kvp"]), jnp.stack(outs["mla_krp"]),
            jnp.stack(outs["mla_kvs"]), jnp.stack(outs["mla_krs"]))
```

```python
import functools
import math

import numpy as np
import jax
import jax.numpy as jnp
from jax import lax
from jax.experimental import pallas as pl
from jax.experimental.pallas import tpu as pltpu

F32 = jnp.float32
BF16 = jnp.bfloat16

VMEM_LIMIT_BYTES = 56 * 1024 * 1024
LANES = 128

RMS_EPS = 1e-6
NEG_INF = -1e30
FFN_RES_W = 0.5

D_MODEL = 2048
GM_GROUPS = 8
GM_CHUNK = 128
SWA_WINDOW = 128
SWA_HEAD_DIM = 64
SWA_HEADS = 32
SWA_KV_HEADS = 8
SWA_GROUP = 4
SWA_SCALE = SWA_HEAD_DIM ** -0.5
N_BUCKETS = 32
BUCKET_MAX_DIST = 128
CONV_WIDTH = 3
MLA_HEADS = 16
MLA_Q_LORA = 512
MLA_KV_LORA = 512
MLA_NOPE = 128
MLA_ROPE = 64
MLA_V = 128
MLA_SCALE = (MLA_NOPE + MLA_ROPE) ** -0.5
MLA_QK = MLA_KV_LORA + LANES
ROPE_THETA = 10000.0
PAGE_SIZE = 128
PAGES_PER_STEP = 64


def _params(n_axes):
    return pltpu.CompilerParams(dimension_semantics=("arbitrary",) * n_axes,
                                vmem_limit_bytes=VMEM_LIMIT_BYTES)


def _rms(x, g):
    return x * lax.rsqrt(jnp.mean(x * x, axis=-1, keepdims=True) + RMS_EPS) * g


def _lane_tile(x, n):
    return x if n == 1 else jnp.concatenate([x] * n, axis=1)


class _Group:
    def __init__(self, mod, tm, tiles_per_mod_row, single_tile=False):
        self.mod = mod
        self.tm = tm
        self.tiles_per_mod_row = tiles_per_mod_row
        self.single_tile = single_tile

    def mod_spec(self, layer, col):
        if self.tiles_per_mod_row is None:
            mode = {"pipeline_mode": pl.Buffered(1)} if self.single_tile else {}
            return pl.BlockSpec((None, self.tm, D_MODEL), lambda i, *_: (layer, i, col), **mode)
        return pl.BlockSpec((None, 8, D_MODEL), lambda i, *_: (layer, 0, col))

    def row(self):
        if self.tiles_per_mod_row is None:
            return None
        return pl.program_id(0) // self.tiles_per_mod_row

    @staticmethod
    def read(ref, row):
        return ref[...] if row is None else ref[pl.ds(row, 1), :]


SUBLANES = 8
ROW_CHUNK = 16


def _sublane_tile(x):
    return jnp.concatenate([x] * (ROW_CHUNK // SUBLANES), axis=0)


def _for_row_chunks(n_rows, body, unroll):
    def step(c, carry):
        body(pl.ds(pl.multiple_of(c * ROW_CHUNK, ROW_CHUNK), ROW_CHUNK))
        return carry
    lax.fori_loop(0, n_rows // ROW_CHUNK, step, 0, unroll=unroll)


def _row_rms_scale(val_ref, stat_ref):
    width = val_ref.shape[1]

    def body(rs):
        parts = [val_ref[rs, i * LANES:(i + 1) * LANES] for i in range(width // LANES)]
        parts = [p * p for p in parts]
        while len(parts) > 1:
            parts = [a + b for a, b in zip(parts[0::2], parts[1::2])] + ([parts[-1]] if len(parts) % 2 else [])
        scale = lax.rsqrt(jnp.sum(parts[0], axis=-1, keepdims=True) * (1.0 / width) + RMS_EPS)
        stat_ref[rs, :] = jnp.broadcast_to(scale, (ROW_CHUNK, LANES))

    _for_row_chunks(val_ref.shape[0], body, unroll=True)


def _ada_in_rows(x_ref, gpre_ref, sc_ref, sh_ref, row, stat_ref, h_ref):
    n_rows, width = x_ref.shape
    n_tile = width // LANES
    _row_rms_scale(x_ref, stat_ref)

    def per_row(rs, bc_ref):
        xn = x_ref[rs, :] * _lane_tile(stat_ref[rs, :], n_tile)
        h_ref[rs, :] = (xn * (gpre_ref[...] * (1.0 + sc_ref[rs, :])) + sh_ref[rs, :]).astype(h_ref.dtype)

    def per_tile(rs, bc_ref):
        xn = x_ref[rs, :] * _lane_tile(stat_ref[rs, :], n_tile)
        h_ref[rs, :] = (xn * _sublane_tile(bc_ref[0]) + _sublane_tile(bc_ref[1])).astype(h_ref.dtype)

    def run(bc_ref):
        if row is not None:
            bc_ref[0] = jnp.broadcast_to(gpre_ref[...] * (1.0 + sc_ref[pl.ds(row, 1), :]), (SUBLANES, width))
            bc_ref[1] = jnp.broadcast_to(sh_ref[pl.ds(row, 1), :], (SUBLANES, width))
        body = per_row if row is None else per_tile
        _for_row_chunks(n_rows, lambda rs: body(rs, bc_ref), unroll=2)

    pl.run_scoped(run, pltpu.VMEM((2, SUBLANES, width), F32))


def _ada_out_rows(x_ref, val_ref, gpost_ref, gt_ref, row, res_w, stat_ref, o_ref):
    n_rows, width = x_ref.shape
    n_tile = width // LANES
    _row_rms_scale(val_ref, stat_ref)

    def per_row(rs, bc_ref):
        vn = val_ref[rs, :] * _lane_tile(stat_ref[rs, :], n_tile)
        o_ref[rs, :] = x_ref[rs, :] + vn * (gpost_ref[...] * (res_w * gt_ref[rs, :]))

    def per_tile(rs, bc_ref):
        vn = val_ref[rs, :] * _lane_tile(stat_ref[rs, :], n_tile)
        o_ref[rs, :] = x_ref[rs, :] + vn * _sublane_tile(bc_ref[0])

    def run(bc_ref):
        if row is not None:
            bc_ref[0] = jnp.broadcast_to(gpost_ref[...] * (res_w * gt_ref[pl.ds(row, 1), :]), (SUBLANES, width))
        body = per_row if row is None else per_tile
        _for_row_chunks(n_rows, lambda rs: body(rs, bc_ref), unroll=2)

    pl.run_scoped(run, pltpu.VMEM((1, SUBLANES, width), F32))


def _shr(x, divisor):
    shift = divisor.bit_length() - 1
    assert 1 << shift == divisor
    return lax.shift_right_logical(x, shift)


def _ada_kernel(c_ref, w_ref, b_ref, o_ref, cs_sc):
    @pl.when((pl.program_id(0) == 0) & (pl.program_id(1) == 0))
    def _():
        c = c_ref[...]
        cs_sc[...] = (c * jax.nn.sigmoid(c)).astype(BF16)

    o_ref[...] = jnp.dot(cs_sc[...], w_ref[...].astype(BF16),
                         preferred_element_type=F32) + b_ref[...]


def _ada(c_all, ada_w, ada_b, tn=1024):
    n_layers, d, n = ada_w.shape
    rows = c_all.shape[0]
    return pl.pallas_call(
        _ada_kernel,
        grid=(n_layers, n // tn),
        in_specs=[pl.BlockSpec((rows, d), lambda l, j: (0, 0)),
                  pl.BlockSpec((None, d, tn), lambda l, j: (l, 0, j)),
                  pl.BlockSpec((None, 1, tn), lambda l, j: (l, 0, j))],
        out_specs=pl.BlockSpec((None, rows, tn), lambda l, j: (l, 0, j)),
        out_shape=jax.ShapeDtypeStruct((n_layers, rows, n), F32),
        scratch_shapes=[pltpu.VMEM((rows, d), BF16)],
        compiler_params=_params(2),
        name="ada_modulation",
    )(c_all, ada_w, ada_b.reshape(n_layers, 1, n))


def _ffn_kernel(x_ref, sh_ref, sc_ref, gt_ref, gpre_ref, gpost_ref, wg_ref, wu_ref, wo_ref, *rest, grp, n_f, share):
    j = pl.program_id(1)
    if len(rest) == 7:
        nwi_ref, nwo_ref, o_ref, nwi_b_ref, nwo_b_ref, h_sc, stat_sc = rest

        @pl.when((pl.program_id(0) * n_f + j) % share == 0)
        def _():
            nwi_b_ref[...] = nwi_ref[...].astype(BF16)
            nwo_b_ref[...] = nwo_ref[...].astype(BF16)
    else:
        o_ref, h_sc, stat_sc = rest
    row = grp.row()
    acc_sc = o_ref

    @pl.when(j == 0)
    def _():
        _ada_in_rows(x_ref, gpre_ref, sc_ref, sh_ref, row, stat_sc, h_sc)
        acc_sc[...] = jnp.zeros_like(acc_sc)

    h = h_sc[...]
    tf = wg_ref.shape[1]
    acts = []
    for c in range(2):
        cols = slice(c * tf // 2, (c + 1) * tf // 2)
        g = jnp.dot(h, wg_ref[:, cols], preferred_element_type=F32)
        u = jnp.dot(h, wu_ref[:, cols], preferred_element_type=F32)
        acts.append((g * jax.nn.sigmoid(g) * u).astype(BF16))
    acc_sc[...] += jnp.dot(jnp.concatenate(acts, axis=1), wo_ref[...], preferred_element_type=F32)

    @pl.when(j == n_f - 1)
    def _():
        _ada_out_rows(x_ref, acc_sc, gpost_ref, gt_ref, row, FFN_RES_W, stat_sc, o_ref)


def _half_ffn(x, grp, layer, which, g_pre, g_post, w_in, w_out, next_w=None, tf=512):
    t, d = x.shape
    f = w_out.shape[0]
    tm, n_f = grp.tm, f // tf
    n_i = t // tm
    sub = 2 * which
    share = 4 if n_i * n_f > 100 else 1
    kern = functools.partial(_ffn_kernel, grp=grp, n_f=n_f, share=share)
    in_specs = [pl.BlockSpec((tm, d), lambda i, j: (i, 0)),
                grp.mod_spec(layer, sub * 3 + 0),
                grp.mod_spec(layer, sub * 3 + 1),
                grp.mod_spec(layer, sub * 3 + 2),
                pl.BlockSpec((1, d), lambda i, j: (0, 0)),
                pl.BlockSpec((1, d), lambda i, j: (0, 0)),
                pl.BlockSpec((d, tf), lambda i, j: (0, j)),
                pl.BlockSpec((d, tf), lambda i, j: (0, n_f + j)),
                pl.BlockSpec((tf, d), lambda i, j: (j, 0))]
    args = [x, grp.mod, grp.mod, grp.mod, g_pre.reshape(1, d), g_post.reshape(1, d), w_in, w_in, w_out]
    out_specs = [pl.BlockSpec((tm, d), lambda i, j: (i, 0))]
    out_shape = [jax.ShapeDtypeStruct((t, d), F32)]
    if next_w is not None:
        nw_in, nw_out, nl, nwh = next_w
        assert n_i % share == 0
        bi = (share * d // n_i, 2 * f // n_f)
        bo = (f // n_f, share * d // n_i)

        def blk(i, j):
            s = (i * n_f + j) // share
            return s // n_f, s % n_f

        in_specs += [pl.BlockSpec((None, None) + bi, lambda i, j: (nl, nwh) + blk(i, j)),
                     pl.BlockSpec((None, None) + bo, lambda i, j: (nl, nwh) + blk(i, j)[::-1])]
        args += [nw_in, nw_out]
        out_specs += [pl.BlockSpec(bi, blk), pl.BlockSpec(bo, lambda i, j: blk(i, j)[::-1])]
        out_shape += [jax.ShapeDtypeStruct((d, 2 * f), BF16), jax.ShapeDtypeStruct((f, d), BF16)]
    res = pl.pallas_call(
        kern,
        grid=(n_i, n_f),
        in_specs=in_specs,
        out_specs=out_specs,
        out_shape=out_shape,
        scratch_shapes=[pltpu.VMEM((tm, d), BF16), pltpu.VMEM((tm, LANES), F32)],
        compiler_params=_params(2),
        name="half_ffn",
    )(*args)
    return res if next_w is not None else res[0]


def _nmm_kernel(x_ref, sh_ref, sc_ref, gpre_ref, w_ref, o_ref, h_sc, stat_sc, *, grp, act):
    row = grp.row()

    @pl.when(pl.program_id(1) == 0)
    def _():
        _ada_in_rows(x_ref, gpre_ref, sc_ref, sh_ref, row, stat_sc, h_sc)

    y = jnp.dot(h_sc[...], w_ref[...], preferred_element_type=F32)
    if act == "gelu":
        y = jax.nn.gelu(y, approximate=True)
    o_ref[...] = y.astype(o_ref.dtype)


def _norm_mod_matmul(x, grp, layer, g_pre, w, act=None, tn=None, out_dtype=F32):
    t, d = x.shape
    n = w.shape[1]
    if tn is None:
        tn = max(c for c in range(LANES, min(n, 2048) + 1, LANES) if n % c == 0)
    tm = grp.tm
    kern = functools.partial(_nmm_kernel, grp=grp, act=act)
    return pl.pallas_call(
        kern,
        grid=(t // tm, n // tn),
        in_specs=[pl.BlockSpec((tm, d), lambda i, j: (i, 0)),
                  grp.mod_spec(layer, 3 + 0),
                  grp.mod_spec(layer, 3 + 1),
                  pl.BlockSpec((1, d), lambda i, j: (0, 0)),
                  pl.BlockSpec((d, tn), lambda i, j: (0, j))],
        out_specs=pl.BlockSpec((tm, tn), lambda i, j: (i, j)),
        out_shape=jax.ShapeDtypeStruct((t, n), out_dtype),
        scratch_shapes=[pltpu.VMEM((tm, d), BF16), pltpu.VMEM((tm, LANES), F32)],
        compiler_params=_params(2),
        name="norm_mod_matmul",
    )(x, grp.mod, grp.mod, g_pre.reshape(1, d), w)


def _mres_kernel(a_ref, w_ref, x_ref, gt_ref, gpost_ref, o_ref, acc_sc, stat_sc, *, grp, n_k):
    k = pl.program_id(1)
    row = grp.row()

    @pl.when(k == 0)
    def _():
        acc_sc[...] = jnp.zeros_like(acc_sc)

    acc_sc[...] += jnp.dot(a_ref[...].astype(BF16), w_ref[...], preferred_element_type=F32)

    @pl.when(k == n_k - 1)
    def _():
        _ada_out_rows(x_ref, acc_sc, gpost_ref, gt_ref, row, 1.0, stat_sc, o_ref)


def _matmul_residual(a, w, x, grp, layer, g_post, tk=2048):
    t, kdim = a.shape
    d = w.shape[1]
    tm, n_k = grp.tm, kdim // tk
    kern = functools.partial(_mres_kernel, grp=grp, n_k=n_k)
    return pl.pallas_call(
        kern,
        grid=(t // tm, n_k),
        in_specs=[pl.BlockSpec((tm, tk), lambda i, k: (i, k)),
                  pl.BlockSpec((tk, d), lambda i, k: (k, 0)),
                  pl.BlockSpec((tm, d), lambda i, k: (i, 0)),
                  grp.mod_spec(layer, 3 + 2),
                  pl.BlockSpec((1, d), lambda i, k: (0, 0))],
        out_specs=pl.BlockSpec((tm, d), lambda i, k: (i, 0)),
        out_shape=jax.ShapeDtypeStruct((t, d), F32),
        scratch_shapes=[pltpu.VMEM((tm, d), F32), pltpu.VMEM((tm, LANES), F32)],
        compiler_params=_params(2),
        name="matmul_residual",
    )(a, w, x, grp.mod, g_post.reshape(1, d))


def _gm_kernel(u_ref, v_ref, lng_ref, lnb_ref, m_ref, b_ref, o_ref, st_ref, *, seq_len):
    v = v_ref[...].astype(F32)
    vc = v - jnp.mean(v, axis=-1, keepdims=True)
    vn = vc * lax.rsqrt(jnp.mean(vc * vc, axis=-1, keepdims=True) + RMS_EPS) * lng_ref[...] + lnb_ref[...]
    st_ref[...] = vn
    vnb = vn.astype(BF16)
    c = v.shape[0]
    gw = v.shape[1] // GM_GROUPS
    row = lax.broadcasted_iota(jnp.int32, (c, c), 0)
    col = lax.broadcasted_iota(jnp.int32, (c, c), 1)
    keep = (col <= row) & (_shr(row, seq_len) == _shr(col, seq_len))
    for g in range(GM_GROUPS):
        mg = jnp.where(keep, m_ref[g], 0.0).astype(BF16)
        mixed = jnp.dot(mg, vnb[:, g * gw:(g + 1) * gw], preferred_element_type=F32)
        mixed = mixed + _lane_tile(b_ref[g], gw // LANES)
        o_ref[:, g * gw:(g + 1) * gw] = (u_ref[:, g * gw:(g + 1) * gw].astype(F32) * mixed).astype(o_ref.dtype)


def _gm_core(uv, ln_g, ln_b, m, bias, chunk, seq_len, chunks_per_state):
    t = uv.shape[0]
    w = uv.shape[1] // 2
    n_chunks = t // chunk
    n_states = n_chunks // chunks_per_state
    kern = functools.partial(_gm_kernel, seq_len=seq_len)
    return pl.pallas_call(
        kern,
        grid=(n_chunks,),
        in_specs=[pl.BlockSpec((chunk, w), lambda c: (c, 0)),
                  pl.BlockSpec((chunk, w), lambda c: (c, 1)),
                  pl.BlockSpec((1, w), lambda c: (0, 0)),
                  pl.BlockSpec((1, w), lambda c: (0, 0)),
                  pl.BlockSpec((GM_GROUPS, chunk, chunk), lambda c: (0, 0, 0)),
                  pl.BlockSpec((GM_GROUPS, chunk, LANES), lambda c: (0, 0, 0))],
        out_specs=[pl.BlockSpec((chunk, w), lambda c: (c, 0)),
                   pl.BlockSpec((None, chunk, w), lambda c: (c // chunks_per_state, 0, 0))],
        out_shape=[jax.ShapeDtypeStruct((t, w), BF16),
                   jax.ShapeDtypeStruct((n_states, chunk, w), F32)],
        compiler_params=_params(1),
        name="gm_core",
    )(uv, uv, ln_g.reshape(1, w), ln_b.reshape(1, w), m, bias)


def _softmax_sink_parts(s, sink_b):
    mx = jnp.maximum(jnp.broadcast_to(jnp.max(s, axis=-1, keepdims=True), sink_b.shape), sink_b)
    n_keys = s.shape[1]
    mx_keys = _lane_tile(mx, n_keys // LANES) if n_keys % LANES == 0 else mx[:, 0:1]
    return jnp.exp(s - mx_keys).astype(BF16), jnp.exp(sink_b - mx)


def _half_lane_mask(rows, parity):
    lane = lax.broadcasted_iota(jnp.int32, (rows, LANES), 1)
    return (lane >= SWA_HEAD_DIM) if parity else (lane < SWA_HEAD_DIM)


def _swa_attend(q_blocks, k, v, bias_ref, sink_ref, masked_keys=None):
    rows = q_blocks[0].shape[0]
    n_keys = k.shape[0]
    logits, values = [], []
    for m in range(SWA_KV_HEADS // 2):
        k2 = k[:, m * LANES:(m + 1) * LANES]
        v2 = v[:, m * LANES:(m + 1) * LANES]
        for parity in range(2):
            keep = _half_lane_mask(n_keys, parity)
            kx = jnp.where(keep, k2, 0.0).astype(BF16)
            values.append(jnp.where(keep, v2, 0.0).astype(BF16))
            logits.append(lax.dot_general(q_blocks[m], kx, (((1,), (1,)), ((), ())), preferred_element_type=F32))
    s = jnp.concatenate(logits, axis=0) + bias_ref[...]
    if masked_keys is not None:
        key_col = lax.broadcasted_iota(jnp.int32, s.shape, 1)
        s = jnp.where(key_col < masked_keys, NEG_INF, s)
    p, p_sink = _softmax_sink_parts(s, sink_ref[...])
    denom = jnp.dot(p, jnp.ones((n_keys, LANES), BF16), preferred_element_type=F32) + p_sink
    outs = []
    for m in range(SWA_KV_HEADS // 2):
        o_m = None
        for parity in range(2):
            idx = 2 * m + parity
            sl = slice(idx * rows, (idx + 1) * rows)
            o_p = jnp.dot(p[sl], values[idx], preferred_element_type=F32) / denom[sl]
            o_m = o_p if o_m is None else o_m + o_p
        outs.append(o_m)
    return outs


def _swa_prompt_kernel(q_ref, kp_ref, ko_ref, vp_ref, vo_ref, bias_ref, sink_ref, o_ref, *, blocks_per_seq):
    w = SWA_WINDOW
    first = (pl.program_id(0) % blocks_per_seq) == 0
    q = (q_ref[...] * SWA_SCALE).astype(BF16)
    k = jnp.concatenate([kp_ref[...], ko_ref[...]], axis=0)
    v = jnp.concatenate([vp_ref[...], vo_ref[...]], axis=0)
    q_blocks = [jnp.concatenate([q[:, (4 * m + g) * LANES:(4 * m + g + 1) * LANES] for g in range(SWA_GROUP)],
                                axis=0) for m in range(SWA_KV_HEADS // 2)]
    outs = _swa_attend(q_blocks, k, v, bias_ref, sink_ref, masked_keys=jnp.where(first, w, 0))
    for m, o_m in enumerate(outs):
        for g in range(SWA_GROUP):
            o_ref[:, (4 * m + g) * LANES:(4 * m + g + 1) * LANES] = o_m[g * w:(g + 1) * w].astype(o_ref.dtype)


def _swa_prompt_core(qkv, bias, sink, seq_len):
    t = qkv.shape[0]
    w = SWA_WINDOW
    nq = SWA_HEADS * SWA_HEAD_DIM
    nkv = SWA_KV_HEADS * SWA_HEAD_DIM
    bps = seq_len // w
    kcol, vcol = nq // nkv, nq // nkv + 1

    def prev(i):
        return jnp.maximum(i - 1, 0)

    kern = functools.partial(_swa_prompt_kernel, blocks_per_seq=bps)
    return pl.pallas_call(
        kern,
        grid=(t // w,),
        in_specs=[pl.BlockSpec((w, nq), lambda i: (i, 0)),
                  pl.BlockSpec((w, nkv), lambda i: (prev(i), kcol)),
                  pl.BlockSpec((w, nkv), lambda i: (i, kcol)),
                  pl.BlockSpec((w, nkv), lambda i: (prev(i), vcol)),
                  pl.BlockSpec((w, nkv), lambda i: (i, vcol)),
                  pl.BlockSpec(bias.shape, lambda i: (0, 0)),
                  pl.BlockSpec(sink.shape, lambda i: (0, 0))],
        out_specs=pl.BlockSpec((w, nq), lambda i: (i, 0)),
        out_shape=jax.ShapeDtypeStruct((t, nq), BF16),
        compiler_params=_params(1),
        name="swa_prompt_core",
    )(qkv, qkv, qkv, qkv, qkv, bias, sink)


def _swa_sample_kernel(q_ref, k_ref, v_ref, bias_ref, sink_ref, o_ref, *, bb):
    def body(b, carry):
        q_blocks = [(q_ref[b, m] * SWA_SCALE).astype(BF16) for m in range(SWA_KV_HEADS // 2)]
        outs = _swa_attend(q_blocks, k_ref[b], v_ref[b], bias_ref, sink_ref)
        for m, o_m in enumerate(outs):
            o_ref[b, m] = o_m.astype(o_ref.dtype)
        return carry

    lax.fori_loop(0, bb, body, 0, unroll=2)


def _swa_sample_core(q, kk, vv, bias, sink, bb=8):
    b, n_pair, rows, _ = q.shape
    n_keys = kk.shape[1]
    nkv = kk.shape[2]
    kern = functools.partial(_swa_sample_kernel, bb=bb)
    return pl.pallas_call(
        kern,
        grid=(b // bb,),
        in_specs=[pl.BlockSpec((bb, n_pair, rows, LANES), lambda i: (i, 0, 0, 0)),
                  pl.BlockSpec((bb, n_keys, nkv), lambda i: (i, 0, 0)),
                  pl.BlockSpec((bb, n_keys, nkv), lambda i: (i, 0, 0)),
                  pl.BlockSpec(bias.shape, lambda i: (0, 0)),
                  pl.BlockSpec(sink.shape, lambda i: (0, 0))],
        out_specs=pl.BlockSpec((bb, n_pair, rows, LANES), lambda i: (i, 0, 0, 0)),
        out_shape=jax.ShapeDtypeStruct(q.shape, BF16),
        compiler_params=_params(1),
        name="swa_sample_core",
    )(q, kk, vv, bias, sink)


def _conv_prompt_kernel(gb_ref, gc_ref, z_ref, hc_ref, hz_ref, cw_ref, o_ref, st_ref, zp_sc, *, tiles_per_seq):
    tm = gb_ref.shape[0]
    hr = hc_ref.shape[0]
    first = (pl.program_id(0) % tiles_per_seq) == 0
    zz = gc_ref[...].astype(F32) * z_ref[...].astype(F32)
    halo = jnp.where(first, 0.0, hc_ref[...].astype(F32) * hz_ref[...].astype(F32))
    zp_sc[0:8, :] = halo[hr - 8:hr]
    zp_sc[8:8 + tm, :] = zz
    y = cw_ref[2:3, :] * zz + cw_ref[1:2, :] * zp_sc[7:7 + tm, :] + cw_ref[0:1, :] * zp_sc[6:6 + tm, :]
    o_ref[...] = (gb_ref[...].astype(F32) * y).astype(o_ref.dtype)
    st_ref[...] = zz[tm - 8:tm]


def _conv_prompt_core(g3, conv_w, seq_len, tm=256):
    t = g3.shape[0]
    c = g3.shape[1] // 3
    tps = seq_len // tm
    n_seq = t // seq_len

    hr = 16

    def halo(col):
        return pl.BlockSpec((hr, c), lambda i: (jnp.maximum(i * (tm // hr) - 1, 0), col))

    kern = functools.partial(_conv_prompt_kernel, tiles_per_seq=tps)
    return pl.pallas_call(
        kern,
        grid=(t // tm,),
        in_specs=[pl.BlockSpec((tm, c), lambda i: (i, 0)),
                  pl.BlockSpec((tm, c), lambda i: (i, 1)),
                  pl.BlockSpec((tm, c), lambda i: (i, 2)),
                  halo(1), halo(2),
                  pl.BlockSpec((CONV_WIDTH, c), lambda i: (0, 0))],
        out_specs=[pl.BlockSpec((tm, c), lambda i: (i, 0)),
                   pl.BlockSpec((None, 8, c), lambda i: (i // tps, 0, 0))],
        out_shape=[jax.ShapeDtypeStruct((t, c), BF16),
                   jax.ShapeDtypeStruct((n_seq, 8, c), F32)],
        scratch_shapes=[pltpu.VMEM((tm + 8, c), F32)],
        compiler_params=_params(1),
        name="conv_prompt_core",
    )(g3, g3, g3, g3, g3, conv_w)


def _conv_sample_kernel(g_ref, prev_ref, cw_ref, o_ref, st_ref, *, seq_len, c):
    zz = [prev_ref[:, 0:c], prev_ref[:, c:2 * c]]
    for t in range(seq_len):
        base = t * 3 * c
        zz.append(g_ref[:, base + c:base + 2 * c].astype(F32) * g_ref[:, base + 2 * c:base + 3 * c].astype(F32))
    for t in range(seq_len):
        y = cw_ref[2:3, :] * zz[t + 2] + cw_ref[1:2, :] * zz[t + 1] + cw_ref[0:1, :] * zz[t]
        o_ref[:, t * c:(t + 1) * c] = (g_ref[:, t * 3 * c:t * 3 * c + c].astype(F32) * y).astype(o_ref.dtype)
    st_ref[:, 0:c] = zz[seq_len]
    st_ref[:, c:2 * c] = zz[seq_len + 1]


def _conv_sample_core(g3, prev, conv_w, seq_len):
    b = g3.shape[0]
    c = g3.shape[1] // (3 * seq_len)
    kern = functools.partial(_conv_sample_kernel, seq_len=seq_len, c=c)
    return pl.pallas_call(
        kern,
        grid=(1,),
        in_specs=[pl.BlockSpec(g3.shape, lambda i: (0, 0)),
                  pl.BlockSpec(prev.shape, lambda i: (0, 0)),
                  pl.BlockSpec((CONV_WIDTH, c), lambda i: (0, 0))],
        out_specs=[pl.BlockSpec((b, seq_len * c), lambda i: (0, 0)),
                   pl.BlockSpec((b, 2 * c), lambda i: (0, 0))],
        out_shape=[jax.ShapeDtypeStruct((b, seq_len * c), BF16),
                   jax.ShapeDtypeStruct((b, 2 * c), F32)],
        compiler_params=_params(1),
        name="conv_sample_core",
    )(g3, prev, conv_w)


def _mla_proj_kernel(p_ref, cos_ref, sin_ref, qan_ref, kvn_ref, wn_ref, wr_ref, wrs_ref, wkv_ref, *out_refs, per_head):
    r = MLA_KV_LORA
    cos = cos_ref[...]
    sin = sin_ref[...]
    qa = _rms(p_ref[:, 0:MLA_Q_LORA], qan_ref[...]).astype(BF16)
    qn = jnp.dot(qa, wn_ref[...], preferred_element_type=F32)
    qr = jnp.dot(qa, wr_ref[...], preferred_element_type=F32)
    qrs = jnp.dot(qa, wrs_ref[...], preferred_element_type=F32)
    off = MLA_Q_LORA
    ckv = _rms(p_ref[:, off:off + r], kvn_ref[...])
    kr = p_ref[:, off + r:off + r + LANES] * cos + p_ref[:, off + r + LANES:off + r + 2 * LANES] * sin
    if per_head:
        q_ref, ckv_ref, kr_ref, k_ref, v_ref = out_refs
        kv_up = jnp.dot(ckv.astype(BF16), wkv_ref[...], preferred_element_type=F32)
        kr_b = kr.astype(k_ref.dtype)
    else:
        q_ref, ckv_ref, kr_ref, kcat_ref = out_refs
    for h in range(MLA_HEADS):
        sl = slice(h * LANES, (h + 1) * LANES)
        q_rope = ((qr[:, sl] * cos + qrs[:, sl] * sin) * MLA_SCALE).astype(q_ref.dtype)
        if per_head:
            q_ref[h, :, 0:MLA_NOPE] = (qn[:, sl] * MLA_SCALE).astype(q_ref.dtype)
            q_ref[h, :, MLA_NOPE:MLA_NOPE + LANES] = q_rope
            base = h * (MLA_NOPE + MLA_V)
            k_ref[h, :, 0:MLA_NOPE] = kv_up[:, base:base + MLA_NOPE].astype(k_ref.dtype)
            k_ref[h, :, MLA_NOPE:MLA_NOPE + LANES] = kr_b
            v_ref[h] = kv_up[:, base + MLA_NOPE:base + MLA_NOPE + MLA_V].astype(v_ref.dtype)
        else:
            q_lat = jnp.dot(qn[:, sl].astype(BF16), wkv_ref[h], preferred_element_type=F32)
            q_ref[:, h * MLA_QK:h * MLA_QK + r] = (q_lat * MLA_SCALE).astype(q_ref.dtype)
            q_ref[:, h * MLA_QK + r:(h + 1) * MLA_QK] = q_rope
    ckv_ref[...] = ckv
    kr_ref[...] = kr
    if not per_head:
        kcat_ref[:, 0:r] = ckv.astype(kcat_ref.dtype)
        kcat_ref[:, r:r + LANES] = kr.astype(kcat_ref.dtype)


def _mla_proj(p, cos, sin, pos_blocks, qa_norm, kva_norm, w_nope, w_rope, w_rope_sw, w_kv, per_head, tm=256):
    t = p.shape[0]
    hq = MLA_HEADS * MLA_QK
    dqk = MLA_NOPE + LANES
    const2 = lambda i: (0, 0)
    row_block = lambda width: pl.BlockSpec((tm, width), lambda i: (i, 0))
    head_block = lambda width: pl.BlockSpec((MLA_HEADS, tm, width), lambda i: (0, i, 0))
    out_specs = [None, row_block(MLA_KV_LORA), row_block(LANES)]
    out_shape = [None, jax.ShapeDtypeStruct((t, MLA_KV_LORA), F32), jax.ShapeDtypeStruct((t, LANES), F32)]
    if per_head:
        out_specs[0] = head_block(dqk)
        out_shape[0] = jax.ShapeDtypeStruct((MLA_HEADS, t, dqk), BF16)
        out_specs += [head_block(dqk), head_block(MLA_V)]
        out_shape += [jax.ShapeDtypeStruct((MLA_HEADS, t, dqk), BF16), jax.ShapeDtypeStruct((MLA_HEADS, t, MLA_V), BF16)]
    else:
        out_specs[0] = row_block(hq)
        out_shape[0] = jax.ShapeDtypeStruct((t, hq), BF16)
        out_specs += [row_block(MLA_QK)]
        out_shape += [jax.ShapeDtypeStruct((t, MLA_QK), BF16)]
    return pl.pallas_call(
        functools.partial(_mla_proj_kernel, per_head=per_head),
        grid=(t // tm,),
        in_specs=[pl.BlockSpec((tm, p.shape[1]), lambda i: (i, 0)),
                  pl.BlockSpec((tm, LANES), lambda i: (i % pos_blocks, 0)),
                  pl.BlockSpec((tm, LANES), lambda i: (i % pos_blocks, 0)),
                  pl.BlockSpec((1, MLA_Q_LORA), const2),
                  pl.BlockSpec((1, MLA_KV_LORA), const2),
                  pl.BlockSpec(w_nope.shape, const2),
                  pl.BlockSpec(w_rope.shape, const2),
                  pl.BlockSpec(w_rope_sw.shape, const2),
                  pl.BlockSpec(w_kv.shape, lambda i: (0,) * w_kv.ndim)],
        out_specs=out_specs,
        out_shape=out_shape,
        compiler_params=_params(1),
        name="mla_proj",
    )(p, cos, sin, qa_norm.reshape(1, -1), kva_norm.reshape(1, -1), w_nope, w_rope, w_rope_sw, w_kv)


def _flash_update(s, v, m_ref, l_ref, acc_ref):
    m_prev = m_ref[...]
    m_new = jnp.maximum(m_prev, jnp.max(s, axis=1, keepdims=True))
    alpha = jnp.exp(m_prev - m_new)
    p = jnp.exp(s - _lane_tile(m_new, s.shape[1] // LANES))
    l_ref[...] = alpha * l_ref[...] + jnp.sum(p, axis=1, keepdims=True)
    acc_ref[...] = acc_ref[...] * _lane_tile(alpha, acc_ref.shape[1] // LANES) + jnp.dot(
        p.astype(BF16), v, preferred_element_type=F32)
    m_ref[...] = m_new


def _flash_init(m_sc, l_sc, acc_sc):
    m_sc[...] = jnp.full_like(m_sc, NEG_INF)
    l_sc[...] = jnp.zeros_like(l_sc)
    acc_sc[...] = jnp.zeros_like(acc_sc)


def _flash_result(l_ref, acc_ref):
    return acc_ref[...] / _lane_tile(l_ref[...], acc_ref.shape[1] // LANES)


def _mla_prompt_kernel(b_tab, qi_tab, ki_tab, q_ref, k_ref, v_ref, o_ref, m_sc, l_sc, acc_sc, *, tq, tk):
    step = pl.program_id(0)
    qi = qi_tab[step]
    ki = ki_tab[step]
    last = ki == (qi * tq) // tk

    @pl.when(ki == 0)
    def _():
        _flash_init(m_sc, l_sc, acc_sc)

    def run(masked):
        if masked:
            q_pos = qi * tq + lax.broadcasted_iota(jnp.int32, (tq, tk), 0)
            k_pos = ki * tk + lax.broadcasted_iota(jnp.int32, (tq, tk), 1)
            visible = k_pos <= q_pos
        for h in range(MLA_HEADS):
            s = lax.dot_general(q_ref[h], k_ref[h], (((1,), (1,)), ((), ())), preferred_element_type=F32)
            if masked:
                s = jnp.where(visible, s, NEG_INF)
            _flash_update(s, v_ref[h], m_sc.at[h], l_sc.at[h], acc_sc.at[h])
            if masked:
                o_ref[:, h * MLA_V:(h + 1) * MLA_V] = _flash_result(l_sc.at[h], acc_sc.at[h]).astype(o_ref.dtype)

    pl.when(last)(lambda: run(True))
    pl.when(jnp.logical_not(last))(lambda: run(False))


def _mla_prompt_attn(q3, k3, v3, n_seq, seq_len, tq=512, tk=512):
    dqk = q3.shape[2]
    nq, nk = seq_len // tq, seq_len // tk
    steps = [(b, qi, ki) for b in range(n_seq) for qi in range(nq) for ki in range((qi * tq) // tk + 1)]
    b_tab, qi_tab, ki_tab = (jnp.asarray(np.array(col, np.int32)) for col in zip(*steps))
    kern = functools.partial(_mla_prompt_kernel, tq=tq, tk=tk)
    grid_spec = pltpu.PrefetchScalarGridSpec(
        num_scalar_prefetch=3,
        grid=(len(steps),),
        in_specs=[pl.BlockSpec((MLA_HEADS, tq, dqk), lambda s, bt, qt, kt: (0, bt[s] * nq + qt[s], 0)),
                  pl.BlockSpec((MLA_HEADS, tk, dqk), lambda s, bt, qt, kt: (0, bt[s] * nk + kt[s], 0)),
                  pl.BlockSpec((MLA_HEADS, tk, MLA_V), lambda s, bt, qt, kt: (0, bt[s] * nk + kt[s], 0))],
        out_specs=pl.BlockSpec((tq, MLA_HEADS * MLA_V), lambda s, bt, qt, kt: (bt[s] * nq + qt[s], 0)),
        scratch_shapes=[pltpu.VMEM((MLA_HEADS, tq, LANES), F32), pltpu.VMEM((MLA_HEADS, tq, LANES), F32),
                        pltpu.VMEM((MLA_HEADS, tq, MLA_V), F32)])
    return pl.pallas_call(
        kern,
        grid_spec=grid_spec,
        out_shape=jax.ShapeDtypeStruct((q3.shape[1], MLA_HEADS * MLA_V), BF16),
        compiler_params=_params(1),
        name="mla_prompt_attn",
    )(b_tab, qi_tab, ki_tab, q3, k3, v3)


def _mla_sample_kernel(pt_ref, q_ref, knew_ref, *refs, n_groups, seq_len):
    kv_refs = refs[:PAGES_PER_STEP]
    krt_refs = refs[PAGES_PER_STEP:2 * PAGES_PER_STEP]
    o_ref, kv_sc, krt_sc, m_sc, l_sc, acc_sc = refs[2 * PAGES_PER_STEP:]
    g = pl.program_id(1)
    r = MLA_KV_LORA
    nt = (((1,), (1,)), ((), ()))

    @pl.when(g == 0)
    def _():
        _flash_init(m_sc, l_sc, acc_sc)

    q = q_ref[...]
    n_part = 4
    part_pages = PAGES_PER_STEP // n_part
    part_keys = part_pages * PAGE_SIZE
    logits = []
    for c in range(n_part):
        for i in range(c * part_pages, (c + 1) * part_pages):
            kv_sc[i * PAGE_SIZE:(i + 1) * PAGE_SIZE, :] = kv_refs[i][...].astype(BF16)
            krt_sc[:, i * PAGE_SIZE:(i + 1) * PAGE_SIZE] = krt_refs[i][...].astype(BF16)
        keys = slice(c * part_keys, (c + 1) * part_keys)
        logits.append(lax.dot_general(q[:, 0:r], kv_sc[keys, :], nt, preferred_element_type=F32)
                      + jnp.dot(q[:, r:r + MLA_ROPE], krt_sc[:, keys], preferred_element_type=F32))
    _flash_update(jnp.concatenate(logits, axis=1), kv_sc[...], m_sc, l_sc, acc_sc)

    @pl.when(g == n_groups - 1)
    def _():
        knew = knew_ref[...]
        n_new = knew.shape[0]
        s_new = lax.dot_general(q, knew, nt, preferred_element_type=F32)
        rows = s_new.shape[0]
        q_t = _shr(lax.broadcasted_iota(jnp.int32, (rows, n_new), 0), MLA_HEADS)
        k_t = lax.broadcasted_iota(jnp.int32, (rows, n_new), 1)
        s_new = jnp.where((k_t <= q_t) & (k_t < seq_len), s_new, NEG_INF)
        m_old = m_sc[...]
        m_fin = jnp.maximum(m_old, jnp.max(s_new, axis=1, keepdims=True))
        a_fin = jnp.exp(m_old - m_fin)
        p_new = jnp.exp(s_new - m_fin[:, 0:n_new])
        l_sc[...] = a_fin * l_sc[...] + jnp.sum(p_new, axis=1, keepdims=True)
        acc_sc[...] = acc_sc[...] * _lane_tile(a_fin, r // LANES) + jnp.dot(
            p_new.astype(BF16), knew[:, 0:r], preferred_element_type=F32)
        o_ref[...] = _flash_result(l_sc, acc_sc).astype(o_ref.dtype)


def _mla_sample_attn(q2d, knew, cache_kv, cache_krt, page_table, layer_j, seq_len):
    b, n_pages = page_table.shape
    n_groups = n_pages // PAGES_PER_STEP
    rows = seq_len * MLA_HEADS
    keys = PAGES_PER_STEP * PAGE_SIZE
    pt_flat = page_table.reshape(-1)

    def page_spec(i, shape):
        return pl.BlockSpec(
            (None, None) + shape,
            lambda bi, gi, pt: (layer_j, pt[bi * n_pages + gi * PAGES_PER_STEP + i], 0, 0))

    kern = functools.partial(_mla_sample_kernel, n_groups=n_groups, seq_len=seq_len)
    grid_spec = pltpu.PrefetchScalarGridSpec(
        num_scalar_prefetch=1,
        grid=(b, n_groups),
        in_specs=([pl.BlockSpec((rows, MLA_QK), lambda bi, gi, pt: (bi, 0)),
                   pl.BlockSpec((None, knew.shape[1], MLA_QK), lambda bi, gi, pt: (bi, 0, 0))]
                  + [page_spec(i, (PAGE_SIZE, MLA_KV_LORA)) for i in range(PAGES_PER_STEP)]
                  + [page_spec(i, (MLA_ROPE, PAGE_SIZE)) for i in range(PAGES_PER_STEP)]),
        out_specs=pl.BlockSpec((rows, MLA_KV_LORA), lambda bi, gi, pt: (bi, 0)),
        scratch_shapes=[pltpu.VMEM((keys, MLA_KV_LORA), BF16), pltpu.VMEM((MLA_ROPE, keys), BF16),
                        pltpu.VMEM((rows, LANES), F32), pltpu.VMEM((rows, LANES), F32),
                        pltpu.VMEM((rows, MLA_KV_LORA), F32)])
    return pl.pallas_call(
        kern,
        grid_spec=grid_spec,
        out_shape=jax.ShapeDtypeStruct((q2d.shape[0], MLA_KV_LORA), BF16),
        compiler_params=_params(2),
        name="mla_sample_attn",
    )(pt_flat, q2d, knew, *([cache_kv] * PAGES_PER_STEP), *([cache_krt] * PAGES_PER_STEP))


def _mla_out_kernel(ol_ref, wuv_ref, wo_ref, x_ref, gt_ref, gpost_ref, o_ref, o_sc, out_sc, stat_sc, *, grp,
                    head_major):
    r = MLA_KV_LORA
    for h in range(MLA_HEADS):
        o_lat = ol_ref[h] if head_major else ol_ref[:, h * r:(h + 1) * r]
        o_h = jnp.dot(o_lat, wuv_ref[h], preferred_element_type=F32)
        o_sc[:, h * MLA_V:(h + 1) * MLA_V] = o_h.astype(BF16)
    out_sc[...] = jnp.dot(o_sc[...], wo_ref[...], preferred_element_type=F32)
    _ada_out_rows(x_ref, out_sc, gpost_ref, gt_ref, grp.row(), 1.0, stat_sc, o_ref)


def _mla_out(o_lat, w_uv, w_o, x, grp, layer, g_post, tm=256):
    t, d = x.shape
    head_major = o_lat.ndim == 3
    sub = _Group(grp.mod, tm, None if grp.tiles_per_mod_row is None else grp.tiles_per_mod_row * grp.tm // tm)
    kern = functools.partial(_mla_out_kernel, grp=sub, head_major=head_major)
    if head_major:
        ol_spec = pl.BlockSpec((MLA_HEADS, tm, MLA_KV_LORA), lambda i: (0, i, 0))
    else:
        ol_spec = pl.BlockSpec((tm, o_lat.shape[1]), lambda i: (i, 0))
    return pl.pallas_call(
        kern,
        grid=(t // tm,),
        in_specs=[ol_spec,
                  pl.BlockSpec(w_uv.shape, lambda i: (0, 0, 0)),
                  pl.BlockSpec(w_o.shape, lambda i: (0, 0)),
                  pl.BlockSpec((tm, d), lambda i: (i, 0)),
                  sub.mod_spec(layer, 3 + 2),
                  pl.BlockSpec((1, d), lambda i: (0, 0))],
        out_specs=pl.BlockSpec((tm, d), lambda i: (i, 0)),
        out_shape=jax.ShapeDtypeStruct((t, d), F32),
        scratch_shapes=[pltpu.VMEM((tm, MLA_HEADS * MLA_V), BF16), pltpu.VMEM((tm, d), F32),
                        pltpu.VMEM((tm, LANES), F32)],
        compiler_params=_params(1),
        name="mla_out",
    )(o_lat, w_uv, w_o, x, sub.mod, g_post.reshape(1, d))


def _t5_buckets(delta):
    n = np.maximum(delta, 0)
    max_exact = N_BUCKETS // 2
    log_ratio = np.log(np.maximum(n, 1).astype(np.float64) / max_exact) / math.log(BUCKET_MAX_DIST / max_exact)
    large = np.minimum(max_exact + (log_ratio * (N_BUCKETS - max_exact)).astype(np.int64), N_BUCKETS - 1)
    return np.where(n < max_exact, n, large).astype(np.int32)


def _swa_bias_table(rel_bias, delta, valid):
    lq, lk = delta.shape
    one_hot = (jnp.asarray(_t5_buckets(delta))[None] == jnp.arange(N_BUCKETS)[:, None, None]).astype(F32)
    bias = jnp.einsum("nh,nqk->hqk", rel_bias.astype(F32), one_hot, precision=lax.Precision.HIGHEST)
    bias = jnp.where(jnp.asarray(valid)[None], bias, NEG_INF)
    return bias.reshape(SWA_HEADS * lq, lk)


def _swa_sink_table(sinks, lq):
    return jnp.broadcast_to(jnp.repeat(sinks.astype(F32), lq)[:, None], (sinks.shape[0] * lq, LANES))


def _swa_q_perm():
    perm = np.zeros(SWA_HEADS * SWA_HEAD_DIM, np.int32)
    for m in range(SWA_KV_HEADS // 2):
        for g in range(SWA_GROUP):
            for p in range(2):
                src = ((2 * m + p) * SWA_GROUP + g) * SWA_HEAD_DIM
                dst = (4 * m + g) * LANES + p * SWA_HEAD_DIM
                perm[dst:dst + SWA_HEAD_DIM] = np.arange(src, src + SWA_HEAD_DIM)
    return perm


def _rope_tables(pos):
    half = MLA_ROPE // 2
    inv = ROPE_THETA ** (-jnp.arange(half, dtype=F32) / half)
    ang = pos.astype(F32)[:, None] * inv[None, :]
    cos, sin = jnp.cos(ang), jnp.sin(ang)
    zeros = jnp.zeros((pos.shape[0], LANES - MLA_ROPE), F32)
    return (jnp.concatenate([cos, cos, zeros], axis=1), jnp.concatenate([-sin, sin, zeros], axis=1))


def _pad_rope_cols(w):
    half = MLA_ROPE // 2
    z = jnp.zeros((w.shape[0], LANES - MLA_ROPE), w.dtype)
    return (jnp.concatenate([w, z], axis=1),
            jnp.concatenate([w[:, half:], w[:, :half], z], axis=1))


def kernel(x_prompt, x_sample, state_swa_k, state_swa_v, state_conv, cache_mla_kv, cache_mla_kr, page_table,
           c_prompt, c_sample, ada_w, ada_b, norm_pre, norm_post, ffn_w_in, ffn_w_out,
           gm_w_in, gm_ln_g, gm_ln_b, gm_w_s, gm_b_s, gm_w_out,
           swa_w_qkv, swa_w_o, swa_sinks, rel_bias,
           sc_w_in, sc_conv, sc_w_out,
           mla_w_qa, mla_qa_norm, mla_w_qb, mla_w_kva, mla_kva_norm, mla_w_kvb, mla_w_o):
    n_seq, seq_len, d = x_prompt.shape
    n_dec, dec_len, _ = x_sample.shape
    depth = ada_w.shape[0]
    past_len = page_table.shape[1] * PAGE_SIZE
    t_p, t_s = n_seq * seq_len, n_dec * dec_len

    c_all = jnp.concatenate([jnp.repeat(c_sample, dec_len, axis=0), c_prompt,
                             jnp.zeros((8 - n_seq, d), F32)], axis=0)
    mod = _ada(c_all, ada_w, ada_b)
    tm_p, tm_s = 512, 256
    mod_p = mod[:, t_s:t_s + 8]
    grp_p = _Group(mod_p, tm_p, seq_len // tm_p)
    grp_s = _Group(mod, tm_s, None)
    tm_ffn = 512
    grp_p_ffn = _Group(mod_p, tm_ffn, seq_len // tm_ffn)
    grp_s_ffn = _Group(mod, t_s, None, single_tile=True) if t_s <= tm_ffn else grp_s

    xs = [x_prompt.reshape(t_p, d), x_sample.reshape(t_s, d)]
    outs = {}
    ffn_w = (ffn_w_in[0, 0].astype(BF16), ffn_w_out[0, 0].astype(BF16))

    def ffn_pair(xs, ffn_w, layer, which):
        nxt = (layer, 1) if which == 0 else (layer + 1, 0)
        next_w = (ffn_w_in, ffn_w_out) + nxt if nxt[0] < depth else None
        sub = 2 * which
        res = _half_ffn(xs[0], grp_p_ffn, layer, which, norm_pre[layer, sub], norm_post[layer, sub], *ffn_w,
                        next_w=next_w)
        x_s = _half_ffn(xs[1], grp_s_ffn, layer, which, norm_pre[layer, sub], norm_post[layer, sub], *ffn_w)
        if next_w is None:
            return [res, x_s], None
        return [res[0], x_s], (res[1], res[2])

    for i in range(depth):
        kind, j = i % 4, i // 4
        xs, ffn_w = ffn_pair(xs, ffn_w, i, 0)

        if kind == 0:
            w_in = gm_w_in[j].astype(BF16)
            w_out = gm_w_out[j].astype(BF16)
            new = []
            for x, g, chunk, sl, cps in ((xs[0], grp_p, GM_CHUNK, GM_CHUNK, seq_len // GM_CHUNK),
                                         (xs[1], grp_s, GM_CHUNK, dec_len, 1)):
                lc = min(sl, GM_CHUNK)
                m = jnp.tile(gm_w_s[j][:, :lc, :lc], (1, chunk // lc, chunk // lc))
                bias = jnp.broadcast_to(jnp.tile(gm_b_s[j][:, :lc], (1, chunk // lc))[:, :, None],
                                        (GM_GROUPS, chunk, LANES))
                uv = _norm_mod_matmul(x, g, i, norm_pre[i, 1], w_in, act="gelu", out_dtype=BF16)
                mixed, st = _gm_core(uv, gm_ln_g[j], gm_ln_b[j], m, bias, chunk, lc, cps)
                new.append(_matmul_residual(mixed, w_out, x, g, i, norm_post[i, 1]))
                outs.setdefault("gm", []).append(st)
            xs = new

        elif kind == 1:
            perm = _swa_q_perm()
            nq = SWA_HEADS * SWA_HEAD_DIM
            nkv = SWA_KV_HEADS * SWA_HEAD_DIM
            w_qkv = jnp.concatenate([swa_w_qkv[j][:, :nq][:, perm], swa_w_qkv[j][:, nq:]], axis=1).astype(BF16)
            w_o = swa_w_o[j][perm, :].astype(BF16)
            w = SWA_WINDOW
            i_q, i_k = np.arange(w), np.arange(2 * w)
            delta = w + i_q[:, None] - i_k[None, :]
            bias_p = _swa_bias_table(rel_bias, delta, (delta >= 0) & (delta < w))
            sink_p = _swa_sink_table(swa_sinks[j], w)
            qkv_p = _norm_mod_matmul(xs[0], grp_p, i, norm_pre[i, 1], w_qkv)
            o_p = _swa_prompt_core(qkv_p, bias_p, sink_p, seq_len)
            x_p = _matmul_residual(o_p, w_o, xs[0], grp_p, i, norm_post[i, 1])
            kv_p = qkv_p.reshape(n_seq, seq_len, -1)[:, seq_len - w:, nq:]
            outs.setdefault("swa_kp", []).append(kv_p[..., :nkv].reshape(n_seq, w, SWA_KV_HEADS, SWA_HEAD_DIM))
            outs.setdefault("swa_vp", []).append(kv_p[..., nkv:].reshape(n_seq, w, SWA_KV_HEADS, SWA_HEAD_DIM))
            lb = state_swa_k.shape[2]
            n_keys = lb + dec_len
            pad = (-n_keys) % 8
            i_q, i_k = np.arange(dec_len), np.arange(n_keys + pad)
            delta = lb + i_q[:, None] - i_k[None, :]
            valid = (delta >= 0) & (delta < w) & (i_k[None, :] < n_keys)
            bias_s = _swa_bias_table(rel_bias, delta, valid)
            sink_s = _swa_sink_table(swa_sinks[j], dec_len)
            qkv_s = _norm_mod_matmul(xs[1], grp_s, i, norm_pre[i, 1], w_qkv).reshape(n_dec, dec_len, -1)
            zpad = jnp.zeros((n_dec, pad, nkv), F32)
            kk = jnp.concatenate([state_swa_k[j].reshape(n_dec, lb, nkv), qkv_s[..., nq:nq + nkv], zpad], axis=1)
            vv = jnp.concatenate([state_swa_v[j].reshape(n_dec, lb, nkv), qkv_s[..., nq + nkv:], zpad], axis=1)
            q_s = qkv_s[..., :nq].reshape(n_dec, dec_len, SWA_KV_HEADS // 2, SWA_GROUP, LANES)
            q_s = q_s.transpose(0, 2, 3, 1, 4).reshape(n_dec, SWA_KV_HEADS // 2, SWA_GROUP * dec_len, LANES)
            o_s = _swa_sample_core(q_s, kk, vv, bias_s, sink_s)
            o_s = o_s.reshape(n_dec, SWA_KV_HEADS // 2, SWA_GROUP, dec_len, LANES).transpose(0, 3, 1, 2, 4)
            x_s = _matmul_residual(o_s.reshape(t_s, nq), w_o, xs[1], grp_s, i, norm_post[i, 1])
            outs.setdefault("swa_ks", []).append(kk[:, n_keys - lb:n_keys].reshape(n_dec, lb, SWA_KV_HEADS, SWA_HEAD_DIM))
            outs.setdefault("swa_vs", []).append(vv[:, n_keys - lb:n_keys].reshape(n_dec, lb, SWA_KV_HEADS, SWA_HEAD_DIM))
            xs = [x_p, x_s]

        elif kind == 2:
            w_in = sc_w_in[j].astype(BF16)
            w_out = sc_w_out[j].astype(BF16)
            c = sc_w_out.shape[1]
            g3_p = _norm_mod_matmul(xs[0], grp_p, i, norm_pre[i, 1], w_in, out_dtype=BF16)
            y_p, st_p = _conv_prompt_core(g3_p, sc_conv[j], seq_len)
            x_p = _matmul_residual(y_p, w_out, xs[0], grp_p, i, norm_post[i, 1])
            outs.setdefault("conv_p", []).append(st_p[:, 8 - (CONV_WIDTH - 1):])
            g3_s = _norm_mod_matmul(xs[1], grp_s, i, norm_pre[i, 1], w_in, out_dtype=BF16)
            y_s, st_s = _conv_sample_core(g3_s.reshape(n_dec, dec_len * 3 * c),
                                          state_conv[j].reshape(n_dec, (CONV_WIDTH - 1) * c), sc_conv[j], dec_len)
            x_s = _matmul_residual(y_s.reshape(t_s, c), w_out, xs[1], grp_s, i, norm_post[i, 1])
            outs.setdefault("conv_s", []).append(st_s.reshape(n_dec, CONV_WIDTH - 1, c))
            xs = [x_p, x_s]

        else:
            r = MLA_KV_LORA
            kr_pad, kr_sw = _pad_rope_cols(mla_w_kva[j][:, r:])
            w_p = jnp.concatenate([mla_w_qa[j], mla_w_kva[j][:, :r], kr_pad, kr_sw], axis=1).astype(BF16)
            w_qb = mla_w_qb[j].reshape(MLA_Q_LORA, MLA_HEADS, MLA_NOPE + MLA_ROPE)
            w_nope = w_qb[:, :, :MLA_NOPE].reshape(MLA_Q_LORA, -1).astype(BF16)
            rope_pairs = [_pad_rope_cols(w_qb[:, h, MLA_NOPE:]) for h in range(MLA_HEADS)]
            w_rope = jnp.concatenate([p[0] for p in rope_pairs], axis=1).astype(BF16)
            w_rope_sw = jnp.concatenate([p[1] for p in rope_pairs], axis=1).astype(BF16)
            w_kvb = mla_w_kvb[j].reshape(r, MLA_HEADS, MLA_NOPE + MLA_V)
            w_uk_t = w_kvb[:, :, :MLA_NOPE].transpose(1, 2, 0).astype(BF16)
            w_uv = w_kvb[:, :, MLA_NOPE:].transpose(1, 0, 2).astype(BF16)
            w_o = mla_w_o[j].astype(BF16)
            tm = 256
            cos_p, sin_p = _rope_tables(jnp.arange(seq_len))
            proj_p = _norm_mod_matmul(xs[0], grp_p, i, norm_pre[i, 1], w_p, tn=w_p.shape[1])
            q_p, ckv_p, kr_p, k_p, v_p = _mla_proj(proj_p, cos_p, sin_p, seq_len // tm, mla_qa_norm[j],
                                                   mla_kva_norm[j], w_nope, w_rope, w_rope_sw,
                                                   mla_w_kvb[j].astype(BF16), True, tm=tm)
            o_p = _mla_prompt_attn(q_p, k_p, v_p, n_seq, seq_len)
            x_p = _matmul_residual(o_p, w_o, xs[0], grp_p, i, norm_post[i, 1])
            outs.setdefault("mla_kvp", []).append(ckv_p.reshape(n_seq, seq_len, r))
            outs.setdefault("mla_krp", []).append(kr_p[:, :MLA_ROPE].reshape(n_seq, seq_len, MLA_ROPE))
            cos_s, sin_s = _rope_tables(past_len + jnp.arange(dec_len))
            cos_s, sin_s = jnp.tile(cos_s, (tm // dec_len, 1)), jnp.tile(sin_s, (tm // dec_len, 1))
            proj_s = _norm_mod_matmul(xs[1], grp_s, i, norm_pre[i, 1], w_p, tn=w_p.shape[1])
            q_s, ckv_s, kr_s, kcat_s = _mla_proj(proj_s, cos_s, sin_s, 1, mla_qa_norm[j],
                                                 mla_kva_norm[j], w_nope, w_rope, w_rope_sw, w_uk_t, False, tm=tm)
            knew = jnp.concatenate([kcat_s.reshape(n_dec, dec_len, MLA_QK),
                                    jnp.zeros((n_dec, 16 - dec_len, MLA_QK), BF16)], axis=1)
            ol_s = _mla_sample_attn(q_s.reshape(t_s * MLA_HEADS, MLA_QK), knew, cache_mla_kv,
                                    jnp.swapaxes(cache_mla_kr, 2, 3), page_table, j, dec_len)
            x_s = _mla_out(ol_s.reshape(t_s, MLA_HEADS * r), w_uv, w_o, xs[1], grp_s, i, norm_post[i, 1])
            outs.setdefault("mla_kvs", []).append(ckv_s.reshape(n_dec, dec_len, r))
            outs.setdefault("mla_krs", []).append(kr_s[:, :MLA_ROPE].reshape(n_dec, dec_len, MLA_ROPE))
            xs = [x_p, x_s]

        xs, ffn_w = ffn_pair(xs, ffn_w, i, 1)

    gm_p, gm_s = outs["gm"][0::2], outs["gm"][1::2]
    return (xs[0].reshape(n_seq, seq_len, d), xs[1].reshape(n_dec, dec_len, d),
            jnp.stack(gm_p),
            jnp.stack([s.reshape(n_dec, dec_len, -1) for s in gm_s]),
            jnp.stack(outs["swa_kp"]), jnp.stack(outs["swa_vp"]),
            jnp.stack(outs["swa_ks"]), jnp.stack(outs["swa_vs"]),
            jnp.stack(outs["conv_p"]), jnp.stack(outs["conv_s"]),
            jnp.stack(outs["mla_kvp"]), jnp.stack(outs["mla_krp"]),
            jnp.stack(outs["mla_kvs"]), jnp.stack(outs["mla_krs"]))
```

```python
import functools
import math

import numpy as np
import jax
import jax.numpy as jnp
from jax import lax
from jax.experimental import pallas as pl
from jax.experimental.pallas import tpu as pltpu

F32 = jnp.float32
BF16 = jnp.bfloat16

VMEM_LIMIT_BYTES = 56 * 1024 * 1024
LANES = 128

RMS_EPS = 1e-6
NEG_INF = -1e30
FFN_RES_W = 0.5

D_MODEL = 2048
GM_GROUPS = 8
GM_CHUNK = 128
SWA_WINDOW = 128
SWA_HEAD_DIM = 64
SWA_HEADS = 32
SWA_KV_HEADS = 8
SWA_GROUP = 4
SWA_SCALE = SWA_HEAD_DIM ** -0.5
N_BUCKETS = 32
BUCKET_MAX_DIST = 128
CONV_WIDTH = 3
MLA_HEADS = 16
MLA_Q_LORA = 512
MLA_KV_LORA = 512
MLA_NOPE = 128
MLA_ROPE = 64
MLA_V = 128
MLA_SCALE = (MLA_NOPE + MLA_ROPE) ** -0.5
MLA_QK = MLA_KV_LORA + LANES
ROPE_THETA = 10000.0
PAGE_SIZE = 128
PAGES_PER_STEP = 64


def _params(n_axes):
    return pltpu.CompilerParams(dimension_semantics=("arbitrary",) * n_axes,
                                vmem_limit_bytes=VMEM_LIMIT_BYTES)


def _rms(x, g):
    return x * lax.rsqrt(jnp.mean(x * x, axis=-1, keepdims=True) + RMS_EPS) * g


def _lane_tile(x, n):
    return x if n == 1 else jnp.concatenate([x] * n, axis=1)


class _Group:
    def __init__(self, mod, tm, tiles_per_mod_row, single_tile=False):
        self.mod = mod
        self.tm = tm
        self.tiles_per_mod_row = tiles_per_mod_row
        self.single_tile = single_tile

    def mod_spec(self, layer, col):
        if self.tiles_per_mod_row is None:
            mode = {"pipeline_mode": pl.Buffered(1)} if self.single_tile else {}
            return pl.BlockSpec((None, self.tm, D_MODEL), lambda i, *_: (layer, i, col), **mode)
        return pl.BlockSpec((None, 8, D_MODEL), lambda i, *_: (layer, 0, col))

    def row(self):
        if self.tiles_per_mod_row is None:
            return None
        return pl.program_id(0) // self.tiles_per_mod_row

    @staticmethod
    def read(ref, row):
        return ref[...] if row is None else ref[pl.ds(row, 1), :]


SUBLANES = 8
ROW_CHUNK = 16


def _sublane_tile(x):
    return jnp.concatenate([x] * (ROW_CHUNK // SUBLANES), axis=0)


def _for_row_chunks(n_rows, body, unroll):
    def step(c, carry):
        body(pl.ds(pl.multiple_of(c * ROW_CHUNK, ROW_CHUNK), ROW_CHUNK))
        return carry
    lax.fori_loop(0, n_rows // ROW_CHUNK, step, 0, unroll=unroll)


def _row_rms_scale(val_ref, stat_ref):
    width = val_ref.shape[1]

    def body(rs):
        parts = [val_ref[rs, i * LANES:(i + 1) * LANES] for i in range(width // LANES)]
        parts = [p * p for p in parts]
        while len(parts) > 1:
            parts = [a + b for a, b in zip(parts[0::2], parts[1::2])] + ([parts[-1]] if len(parts) % 2 else [])
        scale = lax.rsqrt(jnp.sum(parts[0], axis=-1, keepdims=True) * (1.0 / width) + RMS_EPS)
        stat_ref[rs, :] = jnp.broadcast_to(scale, (ROW_CHUNK, LANES))

    _for_row_chunks(val_ref.shape[0], body, unroll=True)


def _ada_in_rows(x_ref, gpre_ref, sc_ref, sh_ref, row, stat_ref, h_ref):
    n_rows, width = x_ref.shape
    n_tile = width // LANES
    _row_rms_scale(x_ref, stat_ref)

    def per_row(rs, bc_ref):
        xn = x_ref[rs, :] * _lane_tile(stat_ref[rs, :], n_tile)
        h_ref[rs, :] = (xn * (gpre_ref[...] * (1.0 + sc_ref[rs, :])) + sh_ref[rs, :]).astype(h_ref.dtype)

    def per_tile(rs, bc_ref):
        xn = x_ref[rs, :] * _lane_tile(stat_ref[rs, :], n_tile)
        h_ref[rs, :] = (xn * _sublane_tile(bc_ref[0]) + _sublane_tile(bc_ref[1])).astype(h_ref.dtype)

    def run(bc_ref):
        if row is not None:
            bc_ref[0] = jnp.broadcast_to(gpre_ref[...] * (1.0 + sc_ref[pl.ds(row, 1), :]), (SUBLANES, width))
            bc_ref[1] = jnp.broadcast_to(sh_ref[pl.ds(row, 1), :], (SUBLANES, width))
        body = per_row if row is None else per_tile
        _for_row_chunks(n_rows, lambda rs: body(rs, bc_ref), unroll=2)

    pl.run_scoped(run, pltpu.VMEM((2, SUBLANES, width), F32))


def _ada_out_rows(x_ref, val_ref, gpost_ref, gt_ref, row, res_w, stat_ref, o_ref):
    n_rows, width = x_ref.shape
    n_tile = width // LANES
    _row_rms_scale(val_ref, stat_ref)

    def per_row(rs, bc_ref):
        vn = val_ref[rs, :] * _lane_tile(stat_ref[rs, :], n_tile)
        o_ref[rs, :] = x_ref[rs, :] + vn * (gpost_ref[...] * (res_w * gt_ref[rs, :]))

    def per_tile(rs, bc_ref):
        vn = val_ref[rs, :] * _lane_tile(stat_ref[rs, :], n_tile)
        o_ref[rs, :] = x_ref[rs, :] + vn * _sublane_tile(bc_ref[0])

    def run(bc_ref):
        if row is not None:
            bc_ref[0] = jnp.broadcast_to(gpost_ref[...] * (res_w * gt_ref[pl.ds(row, 1), :]), (SUBLANES, width))
        body = per_row if row is None else per_tile
        _for_row_chunks(n_rows, lambda rs: body(rs, bc_ref), unroll=2)

    pl.run_scoped(run, pltpu.VMEM((1, SUBLANES, width), F32))


def _shr(x, divisor):
    shift = divisor.bit_length() - 1
    assert 1 << shift == divisor
    return lax.shift_right_logical(x, shift)


def _ada_kernel(c_ref, w_ref, b_ref, o_ref, cs_sc):
    @pl.when((pl.program_id(0) == 0) & (pl.program_id(1) == 0))
    def _():
        c = c_ref[...]
        cs_sc[...] = (c * jax.nn.sigmoid(c)).astype(BF16)

    o_ref[...] = jnp.dot(cs_sc[...], w_ref[...].astype(BF16),
                         preferred_element_type=F32) + b_ref[...]


def _ada(c_all, ada_w, ada_b, tn=1024):
    n_layers, d, n = ada_w.shape
    rows = c_all.shape[0]
    return pl.pallas_call(
        _ada_kernel,
        grid=(n_layers, n // tn),
        in_specs=[pl.BlockSpec((rows, d), lambda l, j: (0, 0)),
                  pl.BlockSpec((None, d, tn), lambda l, j: (l, 0, j)),
                  pl.BlockSpec((None, 1, tn), lambda l, j: (l, 0, j))],
        out_specs=pl.BlockSpec((None, rows, tn), lambda l, j: (l, 0, j)),
        out_shape=jax.ShapeDtypeStruct((n_layers, rows, n), F32),
        scratch_shapes=[pltpu.VMEM((rows, d), BF16)],
        compiler_params=_params(2),
        name="ada_modulation",
    )(c_all, ada_w, ada_b.reshape(n_layers, 1, n))


def _ffn_kernel(x_ref, sh_ref, sc_ref, gt_ref, gpre_ref, gpost_ref, wg_ref, wu_ref, wo_ref, *rest, grp, n_f, share):
    j = pl.program_id(1)
    if len(rest) == 7:
        nwi_ref, nwo_ref, o_ref, nwi_b_ref, nwo_b_ref, h_sc, stat_sc = rest

        def cast_block():
            nwi_b_ref[...] = nwi_ref[...].astype(BF16)
            nwo_b_ref[...] = nwo_ref[...].astype(BF16)

        if share == 1:
            cast_block()
        else:
            pl.when((pl.program_id(0) * n_f + j) % share == 0)(cast_block)
    else:
        o_ref, h_sc, stat_sc = rest
    row = grp.row()
    acc_sc = o_ref

    @pl.when(j == 0)
    def _():
        _ada_in_rows(x_ref, gpre_ref, sc_ref, sh_ref, row, stat_sc, h_sc)
        acc_sc[...] = jnp.zeros_like(acc_sc)

    h = h_sc[...]
    tf = wg_ref.shape[1]
    acts = []
    for c in range(2):
        cols = slice(c * tf // 2, (c + 1) * tf // 2)
        g = jnp.dot(h, wg_ref[:, cols], preferred_element_type=F32)
        u = jnp.dot(h, wu_ref[:, cols], preferred_element_type=F32)
        acts.append((g * jax.nn.sigmoid(g) * u).astype(BF16))
    acc_sc[...] += jnp.dot(jnp.concatenate(acts, axis=1), wo_ref[...], preferred_element_type=F32)

    @pl.when(j == n_f - 1)
    def _():
        _ada_out_rows(x_ref, acc_sc, gpost_ref, gt_ref, row, FFN_RES_W, stat_sc, o_ref)


def _half_ffn(x, grp, layer, which, g_pre, g_post, w_in, w_out, next_w=None, tf=512):
    t, d = x.shape
    f = w_out.shape[0]
    tm, n_f = grp.tm, f // tf
    n_i = t // tm
    sub = 2 * which
    share = 1
    kern = functools.partial(_ffn_kernel, grp=grp, n_f=n_f, share=share)
    in_specs = [pl.BlockSpec((tm, d), lambda i, j: (i, 0)),
                grp.mod_spec(layer, sub * 3 + 0),
                grp.mod_spec(layer, sub * 3 + 1),
                grp.mod_spec(layer, sub * 3 + 2),
                pl.BlockSpec((1, d), lambda i, j: (0, 0)),
                pl.BlockSpec((1, d), lambda i, j: (0, 0)),
                pl.BlockSpec((d, tf), lambda i, j: (0, j)),
                pl.BlockSpec((d, tf), lambda i, j: (0, n_f + j)),
                pl.BlockSpec((tf, d), lambda i, j: (j, 0))]
    args = [x, grp.mod, grp.mod, grp.mod, g_pre.reshape(1, d), g_post.reshape(1, d), w_in, w_in, w_out]
    out_specs = [pl.BlockSpec((tm, d), lambda i, j: (i, 0))]
    out_shape = [jax.ShapeDtypeStruct((t, d), F32)]
    if next_w is not None:
        nw_in, nw_out, nl, nwh = next_w
        assert n_i % share == 0
        bi = (share * d // n_i, 2 * f // n_f)
        bo = (f // n_f, share * d // n_i)

        def blk(i, j):
            s = (i * n_f + j) // share
            return s // n_f, s % n_f

        in_specs += [pl.BlockSpec((None, None) + bi, lambda i, j: (nl, nwh) + blk(i, j)),
                     pl.BlockSpec((None, None) + bo, lambda i, j: (nl, nwh) + blk(i, j)[::-1])]
        args += [nw_in, nw_out]
        out_specs += [pl.BlockSpec(bi, blk), pl.BlockSpec(bo, lambda i, j: blk(i, j)[::-1])]
        out_shape += [jax.ShapeDtypeStruct((d, 2 * f), BF16), jax.ShapeDtypeStruct((f, d), BF16)]
    res = pl.pallas_call(
        kern,
        grid=(n_i, n_f),
        in_specs=in_specs,
        out_specs=out_specs,
        out_shape=out_shape,
        scratch_shapes=[pltpu.VMEM((tm, d), BF16), pltpu.VMEM((tm, LANES), F32)],
        compiler_params=_params(2),
        name="half_ffn",
    )(*args)
    return res if next_w is not None else res[0]


def _nmm_kernel(x_ref, sh_ref, sc_ref, gpre_ref, w_ref, o_ref, h_sc, stat_sc, *, grp, act):
    row = grp.row()

    @pl.when(pl.program_id(1) == 0)
    def _():
        _ada_in_rows(x_ref, gpre_ref, sc_ref, sh_ref, row, stat_sc, h_sc)

    y = jnp.dot(h_sc[...], w_ref[...], preferred_element_type=F32)
    if act == "gelu":
        y = jax.nn.gelu(y, approximate=True)
    o_ref[...] = y.astype(o_ref.dtype)


def _norm_mod_matmul(x, grp, layer, g_pre, w, act=None, tn=None, out_dtype=F32):
    t, d = x.shape
    n = w.shape[1]
    if tn is None:
        tn = max(c for c in range(LANES, min(n, 2048) + 1, LANES) if n % c == 0)
    tm = grp.tm
    kern = functools.partial(_nmm_kernel, grp=grp, act=act)
    return pl.pallas_call(
        kern,
        grid=(t // tm, n // tn),
        in_specs=[pl.BlockSpec((tm, d), lambda i, j: (i, 0)),
                  grp.mod_spec(layer, 3 + 0),
                  grp.mod_spec(layer, 3 + 1),
                  pl.BlockSpec((1, d), lambda i, j: (0, 0)),
                  pl.BlockSpec((d, tn), lambda i, j: (0, j))],
        out_specs=pl.BlockSpec((tm, tn), lambda i, j: (i, j)),
        out_shape=jax.ShapeDtypeStruct((t, n), out_dtype),
        scratch_shapes=[pltpu.VMEM((tm, d), BF16), pltpu.VMEM((tm, LANES), F32)],
        compiler_params=_params(2),
        name="norm_mod_matmul",
    )(x, grp.mod, grp.mod, g_pre.reshape(1, d), w)


def _mres_kernel(a_ref, w_ref, x_ref, gt_ref, gpost_ref, o_ref, acc_sc, stat_sc, *, grp, n_k):
    k = pl.program_id(1)
    row = grp.row()

    @pl.when(k == 0)
    def _():
        acc_sc[...] = jnp.zeros_like(acc_sc)

    acc_sc[...] += jnp.dot(a_ref[...].astype(BF16), w_ref[...], preferred_element_type=F32)

    @pl.when(k == n_k - 1)
    def _():
        _ada_out_rows(x_ref, acc_sc, gpost_ref, gt_ref, row, 1.0, stat_sc, o_ref)


def _matmul_residual(a, w, x, grp, layer, g_post, tk=2048):
    t, kdim = a.shape
    d = w.shape[1]
    tm, n_k = grp.tm, kdim // tk
    kern = functools.partial(_mres_kernel, grp=grp, n_k=n_k)
    return pl.pallas_call(
        kern,
        grid=(t // tm, n_k),
        in_specs=[pl.BlockSpec((tm, tk), lambda i, k: (i, k)),
                  pl.BlockSpec((tk, d), lambda i, k: (k, 0)),
                  pl.BlockSpec((tm, d), lambda i, k: (i, 0)),
                  grp.mod_spec(layer, 3 + 2),
                  pl.BlockSpec((1, d), lambda i, k: (0, 0))],
        out_specs=pl.BlockSpec((tm, d), lambda i, k: (i, 0)),
        out_shape=jax.ShapeDtypeStruct((t, d), F32),
        scratch_shapes=[pltpu.VMEM((tm, d), F32), pltpu.VMEM((tm, LANES), F32)],
        compiler_params=_params(2),
        name="matmul_residual",
    )(a, w, x, grp.mod, g_post.reshape(1, d))


def _gm_kernel(u_ref, v_ref, lng_ref, lnb_ref, m_ref, b_ref, o_ref, st_ref, *, seq_len):
    v = v_ref[...].astype(F32)
    vc = v - jnp.mean(v, axis=-1, keepdims=True)
    vn = vc * lax.rsqrt(jnp.mean(vc * vc, axis=-1, keepdims=True) + RMS_EPS) * lng_ref[...] + lnb_ref[...]
    st_ref[...] = vn
    vnb = vn.astype(BF16)
    c = v.shape[0]
    gw = v.shape[1] // GM_GROUPS
    row = lax.broadcasted_iota(jnp.int32, (c, c), 0)
    col = lax.broadcasted_iota(jnp.int32, (c, c), 1)
    keep = (col <= row) & (_shr(row, seq_len) == _shr(col, seq_len))
    for g in range(GM_GROUPS):
        mg = jnp.where(keep, m_ref[g], 0.0).astype(BF16)
        mixed = jnp.dot(mg, vnb[:, g * gw:(g + 1) * gw], preferred_element_type=F32)
        mixed = mixed + _lane_tile(b_ref[g], gw // LANES)
        o_ref[:, g * gw:(g + 1) * gw] = (u_ref[:, g * gw:(g + 1) * gw].astype(F32) * mixed).astype(o_ref.dtype)


def _gm_core(uv, ln_g, ln_b, m, bias, chunk, seq_len, chunks_per_state):
    t = uv.shape[0]
    w = uv.shape[1] // 2
    n_chunks = t // chunk
    n_states = n_chunks // chunks_per_state
    kern = functools.partial(_gm_kernel, seq_len=seq_len)
    return pl.pallas_call(
        kern,
        grid=(n_chunks,),
        in_specs=[pl.BlockSpec((chunk, w), lambda c: (c, 0)),
                  pl.BlockSpec((chunk, w), lambda c: (c, 1)),
                  pl.BlockSpec((1, w), lambda c: (0, 0)),
                  pl.BlockSpec((1, w), lambda c: (0, 0)),
                  pl.BlockSpec((GM_GROUPS, chunk, chunk), lambda c: (0, 0, 0)),
                  pl.BlockSpec((GM_GROUPS, chunk, LANES), lambda c: (0, 0, 0))],
        out_specs=[pl.BlockSpec((chunk, w), lambda c: (c, 0)),
                   pl.BlockSpec((None, chunk, w), lambda c: (c // chunks_per_state, 0, 0))],
        out_shape=[jax.ShapeDtypeStruct((t, w), BF16),
                   jax.ShapeDtypeStruct((n_states, chunk, w), F32)],
        compiler_params=_params(1),
        name="gm_core",
    )(uv, uv, ln_g.reshape(1, w), ln_b.reshape(1, w), m, bias)


def _softmax_sink_parts(s, sink_b):
    mx = jnp.maximum(jnp.broadcast_to(jnp.max(s, axis=-1, keepdims=True), sink_b.shape), sink_b)
    n_keys = s.shape[1]
    mx_keys = _lane_tile(mx, n_keys // LANES) if n_keys % LANES == 0 else mx[:, 0:1]
    return jnp.exp(s - mx_keys).astype(BF16), jnp.exp(sink_b - mx)


def _half_lane_mask(rows, parity):
    lane = lax.broadcasted_iota(jnp.int32, (rows, LANES), 1)
    return (lane >= SWA_HEAD_DIM) if parity else (lane < SWA_HEAD_DIM)


def _swa_attend(q_blocks, k, v, bias_ref, sink_ref, masked_keys=None):
    rows = q_blocks[0].shape[0]
    n_keys = k.shape[0]
    logits, values = [], []
    for m in range(SWA_KV_HEADS // 2):
        k2 = k[:, m * LANES:(m + 1) * LANES]
        v2 = v[:, m * LANES:(m + 1) * LANES]
        for parity in range(2):
            keep = _half_lane_mask(n_keys, parity)
            kx = jnp.where(keep, k2, 0.0).astype(BF16)
            values.append(jnp.where(keep, v2, 0.0).astype(BF16))
            logits.append(lax.dot_general(q_blocks[m], kx, (((1,), (1,)), ((), ())), preferred_element_type=F32))
    s = jnp.concatenate(logits, axis=0) + bias_ref[...]
    if masked_keys is not None:
        key_col = lax.broadcasted_iota(jnp.int32, s.shape, 1)
        s = jnp.where(key_col < masked_keys, NEG_INF, s)
    p, p_sink = _softmax_sink_parts(s, sink_ref[...])
    denom = jnp.dot(p, jnp.ones((n_keys, LANES), BF16), preferred_element_type=F32) + p_sink
    outs = []
    for m in range(SWA_KV_HEADS // 2):
        o_m = None
        for parity in range(2):
            idx = 2 * m + parity
            sl = slice(idx * rows, (idx + 1) * rows)
            o_p = jnp.dot(p[sl], values[idx], preferred_element_type=F32) / denom[sl]
            o_m = o_p if o_m is None else o_m + o_p
        outs.append(o_m)
    return outs


def _swa_prompt_kernel(q_ref, kp_ref, ko_ref, vp_ref, vo_ref, bias_ref, sink_ref, o_ref, *, blocks_per_seq):
    w = SWA_WINDOW
    first = (pl.program_id(0) % blocks_per_seq) == 0
    q = (q_ref[...] * SWA_SCALE).astype(BF16)
    k = jnp.concatenate([kp_ref[...], ko_ref[...]], axis=0)
    v = jnp.concatenate([vp_ref[...], vo_ref[...]], axis=0)
    q_blocks = [jnp.concatenate([q[:, (4 * m + g) * LANES:(4 * m + g + 1) * LANES] for g in range(SWA_GROUP)],
                                axis=0) for m in range(SWA_KV_HEADS // 2)]
    outs = _swa_attend(q_blocks, k, v, bias_ref, sink_ref, masked_keys=jnp.where(first, w, 0))
    for m, o_m in enumerate(outs):
        for g in range(SWA_GROUP):
            o_ref[:, (4 * m + g) * LANES:(4 * m + g + 1) * LANES] = o_m[g * w:(g + 1) * w].astype(o_ref.dtype)


def _swa_prompt_core(qkv, bias, sink, seq_len):
    t = qkv.shape[0]
    w = SWA_WINDOW
    nq = SWA_HEADS * SWA_HEAD_DIM
    nkv = SWA_KV_HEADS * SWA_HEAD_DIM
    bps = seq_len // w
    kcol, vcol = nq // nkv, nq // nkv + 1

    def prev(i):
        return jnp.maximum(i - 1, 0)

    kern = functools.partial(_swa_prompt_kernel, blocks_per_seq=bps)
    return pl.pallas_call(
        kern,
        grid=(t // w,),
        in_specs=[pl.BlockSpec((w, nq), lambda i: (i, 0)),
                  pl.BlockSpec((w, nkv), lambda i: (prev(i), kcol)),
                  pl.BlockSpec((w, nkv), lambda i: (i, kcol)),
                  pl.BlockSpec((w, nkv), lambda i: (prev(i), vcol)),
                  pl.BlockSpec((w, nkv), lambda i: (i, vcol)),
                  pl.BlockSpec(bias.shape, lambda i: (0, 0)),
                  pl.BlockSpec(sink.shape, lambda i: (0, 0))],
        out_specs=pl.BlockSpec((w, nq), lambda i: (i, 0)),
        out_shape=jax.ShapeDtypeStruct((t, nq), BF16),
        compiler_params=_params(1),
        name="swa_prompt_core",
    )(qkv, qkv, qkv, qkv, qkv, bias, sink)


def _swa_sample_kernel(q_ref, k_ref, v_ref, bias_ref, sink_ref, o_ref, *, bb):
    def body(b, carry):
        q_blocks = [(q_ref[b, m] * SWA_SCALE).astype(BF16) for m in range(SWA_KV_HEADS // 2)]
        outs = _swa_attend(q_blocks, k_ref[b], v_ref[b], bias_ref, sink_ref)
        for m, o_m in enumerate(outs):
            o_ref[b, m] = o_m.astype(o_ref.dtype)
        return carry

    lax.fori_loop(0, bb, body, 0, unroll=2)


def _swa_sample_core(q, kk, vv, bias, sink, bb=8):
    b, n_pair, rows, _ = q.shape
    n_keys = kk.shape[1]
    nkv = kk.shape[2]
    kern = functools.partial(_swa_sample_kernel, bb=bb)
    return pl.pallas_call(
        kern,
        grid=(b // bb,),
        in_specs=[pl.BlockSpec((bb, n_pair, rows, LANES), lambda i: (i, 0, 0, 0)),
                  pl.BlockSpec((bb, n_keys, nkv), lambda i: (i, 0, 0)),
                  pl.BlockSpec((bb, n_keys, nkv), lambda i: (i, 0, 0)),
                  pl.BlockSpec(bias.shape, lambda i: (0, 0)),
                  pl.BlockSpec(sink.shape, lambda i: (0, 0))],
        out_specs=pl.BlockSpec((bb, n_pair, rows, LANES), lambda i: (i, 0, 0, 0)),
        out_shape=jax.ShapeDtypeStruct(q.shape, BF16),
        compiler_params=_params(1),
        name="swa_sample_core",
    )(q, kk, vv, bias, sink)


def _conv_prompt_kernel(gb_ref, gc_ref, z_ref, hc_ref, hz_ref, cw_ref, o_ref, st_ref, zp_sc, *, tiles_per_seq):
    tm = gb_ref.shape[0]
    hr = hc_ref.shape[0]
    first = (pl.program_id(0) % tiles_per_seq) == 0
    zz = gc_ref[...].astype(F32) * z_ref[...].astype(F32)
    halo = jnp.where(first, 0.0, hc_ref[...].astype(F32) * hz_ref[...].astype(F32))
    zp_sc[0:8, :] = halo[hr - 8:hr]
    zp_sc[8:8 + tm, :] = zz
    y = cw_ref[2:3, :] * zz + cw_ref[1:2, :] * zp_sc[7:7 + tm, :] + cw_ref[0:1, :] * zp_sc[6:6 + tm, :]
    o_ref[...] = (gb_ref[...].astype(F32) * y).astype(o_ref.dtype)
    st_ref[...] = zz[tm - 8:tm]


def _conv_prompt_core(g3, conv_w, seq_len, tm=256):
    t = g3.shape[0]
    c = g3.shape[1] // 3
    tps = seq_len // tm
    n_seq = t // seq_len

    hr = 16

    def halo(col):
        return pl.BlockSpec((hr, c), lambda i: (jnp.maximum(i * (tm // hr) - 1, 0), col))

    kern = functools.partial(_conv_prompt_kernel, tiles_per_seq=tps)
    return pl.pallas_call(
        kern,
        grid=(t // tm,),
        in_specs=[pl.BlockSpec((tm, c), lambda i: (i, 0)),
                  pl.BlockSpec((tm, c), lambda i: (i, 1)),
                  pl.BlockSpec((tm, c), lambda i: (i, 2)),
                  halo(1), halo(2),
                  pl.BlockSpec((CONV_WIDTH, c), lambda i: (0, 0))],
        out_specs=[pl.BlockSpec((tm, c), lambda i: (i, 0)),
                   pl.BlockSpec((None, 8, c), lambda i: (i // tps, 0, 0))],
        out_shape=[jax.ShapeDtypeStruct((t, c), BF16),
                   jax.ShapeDtypeStruct((n_seq, 8, c), F32)],
        scratch_shapes=[pltpu.VMEM((tm + 8, c), F32)],
        compiler_params=_params(1),
        name="conv_prompt_core",
    )(g3, g3, g3, g3, g3, conv_w)


def _conv_sample_kernel(g_ref, prev_ref, cw_ref, o_ref, st_ref, *, seq_len, c):
    zz = [prev_ref[:, 0:c], prev_ref[:, c:2 * c]]
    for t in range(seq_len):
        base = t * 3 * c
        zz.append(g_ref[:, base + c:base + 2 * c].astype(F32) * g_ref[:, base + 2 * c:base + 3 * c].astype(F32))
    for t in range(seq_len):
        y = cw_ref[2:3, :] * zz[t + 2] + cw_ref[1:2, :] * zz[t + 1] + cw_ref[0:1, :] * zz[t]
        o_ref[:, t * c:(t + 1) * c] = (g_ref[:, t * 3 * c:t * 3 * c + c].astype(F32) * y).astype(o_ref.dtype)
    st_ref[:, 0:c] = zz[seq_len]
    st_ref[:, c:2 * c] = zz[seq_len + 1]


def _conv_sample_core(g3, prev, conv_w, seq_len):
    b = g3.shape[0]
    c = g3.shape[1] // (3 * seq_len)
    kern = functools.partial(_conv_sample_kernel, seq_len=seq_len, c=c)
    return pl.pallas_call(
        kern,
        grid=(1,),
        in_specs=[pl.BlockSpec(g3.shape, lambda i: (0, 0)),
                  pl.BlockSpec(prev.shape, lambda i: (0, 0)),
                  pl.BlockSpec((CONV_WIDTH, c), lambda i: (0, 0))],
        out_specs=[pl.BlockSpec((b, seq_len * c), lambda i: (0, 0)),
                   pl.BlockSpec((b, 2 * c), lambda i: (0, 0))],
        out_shape=[jax.ShapeDtypeStruct((b, seq_len * c), BF16),
                   jax.ShapeDtypeStruct((b, 2 * c), F32)],
        compiler_params=_params(1),
        name="conv_sample_core",
    )(g3, prev, conv_w)


def _mla_proj_kernel(p_ref, cos_ref, sin_ref, qan_ref, kvn_ref, wn_ref, wr_ref, wrs_ref, wkv_ref, *out_refs, per_head):
    r = MLA_KV_LORA
    cos = cos_ref[...]
    sin = sin_ref[...]
    qa = _rms(p_ref[:, 0:MLA_Q_LORA], qan_ref[...]).astype(BF16)
    qn = jnp.dot(qa, wn_ref[...], preferred_element_type=F32)
    qr = jnp.dot(qa, wr_ref[...], preferred_element_type=F32)
    qrs = jnp.dot(qa, wrs_ref[...], preferred_element_type=F32)
    off = MLA_Q_LORA
    ckv = _rms(p_ref[:, off:off + r], kvn_ref[...])
    kr = p_ref[:, off + r:off + r + LANES] * cos + p_ref[:, off + r + LANES:off + r + 2 * LANES] * sin
    if per_head:
        q_ref, ckv_ref, kr_ref, k_ref, v_ref = out_refs
        kv_up = jnp.dot(ckv.astype(BF16), wkv_ref[...], preferred_element_type=F32)
        kr_b = kr.astype(k_ref.dtype)
    else:
        q_ref, ckv_ref, kr_ref, kcat_ref = out_refs
    for h in range(MLA_HEADS):
        sl = slice(h * LANES, (h + 1) * LANES)
        q_rope = ((qr[:, sl] * cos + qrs[:, sl] * sin) * MLA_SCALE).astype(q_ref.dtype)
        if per_head:
            q_ref[h, :, 0:MLA_NOPE] = (qn[:, sl] * MLA_SCALE).astype(q_ref.dtype)
            q_ref[h, :, MLA_NOPE:MLA_NOPE + LANES] = q_rope
            base = h * (MLA_NOPE + MLA_V)
            k_ref[h, :, 0:MLA_NOPE] = kv_up[:, base:base + MLA_NOPE].astype(k_ref.dtype)
            k_ref[h, :, MLA_NOPE:MLA_NOPE + LANES] = kr_b
            v_ref[h] = kv_up[:, base + MLA_NOPE:base + MLA_NOPE + MLA_V].astype(v_ref.dtype)
        else:
            q_lat = jnp.dot(qn[:, sl].astype(BF16), wkv_ref[h], preferred_element_type=F32)
            q_ref[:, h * MLA_QK:h * MLA_QK + r] = (q_lat * MLA_SCALE).astype(q_ref.dtype)
            q_ref[:, h * MLA_QK + r:(h + 1) * MLA_QK] = q_rope
    ckv_ref[...] = ckv
    kr_ref[...] = kr
    if not per_head:
        kcat_ref[:, 0:r] = ckv.astype(kcat_ref.dtype)
        kcat_ref[:, r:r + LANES] = kr.astype(kcat_ref.dtype)


def _mla_proj(p, cos, sin, pos_blocks, qa_norm, kva_norm, w_nope, w_rope, w_rope_sw, w_kv, per_head, tm=256):
    t = p.shape[0]
    hq = MLA_HEADS * MLA_QK
    dqk = MLA_NOPE + LANES
    const2 = lambda i: (0, 0)
    row_block = lambda width: pl.BlockSpec((tm, width), lambda i: (i, 0))
    head_block = lambda width: pl.BlockSpec((MLA_HEADS, tm, width), lambda i: (0, i, 0))
    out_specs = [None, row_block(MLA_KV_LORA), row_block(LANES)]
    out_shape = [None, jax.ShapeDtypeStruct((t, MLA_KV_LORA), F32), jax.ShapeDtypeStruct((t, LANES), F32)]
    if per_head:
        out_specs[0] = head_block(dqk)
        out_shape[0] = jax.ShapeDtypeStruct((MLA_HEADS, t, dqk), BF16)
        out_specs += [head_block(dqk), head_block(MLA_V)]
        out_shape += [jax.ShapeDtypeStruct((MLA_HEADS, t, dqk), BF16), jax.ShapeDtypeStruct((MLA_HEADS, t, MLA_V), BF16)]
    else:
        out_specs[0] = row_block(hq)
        out_shape[0] = jax.ShapeDtypeStruct((t, hq), BF16)
        out_specs += [row_block(MLA_QK)]
        out_shape += [jax.ShapeDtypeStruct((t, MLA_QK), BF16)]
    return pl.pallas_call(
        functools.partial(_mla_proj_kernel, per_head=per_head),
        grid=(t // tm,),
        in_specs=[pl.BlockSpec((tm, p.shape[1]), lambda i: (i, 0)),
                  pl.BlockSpec((tm, LANES), lambda i: (i % pos_blocks, 0)),
                  pl.BlockSpec((tm, LANES), lambda i: (i % pos_blocks, 0)),
                  pl.BlockSpec((1, MLA_Q_LORA), const2),
                  pl.BlockSpec((1, MLA_KV_LORA), const2),
                  pl.BlockSpec(w_nope.shape, const2),
                  pl.BlockSpec(w_rope.shape, const2),
                  pl.BlockSpec(w_rope_sw.shape, const2),
                  pl.BlockSpec(w_kv.shape, lambda i: (0,) * w_kv.ndim)],
        out_specs=out_specs,
        out_shape=out_shape,
        compiler_params=_params(1),
        name="mla_proj",
    )(p, cos, sin, qa_norm.reshape(1, -1), kva_norm.reshape(1, -1), w_nope, w_rope, w_rope_sw, w_kv)


def _flash_update(s, v, m_ref, l_ref, acc_ref):
    m_prev = m_ref[...]
    m_new = jnp.maximum(m_prev, jnp.max(s, axis=1, keepdims=True))
    alpha = jnp.exp(m_prev - m_new)
    p = jnp.exp(s - _lane_tile(m_new, s.shape[1] // LANES))
    l_ref[...] = alpha * l_ref[...] + jnp.sum(p, axis=1, keepdims=True)
    acc_ref[...] = acc_ref[...] * _lane_tile(alpha, acc_ref.shape[1] // LANES) + jnp.dot(
        p.astype(BF16), v, preferred_element_type=F32)
    m_ref[...] = m_new


def _flash_init(m_sc, l_sc, acc_sc):
    m_sc[...] = jnp.full_like(m_sc, NEG_INF)
    l_sc[...] = jnp.zeros_like(l_sc)
    acc_sc[...] = jnp.zeros_like(acc_sc)


def _flash_result(l_ref, acc_ref):
    return acc_ref[...] / _lane_tile(l_ref[...], acc_ref.shape[1] // LANES)


def _mla_prompt_kernel(b_tab, qi_tab, ki_tab, q_ref, k_ref, v_ref, o_ref, m_sc, l_sc, acc_sc, *, tq, tk):
    step = pl.program_id(0)
    qi = qi_tab[step]
    ki = ki_tab[step]
    last = ki == (qi * tq) // tk

    @pl.when(ki == 0)
    def _():
        _flash_init(m_sc, l_sc, acc_sc)

    def run(masked):
        if masked:
            q_pos = qi * tq + lax.broadcasted_iota(jnp.int32, (tq, tk), 0)
            k_pos = ki * tk + lax.broadcasted_iota(jnp.int32, (tq, tk), 1)
            visible = k_pos <= q_pos
        for h in range(MLA_HEADS):
            s = lax.dot_general(q_ref[h], k_ref[h], (((1,), (1,)), ((), ())), preferred_element_type=F32)
            if masked:
                s = jnp.where(visible, s, NEG_INF)
            _flash_update(s, v_ref[h], m_sc.at[h], l_sc.at[h], acc_sc.at[h])
            if masked:
                o_ref[:, h * MLA_V:(h + 1) * MLA_V] = _flash_result(l_sc.at[h], acc_sc.at[h]).astype(o_ref.dtype)

    pl.when(last)(lambda: run(True))
    pl.when(jnp.logical_not(last))(lambda: run(False))


def _mla_prompt_attn(q3, k3, v3, n_seq, seq_len, tq=512, tk=512):
    assert tk % tq == 0
    dqk = q3.shape[2]
    nq, nk = seq_len // tq, seq_len // tk
    steps = [(b, qi, ki) for b in range(n_seq) for qi in range(nq) for ki in range((qi * tq) // tk + 1)]
    b_tab, qi_tab, ki_tab = (jnp.asarray(np.array(col, np.int32)) for col in zip(*steps))
    kern = functools.partial(_mla_prompt_kernel, tq=tq, tk=tk)
    grid_spec = pltpu.PrefetchScalarGridSpec(
        num_scalar_prefetch=3,
        grid=(len(steps),),
        in_specs=[pl.BlockSpec((MLA_HEADS, tq, dqk), lambda s, bt, qt, kt: (0, bt[s] * nq + qt[s], 0)),
                  pl.BlockSpec((MLA_HEADS, tk, dqk), lambda s, bt, qt, kt: (0, bt[s] * nk + kt[s], 0)),
                  pl.BlockSpec((MLA_HEADS, tk, MLA_V), lambda s, bt, qt, kt: (0, bt[s] * nk + kt[s], 0))],
        out_specs=pl.BlockSpec((tq, MLA_HEADS * MLA_V), lambda s, bt, qt, kt: (bt[s] * nq + qt[s], 0)),
        scratch_shapes=[pltpu.VMEM((MLA_HEADS, tq, LANES), F32), pltpu.VMEM((MLA_HEADS, tq, LANES), F32),
                        pltpu.VMEM((MLA_HEADS, tq, MLA_V), F32)])
    return pl.pallas_call(
        kern,
        grid_spec=grid_spec,
        out_shape=jax.ShapeDtypeStruct((q3.shape[1], MLA_HEADS * MLA_V), BF16),
        compiler_params=_params(1),
        name="mla_prompt_attn",
    )(b_tab, qi_tab, ki_tab, q3, k3, v3)


def _mla_sample_kernel(pt_ref, q_ref, knew_ref, *refs, n_groups, seq_len):
    kv_refs = refs[:PAGES_PER_STEP]
    krt_refs = refs[PAGES_PER_STEP:2 * PAGES_PER_STEP]
    o_ref, kv_sc, krt_sc, m_sc, l_sc, acc_sc = refs[2 * PAGES_PER_STEP:]
    g = pl.program_id(1)
    r = MLA_KV_LORA
    nt = (((1,), (1,)), ((), ()))

    @pl.when(g == 0)
    def _():
        _flash_init(m_sc, l_sc, acc_sc)

    q = q_ref[...]
    n_part = 4
    part_pages = PAGES_PER_STEP // n_part
    part_keys = part_pages * PAGE_SIZE
    logits = []
    for c in range(n_part):
        for i in range(c * part_pages, (c + 1) * part_pages):
            kv_sc[i * PAGE_SIZE:(i + 1) * PAGE_SIZE, :] = kv_refs[i][...].astype(BF16)
            krt_sc[:, i * PAGE_SIZE:(i + 1) * PAGE_SIZE] = krt_refs[i][...].astype(BF16)
        keys = slice(c * part_keys, (c + 1) * part_keys)
        logits.append(lax.dot_general(q[:, 0:r], kv_sc[keys, :], nt, preferred_element_type=F32)
                      + jnp.dot(q[:, r:r + MLA_ROPE], krt_sc[:, keys], preferred_element_type=F32))
    _flash_update(jnp.concatenate(logits, axis=1), kv_sc[...], m_sc, l_sc, acc_sc)

    @pl.when(g == n_groups - 1)
    def _():
        knew = knew_ref[...]
        n_new = knew.shape[0]
        s_new = lax.dot_general(q, knew, nt, preferred_element_type=F32)
        rows = s_new.shape[0]
        q_t = _shr(lax.broadcasted_iota(jnp.int32, (rows, n_new), 0), MLA_HEADS)
        k_t = lax.broadcasted_iota(jnp.int32, (rows, n_new), 1)
        s_new = jnp.where((k_t <= q_t) & (k_t < seq_len), s_new, NEG_INF)
        m_old = m_sc[...]
        m_fin = jnp.maximum(m_old, jnp.max(s_new, axis=1, keepdims=True))
        a_fin = jnp.exp(m_old - m_fin)
        p_new = jnp.exp(s_new - m_fin[:, 0:n_new])
        l_sc[...] = a_fin * l_sc[...] + jnp.sum(p_new, axis=1, keepdims=True)
        acc_sc[...] = acc_sc[...] * _lane_tile(a_fin, r // LANES) + jnp.dot(
            p_new.astype(BF16), knew[:, 0:r], preferred_element_type=F32)
        o_ref[...] = _flash_result(l_sc, acc_sc).astype(o_ref.dtype)


def _mla_sample_attn(q2d, knew, cache_kv, cache_krt, page_table, layer_j, seq_len):
    b, n_pages = page_table.shape
    n_groups = n_pages // PAGES_PER_STEP
    rows = seq_len * MLA_HEADS
    keys = PAGES_PER_STEP * PAGE_SIZE
    pt_flat = page_table.reshape(-1)

    def page_spec(i, shape):
        return pl.BlockSpec(
            (None, None) + shape,
            lambda bi, gi, pt: (layer_j, pt[bi * n_pages + gi * PAGES_PER_STEP + i], 0, 0))

    kern = functools.partial(_mla_sample_kernel, n_groups=n_groups, seq_len=seq_len)
    grid_spec = pltpu.PrefetchScalarGridSpec(
        num_scalar_prefetch=1,
        grid=(b, n_groups),
        in_specs=([pl.BlockSpec((rows, MLA_QK), lambda bi, gi, pt: (bi, 0)),
                   pl.BlockSpec((None, knew.shape[1], MLA_QK), lambda bi, gi, pt: (bi, 0, 0))]
                  + [page_spec(i, (PAGE_SIZE, MLA_KV_LORA)) for i in range(PAGES_PER_STEP)]
                  + [page_spec(i, (MLA_ROPE, PAGE_SIZE)) for i in range(PAGES_PER_STEP)]),
        out_specs=pl.BlockSpec((rows, MLA_KV_LORA), lambda bi, gi, pt: (bi, 0)),
        scratch_shapes=[pltpu.VMEM((keys, MLA_KV_LORA), BF16), pltpu.VMEM((MLA_ROPE, keys), BF16),
                        pltpu.VMEM((rows, LANES), F32), pltpu.VMEM((rows, LANES), F32),
                        pltpu.VMEM((rows, MLA_KV_LORA), F32)])
    return pl.pallas_call(
        kern,
        grid_spec=grid_spec,
        out_shape=jax.ShapeDtypeStruct((q2d.shape[0], MLA_KV_LORA), BF16),
        compiler_params=_params(2),
        name="mla_sample_attn",
    )(pt_flat, q2d, knew, *([cache_kv] * PAGES_PER_STEP), *([cache_krt] * PAGES_PER_STEP))


def _mla_out_kernel(ol_ref, wuv_ref, wo_ref, x_ref, gt_ref, gpost_ref, o_ref, o_sc, out_sc, stat_sc, *, grp,
                    head_major):
    r = MLA_KV_LORA
    for h in range(MLA_HEADS):
        o_lat = ol_ref[h] if head_major else ol_ref[:, h * r:(h + 1) * r]
        o_h = jnp.dot(o_lat, wuv_ref[h], preferred_element_type=F32)
        o_sc[:, h * MLA_V:(h + 1) * MLA_V] = o_h.astype(BF16)
    out_sc[...] = jnp.dot(o_sc[...], wo_ref[...], preferred_element_type=F32)
    _ada_out_rows(x_ref, out_sc, gpost_ref, gt_ref, grp.row(), 1.0, stat_sc, o_ref)


def _mla_out(o_lat, w_uv, w_o, x, grp, layer, g_post, tm=256):
    t, d = x.shape
    head_major = o_lat.ndim == 3
    sub = _Group(grp.mod, tm, None if grp.tiles_per_mod_row is None else grp.tiles_per_mod_row * grp.tm // tm)
    kern = functools.partial(_mla_out_kernel, grp=sub, head_major=head_major)
    if head_major:
        ol_spec = pl.BlockSpec((MLA_HEADS, tm, MLA_KV_LORA), lambda i: (0, i, 0))
    else:
        ol_spec = pl.BlockSpec((tm, o_lat.shape[1]), lambda i: (i, 0))
    return pl.pallas_call(
        kern,
        grid=(t // tm,),
        in_specs=[ol_spec,
                  pl.BlockSpec(w_uv.shape, lambda i: (0, 0, 0)),
                  pl.BlockSpec(w_o.shape, lambda i: (0, 0)),
                  pl.BlockSpec((tm, d), lambda i: (i, 0)),
                  sub.mod_spec(layer, 3 + 2),
                  pl.BlockSpec((1, d), lambda i: (0, 0))],
        out_specs=pl.BlockSpec((tm, d), lambda i: (i, 0)),
        out_shape=jax.ShapeDtypeStruct((t, d), F32),
        scratch_shapes=[pltpu.VMEM((tm, MLA_HEADS * MLA_V), BF16), pltpu.VMEM((tm, d), F32),
                        pltpu.VMEM((tm, LANES), F32)],
        compiler_params=_params(1),
        name="mla_out",
    )(o_lat, w_uv, w_o, x, sub.mod, g_post.reshape(1, d))


def _t5_buckets(delta):
    n = np.maximum(delta, 0)
    max_exact = N_BUCKETS // 2
    log_ratio = np.log(np.maximum(n, 1).astype(np.float64) / max_exact) / math.log(BUCKET_MAX_DIST / max_exact)
    large = np.minimum(max_exact + (log_ratio * (N_BUCKETS - max_exact)).astype(np.int64), N_BUCKETS - 1)
    return np.where(n < max_exact, n, large).astype(np.int32)


def _swa_bias_table(rel_bias, delta, valid):
    lq, lk = delta.shape
    one_hot = (jnp.asarray(_t5_buckets(delta))[None] == jnp.arange(N_BUCKETS)[:, None, None]).astype(F32)
    bias = jnp.einsum("nh,nqk->hqk", rel_bias.astype(F32), one_hot, precision=lax.Precision.HIGHEST)
    bias = jnp.where(jnp.asarray(valid)[None], bias, NEG_INF)
    return bias.reshape(SWA_HEADS * lq, lk)


def _swa_sink_table(sinks, lq):
    return jnp.broadcast_to(jnp.repeat(sinks.astype(F32), lq)[:, None], (sinks.shape[0] * lq, LANES))


def _swa_q_perm():
    perm = np.zeros(SWA_HEADS * SWA_HEAD_DIM, np.int32)
    for m in range(SWA_KV_HEADS // 2):
        for g in range(SWA_GROUP):
            for p in range(2):
                src = ((2 * m + p) * SWA_GROUP + g) * SWA_HEAD_DIM
                dst = (4 * m + g) * LANES + p * SWA_HEAD_DIM
                perm[dst:dst + SWA_HEAD_DIM] = np.arange(src, src + SWA_HEAD_DIM)
    return perm


def _rope_tables(pos):
    half = MLA_ROPE // 2
    inv = ROPE_THETA ** (-jnp.arange(half, dtype=F32) / half)
    ang = pos.astype(F32)[:, None] * inv[None, :]
    cos, sin = jnp.cos(ang), jnp.sin(ang)
    zeros = jnp.zeros((pos.shape[0], LANES - MLA_ROPE), F32)
    return (jnp.concatenate([cos, cos, zeros], axis=1), jnp.concatenate([-sin, sin, zeros], axis=1))


def _pad_rope_cols(w):
    half = MLA_ROPE // 2
    z = jnp.zeros((w.shape[0], LANES - MLA_ROPE), w.dtype)
    return (jnp.concatenate([w, z], axis=1),
            jnp.concatenate([w[:, half:], w[:, :half], z], axis=1))


def kernel(x_prompt, x_sample, state_swa_k, state_swa_v, state_conv, cache_mla_kv, cache_mla_kr, page_table,
           c_prompt, c_sample, ada_w, ada_b, norm_pre, norm_post, ffn_w_in, ffn_w_out,
           gm_w_in, gm_ln_g, gm_ln_b, gm_w_s, gm_b_s, gm_w_out,
           swa_w_qkv, swa_w_o, swa_sinks, rel_bias,
           sc_w_in, sc_conv, sc_w_out,
           mla_w_qa, mla_qa_norm, mla_w_qb, mla_w_kva, mla_kva_norm, mla_w_kvb, mla_w_o):
    n_seq, seq_len, d = x_prompt.shape
    n_dec, dec_len, _ = x_sample.shape
    depth = ada_w.shape[0]
    past_len = page_table.shape[1] * PAGE_SIZE
    t_p, t_s = n_seq * seq_len, n_dec * dec_len

    c_all = jnp.concatenate([jnp.repeat(c_sample, dec_len, axis=0), c_prompt,
                             jnp.zeros((8 - n_seq, d), F32)], axis=0)
    mod = _ada(c_all, ada_w, ada_b)
    tm_p, tm_s = 512, 256
    mod_p = mod[:, t_s:t_s + 8]
    grp_p = _Group(mod_p, tm_p, seq_len // tm_p)
    grp_s = _Group(mod, t_s, None, single_tile=True) if t_s <= tm_p else _Group(mod, tm_s, None)
    tm_ffn = 512
    grp_p_ffn = _Group(mod_p, tm_ffn, seq_len // tm_ffn)
    grp_s_ffn = grp_s

    xs = [x_prompt.reshape(t_p, d), x_sample.reshape(t_s, d)]
    outs = {}
    ffn_w = (ffn_w_in[0, 0].astype(BF16), ffn_w_out[0, 0].astype(BF16))

    def ffn_pair(xs, ffn_w, layer, which):
        nxt = (layer, 1) if which == 0 else (layer + 1, 0)
        next_w = (ffn_w_in, ffn_w_out) + nxt if nxt[0] < depth else None
        sub = 2 * which
        res = _half_ffn(xs[0], grp_p_ffn, layer, which, norm_pre[layer, sub], norm_post[layer, sub], *ffn_w,
                        next_w=next_w)
        x_s = _half_ffn(xs[1], grp_s_ffn, layer, which, norm_pre[layer, sub], norm_post[layer, sub], *ffn_w)
        if next_w is None:
            return [res, x_s], None
        return [res[0], x_s], (res[1], res[2])

    for i in range(depth):
        kind, j = i % 4, i // 4
        xs, ffn_w = ffn_pair(xs, ffn_w, i, 0)

        if kind == 0:
            w_in = gm_w_in[j].astype(BF16)
            w_out = gm_w_out[j].astype(BF16)
            new = []
            for x, g, chunk, sl, cps in ((xs[0], grp_p, GM_CHUNK, GM_CHUNK, seq_len // GM_CHUNK),
                                         (xs[1], grp_s, GM_CHUNK, dec_len, 1)):
                lc = min(sl, GM_CHUNK)
                m = jnp.tile(gm_w_s[j][:, :lc, :lc], (1, chunk // lc, chunk // lc))
                bias = jnp.broadcast_to(jnp.tile(gm_b_s[j][:, :lc], (1, chunk // lc))[:, :, None],
                                        (GM_GROUPS, chunk, LANES))
                uv = _norm_mod_matmul(x, g, i, norm_pre[i, 1], w_in, act="gelu", out_dtype=BF16)
                mixed, st = _gm_core(uv, gm_ln_g[j], gm_ln_b[j], m, bias, chunk, lc, cps)
                new.append(_matmul_residual(mixed, w_out, x, g, i, norm_post[i, 1]))
                outs.setdefault("gm", []).append(st)
            xs = new

        elif kind == 1:
            perm = _swa_q_perm()
            nq = SWA_HEADS * SWA_HEAD_DIM
            nkv = SWA_KV_HEADS * SWA_HEAD_DIM
            w_qkv = jnp.concatenate([swa_w_qkv[j][:, :nq][:, perm], swa_w_qkv[j][:, nq:]], axis=1).astype(BF16)
            w_o = swa_w_o[j][perm, :].astype(BF16)
            w = SWA_WINDOW
            i_q, i_k = np.arange(w), np.arange(2 * w)
            delta = w + i_q[:, None] - i_k[None, :]
            bias_p = _swa_bias_table(rel_bias, delta, (delta >= 0) & (delta < w))
            sink_p = _swa_sink_table(swa_sinks[j], w)
            qkv_p = _norm_mod_matmul(xs[0], grp_p, i, norm_pre[i, 1], w_qkv)
            o_p = _swa_prompt_core(qkv_p, bias_p, sink_p, seq_len)
            x_p = _matmul_residual(o_p, w_o, xs[0], grp_p, i, norm_post[i, 1])
            kv_p = qkv_p.reshape(n_seq, seq_len, -1)[:, seq_len - w:, nq:]
            outs.setdefault("swa_kp", []).append(kv_p[..., :nkv].reshape(n_seq, w, SWA_KV_HEADS, SWA_HEAD_DIM))
            outs.setdefault("swa_vp", []).append(kv_p[..., nkv:].reshape(n_seq, w, SWA_KV_HEADS, SWA_HEAD_DIM))
            lb = state_swa_k.shape[2]
            n_keys = lb + dec_len
            pad = (-n_keys) % 8
            i_q, i_k = np.arange(dec_len), np.arange(n_keys + pad)
            delta = lb + i_q[:, None] - i_k[None, :]
            valid = (delta >= 0) & (delta < w) & (i_k[None, :] < n_keys)
            bias_s = _swa_bias_table(rel_bias, delta, valid)
            sink_s = _swa_sink_table(swa_sinks[j], dec_len)
            qkv_s = _norm_mod_matmul(xs[1], grp_s, i, norm_pre[i, 1], w_qkv).reshape(n_dec, dec_len, -1)
            zpad = jnp.zeros((n_dec, pad, nkv), F32)
            kk = jnp.concatenate([state_swa_k[j].reshape(n_dec, lb, nkv), qkv_s[..., nq:nq + nkv], zpad], axis=1)
            vv = jnp.concatenate([state_swa_v[j].reshape(n_dec, lb, nkv), qkv_s[..., nq + nkv:], zpad], axis=1)
            q_s = qkv_s[..., :nq].reshape(n_dec, dec_len, SWA_KV_HEADS // 2, SWA_GROUP, LANES)
            q_s = q_s.transpose(0, 2, 3, 1, 4).reshape(n_dec, SWA_KV_HEADS // 2, SWA_GROUP * dec_len, LANES)
            o_s = _swa_sample_core(q_s, kk, vv, bias_s, sink_s)
            o_s = o_s.reshape(n_dec, SWA_KV_HEADS // 2, SWA_GROUP, dec_len, LANES).transpose(0, 3, 1, 2, 4)
            x_s = _matmul_residual(o_s.reshape(t_s, nq), w_o, xs[1], grp_s, i, norm_post[i, 1])
            outs.setdefault("swa_ks", []).append(kk[:, n_keys - lb:n_keys].reshape(n_dec, lb, SWA_KV_HEADS, SWA_HEAD_DIM))
            outs.setdefault("swa_vs", []).append(vv[:, n_keys - lb:n_keys].reshape(n_dec, lb, SWA_KV_HEADS, SWA_HEAD_DIM))
            xs = [x_p, x_s]

        elif kind == 2:
            w_in = sc_w_in[j].astype(BF16)
            w_out = sc_w_out[j].astype(BF16)
            c = sc_w_out.shape[1]
            g3_p = _norm_mod_matmul(xs[0], grp_p, i, norm_pre[i, 1], w_in, out_dtype=BF16)
            y_p, st_p = _conv_prompt_core(g3_p, sc_conv[j], seq_len)
            x_p = _matmul_residual(y_p, w_out, xs[0], grp_p, i, norm_post[i, 1])
            outs.setdefault("conv_p", []).append(st_p[:, 8 - (CONV_WIDTH - 1):])
            g3_s = _norm_mod_matmul(xs[1], grp_s, i, norm_pre[i, 1], w_in, out_dtype=BF16)
            y_s, st_s = _conv_sample_core(g3_s.reshape(n_dec, dec_len * 3 * c),
                                          state_conv[j].reshape(n_dec, (CONV_WIDTH - 1) * c), sc_conv[j], dec_len)
            x_s = _matmul_residual(y_s.reshape(t_s, c), w_out, xs[1], grp_s, i, norm_post[i, 1])
            outs.setdefault("conv_s", []).append(st_s.reshape(n_dec, CONV_WIDTH - 1, c))
            xs = [x_p, x_s]

        else:
            r = MLA_KV_LORA
            kr_pad, kr_sw = _pad_rope_cols(mla_w_kva[j][:, r:])
            w_p = jnp.concatenate([mla_w_qa[j], mla_w_kva[j][:, :r], kr_pad, kr_sw], axis=1).astype(BF16)
            w_qb = mla_w_qb[j].reshape(MLA_Q_LORA, MLA_HEADS, MLA_NOPE + MLA_ROPE)
            w_nope = w_qb[:, :, :MLA_NOPE].reshape(MLA_Q_LORA, -1).astype(BF16)
            rope_pairs = [_pad_rope_cols(w_qb[:, h, MLA_NOPE:]) for h in range(MLA_HEADS)]
            w_rope = jnp.concatenate([p[0] for p in rope_pairs], axis=1).astype(BF16)
            w_rope_sw = jnp.concatenate([p[1] for p in rope_pairs], axis=1).astype(BF16)
            w_kvb = mla_w_kvb[j].reshape(r, MLA_HEADS, MLA_NOPE + MLA_V)
            w_uk_t = w_kvb[:, :, :MLA_NOPE].transpose(1, 2, 0).astype(BF16)
            w_uv = w_kvb[:, :, MLA_NOPE:].transpose(1, 0, 2).astype(BF16)
            w_o = mla_w_o[j].astype(BF16)
            tm = 256
            cos_p, sin_p = _rope_tables(jnp.arange(seq_len))
            proj_p = _norm_mod_matmul(xs[0], grp_p, i, norm_pre[i, 1], w_p, tn=w_p.shape[1])
            q_p, ckv_p, kr_p, k_p, v_p = _mla_proj(proj_p, cos_p, sin_p, seq_len // tm, mla_qa_norm[j],
                                                   mla_kva_norm[j], w_nope, w_rope, w_rope_sw,
                                                   mla_w_kvb[j].astype(BF16), True, tm=tm)
            o_p = _mla_prompt_attn(q_p, k_p, v_p, n_seq, seq_len)
            x_p = _matmul_residual(o_p, w_o, xs[0], grp_p, i, norm_post[i, 1])
            outs.setdefault("mla_kvp", []).append(ckv_p.reshape(n_seq, seq_len, r))
            outs.setdefault("mla_krp", []).append(kr_p[:, :MLA_ROPE].reshape(n_seq, seq_len, MLA_ROPE))
            cos_s, sin_s = _rope_tables(past_len + jnp.arange(dec_len))
            cos_s, sin_s = jnp.tile(cos_s, (tm // dec_len, 1)), jnp.tile(sin_s, (tm // dec_len, 1))
            proj_s = _norm_mod_matmul(xs[1], grp_s, i, norm_pre[i, 1], w_p, tn=w_p.shape[1])
            q_s, ckv_s, kr_s, kcat_s = _mla_proj(proj_s, cos_s, sin_s, 1, mla_qa_norm[j],
                                                 mla_kva_norm[j], w_nope, w_rope, w_rope_sw, w_uk_t, False, tm=tm)
            knew = jnp.concatenate([kcat_s.reshape(n_dec, dec_len, MLA_QK),
                                    jnp.zeros((n_dec, 16 - dec_len, MLA_QK), BF16)], axis=1)
            ol_s = _mla_sample_attn(q_s.reshape(t_s * MLA_HEADS, MLA_QK), knew, cache_mla_kv,
                                    jnp.swapaxes(cache_mla_kr, 2, 3), page_table, j, dec_len)
            x_s = _mla_out(ol_s.reshape(t_s, MLA_HEADS * r), w_uv, w_o, xs[1], grp_s, i, norm_post[i, 1])
            outs.setdefault("mla_kvs", []).append(ckv_s.reshape(n_dec, dec_len, r))
            outs.setdefault("mla_krs", []).append(kr_s[:, :MLA_ROPE].reshape(n_dec, dec_len, MLA_ROPE))
            xs = [x_p, x_s]

        xs, ffn_w = ffn_pair(xs, ffn_w, i, 1)

    gm_p, gm_s = outs["gm"][0::2], outs["gm"][1::2]
    return (xs[0].reshape(n_seq, seq_len, d), xs[1].reshape(n_dec, dec_len, d),
            jnp.stack(gm_p),
            jnp.stack([s.reshape(n_dec, dec_len, -1) for s in gm_s]),
            jnp.stack(outs["swa_kp"]), jnp.stack(outs["swa_vp"]),
            jnp.stack(outs["swa_ks"]), jnp.stack(outs["swa_vs"]),
            jnp.stack(outs["conv_p"]), jnp.stack(outs["conv_s"]),
            jnp.stack(outs["mla_kvp"]), jnp.stack(outs["mla_krp"]),
            jnp.stack(outs["mla_kvs"]), jnp.stack(outs["mla_krs"]))
```

```python
import functools
import math

import numpy as np
import jax
import jax.numpy as jnp
from jax import lax
from jax.experimental import pallas as pl
from jax.experimental.pallas import tpu as pltpu

F32 = jnp.float32
BF16 = jnp.bfloat16

VMEM_LIMIT_BYTES = 56 * 1024 * 1024
LANES = 128

RMS_EPS = 1e-6
NEG_INF = -1e30
FFN_RES_W = 0.5

D_MODEL = 2048
GM_GROUPS = 8
GM_CHUNK = 128
SWA_WINDOW = 128
SWA_HEAD_DIM = 64
SWA_HEADS = 32
SWA_KV_HEADS = 8
SWA_GROUP = 4
SWA_SCALE = SWA_HEAD_DIM ** -0.5
N_BUCKETS = 32
BUCKET_MAX_DIST = 128
CONV_WIDTH = 3
MLA_HEADS = 16
MLA_Q_LORA = 512
MLA_KV_LORA = 512
MLA_NOPE = 128
MLA_ROPE = 64
MLA_V = 128
MLA_SCALE = (MLA_NOPE + MLA_ROPE) ** -0.5
MLA_QK = MLA_KV_LORA + LANES
ROPE_THETA = 10000.0
PAGE_SIZE = 128
PAGES_PER_STEP = 64


def _params(n_axes):
    return pltpu.CompilerParams(dimension_semantics=("arbitrary",) * n_axes,
                                vmem_limit_bytes=VMEM_LIMIT_BYTES)


def _rms(x, g):
    return x * lax.rsqrt(jnp.mean(x * x, axis=-1, keepdims=True) + RMS_EPS) * g


def _lane_tile(x, n):
    return x if n == 1 else jnp.concatenate([x] * n, axis=1)


class _Group:
    def __init__(self, mod, tm, tiles_per_mod_row, single_tile=False):
        self.mod = mod
        self.tm = tm
        self.tiles_per_mod_row = tiles_per_mod_row
        self.single_tile = single_tile

    def mod_spec(self, layer, col):
        if self.tiles_per_mod_row is None:
            mode = {"pipeline_mode": pl.Buffered(1)} if self.single_tile else {}
            return pl.BlockSpec((None, self.tm, D_MODEL), lambda i, *_: (layer, i, col), **mode)
        return pl.BlockSpec((None, 8, D_MODEL), lambda i, *_: (layer, 0, col))

    def row(self):
        if self.tiles_per_mod_row is None:
            return None
        return pl.program_id(0) // self.tiles_per_mod_row


SUBLANES = 8
ROW_CHUNK = 16


def _sublane_tile(x):
    return jnp.concatenate([x] * (ROW_CHUNK // SUBLANES), axis=0)


def _for_row_chunks(n_rows, body, unroll):
    def step(c, carry):
        body(pl.ds(pl.multiple_of(c * ROW_CHUNK, ROW_CHUNK), ROW_CHUNK))
        return carry
    lax.fori_loop(0, n_rows // ROW_CHUNK, step, 0, unroll=unroll)


def _row_rms_scale(val_ref, stat_ref):
    width = val_ref.shape[1]

    def body(rs):
        parts = [val_ref[rs, i * LANES:(i + 1) * LANES] for i in range(width // LANES)]
        parts = [p * p for p in parts]
        while len(parts) > 1:
            parts = [a + b for a, b in zip(parts[0::2], parts[1::2])] + ([parts[-1]] if len(parts) % 2 else [])
        scale = lax.rsqrt(jnp.sum(parts[0], axis=-1, keepdims=True) * (1.0 / width) + RMS_EPS)
        stat_ref[rs, :] = jnp.broadcast_to(scale, (ROW_CHUNK, LANES))

    _for_row_chunks(val_ref.shape[0], body, unroll=True)


def _ada_in_rows(x_ref, gpre_ref, sc_ref, sh_ref, row, stat_ref, h_ref):
    n_rows, width = x_ref.shape
    n_tile = width // LANES
    _row_rms_scale(x_ref, stat_ref)

    def per_row(rs, bc_ref):
        xn = x_ref[rs, :] * _lane_tile(stat_ref[rs, :], n_tile)
        h_ref[rs, :] = (xn * (gpre_ref[...] * (1.0 + sc_ref[rs, :])) + sh_ref[rs, :]).astype(h_ref.dtype)

    def per_tile(rs, bc_ref):
        xn = x_ref[rs, :] * _lane_tile(stat_ref[rs, :], n_tile)
        h_ref[rs, :] = (xn * _sublane_tile(bc_ref[0]) + _sublane_tile(bc_ref[1])).astype(h_ref.dtype)

    def run(bc_ref):
        if row is not None:
            bc_ref[0] = jnp.broadcast_to(gpre_ref[...] * (1.0 + sc_ref[pl.ds(row, 1), :]), (SUBLANES, width))
            bc_ref[1] = jnp.broadcast_to(sh_ref[pl.ds(row, 1), :], (SUBLANES, width))
        body = per_row if row is None else per_tile
        _for_row_chunks(n_rows, lambda rs: body(rs, bc_ref), unroll=2)

    pl.run_scoped(run, pltpu.VMEM((2, SUBLANES, width), F32))


def _ada_out_rows(x_ref, val_ref, gpost_ref, gt_ref, row, res_w, stat_ref, o_ref):
    n_rows, width = x_ref.shape
    n_tile = width // LANES
    _row_rms_scale(val_ref, stat_ref)

    def per_row(rs, bc_ref):
        vn = val_ref[rs, :] * _lane_tile(stat_ref[rs, :], n_tile)
        o_ref[rs, :] = x_ref[rs, :] + vn * (gpost_ref[...] * (res_w * gt_ref[rs, :]))

    def per_tile(rs, bc_ref):
        vn = val_ref[rs, :] * _lane_tile(stat_ref[rs, :], n_tile)
        o_ref[rs, :] = x_ref[rs, :] + vn * _sublane_tile(bc_ref[0])

    def run(bc_ref):
        if row is not None:
            bc_ref[0] = jnp.broadcast_to(gpost_ref[...] * (res_w * gt_ref[pl.ds(row, 1), :]), (SUBLANES, width))
        body = per_row if row is None else per_tile
        _for_row_chunks(n_rows, lambda rs: body(rs, bc_ref), unroll=2)

    pl.run_scoped(run, pltpu.VMEM((1, SUBLANES, width), F32))


def _shr(x, divisor):
    shift = divisor.bit_length() - 1
    assert 1 << shift == divisor
    return lax.shift_right_logical(x, shift)


def _ada_kernel(c_ref, w_ref, b_ref, o_ref, cs_sc):
    @pl.when((pl.program_id(0) == 0) & (pl.program_id(1) == 0))
    def _():
        c = c_ref[...]
        cs_sc[...] = (c * jax.nn.sigmoid(c)).astype(BF16)

    o_ref[...] = jnp.dot(cs_sc[...], w_ref[...].astype(BF16),
                         preferred_element_type=F32) + b_ref[...]


def _ada(c_all, ada_w, ada_b, tn=1024):
    n_layers, d, n = ada_w.shape
    rows = c_all.shape[0]
    return pl.pallas_call(
        _ada_kernel,
        grid=(n_layers, n // tn),
        in_specs=[pl.BlockSpec((rows, d), lambda l, j: (0, 0)),
                  pl.BlockSpec((None, d, tn), lambda l, j: (l, 0, j)),
                  pl.BlockSpec((None, 1, tn), lambda l, j: (l, 0, j))],
        out_specs=pl.BlockSpec((None, rows, tn), lambda l, j: (l, 0, j)),
        out_shape=jax.ShapeDtypeStruct((n_layers, rows, n), F32),
        scratch_shapes=[pltpu.VMEM((rows, d), BF16)],
        compiler_params=_params(2),
        name="ada_modulation",
    )(c_all, ada_w, ada_b.reshape(n_layers, 1, n))


def _ffn_kernel(x_ref, sh_ref, sc_ref, gt_ref, gpre_ref, gpost_ref, wg_ref, wu_ref, wo_ref, *rest, grp, n_f):
    j = pl.program_id(1)
    if len(rest) == 7:
        nwi_ref, nwo_ref, o_ref, nwi_b_ref, nwo_b_ref, h_sc, stat_sc = rest
        nwi_b_ref[...] = nwi_ref[...].astype(BF16)
        nwo_b_ref[...] = nwo_ref[...].astype(BF16)
    else:
        o_ref, h_sc, stat_sc = rest
    row = grp.row()
    acc_sc = o_ref

    @pl.when(j == 0)
    def _():
        _ada_in_rows(x_ref, gpre_ref, sc_ref, sh_ref, row, stat_sc, h_sc)
        acc_sc[...] = jnp.zeros_like(acc_sc)

    h = h_sc[...]
    tf = wg_ref.shape[1]
    acts = []
    for c in range(2):
        cols = slice(c * tf // 2, (c + 1) * tf // 2)
        g = jnp.dot(h, wg_ref[:, cols], preferred_element_type=F32)
        u = jnp.dot(h, wu_ref[:, cols], preferred_element_type=F32)
        acts.append((g * jax.nn.sigmoid(g) * u).astype(BF16))
    acc_sc[...] += jnp.dot(jnp.concatenate(acts, axis=1), wo_ref[...], preferred_element_type=F32)

    @pl.when(j == n_f - 1)
    def _():
        _ada_out_rows(x_ref, acc_sc, gpost_ref, gt_ref, row, FFN_RES_W, stat_sc, o_ref)


def _half_ffn(x, grp, layer, which, g_pre, g_post, w_in, w_out, next_w=None, tf=512):
    t, d = x.shape
    f = w_out.shape[0]
    tm, n_f = grp.tm, f // tf
    n_i = t // tm
    sub = 2 * which
    kern = functools.partial(_ffn_kernel, grp=grp, n_f=n_f)
    in_specs = [pl.BlockSpec((tm, d), lambda i, j: (i, 0)),
                grp.mod_spec(layer, sub * 3 + 0),
                grp.mod_spec(layer, sub * 3 + 1),
                grp.mod_spec(layer, sub * 3 + 2),
                pl.BlockSpec((1, d), lambda i, j: (0, 0)),
                pl.BlockSpec((1, d), lambda i, j: (0, 0)),
                pl.BlockSpec((d, tf), lambda i, j: (0, j)),
                pl.BlockSpec((d, tf), lambda i, j: (0, n_f + j)),
                pl.BlockSpec((tf, d), lambda i, j: (j, 0))]
    args = [x, grp.mod, grp.mod, grp.mod, g_pre.reshape(1, d), g_post.reshape(1, d), w_in, w_in, w_out]
    out_specs = [pl.BlockSpec((tm, d), lambda i, j: (i, 0))]
    out_shape = [jax.ShapeDtypeStruct((t, d), F32)]
    if next_w is not None:
        nw_in, nw_out, nl, nwh = next_w
        bi = (d // n_i, 2 * f // n_f)
        bo = (f // n_f, d // n_i)
        in_specs += [pl.BlockSpec((None, None) + bi, lambda i, j: (nl, nwh, i, j)),
                     pl.BlockSpec((None, None) + bo, lambda i, j: (nl, nwh, j, i))]
        args += [nw_in, nw_out]
        out_specs += [pl.BlockSpec(bi, lambda i, j: (i, j)), pl.BlockSpec(bo, lambda i, j: (j, i))]
        out_shape += [jax.ShapeDtypeStruct((d, 2 * f), BF16), jax.ShapeDtypeStruct((f, d), BF16)]
    res = pl.pallas_call(
        kern,
        grid=(n_i, n_f),
        in_specs=in_specs,
        out_specs=out_specs,
        out_shape=out_shape,
        scratch_shapes=[pltpu.VMEM((tm, d), BF16), pltpu.VMEM((tm, LANES), F32)],
        compiler_params=_params(2),
        name="half_ffn",
    )(*args)
    return res if next_w is not None else res[0]


def _nmm_kernel(x_ref, sh_ref, sc_ref, gpre_ref, w_ref, o_ref, h_sc, stat_sc, *, grp, act):
    row = grp.row()

    @pl.when(pl.program_id(1) == 0)
    def _():
        _ada_in_rows(x_ref, gpre_ref, sc_ref, sh_ref, row, stat_sc, h_sc)

    y = jnp.dot(h_sc[...], w_ref[...], preferred_element_type=F32)
    if act == "gelu":
        y = jax.nn.gelu(y, approximate=True)
    o_ref[...] = y.astype(o_ref.dtype)


def _norm_mod_matmul(x, grp, layer, g_pre, w, act=None, tn=None, out_dtype=F32):
    t, d = x.shape
    n = w.shape[1]
    if tn is None:
        tn = max(c for c in range(LANES, min(n, 2048) + 1, LANES) if n % c == 0)
    tm = grp.tm
    kern = functools.partial(_nmm_kernel, grp=grp, act=act)
    return pl.pallas_call(
        kern,
        grid=(t // tm, n // tn),
        in_specs=[pl.BlockSpec((tm, d), lambda i, j: (i, 0)),
                  grp.mod_spec(layer, 3 + 0),
                  grp.mod_spec(layer, 3 + 1),
                  pl.BlockSpec((1, d), lambda i, j: (0, 0)),
                  pl.BlockSpec((d, tn), lambda i, j: (0, j))],
        out_specs=pl.BlockSpec((tm, tn), lambda i, j: (i, j)),
        out_shape=jax.ShapeDtypeStruct((t, n), out_dtype),
        scratch_shapes=[pltpu.VMEM((tm, d), BF16), pltpu.VMEM((tm, LANES), F32)],
        compiler_params=_params(2),
        name="norm_mod_matmul",
    )(x, grp.mod, grp.mod, g_pre.reshape(1, d), w)


def _mres_kernel(a_ref, w_ref, x_ref, gt_ref, gpost_ref, o_ref, acc_sc, stat_sc, *, grp, n_k):
    k = pl.program_id(1)
    row = grp.row()

    @pl.when(k == 0)
    def _():
        acc_sc[...] = jnp.zeros_like(acc_sc)

    acc_sc[...] += jnp.dot(a_ref[...].astype(BF16), w_ref[...], preferred_element_type=F32)

    @pl.when(k == n_k - 1)
    def _():
        _ada_out_rows(x_ref, acc_sc, gpost_ref, gt_ref, row, 1.0, stat_sc, o_ref)


def _matmul_residual(a, w, x, grp, layer, g_post, tk=2048):
    t, kdim = a.shape
    d = w.shape[1]
    tm, n_k = grp.tm, kdim // tk
    kern = functools.partial(_mres_kernel, grp=grp, n_k=n_k)
    return pl.pallas_call(
        kern,
        grid=(t // tm, n_k),
        in_specs=[pl.BlockSpec((tm, tk), lambda i, k: (i, k)),
                  pl.BlockSpec((tk, d), lambda i, k: (k, 0)),
                  pl.BlockSpec((tm, d), lambda i, k: (i, 0)),
                  grp.mod_spec(layer, 3 + 2),
                  pl.BlockSpec((1, d), lambda i, k: (0, 0))],
        out_specs=pl.BlockSpec((tm, d), lambda i, k: (i, 0)),
        out_shape=jax.ShapeDtypeStruct((t, d), F32),
        scratch_shapes=[pltpu.VMEM((tm, d), F32), pltpu.VMEM((tm, LANES), F32)],
        compiler_params=_params(2),
        name="matmul_residual",
    )(a, w, x, grp.mod, g_post.reshape(1, d))


def _gm_kernel(u_ref, v_ref, lng_ref, lnb_ref, m_ref, b_ref, o_ref, st_ref, *, seq_len):
    v = v_ref[...].astype(F32)
    vc = v - jnp.mean(v, axis=-1, keepdims=True)
    vn = vc * lax.rsqrt(jnp.mean(vc * vc, axis=-1, keepdims=True) + RMS_EPS) * lng_ref[...] + lnb_ref[...]
    st_ref[...] = vn
    vnb = vn.astype(BF16)
    c = v.shape[0]
    gw = v.shape[1] // GM_GROUPS
    row = lax.broadcasted_iota(jnp.int32, (c, c), 0)
    col = lax.broadcasted_iota(jnp.int32, (c, c), 1)
    keep = (col <= row) & (_shr(row, seq_len) == _shr(col, seq_len))
    for g in range(GM_GROUPS):
        mg = jnp.where(keep, m_ref[g], 0.0).astype(BF16)
        mixed = jnp.dot(mg, vnb[:, g * gw:(g + 1) * gw], preferred_element_type=F32)
        mixed = mixed + _lane_tile(b_ref[g], gw // LANES)
        o_ref[:, g * gw:(g + 1) * gw] = (u_ref[:, g * gw:(g + 1) * gw].astype(F32) * mixed).astype(o_ref.dtype)


def _gm_core(uv, ln_g, ln_b, m, bias, chunk, seq_len, chunks_per_state):
    t = uv.shape[0]
    w = uv.shape[1] // 2
    n_chunks = t // chunk
    n_states = n_chunks // chunks_per_state
    kern = functools.partial(_gm_kernel, seq_len=seq_len)
    return pl.pallas_call(
        kern,
        grid=(n_chunks,),
        in_specs=[pl.BlockSpec((chunk, w), lambda c: (c, 0)),
                  pl.BlockSpec((chunk, w), lambda c: (c, 1)),
                  pl.BlockSpec((1, w), lambda c: (0, 0)),
                  pl.BlockSpec((1, w), lambda c: (0, 0)),
                  pl.BlockSpec((GM_GROUPS, chunk, chunk), lambda c: (0, 0, 0)),
                  pl.BlockSpec((GM_GROUPS, chunk, LANES), lambda c: (0, 0, 0))],
        out_specs=[pl.BlockSpec((chunk, w), lambda c: (c, 0)),
                   pl.BlockSpec((None, chunk, w), lambda c: (c // chunks_per_state, 0, 0))],
        out_shape=[jax.ShapeDtypeStruct((t, w), BF16),
                   jax.ShapeDtypeStruct((n_states, chunk, w), F32)],
        compiler_params=_params(1),
        name="gm_core",
    )(uv, uv, ln_g.reshape(1, w), ln_b.reshape(1, w), m, bias)


def _softmax_sink_parts(s, sink_b):
    mx = jnp.maximum(jnp.broadcast_to(jnp.max(s, axis=-1, keepdims=True), sink_b.shape), sink_b)
    n_keys = s.shape[1]
    mx_keys = _lane_tile(mx, n_keys // LANES) if n_keys % LANES == 0 else mx[:, 0:1]
    return jnp.exp(s - mx_keys).astype(BF16), jnp.exp(sink_b - mx)


def _half_lane_mask(rows, parity):
    lane = lax.broadcasted_iota(jnp.int32, (rows, LANES), 1)
    return (lane >= SWA_HEAD_DIM) if parity else (lane < SWA_HEAD_DIM)


def _swa_attend(q_blocks, k, v, bias_ref, sink_ref, masked_keys=None):
    rows = q_blocks[0].shape[0]
    n_keys = k.shape[0]
    logits, values = [], []
    for m in range(SWA_KV_HEADS // 2):
        k2 = k[:, m * LANES:(m + 1) * LANES]
        v2 = v[:, m * LANES:(m + 1) * LANES]
        for parity in range(2):
            keep = _half_lane_mask(n_keys, parity)
            kx = jnp.where(keep, k2, 0.0).astype(BF16)
            values.append(jnp.where(keep, v2, 0.0).astype(BF16))
            logits.append(lax.dot_general(q_blocks[m], kx, (((1,), (1,)), ((), ())), preferred_element_type=F32))
    s = jnp.concatenate(logits, axis=0) + bias_ref[...]
    if masked_keys is not None:
        key_col = lax.broadcasted_iota(jnp.int32, s.shape, 1)
        s = jnp.where(key_col < masked_keys, NEG_INF, s)
    p, p_sink = _softmax_sink_parts(s, sink_ref[...])
    denom = jnp.dot(p, jnp.ones((n_keys, LANES), BF16), preferred_element_type=F32) + p_sink
    outs = []
    for m in range(SWA_KV_HEADS // 2):
        o_m = None
        for parity in range(2):
            idx = 2 * m + parity
            sl = slice(idx * rows, (idx + 1) * rows)
            o_p = jnp.dot(p[sl], values[idx], preferred_element_type=F32) / denom[sl]
            o_m = o_p if o_m is None else o_m + o_p
        outs.append(o_m)
    return outs


def _swa_prompt_kernel(q_ref, kp_ref, ko_ref, vp_ref, vo_ref, bias_ref, sink_ref, o_ref, *, blocks_per_seq):
    w = SWA_WINDOW
    first = (pl.program_id(0) % blocks_per_seq) == 0
    q = (q_ref[...] * SWA_SCALE).astype(BF16)
    k = jnp.concatenate([kp_ref[...], ko_ref[...]], axis=0)
    v = jnp.concatenate([vp_ref[...], vo_ref[...]], axis=0)
    q_blocks = [jnp.concatenate([q[:, (4 * m + g) * LANES:(4 * m + g + 1) * LANES] for g in range(SWA_GROUP)],
                                axis=0) for m in range(SWA_KV_HEADS // 2)]
    outs = _swa_attend(q_blocks, k, v, bias_ref, sink_ref, masked_keys=jnp.where(first, w, 0))
    for m, o_m in enumerate(outs):
        for g in range(SWA_GROUP):
            o_ref[:, (4 * m + g) * LANES:(4 * m + g + 1) * LANES] = o_m[g * w:(g + 1) * w].astype(o_ref.dtype)


def _swa_prompt_core(qkv, bias, sink, seq_len):
    t = qkv.shape[0]
    w = SWA_WINDOW
    nq = SWA_HEADS * SWA_HEAD_DIM
    nkv = SWA_KV_HEADS * SWA_HEAD_DIM
    bps = seq_len // w
    kcol, vcol = nq // nkv, nq // nkv + 1

    def prev(i):
        return jnp.maximum(i - 1, 0)

    kern = functools.partial(_swa_prompt_kernel, blocks_per_seq=bps)
    return pl.pallas_call(
        kern,
        grid=(t // w,),
        in_specs=[pl.BlockSpec((w, nq), lambda i: (i, 0)),
                  pl.BlockSpec((w, nkv), lambda i: (prev(i), kcol)),
                  pl.BlockSpec((w, nkv), lambda i: (i, kcol)),
                  pl.BlockSpec((w, nkv), lambda i: (prev(i), vcol)),
                  pl.BlockSpec((w, nkv), lambda i: (i, vcol)),
                  pl.BlockSpec(bias.shape, lambda i: (0, 0)),
                  pl.BlockSpec(sink.shape, lambda i: (0, 0))],
        out_specs=pl.BlockSpec((w, nq), lambda i: (i, 0)),
        out_shape=jax.ShapeDtypeStruct((t, nq), BF16),
        compiler_params=_params(1),
        name="swa_prompt_core",
    )(qkv, qkv, qkv, qkv, qkv, bias, sink)


def _swa_sample_kernel(q_ref, k_ref, v_ref, bias_ref, sink_ref, o_ref, *, bb):
    def body(b, carry):
        q_blocks = [(q_ref[b, m] * SWA_SCALE).astype(BF16) for m in range(SWA_KV_HEADS // 2)]
        outs = _swa_attend(q_blocks, k_ref[b], v_ref[b], bias_ref, sink_ref)
        for m, o_m in enumerate(outs):
            o_ref[b, m] = o_m.astype(o_ref.dtype)
        return carry

    lax.fori_loop(0, bb, body, 0, unroll=2)


def _swa_sample_core(q, kk, vv, bias, sink, bb=8):
    b, n_pair, rows, _ = q.shape
    n_keys = kk.shape[1]
    nkv = kk.shape[2]
    kern = functools.partial(_swa_sample_kernel, bb=bb)
    return pl.pallas_call(
        kern,
        grid=(b // bb,),
        in_specs=[pl.BlockSpec((bb, n_pair, rows, LANES), lambda i: (i, 0, 0, 0)),
                  pl.BlockSpec((bb, n_keys, nkv), lambda i: (i, 0, 0)),
                  pl.BlockSpec((bb, n_keys, nkv), lambda i: (i, 0, 0)),
                  pl.BlockSpec(bias.shape, lambda i: (0, 0)),
                  pl.BlockSpec(sink.shape, lambda i: (0, 0))],
        out_specs=pl.BlockSpec((bb, n_pair, rows, LANES), lambda i: (i, 0, 0, 0)),
        out_shape=jax.ShapeDtypeStruct(q.shape, BF16),
        compiler_params=_params(1),
        name="swa_sample_core",
    )(q, kk, vv, bias, sink)


def _conv_prompt_kernel(gb_ref, gc_ref, z_ref, hc_ref, hz_ref, cw_ref, o_ref, st_ref, zp_sc, *, tiles_per_seq):
    tm = gb_ref.shape[0]
    hr = hc_ref.shape[0]
    first = (pl.program_id(0) % tiles_per_seq) == 0
    zz = gc_ref[...].astype(F32) * z_ref[...].astype(F32)
    halo = jnp.where(first, 0.0, hc_ref[...].astype(F32) * hz_ref[...].astype(F32))
    zp_sc[0:8, :] = halo[hr - 8:hr]
    zp_sc[8:8 + tm, :] = zz
    y = cw_ref[2:3, :] * zz + cw_ref[1:2, :] * zp_sc[7:7 + tm, :] + cw_ref[0:1, :] * zp_sc[6:6 + tm, :]
    o_ref[...] = (gb_ref[...].astype(F32) * y).astype(o_ref.dtype)
    st_ref[...] = zz[tm - 8:tm]


def _conv_prompt_core(g3, conv_w, seq_len, tm=256):
    t = g3.shape[0]
    c = g3.shape[1] // 3
    tps = seq_len // tm
    n_seq = t // seq_len

    hr = 16

    def halo(col):
        return pl.BlockSpec((hr, c), lambda i: (jnp.maximum(i * (tm // hr) - 1, 0), col))

    kern = functools.partial(_conv_prompt_kernel, tiles_per_seq=tps)
    return pl.pallas_call(
        kern,
        grid=(t // tm,),
        in_specs=[pl.BlockSpec((tm, c), lambda i: (i, 0)),
                  pl.BlockSpec((tm, c), lambda i: (i, 1)),
                  pl.BlockSpec((tm, c), lambda i: (i, 2)),
                  halo(1), halo(2),
                  pl.BlockSpec((CONV_WIDTH, c), lambda i: (0, 0))],
        out_specs=[pl.BlockSpec((tm, c), lambda i: (i, 0)),
                   pl.BlockSpec((None, 8, c), lambda i: (i // tps, 0, 0))],
        out_shape=[jax.ShapeDtypeStruct((t, c), BF16),
                   jax.ShapeDtypeStruct((n_seq, 8, c), F32)],
        scratch_shapes=[pltpu.VMEM((tm + 8, c), F32)],
        compiler_params=_params(1),
        name="conv_prompt_core",
    )(g3, g3, g3, g3, g3, conv_w)


def _conv_sample_kernel(g_ref, prev_ref, cw_ref, o_ref, st_ref, *, seq_len, c):
    zz = [prev_ref[:, 0:c], prev_ref[:, c:2 * c]]
    for t in range(seq_len):
        base = t * 3 * c
        zz.append(g_ref[:, base + c:base + 2 * c].astype(F32) * g_ref[:, base + 2 * c:base + 3 * c].astype(F32))
    for t in range(seq_len):
        y = cw_ref[2:3, :] * zz[t + 2] + cw_ref[1:2, :] * zz[t + 1] + cw_ref[0:1, :] * zz[t]
        o_ref[:, t * c:(t + 1) * c] = (g_ref[:, t * 3 * c:t * 3 * c + c].astype(F32) * y).astype(o_ref.dtype)
    st_ref[:, 0:c] = zz[seq_len]
    st_ref[:, c:2 * c] = zz[seq_len + 1]


def _conv_sample_core(g3, prev, conv_w, seq_len):
    b = g3.shape[0]
    c = g3.shape[1] // (3 * seq_len)
    kern = functools.partial(_conv_sample_kernel, seq_len=seq_len, c=c)
    return pl.pallas_call(
        kern,
        grid=(1,),
        in_specs=[pl.BlockSpec(g3.shape, lambda i: (0, 0)),
                  pl.BlockSpec(prev.shape, lambda i: (0, 0)),
                  pl.BlockSpec((CONV_WIDTH, c), lambda i: (0, 0))],
        out_specs=[pl.BlockSpec((b, seq_len * c), lambda i: (0, 0)),
                   pl.BlockSpec((b, 2 * c), lambda i: (0, 0))],
        out_shape=[jax.ShapeDtypeStruct((b, seq_len * c), BF16),
                   jax.ShapeDtypeStruct((b, 2 * c), F32)],
        compiler_params=_params(1),
        name="conv_sample_core",
    )(g3, prev, conv_w)


def _mla_proj_kernel(p_ref, cos_ref, sin_ref, qan_ref, kvn_ref, wn_ref, wr_ref, wrs_ref, wkv_ref, *out_refs, per_head):
    r = MLA_KV_LORA
    cos = cos_ref[...]
    sin = sin_ref[...]
    qa = _rms(p_ref[:, 0:MLA_Q_LORA], qan_ref[...]).astype(BF16)
    qn = jnp.dot(qa, wn_ref[...], preferred_element_type=F32)
    qr = jnp.dot(qa, wr_ref[...], preferred_element_type=F32)
    qrs = jnp.dot(qa, wrs_ref[...], preferred_element_type=F32)
    off = MLA_Q_LORA
    ckv = _rms(p_ref[:, off:off + r], kvn_ref[...])
    kr = p_ref[:, off + r:off + r + LANES] * cos + p_ref[:, off + r + LANES:off + r + 2 * LANES] * sin
    if per_head:
        q_ref, ckv_ref, kr_ref, k_ref, v_ref = out_refs
        kv_up = jnp.dot(ckv.astype(BF16), wkv_ref[...], preferred_element_type=F32)
        kr_b = kr.astype(k_ref.dtype)
    else:
        q_ref, ckv_ref, kr_ref, kcat_ref = out_refs
    for h in range(MLA_HEADS):
        sl = slice(h * LANES, (h + 1) * LANES)
        q_rope = ((qr[:, sl] * cos + qrs[:, sl] * sin) * MLA_SCALE).astype(q_ref.dtype)
        if per_head:
            q_ref[h, :, 0:MLA_NOPE] = (qn[:, sl] * MLA_SCALE).astype(q_ref.dtype)
            q_ref[h, :, MLA_NOPE:MLA_NOPE + LANES] = q_rope
            base = h * (MLA_NOPE + MLA_V)
            k_ref[h, :, 0:MLA_NOPE] = kv_up[:, base:base + MLA_NOPE].astype(k_ref.dtype)
            k_ref[h, :, MLA_NOPE:MLA_NOPE + LANES] = kr_b
            v_ref[h] = kv_up[:, base + MLA_NOPE:base + MLA_NOPE + MLA_V].astype(v_ref.dtype)
        else:
            q_lat = jnp.dot(qn[:, sl].astype(BF16), wkv_ref[h], preferred_element_type=F32)
            q_ref[:, h * MLA_QK:h * MLA_QK + r] = (q_lat * MLA_SCALE).astype(q_ref.dtype)
            q_ref[:, h * MLA_QK + r:(h + 1) * MLA_QK] = q_rope
    ckv_ref[...] = ckv
    kr_ref[...] = kr
    if not per_head:
        kcat_ref[:, 0:r] = ckv.astype(kcat_ref.dtype)
        kcat_ref[:, r:r + LANES] = kr.astype(kcat_ref.dtype)


def _mla_proj(p, cos, sin, pos_blocks, qa_norm, kva_norm, w_nope, w_rope, w_rope_sw, w_kv, per_head, tm=256):
    t = p.shape[0]
    hq = MLA_HEADS * MLA_QK
    dqk = MLA_NOPE + LANES
    const2 = lambda i: (0, 0)
    row_block = lambda width: pl.BlockSpec((tm, width), lambda i: (i, 0))
    head_block = lambda width: pl.BlockSpec((MLA_HEADS, tm, width), lambda i: (0, i, 0))
    out_specs = [None, row_block(MLA_KV_LORA), row_block(LANES)]
    out_shape = [None, jax.ShapeDtypeStruct((t, MLA_KV_LORA), F32), jax.ShapeDtypeStruct((t, LANES), F32)]
    if per_head:
        out_specs[0] = head_block(dqk)
        out_shape[0] = jax.ShapeDtypeStruct((MLA_HEADS, t, dqk), BF16)
        out_specs += [head_block(dqk), head_block(MLA_V)]
        out_shape += [jax.ShapeDtypeStruct((MLA_HEADS, t, dqk), BF16), jax.ShapeDtypeStruct((MLA_HEADS, t, MLA_V), BF16)]
    else:
        out_specs[0] = row_block(hq)
        out_shape[0] = jax.ShapeDtypeStruct((t, hq), BF16)
        out_specs += [row_block(MLA_QK)]
        out_shape += [jax.ShapeDtypeStruct((t, MLA_QK), BF16)]
    return pl.pallas_call(
        functools.partial(_mla_proj_kernel, per_head=per_head),
        grid=(t // tm,),
        in_specs=[pl.BlockSpec((tm, p.shape[1]), lambda i: (i, 0)),
                  pl.BlockSpec((tm, LANES), lambda i: (i % pos_blocks, 0)),
                  pl.BlockSpec((tm, LANES), lambda i: (i % pos_blocks, 0)),
                  pl.BlockSpec((1, MLA_Q_LORA), const2),
                  pl.BlockSpec((1, MLA_KV_LORA), const2),
                  pl.BlockSpec(w_nope.shape, const2),
                  pl.BlockSpec(w_rope.shape, const2),
                  pl.BlockSpec(w_rope_sw.shape, const2),
                  pl.BlockSpec(w_kv.shape, lambda i: (0,) * w_kv.ndim)],
        out_specs=out_specs,
        out_shape=out_shape,
        compiler_params=_params(1),
        name="mla_proj",
    )(p, cos, sin, qa_norm.reshape(1, -1), kva_norm.reshape(1, -1), w_nope, w_rope, w_rope_sw, w_kv)


def _flash_update(s, v, m_ref, l_ref, acc_ref):
    m_prev = m_ref[...]
    m_new = jnp.maximum(m_prev, jnp.max(s, axis=1, keepdims=True))
    alpha = jnp.exp(m_prev - m_new)
    p = jnp.exp(s - _lane_tile(m_new, s.shape[1] // LANES))
    l_ref[...] = alpha * l_ref[...] + jnp.sum(p, axis=1, keepdims=True)
    acc_ref[...] = acc_ref[...] * _lane_tile(alpha, acc_ref.shape[1] // LANES) + jnp.dot(
        p.astype(BF16), v, preferred_element_type=F32)
    m_ref[...] = m_new


def _flash_init(m_sc, l_sc, acc_sc):
    m_sc[...] = jnp.full_like(m_sc, NEG_INF)
    l_sc[...] = jnp.zeros_like(l_sc)
    acc_sc[...] = jnp.zeros_like(acc_sc)


def _flash_result(l_ref, acc_ref):
    return acc_ref[...] / _lane_tile(l_ref[...], acc_ref.shape[1] // LANES)


def _mla_prompt_kernel(b_tab, qi_tab, ki_tab, q_ref, k_ref, v_ref, o_ref, m_sc, l_sc, acc_sc, *, tq, tk):
    step = pl.program_id(0)
    qi = qi_tab[step]
    ki = ki_tab[step]
    last = ki == (qi * tq) // tk

    @pl.when(ki == 0)
    def _():
        _flash_init(m_sc, l_sc, acc_sc)

    def run(masked):
        if masked:
            q_pos = qi * tq + lax.broadcasted_iota(jnp.int32, (tq, tk), 0)
            k_pos = ki * tk + lax.broadcasted_iota(jnp.int32, (tq, tk), 1)
            visible = k_pos <= q_pos
        for h in range(MLA_HEADS):
            s = lax.dot_general(q_ref[h], k_ref[h], (((1,), (1,)), ((), ())), preferred_element_type=F32)
            if masked:
                s = jnp.where(visible, s, NEG_INF)
            _flash_update(s, v_ref[h], m_sc.at[h], l_sc.at[h], acc_sc.at[h])
            if masked:
                o_ref[:, h * MLA_V:(h + 1) * MLA_V] = _flash_result(l_sc.at[h], acc_sc.at[h]).astype(o_ref.dtype)

    pl.when(last)(lambda: run(True))
    pl.when(jnp.logical_not(last))(lambda: run(False))


def _mla_prompt_attn(q3, k3, v3, n_seq, seq_len, tq=512, tk=512):
    assert tk % tq == 0
    dqk = q3.shape[2]
    nq, nk = seq_len // tq, seq_len // tk
    steps = [(b, qi, ki) for b in range(n_seq) for qi in range(nq) for ki in range((qi * tq) // tk + 1)]
    b_tab, qi_tab, ki_tab = (jnp.asarray(np.array(col, np.int32)) for col in zip(*steps))
    kern = functools.partial(_mla_prompt_kernel, tq=tq, tk=tk)
    grid_spec = pltpu.PrefetchScalarGridSpec(
        num_scalar_prefetch=3,
        grid=(len(steps),),
        in_specs=[pl.BlockSpec((MLA_HEADS, tq, dqk), lambda s, bt, qt, kt: (0, bt[s] * nq + qt[s], 0)),
                  pl.BlockSpec((MLA_HEADS, tk, dqk), lambda s, bt, qt, kt: (0, bt[s] * nk + kt[s], 0)),
                  pl.BlockSpec((MLA_HEADS, tk, MLA_V), lambda s, bt, qt, kt: (0, bt[s] * nk + kt[s], 0))],
        out_specs=pl.BlockSpec((tq, MLA_HEADS * MLA_V), lambda s, bt, qt, kt: (bt[s] * nq + qt[s], 0)),
        scratch_shapes=[pltpu.VMEM((MLA_HEADS, tq, LANES), F32), pltpu.VMEM((MLA_HEADS, tq, LANES), F32),
                        pltpu.VMEM((MLA_HEADS, tq, MLA_V), F32)])
    return pl.pallas_call(
        kern,
        grid_spec=grid_spec,
        out_shape=jax.ShapeDtypeStruct((q3.shape[1], MLA_HEADS * MLA_V), BF16),
        compiler_params=_params(1),
        name="mla_prompt_attn",
    )(b_tab, qi_tab, ki_tab, q3, k3, v3)


def _mla_sample_kernel(pt_ref, q_ref, knew_ref, *refs, n_groups, seq_len):
    kv_refs = refs[:PAGES_PER_STEP]
    krt_refs = refs[PAGES_PER_STEP:2 * PAGES_PER_STEP]
    o_ref, kv_sc, krt_sc, m_sc, l_sc, acc_sc = refs[2 * PAGES_PER_STEP:]
    g = pl.program_id(1)
    r = MLA_KV_LORA
    nt = (((1,), (1,)), ((), ()))

    @pl.when(g == 0)
    def _():
        _flash_init(m_sc, l_sc, acc_sc)

    for i in range(PAGES_PER_STEP):
        kv_sc[i * PAGE_SIZE:(i + 1) * PAGE_SIZE, :] = kv_refs[i][...].astype(BF16)
        krt_sc[:, i * PAGE_SIZE:(i + 1) * PAGE_SIZE] = krt_refs[i][...].astype(BF16)
    q = q_ref[...]
    kv = kv_sc[...]
    s = (lax.dot_general(q[:, 0:r], kv, nt, preferred_element_type=F32)
         + jnp.dot(q[:, r:r + MLA_ROPE], krt_sc[...], preferred_element_type=F32))
    _flash_update(s, kv, m_sc, l_sc, acc_sc)

    @pl.when(g == n_groups - 1)
    def _():
        knew = knew_ref[...]
        n_new = knew.shape[0]
        s_new = lax.dot_general(q, knew, nt, preferred_element_type=F32)
        rows = s_new.shape[0]
        q_t = _shr(lax.broadcasted_iota(jnp.int32, (rows, n_new), 0), MLA_HEADS)
        k_t = lax.broadcasted_iota(jnp.int32, (rows, n_new), 1)
        s_new = jnp.where((k_t <= q_t) & (k_t < seq_len), s_new, NEG_INF)
        m_old = m_sc[...]
        m_fin = jnp.maximum(m_old, jnp.max(s_new, axis=1, keepdims=True))
        a_fin = jnp.exp(m_old - m_fin)
        p_new = jnp.exp(s_new - m_fin[:, 0:n_new])
        l_sc[...] = a_fin * l_sc[...] + jnp.sum(p_new, axis=1, keepdims=True)
        acc_sc[...] = acc_sc[...] * _lane_tile(a_fin, r // LANES) + jnp.dot(
            p_new.astype(BF16), knew[:, 0:r], preferred_element_type=F32)
        o_ref[...] = _flash_result(l_sc, acc_sc).astype(o_ref.dtype)


def _mla_sample_attn(q2d, knew, cache_kv, cache_krt, page_table, layer_j, seq_len):
    b, n_pages = page_table.shape
    n_groups = n_pages // PAGES_PER_STEP
    rows = seq_len * MLA_HEADS
    keys = PAGES_PER_STEP * PAGE_SIZE
    pt_flat = page_table.reshape(-1)

    def page_spec(i, shape):
        return pl.BlockSpec(
            (None, None) + shape,
            lambda bi, gi, pt: (layer_j, pt[bi * n_pages + gi * PAGES_PER_STEP + i], 0, 0))

    kern = functools.partial(_mla_sample_kernel, n_groups=n_groups, seq_len=seq_len)
    grid_spec = pltpu.PrefetchScalarGridSpec(
        num_scalar_prefetch=1,
        grid=(b, n_groups),
        in_specs=([pl.BlockSpec((rows, MLA_QK), lambda bi, gi, pt: (bi, 0)),
                   pl.BlockSpec((None, knew.shape[1], MLA_QK), lambda bi, gi, pt: (bi, 0, 0))]
                  + [page_spec(i, (PAGE_SIZE, MLA_KV_LORA)) for i in range(PAGES_PER_STEP)]
                  + [page_spec(i, (MLA_ROPE, PAGE_SIZE)) for i in range(PAGES_PER_STEP)]),
        out_specs=pl.BlockSpec((rows, MLA_KV_LORA), lambda bi, gi, pt: (bi, 0)),
        scratch_shapes=[pltpu.VMEM((keys, MLA_KV_LORA), BF16), pltpu.VMEM((MLA_ROPE, keys), BF16),
                        pltpu.VMEM((rows, LANES), F32), pltpu.VMEM((rows, LANES), F32),
                        pltpu.VMEM((rows, MLA_KV_LORA), F32)])
    return pl.pallas_call(
        kern,
        grid_spec=grid_spec,
        out_shape=jax.ShapeDtypeStruct((q2d.shape[0], MLA_KV_LORA), BF16),
        compiler_params=_params(2),
        name="mla_sample_attn",
    )(pt_flat, q2d, knew, *([cache_kv] * PAGES_PER_STEP), *([cache_krt] * PAGES_PER_STEP))


def _mla_out_kernel(ol_ref, wuv_ref, wo_ref, x_ref, gt_ref, gpost_ref, o_ref, o_sc, out_sc, stat_sc, *, grp):
    r = MLA_KV_LORA
    for h in range(MLA_HEADS):
        o_h = jnp.dot(ol_ref[:, h * r:(h + 1) * r], wuv_ref[h], preferred_element_type=F32)
        o_sc[:, h * MLA_V:(h + 1) * MLA_V] = o_h.astype(BF16)
    out_sc[...] = jnp.dot(o_sc[...], wo_ref[...], preferred_element_type=F32)
    _ada_out_rows(x_ref, out_sc, gpost_ref, gt_ref, grp.row(), 1.0, stat_sc, o_ref)


def _mla_out(o_lat, w_uv, w_o, x, grp, layer, g_post, tm=256):
    t, d = x.shape
    assert grp.tiles_per_mod_row is None
    sub = _Group(grp.mod, tm, None)
    kern = functools.partial(_mla_out_kernel, grp=sub)
    return pl.pallas_call(
        kern,
        grid=(t // tm,),
        in_specs=[pl.BlockSpec((tm, o_lat.shape[1]), lambda i: (i, 0)),
                  pl.BlockSpec(w_uv.shape, lambda i: (0, 0, 0)),
                  pl.BlockSpec(w_o.shape, lambda i: (0, 0)),
                  pl.BlockSpec((tm, d), lambda i: (i, 0)),
                  sub.mod_spec(layer, 3 + 2),
                  pl.BlockSpec((1, d), lambda i: (0, 0))],
        out_specs=pl.BlockSpec((tm, d), lambda i: (i, 0)),
        out_shape=jax.ShapeDtypeStruct((t, d), F32),
        scratch_shapes=[pltpu.VMEM((tm, MLA_HEADS * MLA_V), BF16), pltpu.VMEM((tm, d), F32),
                        pltpu.VMEM((tm, LANES), F32)],
        compiler_params=_params(1),
        name="mla_out",
    )(o_lat, w_uv, w_o, x, sub.mod, g_post.reshape(1, d))


def _t5_buckets(delta):
    n = np.maximum(delta, 0)
    max_exact = N_BUCKETS // 2
    log_ratio = np.log(np.maximum(n, 1).astype(np.float64) / max_exact) / math.log(BUCKET_MAX_DIST / max_exact)
    large = np.minimum(max_exact + (log_ratio * (N_BUCKETS - max_exact)).astype(np.int64), N_BUCKETS - 1)
    return np.where(n < max_exact, n, large).astype(np.int32)


def _swa_bias_table(rel_bias, delta, valid):
    lq, lk = delta.shape
    one_hot = (jnp.asarray(_t5_buckets(delta))[None] == jnp.arange(N_BUCKETS)[:, None, None]).astype(F32)
    bias = jnp.einsum("nh,nqk->hqk", rel_bias.astype(F32), one_hot, precision=lax.Precision.HIGHEST)
    bias = jnp.where(jnp.asarray(valid)[None], bias, NEG_INF)
    return bias.reshape(SWA_HEADS * lq, lk)


def _swa_sink_table(sinks, lq):
    return jnp.broadcast_to(jnp.repeat(sinks.astype(F32), lq)[:, None], (sinks.shape[0] * lq, LANES))


def _swa_q_perm():
    perm = np.zeros(SWA_HEADS * SWA_HEAD_DIM, np.int32)
    for m in range(SWA_KV_HEADS // 2):
        for g in range(SWA_GROUP):
            for p in range(2):
                src = ((2 * m + p) * SWA_GROUP + g) * SWA_HEAD_DIM
                dst = (4 * m + g) * LANES + p * SWA_HEAD_DIM
                perm[dst:dst + SWA_HEAD_DIM] = np.arange(src, src + SWA_HEAD_DIM)
    return perm


def _rope_tables(pos):
    half = MLA_ROPE // 2
    inv = ROPE_THETA ** (-jnp.arange(half, dtype=F32) / half)
    ang = pos.astype(F32)[:, None] * inv[None, :]
    cos, sin = jnp.cos(ang), jnp.sin(ang)
    zeros = jnp.zeros((pos.shape[0], LANES - MLA_ROPE), F32)
    return (jnp.concatenate([cos, cos, zeros], axis=1), jnp.concatenate([-sin, sin, zeros], axis=1))


def _pad_rope_cols(w):
    half = MLA_ROPE // 2
    z = jnp.zeros((w.shape[0], LANES - MLA_ROPE), w.dtype)
    return (jnp.concatenate([w, z], axis=1),
            jnp.concatenate([w[:, half:], w[:, :half], z], axis=1))


def kernel(x_prompt, x_sample, state_swa_k, state_swa_v, state_conv, cache_mla_kv, cache_mla_kr, page_table,
           c_prompt, c_sample, ada_w, ada_b, norm_pre, norm_post, ffn_w_in, ffn_w_out,
           gm_w_in, gm_ln_g, gm_ln_b, gm_w_s, gm_b_s, gm_w_out,
           swa_w_qkv, swa_w_o, swa_sinks, rel_bias,
           sc_w_in, sc_conv, sc_w_out,
           mla_w_qa, mla_qa_norm, mla_w_qb, mla_w_kva, mla_kva_norm, mla_w_kvb, mla_w_o):
    n_seq, seq_len, d = x_prompt.shape
    n_dec, dec_len, _ = x_sample.shape
    depth = ada_w.shape[0]
    past_len = page_table.shape[1] * PAGE_SIZE
    t_p, t_s = n_seq * seq_len, n_dec * dec_len

    c_all = jnp.concatenate([jnp.repeat(c_sample, dec_len, axis=0), c_prompt,
                             jnp.zeros((8 - n_seq, d), F32)], axis=0)
    mod = _ada(c_all, ada_w, ada_b)
    tm_p, tm_s = 512, 256
    grp_p = _Group(mod[:, t_s:t_s + 8], tm_p, seq_len // tm_p)
    grp_s = _Group(mod, t_s, None, single_tile=True) if t_s <= tm_p else _Group(mod, tm_s, None)

    xs = [x_prompt.reshape(t_p, d), x_sample.reshape(t_s, d)]
    outs = {}
    ffn_w = (ffn_w_in[0, 0].astype(BF16), ffn_w_out[0, 0].astype(BF16))

    def ffn_pair(xs, ffn_w, layer, which):
        nxt = (layer, 1) if which == 0 else (layer + 1, 0)
        next_w = (ffn_w_in, ffn_w_out) + nxt if nxt[0] < depth else None
        sub = 2 * which
        res = _half_ffn(xs[0], grp_p, layer, which, norm_pre[layer, sub], norm_post[layer, sub], *ffn_w,
                        next_w=next_w)
        x_s = _half_ffn(xs[1], grp_s, layer, which, norm_pre[layer, sub], norm_post[layer, sub], *ffn_w)
        if next_w is None:
            return [res, x_s], None
        return [res[0], x_s], (res[1], res[2])

    for i in range(depth):
        kind, j = i % 4, i // 4
        xs, ffn_w = ffn_pair(xs, ffn_w, i, 0)

        if kind == 0:
            w_in = gm_w_in[j].astype(BF16)
            w_out = gm_w_out[j].astype(BF16)
            new = []
            for x, g, chunk, sl, cps in ((xs[0], grp_p, GM_CHUNK, GM_CHUNK, seq_len // GM_CHUNK),
                                         (xs[1], grp_s, GM_CHUNK, dec_len, 1)):
                lc = min(sl, GM_CHUNK)
                m = jnp.tile(gm_w_s[j][:, :lc, :lc], (1, chunk // lc, chunk // lc))
                bias = jnp.broadcast_to(jnp.tile(gm_b_s[j][:, :lc], (1, chunk // lc))[:, :, None],
                                        (GM_GROUPS, chunk, LANES))
                uv = _norm_mod_matmul(x, g, i, norm_pre[i, 1], w_in, act="gelu", out_dtype=BF16)
                mixed, st = _gm_core(uv, gm_ln_g[j], gm_ln_b[j], m, bias, chunk, lc, cps)
                new.append(_matmul_residual(mixed, w_out, x, g, i, norm_post[i, 1]))
                outs.setdefault("gm", []).append(st)
            xs = new

        elif kind == 1:
            perm = _swa_q_perm()
            nq = SWA_HEADS * SWA_HEAD_DIM
            nkv = SWA_KV_HEADS * SWA_HEAD_DIM
            w_qkv = jnp.concatenate([swa_w_qkv[j][:, :nq][:, perm], swa_w_qkv[j][:, nq:]], axis=1).astype(BF16)
            w_o = swa_w_o[j][perm, :].astype(BF16)
            w = SWA_WINDOW
            i_q, i_k = np.arange(w), np.arange(2 * w)
            delta = w + i_q[:, None] - i_k[None, :]
            bias_p = _swa_bias_table(rel_bias, delta, (delta >= 0) & (delta < w))
            sink_p = _swa_sink_table(swa_sinks[j], w)
            qkv_p = _norm_mod_matmul(xs[0], grp_p, i, norm_pre[i, 1], w_qkv)
            o_p = _swa_prompt_core(qkv_p, bias_p, sink_p, seq_len)
            x_p = _matmul_residual(o_p, w_o, xs[0], grp_p, i, norm_post[i, 1])
            kv_p = qkv_p.reshape(n_seq, seq_len, -1)[:, seq_len - w:, nq:]
            outs.setdefault("swa_kp", []).append(kv_p[..., :nkv].reshape(n_seq, w, SWA_KV_HEADS, SWA_HEAD_DIM))
            outs.setdefault("swa_vp", []).append(kv_p[..., nkv:].reshape(n_seq, w, SWA_KV_HEADS, SWA_HEAD_DIM))
            lb = state_swa_k.shape[2]
            n_keys = lb + dec_len
            pad = (-n_keys) % 8
            i_q, i_k = np.arange(dec_len), np.arange(n_keys + pad)
            delta = lb + i_q[:, None] - i_k[None, :]
            valid = (delta >= 0) & (delta < w) & (i_k[None, :] < n_keys)
            bias_s = _swa_bias_table(rel_bias, delta, valid)
            sink_s = _swa_sink_table(swa_sinks[j], dec_len)
            qkv_s = _norm_mod_matmul(xs[1], grp_s, i, norm_pre[i, 1], w_qkv).reshape(n_dec, dec_len, -1)
            zpad = jnp.zeros((n_dec, pad, nkv), F32)
            kk = jnp.concatenate([state_swa_k[j].reshape(n_dec, lb, nkv), qkv_s[..., nq:nq + nkv], zpad], axis=1)
            vv = jnp.concatenate([state_swa_v[j].reshape(n_dec, lb, nkv), qkv_s[..., nq + nkv:], zpad], axis=1)
            q_s = qkv_s[..., :nq].reshape(n_dec, dec_len, SWA_KV_HEADS // 2, SWA_GROUP, LANES)
            q_s = q_s.transpose(0, 2, 3, 1, 4).reshape(n_dec, SWA_KV_HEADS // 2, SWA_GROUP * dec_len, LANES)
            o_s = _swa_sample_core(q_s, kk, vv, bias_s, sink_s)
            o_s = o_s.reshape(n_dec, SWA_KV_HEADS // 2, SWA_GROUP, dec_len, LANES).transpose(0, 3, 1, 2, 4)
            x_s = _matmul_residual(o_s.reshape(t_s, nq), w_o, xs[1], grp_s, i, norm_post[i, 1])
            outs.setdefault("swa_ks", []).append(kk[:, n_keys - lb:n_keys].reshape(n_dec, lb, SWA_KV_HEADS, SWA_HEAD_DIM))
            outs.setdefault("swa_vs", []).append(vv[:, n_keys - lb:n_keys].reshape(n_dec, lb, SWA_KV_HEADS, SWA_HEAD_DIM))
            xs = [x_p, x_s]

        elif kind == 2:
            w_in = sc_w_in[j].astype(BF16)
            w_out = sc_w_out[j].astype(BF16)
            c = sc_w_out.shape[1]
            g3_p = _norm_mod_matmul(xs[0], grp_p, i, norm_pre[i, 1], w_in, out_dtype=BF16)
            y_p, st_p = _conv_prompt_core(g3_p, sc_conv[j], seq_len)
            x_p = _matmul_residual(y_p, w_out, xs[0], grp_p, i, norm_post[i, 1])
            outs.setdefault("conv_p", []).append(st_p[:, 8 - (CONV_WIDTH - 1):])
            g3_s = _norm_mod_matmul(xs[1], grp_s, i, norm_pre[i, 1], w_in, out_dtype=BF16)
            y_s, st_s = _conv_sample_core(g3_s.reshape(n_dec, dec_len * 3 * c),
                                          state_conv[j].reshape(n_dec, (CONV_WIDTH - 1) * c), sc_conv[j], dec_len)
            x_s = _matmul_residual(y_s.reshape(t_s, c), w_out, xs[1], grp_s, i, norm_post[i, 1])
            outs.setdefault("conv_s", []).append(st_s.reshape(n_dec, CONV_WIDTH - 1, c))
            xs = [x_p, x_s]

        else:
            r = MLA_KV_LORA
            kr_pad, kr_sw = _pad_rope_cols(mla_w_kva[j][:, r:])
            w_p = jnp.concatenate([mla_w_qa[j], mla_w_kva[j][:, :r], kr_pad, kr_sw], axis=1).astype(BF16)
            w_qb = mla_w_qb[j].reshape(MLA_Q_LORA, MLA_HEADS, MLA_NOPE + MLA_ROPE)
            w_nope = w_qb[:, :, :MLA_NOPE].reshape(MLA_Q_LORA, -1).astype(BF16)
            rope_pairs = [_pad_rope_cols(w_qb[:, h, MLA_NOPE:]) for h in range(MLA_HEADS)]
            w_rope = jnp.concatenate([p[0] for p in rope_pairs], axis=1).astype(BF16)
            w_rope_sw = jnp.concatenate([p[1] for p in rope_pairs], axis=1).astype(BF16)
            w_kvb = mla_w_kvb[j].reshape(r, MLA_HEADS, MLA_NOPE + MLA_V)
            w_uk_t = w_kvb[:, :, :MLA_NOPE].transpose(1, 2, 0).astype(BF16)
            w_uv = w_kvb[:, :, MLA_NOPE:].transpose(1, 0, 2).astype(BF16)
            w_o = mla_w_o[j].astype(BF16)
            tm = 256
            cos_p, sin_p = _rope_tables(jnp.arange(seq_len))
            proj_p = _norm_mod_matmul(xs[0], grp_p, i, norm_pre[i, 1], w_p, tn=w_p.shape[1])
            q_p, ckv_p, kr_p, k_p, v_p = _mla_proj(proj_p, cos_p, sin_p, seq_len // tm, mla_qa_norm[j],
                                                   mla_kva_norm[j], w_nope, w_rope, w_rope_sw,
                                                   mla_w_kvb[j].astype(BF16), True, tm=tm)
            o_p = _mla_prompt_attn(q_p, k_p, v_p, n_seq, seq_len)
            x_p = _matmul_residual(o_p, w_o, xs[0], grp_p, i, norm_post[i, 1])
            outs.setdefault("mla_kvp", []).append(ckv_p.reshape(n_seq, seq_len, r))
            outs.setdefault("mla_krp", []).append(kr_p[:, :MLA_ROPE].reshape(n_seq, seq_len, MLA_ROPE))
            cos_s, sin_s = _rope_tables(past_len + jnp.arange(dec_len))
            cos_s, sin_s = jnp.tile(cos_s, (tm // dec_len, 1)), jnp.tile(sin_s, (tm // dec_len, 1))
            proj_s = _norm_mod_matmul(xs[1], grp_s, i, norm_pre[i, 1], w_p, tn=w_p.shape[1])
            q_s, ckv_s, kr_s, kcat_s = _mla_proj(proj_s, cos_s, sin_s, 1, mla_qa_norm[j],
                                                 mla_kva_norm[j], w_nope, w_rope, w_rope_sw, w_uk_t, False, tm=tm)
            knew = jnp.concatenate([kcat_s.reshape(n_dec, dec_len, MLA_QK),
                                    jnp.zeros((n_dec, 16 - dec_len, MLA_QK), BF16)], axis=1)
            ol_s = _mla_sample_attn(q_s.reshape(t_s * MLA_HEADS, MLA_QK), knew, cache_mla_kv,
                                    jnp.swapaxes(cache_mla_kr, 2, 3), page_table, j, dec_len)
            x_s = _mla_out(ol_s.reshape(t_s, MLA_HEADS * r), w_uv, w_o, xs[1], grp_s, i, norm_post[i, 1])
            outs.setdefault("mla_kvs", []).append(ckv_s.reshape(n_dec, dec_len, r))
            outs.setdefault("mla_krs", []).append(kr_s[:, :MLA_ROPE].reshape(n_dec, dec_len, MLA_ROPE))
            xs = [x_p, x_s]

        xs, ffn_w = ffn_pair(xs, ffn_w, i, 1)

    gm_p, gm_s = outs["gm"][0::2], outs["gm"][1::2]
    return (xs[0].reshape(n_seq, seq_len, d), xs[1].reshape(n_dec, dec_len, d),
            jnp.stack(gm_p),
            jnp.stack([s.reshape(n_dec, dec_len, -1) for s in gm_s]),
            jnp.stack(outs["swa_kp"]), jnp.stack(outs["swa_vp"]),
            jnp.stack(outs["swa_ks"]), jnp.stack(outs["swa_vs"]),
            jnp.stack(outs["conv_p"]), jnp.stack(outs["conv_s"]),
            jnp.stack(outs["mla_kvp"]), jnp.stack(outs["mla_krp"]),
            jnp.stack(outs["mla_kvs"]), jnp.stack(outs["mla_krs"]))
```
